```python
import jax, jax.numpy as jnp
from jax import lax
import numpy as np

D_MODEL = 1024
BATCH = 4
SEQ = 4096
DEPTH = 2

GRID_W = 64
CTX_LEN = 256
N_MOD = 9
D_FF = 2752
EPS = 1e-6
ATTN_HEADS = 4
Q_LORA = 256
KV_LORA = 128
NOPE_DIM = 128
ROPE_DIM = 64
AXIS_FREQS = ROPE_DIM // 4
V_DIM = 128
ATTN_WIDTH = ATTN_HEADS * V_DIM
ROPE_BASE = 10000.0
SM_SCALE = (NOPE_DIM + ROPE_DIM) ** -0.5
Q_BLOCK = 128
MLSTM_HEADS = 4
MLSTM_DH = 128
MLSTM_WIDTH = MLSTM_HEADS * MLSTM_DH
MLSTM_CHUNK = 64
CONV_K = 3
N_GATES = 4 * MLSTM_HEADS
IN_SIZES = (Q_LORA, KV_LORA, ROPE_DIM, 2 * MLSTM_WIDTH, MLSTM_WIDTH, MLSTM_WIDTH, N_GATES)
D_IN = Q_LORA + KV_LORA + ROPE_DIM + 4 * MLSTM_WIDTH + N_GATES
MIX_WIDTH = ATTN_WIDTH + MLSTM_WIDTH

kernel_name = 'hybrid_mla_mlstm_dit_block'


def rms_norm(x, g):
    xf = x.astype(jnp.float32)
    y = xf * lax.rsqrt(jnp.mean(xf * xf, axis=-1, keepdims=True) + EPS)
    return (y * g.astype(jnp.float32)).astype(x.dtype)


def head_layer_norm(h, g):
    mu = jnp.mean(h, axis=-1, keepdims=True)
    var = jnp.mean(jnp.square(h - mu), axis=-1, keepdims=True)
    y = (h - mu) * lax.rsqrt(var + EPS)
    return y.reshape(h.shape[:2] + (-1,)) * g.astype(jnp.float32)


def swiglu(h, wg, wu, wd):
    return (jax.nn.silu(h @ wg) * (h @ wu)) @ wd


def split_in(p):
    offs = [int(o) for o in np.cumsum(IN_SIZES)[:-1]]
    return jnp.split(p, offs, axis=-1)


def conv_centred(u, w, b):
    pad = CONV_K // 2
    t = u.shape[1]
    up = jnp.pad(u, ((0, 0), (pad, pad), (0, 0)))
    out = up[:, 0:t] * w[0]
    for j in range(1, CONV_K):
        out = out + up[:, j:j + t] * w[j]
    return out + b


def axial_rope_tables(n_tokens):
    rows = n_tokens // GRID_W
    t_row = jnp.repeat(jnp.arange(rows), GRID_W).astype(jnp.float32)
    t_col = jnp.tile(jnp.arange(GRID_W), rows).astype(jnp.float32)
    inv = ROPE_BASE ** (-jnp.arange(AXIS_FREQS, dtype=jnp.float32) / AXIS_FREQS)
    ang = jnp.stack([t_row[:, None] * inv, t_col[:, None] * inv], axis=1)
    return jnp.cos(ang)[:, None], jnp.sin(ang)[:, None]


def apply_axial_rope(x, cos, sin):
    sh = x.shape
    xr = x.reshape(sh[:-1] + (2, 2, AXIS_FREQS))
    x1, x2 = xr[..., 0, :], xr[..., 1, :]
    cos = cos.astype(x.dtype)
    sin = sin.astype(x.dtype)
    out = jnp.stack([x1 * cos - x2 * sin, x2 * cos + x1 * sin], axis=-2)
    return out.reshape(sh)


def mla_query(cq, q_norm, w_uq):
    b, t, _ = cq.shape
    return (rms_norm(cq, q_norm) @ w_uq).reshape(b, t, ATTN_HEADS, NOPE_DIM + ROPE_DIM)


def mla_keys(ckv, k_rope, kv_norm, w_ukv):
    b, t, _ = ckv.shape
    kv = (rms_norm(ckv, kv_norm) @ w_ukv).reshape(b, t, ATTN_HEADS, NOPE_DIM + V_DIM)
    k = jnp.concatenate([kv[..., :NOPE_DIM],
                         jnp.broadcast_to(k_rope, (b, t, ATTN_HEADS, ROPE_DIM)).astype(kv.dtype)], axis=-1)
    return k, kv[..., NOPE_DIM:]


def softmax_attend(q, k, v):
    s = jnp.einsum('bqhd,bkhd->bhqk', q, k).astype(jnp.float32) * SM_SCALE
    p = jax.nn.softmax(s, axis=-1).astype(v.dtype)
    return jnp.einsum('bhqk,bkhd->bqhd', p, v)


def blocked_attention(q, k, v):
    b, s, h, dk = q.shape
    nb = s // Q_BLOCK
    qb = jnp.swapaxes(q.reshape(b, nb, Q_BLOCK, h, dk), 0, 1)
    ob = lax.map(lambda qq: softmax_attend(qq, k, v), qb)
    return jnp.swapaxes(ob, 0, 1).reshape(b, s, h * v.shape[-1])


def mlstm_state_update(state, k, v, ig, b):
    C, n, m = state
    b_last = b[..., -1]
    log_w = b_last[..., None] - b + ig
    m_new = jnp.maximum(b_last + m, jnp.max(log_w, axis=-1))
    w = jnp.exp(log_w - m_new[..., None])
    decay = jnp.exp(b_last + m - m_new)
    C = decay[..., None, None] * C + jnp.einsum('bhsd,bhse->bhde', w[..., None] * k, v)
    n = decay[..., None] * n + jnp.einsum('bhs,bhsd->bhd', w, k)
    return (C, n, m_new)


def mlstm_chunkwise(q, k, v, ig, lf, state):
    bq, h, t, dh = q.shape
    nc = t // MLSTM_CHUNK
    tri = jnp.tril(jnp.ones((MLSTM_CHUNK, MLSTM_CHUNK), dtype=bool))

    def chunks(a):
        a = a.reshape((bq, h, nc, MLSTM_CHUNK) + a.shape[3:])
        return jnp.moveaxis(a, 2, 0)

    def step(carry, inp):
        C, n, m = carry
        qc, kc, vc, igc, lfc = inp
        b = jnp.cumsum(lfc, axis=-1)
        log_d = jnp.where(tri, b[..., :, None] - b[..., None, :] + igc[..., None, :], -jnp.inf)
        log_inter = b + m[..., None]
        m_t = jnp.maximum(log_inter, jnp.max(log_d, axis=-1))
        s = jnp.einsum('bhtd,bhsd->bhts', qc, kc) * jnp.exp(log_d - m_t[..., None])
        inter = jnp.exp(log_inter - m_t)
        num = jnp.einsum('bhts,bhsd->bhtd', s, vc) + inter[..., None] * jnp.einsum('bhtd,bhde->bhte', qc, C)
        den = jnp.sum(s, axis=-1) + inter * jnp.einsum('bhtd,bhd->bht', qc, n)
        hc = num / jnp.maximum(jnp.abs(den), jnp.exp(-m_t))[..., None]
        return mlstm_state_update(carry, kc, vc, igc, b), hc

    state, hs = lax.scan(step, state, (chunks(q), chunks(k), chunks(v), chunks(ig), chunks(lf)))
    return jnp.moveaxis(hs, 0, 2).reshape(bq, h, t, dh), state


def mlstm_inputs(qk_pre, v, gates, conv_w, conv_b, gate_b):
    b, t, _ = v.shape
    u = jax.nn.silu(conv_centred(qk_pre, conv_w, conv_b))
    q, k = jnp.split(u, 2, axis=-1)

    def heads(a):
        return a.reshape(b, t, MLSTM_HEADS, MLSTM_DH).transpose(0, 2, 1, 3).astype(jnp.float32)

    g = gates.astype(jnp.float32) + gate_b.reshape(-1).astype(jnp.float32)
    g = g.reshape(b, t, 4, MLSTM_HEADS).transpose(2, 0, 3, 1)
    return heads(q), heads(k) * (MLSTM_DH ** -0.5), heads(v), g


def mlstm_bidirectional(lat, ctx, need_ctx_out):
    qx, kx, vx, gx = lat
    qc, kc, vc, gc = ctx
    bq = qx.shape[0]
    zero = (jnp.zeros((bq, MLSTM_HEADS, MLSTM_DH, MLSTM_DH), jnp.float32),
            jnp.zeros((bq, MLSTM_HEADS, MLSTM_DH), jnp.float32),
            jnp.zeros((bq, MLSTM_HEADS), jnp.float32))
    hx_tot = jnp.zeros(qx.shape, jnp.float32)
    hc_tot = jnp.zeros(qc.shape, jnp.float32)
    for d in range(2):
        sx = [qx, kx, vx, gx[2 * d], jax.nn.log_sigmoid(gx[2 * d + 1])]
        sc = [qc, kc, vc, gc[2 * d], jax.nn.log_sigmoid(gc[2 * d + 1])]
        if d == 1:
            sx = [jnp.flip(a, axis=2) for a in sx]
            sc = [jnp.flip(a, axis=2) for a in sc]
        if need_ctx_out:
            hc, st = mlstm_chunkwise(sc[0], sc[1], sc[2], sc[3], sc[4], zero)
            hc_tot = hc_tot + (jnp.flip(hc, axis=2) if d == 1 else hc)
        else:
            st = mlstm_state_update(zero, sc[1], sc[2], sc[3], jnp.cumsum(sc[4], axis=-1))
        hx, _ = mlstm_chunkwise(sx[0], sx[1], sx[2], sx[3], sx[4], st)
        hx_tot = hx_tot + (jnp.flip(hx, axis=2) if d == 1 else hx)
    return hx_tot, hc_tot


def merge_groups(a, hm, o, p):
    b, t, _ = o.shape
    hm = jnp.swapaxes(hm, 1, 2) * jax.nn.sigmoid(o.astype(jnp.float32)).reshape(b, t, MLSTM_HEADS, MLSTM_DH)
    hm = head_layer_norm(hm, p['mlstm_norm']).astype(a.dtype)
    return jnp.concatenate([rms_norm(a, p['attn_out_norm']), hm], axis=-1) @ p['w_out']


def mixer(hx, hc, cos, sin, p, need_ctx_out):
    b, s, _ = hx.shape
    cq_x, ckv_x, kr_x, qk_x, v_x, o_x, g_x = split_in(hx @ p['w_in'])
    cq_c, ckv_c, kr_c, qk_c, v_c, o_c, g_c = split_in(hc @ p['w_in'])
    k_c, va_c = mla_keys(ckv_c, kr_c[:, :, None, :], p['kv_norm'], p['w_ukv'])
    k_x, va_x = mla_keys(ckv_x, apply_axial_rope(kr_x[:, :, None, :], cos, sin), p['kv_norm'], p['w_ukv'])
    q_x = mla_query(cq_x, p['q_norm'], p['w_uq'])
    q_x = jnp.concatenate([q_x[..., :NOPE_DIM], apply_axial_rope(q_x[..., NOPE_DIM:], cos, sin)], axis=-1)
    k_all = jnp.concatenate([k_c, k_x], axis=1)
    v_all = jnp.concatenate([va_c, va_x], axis=1)
    a_x = blocked_attention(q_x, k_all, v_all)
    lat = mlstm_inputs(qk_x, v_x, g_x, p['conv_w'], p['conv_b'], p['gate_b'])
    ctm = mlstm_inputs(qk_c, v_c, g_c, p['conv_w'], p['conv_b'], p['gate_b'])
    m_x, m_c = mlstm_bidirectional(lat, ctm, need_ctx_out)
    y_x = merge_groups(a_x, m_x, o_x, p)
    if not need_ctx_out:
        return y_x, None
    q_c = mla_query(cq_c, p['q_norm'], p['w_uq'])
    a_c = softmax_attend(q_c, k_c, va_c).reshape(b, hc.shape[1], ATTN_WIDTH)
    return y_x, merge_groups(a_c, m_c, o_c, p)


def sandwich_in(h, m, sub, p):
    return rms_norm(h, p['norm_pre'][sub]) * (1 + m[:, 3 * sub + 1]) + m[:, 3 * sub]


def sandwich_out(h, y, m, sub, p, weight):
    return h + weight * m[:, 3 * sub + 2] * rms_norm(y, p['norm_post'][sub])


def ffn_sub(h, m, sub, j, p):
    y = swiglu(sandwich_in(h, m, sub, p), p['w_gate'][j], p['w_up'][j], p['w_down'][j])
    return sandwich_out(h, y, m, sub, p, 0.5)


def trunk_layer(x, xc, mod_x, mod_c, cos, sin, p, need_ctx_out):
    x = ffn_sub(x, mod_x, 0, 0, p)
    xc = ffn_sub(xc, mod_c, 0, 0, p)
    y_x, y_c = mixer(sandwich_in(x, mod_x, 1, p), sandwich_in(xc, mod_c, 1, p), cos, sin, p, need_ctx_out)
    x = sandwich_out(x, y_x, mod_x, 1, p, 1.0)
    x = ffn_sub(x, mod_x, 2, 1, p)
    if need_ctx_out:
        xc = sandwich_out(xc, y_c, mod_c, 1, p, 1.0)
        xc = ffn_sub(xc, mod_c, 2, 1, p)
    return x, xc


def setup_inputs(seed: int = 0) -> dict:
    key = jax.random.key(seed)
    ks = jax.random.split(key, 26)
    f32 = jnp.float32
    L, D = DEPTH, D_MODEL

    def nrm(k, shape, fan):
        return jax.random.normal(k, shape, f32) * (fan ** -0.5)

    def gain(k, shape):
        return 1.0 + 0.05 * jax.random.normal(k, shape, f32)

    i_bias = 0.1 * jax.random.normal(ks[20], (L, 2, MLSTM_HEADS), f32)
    f_bias = jnp.linspace(3.0, 6.0, MLSTM_HEADS, dtype=f32) + 0.1 * jax.random.normal(ks[21], (L, 2, MLSTM_HEADS), f32)
    return {
        'x': jax.random.normal(ks[0], (BATCH, SEQ, D), f32),
        'c': jax.random.normal(ks[1], (BATCH, D), f32),
        'ctx': jax.random.normal(ks[2], (BATCH, CTX_LEN, D), f32),
        'c_ctx': jax.random.normal(ks[3], (D,), f32),
        'w_mod': 0.5 * nrm(ks[4], (L, D, N_MOD * D), D),
        'b_mod': 0.02 * jax.random.normal(ks[5], (L, N_MOD * D), f32),
        'norm_pre': gain(ks[6], (L, 3, D)),
        'norm_post': gain(ks[7], (L, 3, D)),
        'ffn_w_gate': nrm(ks[8], (L, 2, D, D_FF), D),
        'ffn_w_up': nrm(ks[9], (L, 2, D, D_FF), D),
        'ffn_w_down': nrm(ks[10], (L, 2, D_FF, D), D_FF),
        'w_in': nrm(ks[11], (L, D, D_IN), D),
        'q_norm': gain(ks[12], (L, Q_LORA)),
        'w_uq': nrm(ks[13], (L, Q_LORA, ATTN_HEADS * (NOPE_DIM + ROPE_DIM)), Q_LORA),
        'kv_norm': gain(ks[14], (L, KV_LORA)),
        'w_ukv': nrm(ks[15], (L, KV_LORA, ATTN_HEADS * (NOPE_DIM + V_DIM)), KV_LORA),
        'attn_out_norm': gain(ks[16], (L, ATTN_WIDTH)),
        'conv_w': nrm(ks[17], (L, CONV_K, 2 * MLSTM_WIDTH), CONV_K),
        'conv_b': 0.02 * jax.random.normal(ks[18], (L, 2 * MLSTM_WIDTH), f32),
        'gate_b': jnp.stack([i_bias, f_bias], axis=2).reshape(L, 4, MLSTM_HEADS),
        'mlstm_norm': gain(ks[19], (L, MLSTM_WIDTH)),
        'w_out': nrm(ks[22], (L, MIX_WIDTH, D), MIX_WIDTH),
    }


def reference(x, c, ctx, c_ctx, w_mod, b_mod, norm_pre, norm_post, ffn_w_gate, ffn_w_up, ffn_w_down,
              w_in, q_norm, w_uq, kv_norm, w_ukv, attn_out_norm, conv_w, conv_b, gate_b, mlstm_norm, w_out):
    bsz, n_tok, d = x.shape
    cos, sin = axial_rope_tables(n_tok)
    xc = ctx
    for l in range(DEPTH):
        p = {'norm_pre': norm_pre[l], 'norm_post': norm_post[l], 'w_gate': ffn_w_gate[l],
             'w_up': ffn_w_up[l], 'w_down': ffn_w_down[l], 'w_in': w_in[l], 'q_norm': q_norm[l],
             'w_uq': w_uq[l], 'kv_norm': kv_norm[l], 'w_ukv': w_ukv[l],
             'attn_out_norm': attn_out_norm[l], 'conv_w': conv_w[l], 'conv_b': conv_b[l],
             'gate_b': gate_b[l], 'mlstm_norm': mlstm_norm[l], 'w_out': w_out[l]}
        mod_x = (jax.nn.silu(c) @ w_mod[l] + b_mod[l]).reshape(bsz, N_MOD, 1, d)
        mod_c = (jax.nn.silu(c_ctx) @ w_mod[l] + b_mod[l]).reshape(1, N_MOD, 1, d)
        x, xc = trunk_layer(x, xc, mod_x, mod_c, cos, sin, p, l < DEPTH - 1)
    return x
```

```python
import functools

import jax
import jax.numpy as jnp
import numpy as np
from jax import lax
from jax.experimental import pallas as pl
from jax.experimental.pallas import tpu as pltpu

F32 = jnp.float32
BF16 = jnp.bfloat16

N_MOD = 9
EPS = 1e-6
GRID_W = 64
ATTN_HEADS = 4
Q_LORA = 256
KV_LORA = 128
NOPE_DIM = 128
ROPE_DIM = 64
AXIS_FREQS = ROPE_DIM // 4
V_DIM = 128
QK_DIM = NOPE_DIM + ROPE_DIM
ATTN_WIDTH = ATTN_HEADS * V_DIM
ROPE_BASE = 10000.0
SM_SCALE = QK_DIM ** -0.5
MLSTM_HEADS = 4
MLSTM_DH = 128
MLSTM_WIDTH = MLSTM_HEADS * MLSTM_DH
CONV_K = 3
N_GATES = 4 * MLSTM_HEADS

LANES = 128
SUBLANES = 8
MXU_DIM = 256
TILE = MXU_DIM
VMEM_LIMIT = 56 * 1024 * 1024

A_W = Q_LORA + KV_LORA + 2 * ROPE_DIM
QK_OFF = A_W
V_OFF = QK_OFF + 2 * MLSTM_WIDTH
O_OFF = V_OFF + MLSTM_WIDTH
G_OFF = O_OFF + MLSTM_WIDTH
IN_W = G_OFF + LANES


def _round_up(n, m):
    return (n + m - 1) // m * m


def _rms(x, g):
    return x * lax.rsqrt(jnp.mean(x * x, axis=-1, keepdims=True) + EPS) * g


def _sandwich_in(x, mod, sub, g):
    return _rms(x, g) * (1.0 + mod[3 * sub + 1:3 * sub + 2]) + mod[3 * sub:3 * sub + 1]


def _params(n_grid):
    return pltpu.CompilerParams(dimension_semantics=("arbitrary",) * n_grid,
                                vmem_limit_bytes=VMEM_LIMIT)


def _const_spec(shape, index_map):
    return pl.BlockSpec(shape, index_map, pipeline_mode=pl.Buffered(1))


def _mod_spec(d, mod_t0, n_ctx_tiles):
    return pl.BlockSpec((1, 1, N_MOD, d),
                        lambda b, t: (b, (t + mod_t0 >= n_ctx_tiles).astype(jnp.int32), 0, 0))


def _mod_kernel(c_ref, w_ref, b_ref, o_ref):
    c = c_ref[...]
    h = (c * jax.nn.sigmoid(c)).astype(BF16)
    o_ref[0] = jnp.dot(h, w_ref[0].astype(BF16), preferred_element_type=F32) + b_ref[0]


def _modulation(c_rows, w_mod, b_mod):
    n_layers, d, n_out = w_mod.shape
    rows = c_rows.shape[0]
    n_blk = N_MOD * LANES
    return pl.pallas_call(
        _mod_kernel,
        grid=(n_layers, n_out // n_blk),
        in_specs=[pl.BlockSpec((rows, d), lambda l, j: (0, 0)),
                  pl.BlockSpec((1, d, n_blk), lambda l, j: (l, 0, j)),
                  pl.BlockSpec((1, 1, n_blk), lambda l, j: (l, 0, j))],
        out_specs=pl.BlockSpec((1, rows, n_blk), lambda l, j: (l, 0, j)),
        out_shape=jax.ShapeDtypeStruct((n_layers, rows, n_out), F32),
        compiler_params=_params(2),
        name="modulation",
    )(c_rows, w_mod, b_mod.reshape(n_layers, 1, n_out))


def _ffn_kernel(x_ref, mod_ref, gpre_ref, gpost_ref, wg_ref, wu_ref, wd_ref, o_ref, a_ref, *, sub):
    x = x_ref[0]
    mod = mod_ref[0, 0]
    h = _sandwich_in(x, mod, sub, gpre_ref[...]).astype(BF16)
    ff = a_ref.shape[1]
    for lo in range(0, ff, MXU_DIM):
        g = jnp.dot(h, wg_ref[:, lo:lo + MXU_DIM], preferred_element_type=F32)
        u = jnp.dot(h, wu_ref[:, lo:lo + MXU_DIM], preferred_element_type=F32)
        a_ref[:, lo:lo + MXU_DIM] = (g * jax.nn.sigmoid(g) * u).astype(BF16)
    y = jnp.dot(a_ref[...], wd_ref[...], preferred_element_type=F32)
    o_ref[0] = x + 0.5 * mod[3 * sub + 2:3 * sub + 3] * _rms(y, gpost_ref[...])


def _ffn(x, mod_all, g_pre, g_post, wg, wu, wd, *, layer, j, sub, n_ctx_tiles, mod_t0):
    bsz, t_in, d = x.shape
    ff = wg.shape[-1]
    return pl.pallas_call(
        functools.partial(_ffn_kernel, sub=sub),
        grid=(bsz, t_in // TILE),
        in_specs=[
            pl.BlockSpec((1, TILE, d), lambda b, t: (b, t, 0)),
            _mod_spec(d, mod_t0, n_ctx_tiles),
            pl.BlockSpec((None, None, 1, d), lambda b, t: (layer, sub, 0, 0)),
            pl.BlockSpec((None, None, 1, d), lambda b, t: (layer, sub, 0, 0)),
            _const_spec((None, None, d, ff), lambda b, t: (layer, j, 0, 0)),
            _const_spec((None, None, d, ff), lambda b, t: (layer, j, 0, 0)),
            _const_spec((None, None, ff, d), lambda b, t: (layer, j, 0, 0)),
        ],
        out_specs=pl.BlockSpec((1, TILE, d), lambda b, t: (b, t, 0)),
        out_shape=jax.ShapeDtypeStruct((bsz, t_in, d), F32),
        scratch_shapes=[pltpu.VMEM((TILE, ff), BF16)],
        compiler_params=_params(2),
        name=f"ffn_l{layer}_{j}",
    )(x, mod_all, g_pre, g_post, wg, wu, wd)


def _mixin_kernel(x_ref, mod_ref, gpre_ref, win_ref, qn_ref, wuq_ref, kvn_ref, wukv_ref, cos_ref, sin_ref,
                  q_ref, k_ref, vt_ref, qk_ref, vm_ref, o_ref, g_ref):
    h = _sandwich_in(x_ref[0], mod_ref[0, 0], 1, gpre_ref[...]).astype(BF16)
    cos4 = cos_ref[...]
    sin4 = sin_ref[...]

    pa = jnp.dot(h, win_ref[:, 0:A_W], preferred_element_type=F32)
    cq = pa[:, 0:Q_LORA]
    ckv = pa[:, Q_LORA:Q_LORA + KV_LORA]
    kr = pa[:, Q_LORA + KV_LORA:Q_LORA + KV_LORA + ROPE_DIM]
    kr_sw = pa[:, Q_LORA + KV_LORA + ROPE_DIM:A_W]
    k_rope = (kr * cos4[:, 0:ROPE_DIM] + kr_sw * sin4[:, 0:ROPE_DIM]).astype(BF16)

    nope_w = ATTN_HEADS * NOPE_DIM
    rope_w = ATTN_HEADS * ROPE_DIM
    qa = jnp.dot(_rms(cq, qn_ref[...]).astype(BF16), wuq_ref[...], preferred_element_type=F32)
    q_rope = qa[:, nope_w:nope_w + rope_w] * cos4 + qa[:, nope_w + rope_w:nope_w + 2 * rope_w] * sin4
    kva = jnp.dot(_rms(ckv, kvn_ref[...]).astype(BF16), wukv_ref[...], preferred_element_type=F32)
    for hd in range(ATTN_HEADS):
        q_ref[0, hd, :, 0:NOPE_DIM] = (qa[:, hd * NOPE_DIM:(hd + 1) * NOPE_DIM] * SM_SCALE).astype(BF16)
        q_ref[0, hd, :, NOPE_DIM:QK_DIM] = (q_rope[:, hd * ROPE_DIM:(hd + 1) * ROPE_DIM] * SM_SCALE).astype(BF16)
        k_ref[0, hd, :, 0:NOPE_DIM] = kva[:, hd * NOPE_DIM:(hd + 1) * NOPE_DIM].astype(BF16)
        k_ref[0, hd, :, NOPE_DIM:QK_DIM] = k_rope
        vt_ref[0, hd] = kva[:, nope_w + hd * V_DIM:nope_w + (hd + 1) * V_DIM].T.astype(BF16)

    qk_ref[0] = jnp.dot(h, win_ref[:, QK_OFF:V_OFF], preferred_element_type=F32)
    vm_ref[0] = jnp.dot(h, win_ref[:, V_OFF:O_OFF], preferred_element_type=F32).astype(BF16)
    o_ref[0] = jnp.dot(h, win_ref[:, O_OFF:G_OFF], preferred_element_type=F32)
    g_ref[0] = jnp.dot(h, win_ref[:, G_OFF:IN_W], preferred_element_type=F32)


def _mixin(x, mod_all, g_pre, win, q_norm, wuq, kv_norm, wukv, cos4, sin4, *, layer, n_ctx_tiles):
    bsz, t_all, d = x.shape
    rope_w = ATTN_HEADS * ROPE_DIM
    tile_map = lambda b, t: (b, t, 0)
    head_map = lambda b, t: (b, 0, t, 0)
    out_shape = (
        jax.ShapeDtypeStruct((bsz, ATTN_HEADS, t_all, QK_DIM), BF16),
        jax.ShapeDtypeStruct((bsz, ATTN_HEADS, t_all, QK_DIM), BF16),
        jax.ShapeDtypeStruct((bsz, ATTN_HEADS, V_DIM, t_all), BF16),
        jax.ShapeDtypeStruct((bsz, t_all, 2 * MLSTM_WIDTH), F32),
        jax.ShapeDtypeStruct((bsz, t_all, MLSTM_WIDTH), BF16),
        jax.ShapeDtypeStruct((bsz, t_all, MLSTM_WIDTH), F32),
        jax.ShapeDtypeStruct((bsz, t_all, LANES), F32),
    )
    return pl.pallas_call(
        _mixin_kernel,
        grid=(bsz, t_all // TILE),
        in_specs=[
            pl.BlockSpec((1, TILE, d), tile_map),
            _mod_spec(d, 0, n_ctx_tiles),
            pl.BlockSpec((None, None, 1, d), lambda b, t: (layer, 1, 0, 0)),
            _const_spec((None, d, IN_W), lambda b, t: (layer, 0, 0)),
            pl.BlockSpec((None, 1, Q_LORA), lambda b, t: (layer, 0, 0)),
            _const_spec((None, Q_LORA, wuq.shape[-1]), lambda b, t: (layer, 0, 0)),
            pl.BlockSpec((None, 1, KV_LORA), lambda b, t: (layer, 0, 0)),
            _const_spec((None, KV_LORA, wukv.shape[-1]), lambda b, t: (layer, 0, 0)),
            pl.BlockSpec((TILE, rope_w), lambda b, t: (t, 0)),
            pl.BlockSpec((TILE, rope_w), lambda b, t: (t, 0)),
        ],
        out_specs=(
            pl.BlockSpec((1, ATTN_HEADS, TILE, QK_DIM), head_map),
            pl.BlockSpec((1, ATTN_HEADS, TILE, QK_DIM), head_map),
            pl.BlockSpec((1, ATTN_HEADS, V_DIM, TILE), lambda b, t: (b, 0, 0, t)),
            pl.BlockSpec((1, TILE, 2 * MLSTM_WIDTH), tile_map),
            pl.BlockSpec((1, TILE, MLSTM_WIDTH), tile_map),
            pl.BlockSpec((1, TILE, MLSTM_WIDTH), tile_map),
            pl.BlockSpec((1, TILE, LANES), tile_map),
        ),
        out_shape=out_shape,
        compiler_params=_params(2),
        name=f"mixin_l{layer}",
    )(x, mod_all, g_pre, win, q_norm, wuq, kv_norm, wukv, cos4, sin4)


def _attend(q, k_ref, vt_ref, chunks):
    m = jnp.full((1, TILE), -jnp.inf, F32)
    l = jnp.zeros((1, TILE), F32)
    acc = jnp.zeros((V_DIM, TILE), F32)
    for lo, hi in chunks:
        s = lax.dot_general(k_ref[0, 0, lo:hi, :], q, (((1,), (1,)), ((), ())), preferred_element_type=F32)
        m_new = jnp.maximum(m, jnp.max(s, axis=0, keepdims=True))
        p = jnp.exp(s - m_new)
        alpha = jnp.exp(m - m_new)
        l = alpha * l + jnp.sum(p, axis=0, keepdims=True)
        acc = alpha * acc + jnp.dot(vt_ref[0, 0, :, lo:hi], p.astype(BF16), preferred_element_type=F32)
        m = m_new
    return (acc / l).T


def _attn_kernel(q_ref, k_ref, vt_ref, o_ref, *, t0, n_ctx_tiles, ctx_chunks, all_chunks):
    q = q_ref[0, 0]
    if t0 >= n_ctx_tiles:
        o_ref[0] = _attend(q, k_ref, vt_ref, all_chunks)
    else:
        is_ctx = pl.program_id(2) + t0 < n_ctx_tiles

        @pl.when(is_ctx)
        def _():
            o_ref[0] = _attend(q, k_ref, vt_ref, ctx_chunks)

        @pl.when(jnp.logical_not(is_ctx))
        def _():
            o_ref[0] = _attend(q, k_ref, vt_ref, all_chunks)


def _key_chunks(n_keys, width):
    return tuple((lo, min(lo + width, n_keys)) for lo in range(0, n_keys, width))


def _attention(q, k, vt, *, n_ctx_tiles, t0, layer):
    bsz, n_heads, t_all, _ = q.shape
    n_q = t_all // TILE - t0
    kernel = functools.partial(
        _attn_kernel, t0=t0, n_ctx_tiles=n_ctx_tiles,
        ctx_chunks=_key_chunks(n_ctx_tiles * TILE, 2 * TILE), all_chunks=_key_chunks(t_all, 2 * TILE))
    return pl.pallas_call(
        kernel,
        grid=(bsz, n_heads, n_q),
        in_specs=[
            pl.BlockSpec((1, 1, TILE, QK_DIM), lambda b, h, i: (b, h, i + t0, 0)),
            pl.BlockSpec((1, 1, t_all, QK_DIM), lambda b, h, i: (b, h, 0, 0)),
            pl.BlockSpec((1, 1, V_DIM, t_all), lambda b, h, i: (b, h, 0, 0)),
        ],
        out_specs=pl.BlockSpec((1, TILE, V_DIM), lambda b, h, i: (b, i, h)),
        out_shape=jax.ShapeDtypeStruct((bsz, n_q * TILE, n_heads * V_DIM), F32),
        compiler_params=_params(3),
        name=f"attention_l{layer}",
    )(q, k, vt)


def _log_sigmoid(x):
    return jnp.minimum(x, 0.0) - jnp.log1p(jnp.exp(-jnp.abs(x)))


def _chunk_index(d, j, n_ctx_tiles, n_tiles):
    bwd = jnp.where(j < n_ctx_tiles, n_ctx_tiles - 1 - j, n_tiles - 1 - (j - n_ctx_tiles))
    return jnp.where(d == 0, j, bwd)


def _mlstm_kernel(qk_ref, prev_ref, next_ref, v_ref, g_ref, cw_ref, cb_ref, gb_ref, o_ref,
                  c_scr, n_scr, m_scr, *, n_ctx_tiles, n_tiles):
    d = pl.program_id(1)
    j = pl.program_id(2)
    ci = _chunk_index(d, j, n_ctx_tiles, n_tiles)
    is_fwd = d == 0

    @pl.when(j == 0)
    def _():
        c_scr[...] = jnp.zeros_like(c_scr)
        n_scr[...] = jnp.zeros_like(n_scr)
        m_scr[...] = jnp.zeros_like(m_scr)

    x = qk_ref[0]
    row = lax.broadcasted_iota(jnp.int32, (TILE, 1), 0)
    has_prev = jnp.logical_and(ci != 0, ci != n_ctx_tiles)
    has_next = jnp.logical_and(ci != n_ctx_tiles - 1, ci != n_tiles - 1)
    prev_row = jnp.where(has_prev, prev_ref[0, SUBLANES - 1:SUBLANES, :], 0.0)
    next_row = jnp.where(has_next, next_ref[0, 0:1, :], 0.0)
    x_prev = jnp.where(row == 0, prev_row, pltpu.roll(x, 1, axis=0))
    x_next = jnp.where(row == TILE - 1, next_row, pltpu.roll(x, TILE - 1, axis=0))
    u = x_prev * cw_ref[0:1, :] + x * cw_ref[1:2, :] + x_next * cw_ref[2:3, :] + cb_ref[...]
    u = u * jax.nn.sigmoid(u)
    q_all = u[:, 0:MLSTM_WIDTH]
    k_all = u[:, MLSTM_WIDTH:2 * MLSTM_WIDTH] * (MLSTM_DH ** -0.5)
    v_all = v_ref[0]

    nh = MLSTM_HEADS
    g = g_ref[0] + gb_ref[...]
    g_t = g.T
    t_idx = lax.broadcasted_iota(jnp.int32, (TILE, TILE), 0)
    s_idx = lax.broadcasted_iota(jnp.int32, (TILE, TILE), 1)
    seen = (s_idx - t_idx) * jnp.where(is_fwd, 1, -1) <= 0
    tri = seen.astype(F32)
    cum_c = jnp.dot(tri, _log_sigmoid(g), preferred_element_type=F32, precision=lax.Precision.HIGHEST)
    cum_r = lax.dot_general(_log_sigmoid(g_t), tri, (((1,), (1,)), ((), ())), preferred_element_type=F32,
                            precision=lax.Precision.HIGHEST)
    ig_c = jnp.where(is_fwd, g[:, 0:nh], g[:, 2 * nh:3 * nh])
    b_c = jnp.where(is_fwd, cum_c[:, nh:2 * nh], cum_c[:, 3 * nh:4 * nh])
    ig_r = jnp.where(is_fwd, g_t[0:nh, :], g_t[2 * nh:3 * nh, :])
    b_r = jnp.where(is_fwd, cum_r[nh:2 * nh, :], cum_r[3 * nh:4 * nh, :])
    b_end = jnp.where(is_fwd, b_c[TILE - 1:TILE, :], b_c[0:1, :])

    for hd in range(nh):
        sl = slice(hd * MLSTM_DH, (hd + 1) * MLSTM_DH)
        q_h = q_all[:, sl]
        k_h = k_all[:, sl]
        v_h = v_all[:, sl]
        q_b = q_h.astype(BF16)
        c_old = c_scr[hd]
        n_old = n_scr[hd]
        m_old = m_scr[hd][:, 0:1]
        bt = b_c[:, hd:hd + 1]
        b_last = b_end[:, hd:hd + 1]

        log_d = jnp.where(seen, bt - b_r[hd:hd + 1, :] + ig_r[hd:hd + 1, :], -jnp.inf)
        log_inter = bt + m_old
        m_t = jnp.maximum(log_inter, jnp.max(log_d, axis=1, keepdims=True))
        s = lax.dot_general(q_b, k_h.astype(BF16), (((1,), (1,)), ((), ())), preferred_element_type=F32)
        s = s * jnp.exp(log_d - m_t)
        inter = jnp.exp(log_inter - m_t)
        num = jnp.dot(s.astype(BF16), v_h, preferred_element_type=F32)
        num = num + inter * jnp.dot(q_b, c_old.astype(BF16), preferred_element_type=F32)
        den = jnp.sum(s, axis=1, keepdims=True) + inter * jnp.sum(q_h * n_old, axis=1, keepdims=True)
        o_ref[0, 0, :, sl] = num / jnp.maximum(jnp.abs(den), jnp.exp(-m_t))

        log_w = b_last - bt + ig_c[:, hd:hd + 1]
        m_new = jnp.maximum(b_last + m_old, jnp.max(log_w, axis=0, keepdims=True))
        kw = k_h * jnp.exp(log_w - m_new)
        decay = jnp.exp(b_last + m_old - m_new)
        c_scr[hd] = decay * c_old + jnp.dot(kw.T.astype(BF16), v_h, preferred_element_type=F32)
        n_scr[hd] = decay * n_old + jnp.sum(kw, axis=0, keepdims=True)
        m_scr[hd] = jnp.broadcast_to(m_new, (1, LANES))


def _mlstm(qk_pre, v_m, gates, conv_w, conv_b, gate_b, *, n_ctx_tiles, layer):
    bsz, t_all, w2 = qk_pre.shape
    n_tiles = t_all // TILE
    halo_per_tile = TILE // SUBLANES
    n_halo = t_all // SUBLANES
    ci_of = functools.partial(_chunk_index, n_ctx_tiles=n_ctx_tiles, n_tiles=n_tiles)
    tile_map = lambda b, d, j: (b, ci_of(d, j), 0)
    prev_map = lambda b, d, j: (b, jnp.maximum(ci_of(d, j) * halo_per_tile - 1, 0), 0)
    next_map = lambda b, d, j: (b, jnp.minimum((ci_of(d, j) + 1) * halo_per_tile, n_halo - 1), 0)
    return pl.pallas_call(
        functools.partial(_mlstm_kernel, n_ctx_tiles=n_ctx_tiles, n_tiles=n_tiles),
        grid=(bsz, 2, n_tiles),
        in_specs=[
            pl.BlockSpec((1, TILE, w2), tile_map),
            pl.BlockSpec((1, SUBLANES, w2), prev_map),
            pl.BlockSpec((1, SUBLANES, w2), next_map),
            pl.BlockSpec((1, TILE, MLSTM_WIDTH), tile_map),
            pl.BlockSpec((1, TILE, LANES), tile_map),
            pl.BlockSpec((None, CONV_K, w2), lambda b, d, j: (layer, 0, 0)),
            pl.BlockSpec((None, 1, w2), lambda b, d, j: (layer, 0, 0)),
            pl.BlockSpec((None, 1, LANES), lambda b, d, j: (layer, 0, 0)),
        ],
        out_specs=pl.BlockSpec((1, 1, TILE, MLSTM_WIDTH), lambda b, d, j: (b, d, ci_of(d, j), 0)),
        out_shape=jax.ShapeDtypeStruct((bsz, 2, t_all, MLSTM_WIDTH), F32),
        scratch_shapes=[pltpu.VMEM((MLSTM_HEADS, MLSTM_DH, MLSTM_DH), F32),
                        pltpu.VMEM((MLSTM_HEADS, 1, MLSTM_DH), F32),
                        pltpu.VMEM((MLSTM_HEADS, 1, LANES), F32)],
        compiler_params=_params(3),
        name=f"mlstm_l{layer}",
    )(qk_pre, qk_pre, qk_pre, v_m, gates, conv_w, conv_b, gate_b)


def _merge_kernel(x_ref, a_ref, hm_ref, o_ref, mod_ref, ga_ref, gm_ref, gpost_ref, wout_ref, out_ref):
    x = x_ref[0]
    mod = mod_ref[0, 0]
    hm = (hm_ref[0, 0] + hm_ref[0, 1]) * jax.nn.sigmoid(o_ref[0])
    normed = []
    for hd in range(MLSTM_HEADS):
        seg = hm[:, hd * MLSTM_DH:(hd + 1) * MLSTM_DH]
        cen = seg - jnp.mean(seg, axis=-1, keepdims=True)
        normed.append(cen * lax.rsqrt(jnp.mean(cen * cen, axis=-1, keepdims=True) + EPS))
    hm_n = (jnp.concatenate(normed, axis=-1) * gm_ref[...]).astype(BF16)
    a_n = _rms(a_ref[0], ga_ref[...]).astype(BF16)
    y = jnp.dot(a_n, wout_ref[0:ATTN_WIDTH, :], preferred_element_type=F32)
    y = y + jnp.dot(hm_n, wout_ref[ATTN_WIDTH:ATTN_WIDTH + MLSTM_WIDTH, :], preferred_element_type=F32)
    out_ref[0] = x + mod[5:6] * _rms(y, gpost_ref[...])


def _merge(x, a, hm, o, mod_all, g_attn, g_mlstm, g_post, wout, *, layer, n_ctx_tiles, t0):
    bsz, t_all, d = x.shape
    n_tiles = t_all // TILE - t0
    return pl.pallas_call(
        _merge_kernel,
        grid=(bsz, n_tiles),
        in_specs=[
            pl.BlockSpec((1, TILE, d), lambda b, t: (b, t + t0, 0)),
            pl.BlockSpec((1, TILE, ATTN_WIDTH), lambda b, t: (b, t, 0)),
            pl.BlockSpec((1, 2, TILE, MLSTM_WIDTH), lambda b, t: (b, 0, t + t0, 0)),
            pl.BlockSpec((1, TILE, MLSTM_WIDTH), lambda b, t: (b, t + t0, 0)),
            _mod_spec(d, t0, n_ctx_tiles),
            pl.BlockSpec((None, 1, ATTN_WIDTH), lambda b, t: (layer, 0, 0)),
            pl.BlockSpec((None, 1, MLSTM_WIDTH), lambda b, t: (layer, 0, 0)),
            pl.BlockSpec((None, None, 1, d), lambda b, t: (layer, 1, 0, 0)),
            _const_spec((None, ATTN_WIDTH + MLSTM_WIDTH, d), lambda b, t: (layer, 0, 0)),
        ],
        out_specs=pl.BlockSpec((1, TILE, d), lambda b, t: (b, t, 0)),
        out_shape=jax.ShapeDtypeStruct((bsz, n_tiles * TILE, d), F32),
        compiler_params=_params(2),
        name=f"merge_l{layer}",
    )(x, a, hm, o, mod_all, g_attn, g_mlstm, g_post, wout)


def _half_swap_perm():
    idx = np.arange(ROPE_DIM)
    axis, half, freq = idx // (2 * AXIS_FREQS), (idx // AXIS_FREQS) % 2, idx % AXIS_FREQS
    return axis * 2 * AXIS_FREQS + (1 - half) * AXIS_FREQS + freq


def _prep_w_in(w_in):
    sizes = (Q_LORA, KV_LORA, ROPE_DIM, 2 * MLSTM_WIDTH, MLSTM_WIDTH, MLSTM_WIDTH, N_GATES)
    offs = np.concatenate([[0], np.cumsum(sizes)])
    cq, ckv, kr, qk, v, o, g = (w_in[..., offs[i]:offs[i + 1]] for i in range(len(sizes)))
    g = jnp.pad(g, ((0, 0), (0, 0), (0, LANES - N_GATES)))
    return jnp.concatenate([cq, ckv, kr, kr[..., _half_swap_perm()], qk, v, o, g], axis=-1).astype(BF16)


def _prep_w_uq(w_uq):
    n_layers, q_lora, _ = w_uq.shape
    w = w_uq.reshape(n_layers, q_lora, ATTN_HEADS, QK_DIM)
    nope = w[..., :NOPE_DIM].reshape(n_layers, q_lora, -1)
    rope = w[..., NOPE_DIM:]
    rope_sw = rope[..., _half_swap_perm()]
    return jnp.concatenate([nope, rope.reshape(n_layers, q_lora, -1), rope_sw.reshape(n_layers, q_lora, -1)],
                           axis=-1).astype(BF16)


def _prep_w_ukv(w_ukv):
    n_layers, kv_lora, _ = w_ukv.shape
    w = w_ukv.reshape(n_layers, kv_lora, ATTN_HEADS, NOPE_DIM + V_DIM)
    return jnp.concatenate([w[..., :NOPE_DIM].reshape(n_layers, kv_lora, -1),
                            w[..., NOPE_DIM:].reshape(n_layers, kv_lora, -1)], axis=-1).astype(BF16)


def _rope_tables(n_ctx, n_tok):
    rows = n_tok // GRID_W
    t_row = jnp.repeat(jnp.arange(rows), GRID_W).astype(F32)
    t_col = jnp.tile(jnp.arange(GRID_W), rows).astype(F32)
    inv = ROPE_BASE ** (-jnp.arange(AXIS_FREQS, dtype=F32) / AXIS_FREQS)
    ang_r = t_row[:, None] * inv
    ang_c = t_col[:, None] * inv
    cos = jnp.concatenate([jnp.cos(ang_r), jnp.cos(ang_r), jnp.cos(ang_c), jnp.cos(ang_c)], axis=-1)
    sin = jnp.concatenate([-jnp.sin(ang_r), jnp.sin(ang_r), -jnp.sin(ang_c), jnp.sin(ang_c)], axis=-1)
    cos = jnp.concatenate([jnp.ones((n_ctx, ROPE_DIM), F32), cos], axis=0)
    sin = jnp.concatenate([jnp.zeros((n_ctx, ROPE_DIM), F32), sin], axis=0)
    return jnp.tile(cos, (1, ATTN_HEADS)), jnp.tile(sin, (1, ATTN_HEADS))


def kernel(x, c, ctx, c_ctx, w_mod, b_mod, norm_pre, norm_post, ffn_w_gate, ffn_w_up, ffn_w_down,
           w_in, q_norm, w_uq, kv_norm, w_ukv, attn_out_norm, conv_w, conv_b, gate_b, mlstm_norm, w_out):
    bsz, n_tok, d = x.shape
    n_ctx = ctx.shape[1]
    n_layers = w_mod.shape[0]
    assert n_tok % TILE == 0 and n_ctx % TILE == 0 and n_tok % GRID_W == 0
    n_ctx_tiles = n_ctx // TILE

    ff = ffn_w_gate.shape[-1]
    ff_pad = _round_up(ff, MXU_DIM) - ff
    wg = jnp.pad(ffn_w_gate, ((0, 0), (0, 0), (0, 0), (0, ff_pad))).astype(BF16)
    wu = jnp.pad(ffn_w_up, ((0, 0), (0, 0), (0, 0), (0, ff_pad))).astype(BF16)
    wd = jnp.pad(ffn_w_down, ((0, 0), (0, 0), (0, ff_pad), (0, 0))).astype(BF16)
    win = _prep_w_in(w_in)
    wuq = _prep_w_uq(w_uq)
    wukv = _prep_w_ukv(w_ukv)
    wout = w_out.astype(BF16)
    g_pre = norm_pre[:, :, None, :]
    g_post = norm_post[:, :, None, :]
    q_gain = q_norm[:, None, :]
    kv_gain = kv_norm[:, None, :]
    a_gain = attn_out_norm[:, None, :]
    m_gain = mlstm_norm[:, None, :]
    conv_bias = conv_b[:, None, :]
    gate_bias = jnp.pad(gate_b.reshape(n_layers, 1, N_GATES), ((0, 0), (0, 0), (0, LANES - N_GATES)))
    cos4, sin4 = _rope_tables(n_ctx, n_tok)

    c_rows = jnp.zeros((_round_up(bsz + 1, SUBLANES), d), F32).at[:bsz].set(c).at[bsz].set(c_ctx)
    mod = _modulation(c_rows, w_mod, b_mod).reshape(n_layers, -1, N_MOD, d)
    mod_all = jnp.stack([jnp.broadcast_to(mod[:, bsz:bsz + 1], (n_layers, bsz, N_MOD, d)), mod[:, :bsz]], axis=2)

    h = jnp.concatenate([ctx, x], axis=1)
    for l in range(n_layers):
        t0 = n_ctx_tiles if l == n_layers - 1 else 0
        ml = mod_all[l]
        h = _ffn(h, ml, g_pre, g_post, wg, wu, wd, layer=l, j=0, sub=0, n_ctx_tiles=n_ctx_tiles, mod_t0=0)
        q, k, vt, qk_pre, v_m, o_pre, gates = _mixin(h, ml, g_pre, win, q_gain, wuq, kv_gain, wukv,
                                                     cos4, sin4, layer=l, n_ctx_tiles=n_ctx_tiles)
        a = _attention(q, k, vt, n_ctx_tiles=n_ctx_tiles, t0=t0, layer=l)
        hm = _mlstm(qk_pre, v_m, gates, conv_w, conv_bias, gate_bias, n_ctx_tiles=n_ctx_tiles, layer=l)
        h = _merge(h, a, hm, o_pre, ml, a_gain, m_gain, g_post, wout, layer=l, n_ctx_tiles=n_ctx_tiles, t0=t0)
        h = _ffn(h, ml, g_pre, g_post, wg, wu, wd, layer=l, j=1, sub=2, n_ctx_tiles=n_ctx_tiles, mod_t0=t0)
    return h
```

```python
import functools

import jax
import jax.numpy as jnp
import numpy as np
from jax import lax
from jax.experimental import pallas as pl
from jax.experimental.pallas import tpu as pltpu

F32 = jnp.float32
BF16 = jnp.bfloat16

N_MOD = 9
EPS = 1e-6
GRID_W = 64
ATTN_HEADS = 4
Q_LORA = 256
KV_LORA = 128
NOPE_DIM = 128
ROPE_DIM = 64
AXIS_FREQS = ROPE_DIM // 4
V_DIM = 128
QK_DIM = NOPE_DIM + ROPE_DIM
ATTN_WIDTH = ATTN_HEADS * V_DIM
ROPE_BASE = 10000.0
SM_SCALE = QK_DIM ** -0.5
Q_SCALE = SM_SCALE * float(np.log2(np.e))
MLSTM_HEADS = 4
MLSTM_DH = 128
MLSTM_WIDTH = MLSTM_HEADS * MLSTM_DH
CONV_K = 3
N_GATES = 4 * MLSTM_HEADS

LANES = 128
SUBLANES = 8
MXU_DIM = 256
TILE = MXU_DIM
VMEM_LIMIT = 56 * 1024 * 1024

A_W = Q_LORA + KV_LORA + 2 * ROPE_DIM
QK_OFF = A_W
V_OFF = QK_OFF + 2 * MLSTM_WIDTH
O_OFF = V_OFF + MLSTM_WIDTH
G_OFF = O_OFF + MLSTM_WIDTH
IN_W = G_OFF + LANES


def _round_up(n, m):
    return (n + m - 1) // m * m


def _rms(x, g):
    return x * lax.rsqrt(jnp.mean(x * x, axis=-1, keepdims=True) + EPS) * g


def _sandwich_in(x, mod, sub, g):
    return _rms(x, g) * (1.0 + mod[3 * sub + 1:3 * sub + 2]) + mod[3 * sub:3 * sub + 1]


def _params(n_grid):
    return pltpu.CompilerParams(dimension_semantics=("arbitrary",) * n_grid,
                                vmem_limit_bytes=VMEM_LIMIT)


def _const_spec(shape, index_map):
    return pl.BlockSpec(shape, index_map, pipeline_mode=pl.Buffered(1))


def _mod_spec(d, mod_t0, n_ctx_tiles):
    return pl.BlockSpec((1, 1, N_MOD, d),
                        lambda b, t: (b, (t + mod_t0 >= n_ctx_tiles).astype(jnp.int32), 0, 0))


def _mod_kernel(c_ref, w_ref, b_ref, o_ref):
    c = c_ref[...]
    h = (c * jax.nn.sigmoid(c)).astype(BF16)
    o_ref[0] = jnp.dot(h, w_ref[0].astype(BF16), preferred_element_type=F32) + b_ref[0]


def _modulation(c_rows, w_mod, b_mod):
    n_layers, d, n_out = w_mod.shape
    rows = c_rows.shape[0]
    n_blk = N_MOD * LANES
    return pl.pallas_call(
        _mod_kernel,
        grid=(n_layers, n_out // n_blk),
        in_specs=[pl.BlockSpec((rows, d), lambda l, j: (0, 0)),
                  pl.BlockSpec((1, d, n_blk), lambda l, j: (l, 0, j)),
                  pl.BlockSpec((1, 1, n_blk), lambda l, j: (l, 0, j))],
        out_specs=pl.BlockSpec((1, rows, n_blk), lambda l, j: (l, 0, j)),
        out_shape=jax.ShapeDtypeStruct((n_layers, rows, n_out), F32),
        compiler_params=_params(2),
        name="modulation",
    )(c_rows, w_mod, b_mod.reshape(n_layers, 1, n_out))


def _ffn_kernel(x_ref, mod_ref, gpre_ref, gpost_ref, wg_ref, wu_ref, wd_ref, o_ref, a_ref, *, sub):
    x = x_ref[0]
    mod = mod_ref[0, 0]
    h = _sandwich_in(x, mod, sub, gpre_ref[...]).astype(BF16)
    ff = a_ref.shape[1]
    for lo in range(0, ff, MXU_DIM):
        g = jnp.dot(h, wg_ref[:, lo:lo + MXU_DIM], preferred_element_type=F32)
        u = jnp.dot(h, wu_ref[:, lo:lo + MXU_DIM], preferred_element_type=F32)
        a_ref[:, lo:lo + MXU_DIM] = (g * jax.nn.sigmoid(g) * u).astype(BF16)
    y = jnp.dot(a_ref[...], wd_ref[...], preferred_element_type=F32)
    o_ref[0] = x + 0.5 * mod[3 * sub + 2:3 * sub + 3] * _rms(y, gpost_ref[...])


def _ffn(x, mod_all, g_pre, g_post, wg, wu, wd, *, layer, j, sub, n_ctx_tiles, mod_t0):
    bsz, t_in, d = x.shape
    ff = wg.shape[-1]
    return pl.pallas_call(
        functools.partial(_ffn_kernel, sub=sub),
        grid=(bsz, t_in // TILE),
        in_specs=[
            pl.BlockSpec((1, TILE, d), lambda b, t: (b, t, 0)),
            _mod_spec(d, mod_t0, n_ctx_tiles),
            pl.BlockSpec((None, None, 1, d), lambda b, t: (layer, sub, 0, 0)),
            pl.BlockSpec((None, None, 1, d), lambda b, t: (layer, sub, 0, 0)),
            _const_spec((None, None, d, ff), lambda b, t: (layer, j, 0, 0)),
            _const_spec((None, None, d, ff), lambda b, t: (layer, j, 0, 0)),
            _const_spec((None, None, ff, d), lambda b, t: (layer, j, 0, 0)),
        ],
        out_specs=pl.BlockSpec((1, TILE, d), lambda b, t: (b, t, 0)),
        out_shape=jax.ShapeDtypeStruct((bsz, t_in, d), F32),
        scratch_shapes=[pltpu.VMEM((TILE, ff), BF16)],
        compiler_params=_params(2),
        name=f"ffn_l{layer}_{j}",
    )(x, mod_all, g_pre, g_post, wg, wu, wd)


def _mixin_kernel(x_ref, mod_ref, gpre_ref, win_ref, qn_ref, wuq_ref, kvn_ref, wukv_ref, cos_ref, sin_ref,
                  q_ref, k_ref, vt_ref, qk_ref, vm_ref, o_ref, g_ref):
    h = _sandwich_in(x_ref[0], mod_ref[0, 0], 1, gpre_ref[...]).astype(BF16)
    cos4 = cos_ref[...]
    sin4 = sin_ref[...]

    pa = jnp.dot(h, win_ref[:, 0:A_W], preferred_element_type=F32)
    cq = pa[:, 0:Q_LORA]
    ckv = pa[:, Q_LORA:Q_LORA + KV_LORA]
    kr = pa[:, Q_LORA + KV_LORA:Q_LORA + KV_LORA + ROPE_DIM]
    kr_sw = pa[:, Q_LORA + KV_LORA + ROPE_DIM:A_W]
    k_rope = (kr * cos4[:, 0:ROPE_DIM] + kr_sw * sin4[:, 0:ROPE_DIM]).astype(BF16)

    nope_w = ATTN_HEADS * NOPE_DIM
    rope_w = ATTN_HEADS * ROPE_DIM
    qa = jnp.dot(_rms(cq, qn_ref[...]).astype(BF16), wuq_ref[...], preferred_element_type=F32)
    q_rope = qa[:, nope_w:nope_w + rope_w] * cos4 + qa[:, nope_w + rope_w:nope_w + 2 * rope_w] * sin4
    kva = jnp.dot(_rms(ckv, kvn_ref[...]).astype(BF16), wukv_ref[...], preferred_element_type=F32)
    for hd in range(ATTN_HEADS):
        q_ref[0, hd, :, 0:NOPE_DIM] = (qa[:, hd * NOPE_DIM:(hd + 1) * NOPE_DIM] * Q_SCALE).astype(BF16)
        q_ref[0, hd, :, NOPE_DIM:QK_DIM] = (q_rope[:, hd * ROPE_DIM:(hd + 1) * ROPE_DIM] * Q_SCALE).astype(BF16)
        k_ref[0, hd, :, 0:NOPE_DIM] = kva[:, hd * NOPE_DIM:(hd + 1) * NOPE_DIM].astype(BF16)
        k_ref[0, hd, :, NOPE_DIM:QK_DIM] = k_rope
        vt_ref[0, hd] = kva[:, nope_w + hd * V_DIM:nope_w + (hd + 1) * V_DIM].T.astype(BF16)

    qk_ref[0] = jnp.dot(h, win_ref[:, QK_OFF:V_OFF], preferred_element_type=F32)
    vm_ref[0] = jnp.dot(h, win_ref[:, V_OFF:O_OFF], preferred_element_type=F32).astype(BF16)
    o_ref[0] = jnp.dot(h, win_ref[:, O_OFF:G_OFF], preferred_element_type=F32)
    g_ref[0] = jnp.dot(h, win_ref[:, G_OFF:IN_W], preferred_element_type=F32)


def _mixin(x, mod_all, g_pre, win, q_norm, wuq, kv_norm, wukv, cos4, sin4, *, layer, n_ctx_tiles):
    bsz, t_all, d = x.shape
    rope_w = ATTN_HEADS * ROPE_DIM
    tile_map = lambda b, t: (b, t, 0)
    head_map = lambda b, t: (b, 0, t, 0)
    out_shape = (
        jax.ShapeDtypeStruct((bsz, ATTN_HEADS, t_all, QK_DIM), BF16),
        jax.ShapeDtypeStruct((bsz, ATTN_HEADS, t_all, QK_DIM), BF16),
        jax.ShapeDtypeStruct((bsz, ATTN_HEADS, V_DIM, t_all), BF16),
        jax.ShapeDtypeStruct((bsz, t_all, 2 * MLSTM_WIDTH), F32),
        jax.ShapeDtypeStruct((bsz, t_all, MLSTM_WIDTH), BF16),
        jax.ShapeDtypeStruct((bsz, t_all, MLSTM_WIDTH), F32),
        jax.ShapeDtypeStruct((bsz, t_all, LANES), F32),
    )
    return pl.pallas_call(
        _mixin_kernel,
        grid=(bsz, t_all // TILE),
        in_specs=[
            pl.BlockSpec((1, TILE, d), tile_map),
            _mod_spec(d, 0, n_ctx_tiles),
            pl.BlockSpec((None, None, 1, d), lambda b, t: (layer, 1, 0, 0)),
            _const_spec((None, d, IN_W), lambda b, t: (layer, 0, 0)),
            pl.BlockSpec((None, 1, Q_LORA), lambda b, t: (layer, 0, 0)),
            _const_spec((None, Q_LORA, wuq.shape[-1]), lambda b, t: (layer, 0, 0)),
            pl.BlockSpec((None, 1, KV_LORA), lambda b, t: (layer, 0, 0)),
            _const_spec((None, KV_LORA, wukv.shape[-1]), lambda b, t: (layer, 0, 0)),
            pl.BlockSpec((TILE, rope_w), lambda b, t: (t, 0)),
            pl.BlockSpec((TILE, rope_w), lambda b, t: (t, 0)),
        ],
        out_specs=(
            pl.BlockSpec((1, ATTN_HEADS, TILE, QK_DIM), head_map),
            pl.BlockSpec((1, ATTN_HEADS, TILE, QK_DIM), head_map),
            pl.BlockSpec((1, ATTN_HEADS, V_DIM, TILE), lambda b, t: (b, 0, 0, t)),
            pl.BlockSpec((1, TILE, 2 * MLSTM_WIDTH), tile_map),
            pl.BlockSpec((1, TILE, MLSTM_WIDTH), tile_map),
            pl.BlockSpec((1, TILE, MLSTM_WIDTH), tile_map),
            pl.BlockSpec((1, TILE, LANES), tile_map),
        ),
        out_shape=out_shape,
        compiler_params=_params(2),
        name=f"mixin_l{layer}",
    )(x, mod_all, g_pre, win, q_norm, wuq, kv_norm, wukv, cos4, sin4)


ATTN_HEADS_PER_STEP = 2
KEY_CHUNK = 2 * TILE


def _attn_scores(q, k_ref, s_scr, hh, chunks):
    m8 = jnp.full((SUBLANES, TILE), -jnp.inf, F32)
    for lo, hi in chunks:
        s = lax.dot_general(k_ref[0, hh, lo:hi, :], q, (((1,), (1,)), ((), ())), preferred_element_type=F32)
        s_scr[hh, lo:hi, :] = s
        m8 = jnp.maximum(m8, jnp.max(s.reshape(-1, SUBLANES, TILE), axis=0))
    return m8


def _attn_probs(m8, s_scr, p_scr, hh, chunks):
    m = jnp.max(m8, axis=0, keepdims=True)
    l8 = jnp.zeros((SUBLANES, TILE), F32)
    for lo, hi in chunks:
        p = jnp.exp2(s_scr[hh, lo:hi, :] - m)
        l8 = l8 + jnp.sum(p.reshape(-1, SUBLANES, TILE), axis=0)
        p_scr[hh, lo:hi, :] = p.astype(BF16)
    return jnp.sum(l8, axis=0, keepdims=True)


def _attn_heads(q_ref, k_ref, vt_ref, o_ref, s_scr, p_scr, chunks):
    lo, hi = chunks[0][0], chunks[-1][1]
    heads = range(ATTN_HEADS_PER_STEP)
    m8 = [_attn_scores(q_ref[0, hh], k_ref, s_scr, hh, chunks) for hh in heads]
    for hh in heads:
        l = _attn_probs(m8[hh], s_scr, p_scr, hh, chunks)
        acc = jnp.dot(vt_ref[0, hh, :, lo:hi], p_scr[hh, lo:hi, :], preferred_element_type=F32)
        o_ref[0, :, hh * V_DIM:(hh + 1) * V_DIM] = (acc / l).T


def _attn_kernel(q_ref, k_ref, vt_ref, o_ref, s_scr, p_scr, *, t0, n_ctx_tiles, ctx_chunks, all_chunks):
    if t0 >= n_ctx_tiles:
        _attn_heads(q_ref, k_ref, vt_ref, o_ref, s_scr, p_scr, all_chunks)
    else:
        is_ctx = pl.program_id(2) + t0 < n_ctx_tiles

        @pl.when(is_ctx)
        def _():
            _attn_heads(q_ref, k_ref, vt_ref, o_ref, s_scr, p_scr, ctx_chunks)

        @pl.when(jnp.logical_not(is_ctx))
        def _():
            _attn_heads(q_ref, k_ref, vt_ref, o_ref, s_scr, p_scr, all_chunks)


def _key_chunks(n_keys, width):
    return tuple((lo, min(lo + width, n_keys)) for lo in range(0, n_keys, width))


def _attention(q, k, vt, *, n_ctx_tiles, t0, layer):
    bsz, n_heads, t_all, _ = q.shape
    n_q = t_all // TILE - t0
    hps = ATTN_HEADS_PER_STEP
    kernel = functools.partial(
        _attn_kernel, t0=t0, n_ctx_tiles=n_ctx_tiles,
        ctx_chunks=_key_chunks(n_ctx_tiles * TILE, KEY_CHUNK), all_chunks=_key_chunks(t_all, KEY_CHUNK))
    return pl.pallas_call(
        kernel,
        grid=(bsz, n_heads // hps, n_q),
        in_specs=[
            pl.BlockSpec((1, hps, TILE, QK_DIM), lambda b, h, i: (b, h, i + t0, 0)),
            pl.BlockSpec((1, hps, t_all, QK_DIM), lambda b, h, i: (b, h, 0, 0)),
            pl.BlockSpec((1, hps, V_DIM, t_all), lambda b, h, i: (b, h, 0, 0)),
        ],
        out_specs=pl.BlockSpec((1, TILE, hps * V_DIM), lambda b, h, i: (b, i, h)),
        out_shape=jax.ShapeDtypeStruct((bsz, n_q * TILE, n_heads * V_DIM), F32),
        scratch_shapes=[pltpu.VMEM((hps, t_all, TILE), F32), pltpu.VMEM((hps, t_all, TILE), BF16)],
        compiler_params=_params(3),
        name=f"attention_l{layer}",
    )(q, k, vt)


def _log_sigmoid(x):
    return jnp.minimum(x, 0.0) - jnp.log1p(jnp.exp(-jnp.abs(x)))


def _chunk_index(d, j, n_ctx_tiles, n_tiles):
    bwd = jnp.where(j < n_ctx_tiles, n_ctx_tiles - 1 - j, n_tiles - 1 - (j - n_ctx_tiles))
    return jnp.where(d == 0, j, bwd)


def _mlstm_kernel(qk_ref, prev_ref, next_ref, v_ref, g_ref, cw_ref, cb_ref, gb_ref, o_ref,
                  c_scr, n_scr, m_scr, *, n_ctx_tiles, n_tiles):
    d = pl.program_id(1)
    j = pl.program_id(2)
    ci = _chunk_index(d, j, n_ctx_tiles, n_tiles)
    is_fwd = d == 0

    @pl.when(j == 0)
    def _():
        c_scr[...] = jnp.zeros_like(c_scr)
        n_scr[...] = jnp.zeros_like(n_scr)
        m_scr[...] = jnp.zeros_like(m_scr)

    x = qk_ref[0]
    row = lax.broadcasted_iota(jnp.int32, (TILE, 1), 0)
    has_prev = jnp.logical_and(ci != 0, ci != n_ctx_tiles)
    has_next = jnp.logical_and(ci != n_ctx_tiles - 1, ci != n_tiles - 1)
    prev_row = jnp.where(has_prev, prev_ref[0, SUBLANES - 1:SUBLANES, :], 0.0)
    next_row = jnp.where(has_next, next_ref[0, 0:1, :], 0.0)
    x_prev = jnp.where(row == 0, prev_row, pltpu.roll(x, 1, axis=0))
    x_next = jnp.where(row == TILE - 1, next_row, pltpu.roll(x, TILE - 1, axis=0))
    u = x_prev * cw_ref[0:1, :] + x * cw_ref[1:2, :] + x_next * cw_ref[2:3, :] + cb_ref[...]
    u = u * jax.nn.sigmoid(u)
    q_all = u[:, 0:MLSTM_WIDTH]
    k_all = u[:, MLSTM_WIDTH:2 * MLSTM_WIDTH] * (MLSTM_DH ** -0.5)
    v_all = v_ref[0]

    nh = MLSTM_HEADS
    g = g_ref[0] + gb_ref[...]
    g_t = g.T
    t_idx = lax.broadcasted_iota(jnp.int32, (TILE, TILE), 0)
    s_idx = lax.broadcasted_iota(jnp.int32, (TILE, TILE), 1)
    seen = (s_idx - t_idx) * jnp.where(is_fwd, 1, -1) <= 0
    tri = seen.astype(F32)
    cum_c = jnp.dot(tri, _log_sigmoid(g), preferred_element_type=F32, precision=lax.Precision.HIGHEST)
    cum_r = lax.dot_general(_log_sigmoid(g_t), tri, (((1,), (1,)), ((), ())), preferred_element_type=F32,
                            precision=lax.Precision.HIGHEST)
    ig_c = jnp.where(is_fwd, g[:, 0:nh], g[:, 2 * nh:3 * nh])
    b_c = jnp.where(is_fwd, cum_c[:, nh:2 * nh], cum_c[:, 3 * nh:4 * nh])
    ig_r = jnp.where(is_fwd, g_t[0:nh, :], g_t[2 * nh:3 * nh, :])
    b_r = jnp.where(is_fwd, cum_r[nh:2 * nh, :], cum_r[3 * nh:4 * nh, :])
    b_end = jnp.where(is_fwd, b_c[TILE - 1:TILE, :], b_c[0:1, :])

    for hd in range(nh):
        sl = slice(hd * MLSTM_DH, (hd + 1) * MLSTM_DH)
        q_h = q_all[:, sl]
        k_h = k_all[:, sl]
        v_h = v_all[:, sl]
        q_b = q_h.astype(BF16)
        c_old = c_scr[hd]
        n_old = n_scr[hd]
        m_old = m_scr[hd][:, 0:1]
        bt = b_c[:, hd:hd + 1]
        b_last = b_end[:, hd:hd + 1]

        log_d = jnp.where(seen, bt - b_r[hd:hd + 1, :] + ig_r[hd:hd + 1, :], -jnp.inf)
        log_inter = bt + m_old
        m_t = jnp.maximum(log_inter, jnp.max(log_d, axis=1, keepdims=True))
        s = lax.dot_general(q_b, k_h.astype(BF16), (((1,), (1,)), ((), ())), preferred_element_type=F32)
        s = s * jnp.exp(log_d - m_t)
        inter = jnp.exp(log_inter - m_t)
        num = jnp.dot(s.astype(BF16), v_h, preferred_element_type=F32)
        num = num + inter * jnp.dot(q_b, c_old.astype(BF16), preferred_element_type=F32)
        den = jnp.sum(s, axis=1, keepdims=True) + inter * jnp.sum(q_h * n_old, axis=1, keepdims=True)
        o_ref[0, 0, :, sl] = num / jnp.maximum(jnp.abs(den), jnp.exp(-m_t))

        log_w = b_last - bt + ig_c[:, hd:hd + 1]
        m_new = jnp.maximum(b_last + m_old, jnp.max(log_w, axis=0, keepdims=True))
        kw = k_h * jnp.exp(log_w - m_new)
        decay = jnp.exp(b_last + m_old - m_new)
        c_scr[hd] = decay * c_old + jnp.dot(kw.T.astype(BF16), v_h, preferred_element_type=F32)
        n_scr[hd] = decay * n_old + jnp.sum(kw, axis=0, keepdims=True)
        m_scr[hd] = jnp.broadcast_to(m_new, (1, LANES))


def _mlstm(qk_pre, v_m, gates, conv_w, conv_b, gate_b, *, n_ctx_tiles, layer):
    bsz, t_all, w2 = qk_pre.shape
    n_tiles = t_all // TILE
    halo_per_tile = TILE // SUBLANES
    n_halo = t_all // SUBLANES
    ci_of = functools.partial(_chunk_index, n_ctx_tiles=n_ctx_tiles, n_tiles=n_tiles)
    tile_map = lambda b, d, j: (b, ci_of(d, j), 0)
    prev_map = lambda b, d, j: (b, jnp.maximum(ci_of(d, j) * halo_per_tile - 1, 0), 0)
    next_map = lambda b, d, j: (b, jnp.minimum((ci_of(d, j) + 1) * halo_per_tile, n_halo - 1), 0)
    return pl.pallas_call(
        functools.partial(_mlstm_kernel, n_ctx_tiles=n_ctx_tiles, n_tiles=n_tiles),
        grid=(bsz, 2, n_tiles),
        in_specs=[
            pl.BlockSpec((1, TILE, w2), tile_map),
            pl.BlockSpec((1, SUBLANES, w2), prev_map),
            pl.BlockSpec((1, SUBLANES, w2), next_map),
            pl.BlockSpec((1, TILE, MLSTM_WIDTH), tile_map),
            pl.BlockSpec((1, TILE, LANES), tile_map),
            pl.BlockSpec((None, CONV_K, w2), lambda b, d, j: (layer, 0, 0)),
            pl.BlockSpec((None, 1, w2), lambda b, d, j: (layer, 0, 0)),
            pl.BlockSpec((None, 1, LANES), lambda b, d, j: (layer, 0, 0)),
        ],
        out_specs=pl.BlockSpec((1, 1, TILE, MLSTM_WIDTH), lambda b, d, j: (b, d, ci_of(d, j), 0)),
        out_shape=jax.ShapeDtypeStruct((bsz, 2, t_all, MLSTM_WIDTH), F32),
        scratch_shapes=[pltpu.VMEM((MLSTM_HEADS, MLSTM_DH, MLSTM_DH), F32),
                        pltpu.VMEM((MLSTM_HEADS, 1, MLSTM_DH), F32),
                        pltpu.VMEM((MLSTM_HEADS, 1, LANES), F32)],
        compiler_params=_params(3),
        name=f"mlstm_l{layer}",
    )(qk_pre, qk_pre, qk_pre, v_m, gates, conv_w, conv_b, gate_b)


def _merge_kernel(x_ref, a_ref, hm_ref, o_ref, mod_ref, ga_ref, gm_ref, gpost_ref, wout_ref, out_ref):
    x = x_ref[0]
    mod = mod_ref[0, 0]
    hm = (hm_ref[0, 0] + hm_ref[0, 1]) * jax.nn.sigmoid(o_ref[0])
    normed = []
    for hd in range(MLSTM_HEADS):
        seg = hm[:, hd * MLSTM_DH:(hd + 1) * MLSTM_DH]
        cen = seg - jnp.mean(seg, axis=-1, keepdims=True)
        normed.append(cen * lax.rsqrt(jnp.mean(cen * cen, axis=-1, keepdims=True) + EPS))
    hm_n = (jnp.concatenate(normed, axis=-1) * gm_ref[...]).astype(BF16)
    a_n = _rms(a_ref[0], ga_ref[...]).astype(BF16)
    y = jnp.dot(a_n, wout_ref[0:ATTN_WIDTH, :], preferred_element_type=F32)
    y = y + jnp.dot(hm_n, wout_ref[ATTN_WIDTH:ATTN_WIDTH + MLSTM_WIDTH, :], preferred_element_type=F32)
    out_ref[0] = x + mod[5:6] * _rms(y, gpost_ref[...])


def _merge(x, a, hm, o, mod_all, g_attn, g_mlstm, g_post, wout, *, layer, n_ctx_tiles, t0):
    bsz, t_all, d = x.shape
    n_tiles = t_all // TILE - t0
    return pl.pallas_call(
        _merge_kernel,
        grid=(bsz, n_tiles),
        in_specs=[
            pl.BlockSpec((1, TILE, d), lambda b, t: (b, t + t0, 0)),
            pl.BlockSpec((1, TILE, ATTN_WIDTH), lambda b, t: (b, t, 0)),
            pl.BlockSpec((1, 2, TILE, MLSTM_WIDTH), lambda b, t: (b, 0, t + t0, 0)),
            pl.BlockSpec((1, TILE, MLSTM_WIDTH), lambda b, t: (b, t + t0, 0)),
            _mod_spec(d, t0, n_ctx_tiles),
            pl.BlockSpec((None, 1, ATTN_WIDTH), lambda b, t: (layer, 0, 0)),
            pl.BlockSpec((None, 1, MLSTM_WIDTH), lambda b, t: (layer, 0, 0)),
            pl.BlockSpec((None, None, 1, d), lambda b, t: (layer, 1, 0, 0)),
            _const_spec((None, ATTN_WIDTH + MLSTM_WIDTH, d), lambda b, t: (layer, 0, 0)),
        ],
        out_specs=pl.BlockSpec((1, TILE, d), lambda b, t: (b, t, 0)),
        out_shape=jax.ShapeDtypeStruct((bsz, n_tiles * TILE, d), F32),
        compiler_params=_params(2),
        name=f"merge_l{layer}",
    )(x, a, hm, o, mod_all, g_attn, g_mlstm, g_post, wout)


def _half_swap_perm():
    idx = np.arange(ROPE_DIM)
    axis, half, freq = idx // (2 * AXIS_FREQS), (idx // AXIS_FREQS) % 2, idx % AXIS_FREQS
    return axis * 2 * AXIS_FREQS + (1 - half) * AXIS_FREQS + freq


def _prep_w_in(w_in):
    sizes = (Q_LORA, KV_LORA, ROPE_DIM, 2 * MLSTM_WIDTH, MLSTM_WIDTH, MLSTM_WIDTH, N_GATES)
    offs = np.concatenate([[0], np.cumsum(sizes)])
    cq, ckv, kr, qk, v, o, g = (w_in[..., offs[i]:offs[i + 1]] for i in range(len(sizes)))
    g = jnp.pad(g, ((0, 0), (0, 0), (0, LANES - N_GATES)))
    return jnp.concatenate([cq, ckv, kr, kr[..., _half_swap_perm()], qk, v, o, g], axis=-1).astype(BF16)


def _prep_w_uq(w_uq):
    n_layers, q_lora, _ = w_uq.shape
    w = w_uq.reshape(n_layers, q_lora, ATTN_HEADS, QK_DIM)
    nope = w[..., :NOPE_DIM].reshape(n_layers, q_lora, -1)
    rope = w[..., NOPE_DIM:]
    rope_sw = rope[..., _half_swap_perm()]
    return jnp.concatenate([nope, rope.reshape(n_layers, q_lora, -1), rope_sw.reshape(n_layers, q_lora, -1)],
                           axis=-1).astype(BF16)


def _prep_w_ukv(w_ukv):
    n_layers, kv_lora, _ = w_ukv.shape
    w = w_ukv.reshape(n_layers, kv_lora, ATTN_HEADS, NOPE_DIM + V_DIM)
    return jnp.concatenate([w[..., :NOPE_DIM].reshape(n_layers, kv_lora, -1),
                            w[..., NOPE_DIM:].reshape(n_layers, kv_lora, -1)], axis=-1).astype(BF16)


def _rope_tables(n_ctx, n_tok):
    rows = n_tok // GRID_W
    t_row = jnp.repeat(jnp.arange(rows), GRID_W).astype(F32)
    t_col = jnp.tile(jnp.arange(GRID_W), rows).astype(F32)
    inv = ROPE_BASE ** (-jnp.arange(AXIS_FREQS, dtype=F32) / AXIS_FREQS)
    ang_r = t_row[:, None] * inv
    ang_c = t_col[:, None] * inv
    cos = jnp.concatenate([jnp.cos(ang_r), jnp.cos(ang_r), jnp.cos(ang_c), jnp.cos(ang_c)], axis=-1)
    sin = jnp.concatenate([-jnp.sin(ang_r), jnp.sin(ang_r), -jnp.sin(ang_c), jnp.sin(ang_c)], axis=-1)
    cos = jnp.concatenate([jnp.ones((n_ctx, ROPE_DIM), F32), cos], axis=0)
    sin = jnp.concatenate([jnp.zeros((n_ctx, ROPE_DIM), F32), sin], axis=0)
    return jnp.tile(cos, (1, ATTN_HEADS)), jnp.tile(sin, (1, ATTN_HEADS))


def kernel(x, c, ctx, c_ctx, w_mod, b_mod, norm_pre, norm_post, ffn_w_gate, ffn_w_up, ffn_w_down,
           w_in, q_norm, w_uq, kv_norm, w_ukv, attn_out_norm, conv_w, conv_b, gate_b, mlstm_norm, w_out):
    bsz, n_tok, d = x.shape
    n_ctx = ctx.shape[1]
    n_layers = w_mod.shape[0]
    assert n_tok % TILE == 0 and n_ctx % TILE == 0 and n_tok % GRID_W == 0
    n_ctx_tiles = n_ctx // TILE

    ff = ffn_w_gate.shape[-1]
    ff_pad = _round_up(ff, MXU_DIM) - ff
    wg = jnp.pad(ffn_w_gate, ((0, 0), (0, 0), (0, 0), (0, ff_pad))).astype(BF16)
    wu = jnp.pad(ffn_w_up, ((0, 0), (0, 0), (0, 0), (0, ff_pad))).astype(BF16)
    wd = jnp.pad(ffn_w_down, ((0, 0), (0, 0), (0, ff_pad), (0, 0))).astype(BF16)
    win = _prep_w_in(w_in)
    wuq = _prep_w_uq(w_uq)
    wukv = _prep_w_ukv(w_ukv)
    wout = w_out.astype(BF16)
    g_pre = norm_pre[:, :, None, :]
    g_post = norm_post[:, :, None, :]
    q_gain = q_norm[:, None, :]
    kv_gain = kv_norm[:, None, :]
    a_gain = attn_out_norm[:, None, :]
    m_gain = mlstm_norm[:, None, :]
    conv_bias = conv_b[:, None, :]
    gate_bias = jnp.pad(gate_b.reshape(n_layers, 1, N_GATES), ((0, 0), (0, 0), (0, LANES - N_GATES)))
    cos4, sin4 = _rope_tables(n_ctx, n_tok)

    c_rows = jnp.zeros((_round_up(bsz + 1, SUBLANES), d), F32).at[:bsz].set(c).at[bsz].set(c_ctx)
    mod = _modulation(c_rows, w_mod, b_mod).reshape(n_layers, -1, N_MOD, d)
    mod_all = jnp.stack([jnp.broadcast_to(mod[:, bsz:bsz + 1], (n_layers, bsz, N_MOD, d)), mod[:, :bsz]], axis=2)

    h = jnp.concatenate([ctx, x], axis=1)
    for l in range(n_layers):
        t0 = n_ctx_tiles if l == n_layers - 1 else 0
        ml = mod_all[l]
        h = _ffn(h, ml, g_pre, g_post, wg, wu, wd, layer=l, j=0, sub=0, n_ctx_tiles=n_ctx_tiles, mod_t0=0)
        q, k, vt, qk_pre, v_m, o_pre, gates = _mixin(h, ml, g_pre, win, q_gain, wuq, kv_gain, wukv,
                                                     cos4, sin4, layer=l, n_ctx_tiles=n_ctx_tiles)
        a = _attention(q, k, vt, n_ctx_tiles=n_ctx_tiles, t0=t0, layer=l)
        hm = _mlstm(qk_pre, v_m, gates, conv_w, conv_bias, gate_bias, n_ctx_tiles=n_ctx_tiles, layer=l)
        h = _merge(h, a, hm, o_pre, ml, a_gain, m_gain, g_post, wout, layer=l, n_ctx_tiles=n_ctx_tiles, t0=t0)
        h = _ffn(h, ml, g_pre, g_post, wg, wu, wd, layer=l, j=1, sub=2, n_ctx_tiles=n_ctx_tiles, mod_t0=t0)
    return h
```

```python
import functools

import jax
import jax.numpy as jnp
import numpy as np
from jax import lax
from jax.experimental import pallas as pl
from jax.experimental.pallas import tpu as pltpu

F32 = jnp.float32
BF16 = jnp.bfloat16

N_MOD = 9
EPS = 1e-6
GRID_W = 64
ATTN_HEADS = 4
Q_LORA = 256
KV_LORA = 128
NOPE_DIM = 128
ROPE_DIM = 64
AXIS_FREQS = ROPE_DIM // 4
V_DIM = 128
QK_DIM = NOPE_DIM + ROPE_DIM
ATTN_WIDTH = ATTN_HEADS * V_DIM
ROPE_BASE = 10000.0
SM_SCALE = QK_DIM ** -0.5
Q_SCALE = SM_SCALE * float(np.log2(np.e))
MLSTM_HEADS = 4
MLSTM_DH = 128
MLSTM_WIDTH = MLSTM_HEADS * MLSTM_DH
CONV_K = 3
N_GATES = 4 * MLSTM_HEADS

LANES = 128
SUBLANES = 8
MXU_DIM = 256
TILE = MXU_DIM
VMEM_LIMIT = 56 * 1024 * 1024

A_W = Q_LORA + KV_LORA + 2 * ROPE_DIM
QK_OFF = A_W
V_OFF = QK_OFF + 2 * MLSTM_WIDTH
O_OFF = V_OFF + MLSTM_WIDTH
G_OFF = O_OFF + MLSTM_WIDTH
IN_W = G_OFF + LANES


def _round_up(n, m):
    return (n + m - 1) // m * m


def _rms(x, g):
    return x * lax.rsqrt(jnp.mean(x * x, axis=-1, keepdims=True) + EPS) * g


def _sandwich_in(x, mod, sub, g):
    return _rms(x, g) * (1.0 + mod[3 * sub + 1:3 * sub + 2]) + mod[3 * sub:3 * sub + 1]


def _params(n_grid):
    return pltpu.CompilerParams(dimension_semantics=("arbitrary",) * n_grid,
                                vmem_limit_bytes=VMEM_LIMIT)


def _const_spec(shape, index_map):
    return pl.BlockSpec(shape, index_map, pipeline_mode=pl.Buffered(1))


def _mod_spec(d, mod_t0, n_ctx_tiles):
    return pl.BlockSpec((1, 1, N_MOD, d),
                        lambda b, t: (b, (t + mod_t0 >= n_ctx_tiles).astype(jnp.int32), 0, 0))


def _mod_kernel(c_ref, w_ref, b_ref, o_ref):
    c = c_ref[...]
    h = (c * jax.nn.sigmoid(c)).astype(BF16)
    o_ref[0] = jnp.dot(h, w_ref[0].astype(BF16), preferred_element_type=F32) + b_ref[0]


def _modulation(c_rows, w_mod, b_mod):
    n_layers, d, n_out = w_mod.shape
    rows = c_rows.shape[0]
    n_blk = N_MOD * LANES
    return pl.pallas_call(
        _mod_kernel,
        grid=(n_layers, n_out // n_blk),
        in_specs=[pl.BlockSpec((rows, d), lambda l, j: (0, 0)),
                  pl.BlockSpec((1, d, n_blk), lambda l, j: (l, 0, j)),
                  pl.BlockSpec((1, 1, n_blk), lambda l, j: (l, 0, j))],
        out_specs=pl.BlockSpec((1, rows, n_blk), lambda l, j: (l, 0, j)),
        out_shape=jax.ShapeDtypeStruct((n_layers, rows, n_out), F32),
        compiler_params=_params(2),
        name="modulation",
    )(c_rows, w_mod, b_mod.reshape(n_layers, 1, n_out))


def _ffn_kernel(x_ref, mod_ref, gpre_ref, gpost_ref, wg_ref, wu_ref, wd_ref, o_ref, a_ref, *, sub):
    x = x_ref[0]
    mod = mod_ref[0, 0]
    h = _sandwich_in(x, mod, sub, gpre_ref[...]).astype(BF16)
    ff = a_ref.shape[1]
    for lo in range(0, ff, MXU_DIM):
        g = jnp.dot(h, wg_ref[:, lo:lo + MXU_DIM], preferred_element_type=F32)
        u = jnp.dot(h, wu_ref[:, lo:lo + MXU_DIM], preferred_element_type=F32)
        a_ref[:, lo:lo + MXU_DIM] = (g * jax.nn.sigmoid(g) * u).astype(BF16)
    y = jnp.dot(a_ref[...], wd_ref[...], preferred_element_type=F32)
    o_ref[0] = x + 0.5 * mod[3 * sub + 2:3 * sub + 3] * _rms(y, gpost_ref[...])


def _ffn(x, mod_all, g_pre, g_post, wg, wu, wd, *, layer, j, sub, n_ctx_tiles, mod_t0):
    bsz, t_in, d = x.shape
    ff = wg.shape[-1]
    return pl.pallas_call(
        functools.partial(_ffn_kernel, sub=sub),
        grid=(bsz, t_in // TILE),
        in_specs=[
            pl.BlockSpec((1, TILE, d), lambda b, t: (b, t, 0)),
            _mod_spec(d, mod_t0, n_ctx_tiles),
            pl.BlockSpec((None, None, 1, d), lambda b, t: (layer, sub, 0, 0)),
            pl.BlockSpec((None, None, 1, d), lambda b, t: (layer, sub, 0, 0)),
            _const_spec((None, None, d, ff), lambda b, t: (layer, j, 0, 0)),
            _const_spec((None, None, d, ff), lambda b, t: (layer, j, 0, 0)),
            _const_spec((None, None, ff, d), lambda b, t: (layer, j, 0, 0)),
        ],
        out_specs=pl.BlockSpec((1, TILE, d), lambda b, t: (b, t, 0)),
        out_shape=jax.ShapeDtypeStruct((bsz, t_in, d), F32),
        scratch_shapes=[pltpu.VMEM((TILE, ff), BF16)],
        compiler_params=_params(2),
        name=f"ffn_l{layer}_{j}",
    )(x, mod_all, g_pre, g_post, wg, wu, wd)


def _mixin_kernel(x_ref, mod_ref, gpre_ref, win_ref, qn_ref, wuq_ref, kvn_ref, wukv_ref, cos_ref, sin_ref, gb_ref,
                  q_ref, k_ref, vt_ref, qk_ref, vmt_ref, o_ref, gt_ref):
    h = _sandwich_in(x_ref[0], mod_ref[0, 0], 1, gpre_ref[...]).astype(BF16)
    cos4 = cos_ref[...]
    sin4 = sin_ref[...]

    gates = jnp.dot(h, win_ref[:, G_OFF:IN_W], preferred_element_type=F32) + gb_ref[...]
    g_t = gates.T[0:N_GATES, :]
    half = N_GATES // 2
    gt_ref[0] = jnp.concatenate(_gate_rows(g_t[0:half], False) + _gate_rows(g_t[half:N_GATES], True), axis=0)

    pa = jnp.dot(h, win_ref[:, 0:A_W], preferred_element_type=F32)
    cq = pa[:, 0:Q_LORA]
    ckv = pa[:, Q_LORA:Q_LORA + KV_LORA]
    kr = pa[:, Q_LORA + KV_LORA:Q_LORA + KV_LORA + ROPE_DIM]
    kr_sw = pa[:, Q_LORA + KV_LORA + ROPE_DIM:A_W]
    k_rope = (kr * cos4[:, 0:ROPE_DIM] + kr_sw * sin4[:, 0:ROPE_DIM]).astype(BF16)

    nope_w = ATTN_HEADS * NOPE_DIM
    rope_w = ATTN_HEADS * ROPE_DIM
    qa = jnp.dot(_rms(cq, qn_ref[...]).astype(BF16), wuq_ref[...], preferred_element_type=F32)
    q_rope = qa[:, nope_w:nope_w + rope_w] * cos4 + qa[:, nope_w + rope_w:nope_w + 2 * rope_w] * sin4
    kva = jnp.dot(_rms(ckv, kvn_ref[...]).astype(BF16), wukv_ref[...], preferred_element_type=F32)
    for hd in range(ATTN_HEADS):
        q_ref[0, hd, :, 0:NOPE_DIM] = (qa[:, hd * NOPE_DIM:(hd + 1) * NOPE_DIM] * Q_SCALE).astype(BF16)
        q_ref[0, hd, :, NOPE_DIM:QK_DIM] = (q_rope[:, hd * ROPE_DIM:(hd + 1) * ROPE_DIM] * Q_SCALE).astype(BF16)
        k_ref[0, hd, :, 0:NOPE_DIM] = kva[:, hd * NOPE_DIM:(hd + 1) * NOPE_DIM].astype(BF16)
        k_ref[0, hd, :, NOPE_DIM:QK_DIM] = k_rope
        vt_ref[0, hd] = kva[:, nope_w + hd * V_DIM:nope_w + (hd + 1) * V_DIM].T.astype(BF16)

    qk_ref[0] = jnp.dot(h, win_ref[:, QK_OFF:V_OFF], preferred_element_type=F32)
    vmt_ref[0] = jnp.dot(h, win_ref[:, V_OFF:O_OFF], preferred_element_type=F32).T.astype(BF16)
    o_ref[0] = jnp.dot(h, win_ref[:, O_OFF:G_OFF], preferred_element_type=F32)


def _mixin(x, mod_all, g_pre, win, q_norm, wuq, kv_norm, wukv, cos4, sin4, gate_b, *, layer, n_ctx_tiles):
    bsz, t_all, d = x.shape
    rope_w = ATTN_HEADS * ROPE_DIM
    tile_map = lambda b, t: (b, t, 0)
    head_map = lambda b, t: (b, 0, t, 0)
    time_last_map = lambda b, t: (b, 0, t)
    out_shape = (
        jax.ShapeDtypeStruct((bsz, ATTN_HEADS, t_all, QK_DIM), BF16),
        jax.ShapeDtypeStruct((bsz, ATTN_HEADS, t_all, QK_DIM), BF16),
        jax.ShapeDtypeStruct((bsz, ATTN_HEADS, V_DIM, t_all), BF16),
        jax.ShapeDtypeStruct((bsz, t_all, 2 * MLSTM_WIDTH), F32),
        jax.ShapeDtypeStruct((bsz, MLSTM_WIDTH, t_all), BF16),
        jax.ShapeDtypeStruct((bsz, t_all, MLSTM_WIDTH), F32),
        jax.ShapeDtypeStruct((bsz, GATE_ROWS, t_all), F32),
    )
    return pl.pallas_call(
        _mixin_kernel,
        grid=(bsz, t_all // TILE),
        in_specs=[
            pl.BlockSpec((1, TILE, d), tile_map),
            _mod_spec(d, 0, n_ctx_tiles),
            pl.BlockSpec((None, None, 1, d), lambda b, t: (layer, 1, 0, 0)),
            _const_spec((None, d, IN_W), lambda b, t: (layer, 0, 0)),
            pl.BlockSpec((None, 1, Q_LORA), lambda b, t: (layer, 0, 0)),
            _const_spec((None, Q_LORA, wuq.shape[-1]), lambda b, t: (layer, 0, 0)),
            pl.BlockSpec((None, 1, KV_LORA), lambda b, t: (layer, 0, 0)),
            _const_spec((None, KV_LORA, wukv.shape[-1]), lambda b, t: (layer, 0, 0)),
            pl.BlockSpec((TILE, rope_w), lambda b, t: (t, 0)),
            pl.BlockSpec((TILE, rope_w), lambda b, t: (t, 0)),
            pl.BlockSpec((None, 1, LANES), lambda b, t: (layer, 0, 0)),
        ],
        out_specs=(
            pl.BlockSpec((1, ATTN_HEADS, TILE, QK_DIM), head_map),
            pl.BlockSpec((1, ATTN_HEADS, TILE, QK_DIM), head_map),
            pl.BlockSpec((1, ATTN_HEADS, V_DIM, TILE), lambda b, t: (b, 0, 0, t)),
            pl.BlockSpec((1, TILE, 2 * MLSTM_WIDTH), tile_map),
            pl.BlockSpec((1, MLSTM_WIDTH, TILE), time_last_map),
            pl.BlockSpec((1, TILE, MLSTM_WIDTH), tile_map),
            pl.BlockSpec((1, GATE_ROWS, TILE), time_last_map),
        ),
        out_shape=out_shape,
        compiler_params=_params(2),
        name=f"mixin_l{layer}",
    )(x, mod_all, g_pre, win, q_norm, wuq, kv_norm, wukv, cos4, sin4, gate_b)


ATTN_HEADS_PER_STEP = 2
KEY_CHUNK = 2 * TILE


def _attn_scores(q, k_ref, s_scr, hh, chunks):
    m8 = jnp.full((SUBLANES, TILE), -jnp.inf, F32)
    for lo, hi in chunks:
        s = lax.dot_general(k_ref[0, hh, lo:hi, :], q, (((1,), (1,)), ((), ())), preferred_element_type=F32)
        s_scr[hh, lo:hi, :] = s
        m8 = jnp.maximum(m8, jnp.max(s.reshape(-1, SUBLANES, TILE), axis=0))
    return m8


def _attn_probs(m8, s_scr, p_scr, hh, chunks):
    m = jnp.max(m8, axis=0, keepdims=True)
    l8 = jnp.zeros((SUBLANES, TILE), F32)
    for lo, hi in chunks:
        p = jnp.exp2(s_scr[hh, lo:hi, :] - m)
        l8 = l8 + jnp.sum(p.reshape(-1, SUBLANES, TILE), axis=0)
        p_scr[hh, lo:hi, :] = p.astype(BF16)
    return jnp.sum(l8, axis=0, keepdims=True)


def _attn_heads(q_ref, k_ref, vt_ref, o_ref, s_scr, p_scr, chunks):
    lo, hi = chunks[0][0], chunks[-1][1]
    heads = range(ATTN_HEADS_PER_STEP)
    m8 = [_attn_scores(q_ref[0, hh], k_ref, s_scr, hh, chunks) for hh in heads]
    for hh in heads:
        l = _attn_probs(m8[hh], s_scr, p_scr, hh, chunks)
        acc = jnp.dot(vt_ref[0, hh, :, lo:hi], p_scr[hh, lo:hi, :], preferred_element_type=F32)
        o_ref[0, :, hh * V_DIM:(hh + 1) * V_DIM] = (acc / l).T


def _attn_kernel(q_ref, k_ref, vt_ref, o_ref, s_scr, p_scr, *, t0, n_ctx_tiles, ctx_chunks, all_chunks):
    if t0 >= n_ctx_tiles:
        _attn_heads(q_ref, k_ref, vt_ref, o_ref, s_scr, p_scr, all_chunks)
    else:
        is_ctx = pl.program_id(2) + t0 < n_ctx_tiles

        @pl.when(is_ctx)
        def _():
            _attn_heads(q_ref, k_ref, vt_ref, o_ref, s_scr, p_scr, ctx_chunks)

        @pl.when(jnp.logical_not(is_ctx))
        def _():
            _attn_heads(q_ref, k_ref, vt_ref, o_ref, s_scr, p_scr, all_chunks)


def _key_chunks(n_keys, width):
    return tuple((lo, min(lo + width, n_keys)) for lo in range(0, n_keys, width))


def _attention(q, k, vt, *, n_ctx_tiles, t0, layer):
    bsz, n_heads, t_all, _ = q.shape
    n_q = t_all // TILE - t0
    hps = ATTN_HEADS_PER_STEP
    kernel = functools.partial(
        _attn_kernel, t0=t0, n_ctx_tiles=n_ctx_tiles,
        ctx_chunks=_key_chunks(n_ctx_tiles * TILE, KEY_CHUNK), all_chunks=_key_chunks(t_all, KEY_CHUNK))
    return pl.pallas_call(
        kernel,
        grid=(bsz, n_heads // hps, n_q),
        in_specs=[
            pl.BlockSpec((1, hps, TILE, QK_DIM), lambda b, h, i: (b, h, i + t0, 0)),
            pl.BlockSpec((1, hps, t_all, QK_DIM), lambda b, h, i: (b, h, 0, 0)),
            pl.BlockSpec((1, hps, V_DIM, t_all), lambda b, h, i: (b, h, 0, 0)),
        ],
        out_specs=pl.BlockSpec((1, TILE, hps * V_DIM), lambda b, h, i: (b, i, h)),
        out_shape=jax.ShapeDtypeStruct((bsz, n_q * TILE, n_heads * V_DIM), F32),
        scratch_shapes=[pltpu.VMEM((hps, t_all, TILE), F32), pltpu.VMEM((hps, t_all, TILE), BF16)],
        compiler_params=_params(3),
        name=f"attention_l{layer}",
    )(q, k, vt)


def _qkprep_kernel(qk_ref, prev_ref, next_ref, cw_ref, cb_ref, qt_ref, k_ref, *, n_ctx_tiles, n_tiles):
    ci = pl.program_id(1)
    x = qk_ref[0]
    row = lax.broadcasted_iota(jnp.int32, (TILE, 1), 0)
    has_prev = jnp.logical_and(ci != 0, ci != n_ctx_tiles)
    has_next = jnp.logical_and(ci != n_ctx_tiles - 1, ci != n_tiles - 1)
    prev_row = jnp.where(has_prev, prev_ref[0, SUBLANES - 1:SUBLANES, :], 0.0)
    next_row = jnp.where(has_next, next_ref[0, 0:1, :], 0.0)
    x_prev = jnp.where(row == 0, prev_row, pltpu.roll(x, 1, axis=0))
    x_next = jnp.where(row == TILE - 1, next_row, pltpu.roll(x, TILE - 1, axis=0))
    u = x_prev * cw_ref[0:1, :] + x * cw_ref[1:2, :] + x_next * cw_ref[2:3, :] + cb_ref[...]
    u = u * jax.nn.sigmoid(u)
    qt_ref[0] = u[:, 0:MLSTM_WIDTH].T.astype(BF16)
    k_ref[0] = (u[:, MLSTM_WIDTH:2 * MLSTM_WIDTH] * (MLSTM_DH ** -0.5)).astype(BF16)


def _qkprep(qk_pre, conv_w, conv_b, *, n_ctx_tiles, layer):
    bsz, t_all, w2 = qk_pre.shape
    n_tiles = t_all // TILE
    halo_per_tile = TILE // SUBLANES
    n_halo = t_all // SUBLANES
    return pl.pallas_call(
        functools.partial(_qkprep_kernel, n_ctx_tiles=n_ctx_tiles, n_tiles=n_tiles),
        grid=(bsz, n_tiles),
        in_specs=[
            pl.BlockSpec((1, TILE, w2), lambda b, t: (b, t, 0)),
            pl.BlockSpec((1, SUBLANES, w2), lambda b, t: (b, jnp.maximum(t * halo_per_tile - 1, 0), 0)),
            pl.BlockSpec((1, SUBLANES, w2), lambda b, t: (b, jnp.minimum((t + 1) * halo_per_tile, n_halo - 1), 0)),
            pl.BlockSpec((None, CONV_K, w2), lambda b, t: (layer, 0, 0)),
            pl.BlockSpec((None, 1, w2), lambda b, t: (layer, 0, 0)),
        ],
        out_specs=(pl.BlockSpec((1, MLSTM_WIDTH, TILE), lambda b, t: (b, 0, t)),
                   pl.BlockSpec((1, TILE, MLSTM_WIDTH), lambda b, t: (b, t, 0))),
        out_shape=(jax.ShapeDtypeStruct((bsz, MLSTM_WIDTH, t_all), BF16),
                   jax.ShapeDtypeStruct((bsz, t_all, MLSTM_WIDTH), BF16)),
        compiler_params=_params(2),
        name=f"qkprep_l{layer}",
    )(qk_pre, qk_pre, qk_pre, conv_w, conv_b)


STATE_ROWS = MLSTM_DH + 2 * SUBLANES
GATE_ROWS = 2 * 3 * SUBLANES
LOG2E = float(np.log2(np.e))


def _log_sigmoid(x):
    return jnp.minimum(x, 0.0) - jnp.log1p(jnp.exp(-jnp.abs(x)))


def _bwd_chunk(j, n_ctx_tiles, n_tiles):
    return jnp.where(j < n_ctx_tiles, n_ctx_tiles - 1 - j, n_tiles - 1 - (j - n_ctx_tiles))


def _lane_scan(x, op, identity, reverse):
    n = x.shape[1]
    lane = lax.broadcasted_iota(jnp.int32, x.shape, 1)
    shift = 1
    while shift < n:
        if reverse:
            moved = jnp.where(lane < n - shift, pltpu.roll(x, n - shift, axis=1), identity)
        else:
            moved = jnp.where(lane >= shift, pltpu.roll(x, shift, axis=1), identity)
        x = op(x, moved)
        shift *= 2
    return x


def _gate_rows(g8, reverse):
    nh = MLSTM_HEADS
    head_rows = lax.broadcasted_iota(jnp.int32, g8.shape, 0) < nh
    b = _lane_scan(_log_sigmoid(g8), jnp.add, 0.0, reverse)
    b = jnp.where(head_rows, pltpu.roll(b, nh, axis=0), 0.0)
    a = jnp.where(head_rows, g8 - b, 0.0)
    return a, b, _lane_scan(a, jnp.maximum, -jnp.inf, reverse)


def _mlstm_direction(reverse, a, b, a_max, k_ref, qt_ref, vt_ref, o_ref, st_scr, m_scr, d):
    nh = MLSTM_HEADS
    last = 0 if reverse else TILE - 1
    m_old = m_scr[d][:, 0:1]
    mx = jnp.maximum(m_old, a_max)
    inter = jnp.exp(m_old - mx)
    e_inv = jnp.exp(-(b + mx))
    mx_last = mx[:, last:last + 1]
    w = jnp.exp(a - mx_last)
    decay = jnp.exp(m_old - mx_last)
    m_scr[d] = jnp.broadcast_to(b[:, last:last + 1] + mx_last, (SUBLANES, LANES))

    s_idx = lax.broadcasted_iota(jnp.int32, (TILE, TILE), 0)
    t_idx = lax.broadcasted_iota(jnp.int32, (TILE, TILE), 1)
    seen = s_idx >= t_idx if reverse else s_idx <= t_idx
    pad_row = lax.broadcasted_iota(jnp.int32, (2 * SUBLANES, TILE), 0) == 0
    ones_rows = jnp.where(pad_row, 1.0, 0.0).astype(BF16)

    for hd in range(nh):
        sl = slice(hd * MLSTM_DH, (hd + 1) * MLSTM_DH)
        k_h = k_ref[0, :, sl]
        qt_h = qt_ref[0, sl, :]
        vt_h = vt_ref[0, sl, :]
        state = st_scr[d * nh + hd]
        a_rep = jnp.broadcast_to(a[hd:hd + 1] * LOG2E, (LANES, TILE)).T
        log_d = jnp.concatenate([a_rep] * (TILE // LANES), axis=1) - mx[hd:hd + 1] * LOG2E
        d_t = jnp.exp2(jnp.where(seen, log_d, -jnp.inf))
        r = jnp.dot(jnp.concatenate([k_h, state.astype(BF16)], axis=0), qt_h, preferred_element_type=F32)
        p = (r[0:TILE] * d_t).astype(BF16)
        nd = jnp.dot(jnp.concatenate([vt_h, ones_rows], axis=0), p, preferred_element_type=F32)
        inter_h = inter[hd:hd + 1]
        den = nd[MLSTM_DH:MLSTM_DH + 1] + inter_h * r[TILE + MLSTM_DH:TILE + MLSTM_DH + 1]
        scale = 1.0 / jnp.maximum(jnp.abs(den), e_inv[hd:hd + 1])
        h_t = (nd[0:MLSTM_DH] + inter_h * r[TILE:TILE + MLSTM_DH]) * scale
        o_ref[0, :, sl] = h_t.T

        w_h = w[hd:hd + 1]
        vw = jnp.concatenate([(vt_h.astype(F32) * w_h).astype(BF16), jnp.where(pad_row, w_h, 0.0).astype(BF16)],
                             axis=0)
        st_scr[d * nh + hd] = decay[hd:hd + 1] * state + jnp.dot(vw, k_h, preferred_element_type=F32)


def _mlstm_kernel(kf_ref, qtf_ref, vtf_ref, gf_ref, kb_ref, qtb_ref, vtb_ref, gb_ref, of_ref, ob_ref,
                  st_scr, m_scr):
    @pl.when(pl.program_id(1) == 0)
    def _():
        st_scr[...] = jnp.zeros_like(st_scr)
        m_scr[...] = jnp.zeros_like(m_scr)

    rows = [slice(i * SUBLANES, (i + 1) * SUBLANES) for i in range(GATE_ROWS // SUBLANES)]
    _mlstm_direction(False, gf_ref[0, rows[0], :], gf_ref[0, rows[1], :], gf_ref[0, rows[2], :],
                     kf_ref, qtf_ref, vtf_ref, of_ref, st_scr, m_scr, 0)
    _mlstm_direction(True, gb_ref[0, rows[3], :], gb_ref[0, rows[4], :], gb_ref[0, rows[5], :],
                     kb_ref, qtb_ref, vtb_ref, ob_ref, st_scr, m_scr, 1)


def _mlstm(k_m, qt_m, vt_m, gates_t, *, n_ctx_tiles, layer):
    bsz, t_all, width = k_m.shape
    n_tiles = t_all // TILE
    bwd = functools.partial(_bwd_chunk, n_ctx_tiles=n_ctx_tiles, n_tiles=n_tiles)

    def specs(chunk):
        return [pl.BlockSpec((1, TILE, width), lambda b, j: (b, chunk(j), 0)),
                pl.BlockSpec((1, width, TILE), lambda b, j: (b, 0, chunk(j))),
                pl.BlockSpec((1, width, TILE), lambda b, j: (b, 0, chunk(j))),
                pl.BlockSpec((1, GATE_ROWS, TILE), lambda b, j: (b, 0, chunk(j)))]

    out_sds = jax.ShapeDtypeStruct((bsz, t_all, width), F32)
    return pl.pallas_call(
        _mlstm_kernel,
        grid=(bsz, n_tiles),
        in_specs=specs(lambda j: j) + specs(bwd),
        out_specs=(pl.BlockSpec((1, TILE, width), lambda b, j: (b, j, 0)),
                   pl.BlockSpec((1, TILE, width), lambda b, j: (b, bwd(j), 0))),
        out_shape=(out_sds, out_sds),
        scratch_shapes=[pltpu.VMEM((2 * MLSTM_HEADS, STATE_ROWS, MLSTM_DH), F32),
                        pltpu.VMEM((2, SUBLANES, LANES), F32)],
        compiler_params=_params(2),
        name=f"mlstm_l{layer}",
    )(k_m, qt_m, vt_m, gates_t, k_m, qt_m, vt_m, gates_t)


def _merge_kernel(x_ref, a_ref, hf_ref, hb_ref, o_ref, mod_ref, ga_ref, gm_ref, gpost_ref, wout_ref, out_ref):
    x = x_ref[0]
    mod = mod_ref[0, 0]
    hm = (hf_ref[0] + hb_ref[0]) * jax.nn.sigmoid(o_ref[0])
    normed = []
    for hd in range(MLSTM_HEADS):
        seg = hm[:, hd * MLSTM_DH:(hd + 1) * MLSTM_DH]
        cen = seg - jnp.mean(seg, axis=-1, keepdims=True)
        normed.append(cen * lax.rsqrt(jnp.mean(cen * cen, axis=-1, keepdims=True) + EPS))
    hm_n = (jnp.concatenate(normed, axis=-1) * gm_ref[...]).astype(BF16)
    a_n = _rms(a_ref[0], ga_ref[...]).astype(BF16)
    y = jnp.dot(a_n, wout_ref[0:ATTN_WIDTH, :], preferred_element_type=F32)
    y = y + jnp.dot(hm_n, wout_ref[ATTN_WIDTH:ATTN_WIDTH + MLSTM_WIDTH, :], preferred_element_type=F32)
    out_ref[0] = x + mod[5:6] * _rms(y, gpost_ref[...])


def _merge(x, a, h_fwd, h_bwd, o, mod_all, g_attn, g_mlstm, g_post, wout, *, layer, n_ctx_tiles, t0):
    bsz, t_all, d = x.shape
    n_tiles = t_all // TILE - t0
    return pl.pallas_call(
        _merge_kernel,
        grid=(bsz, n_tiles),
        in_specs=[
            pl.BlockSpec((1, TILE, d), lambda b, t: (b, t + t0, 0)),
            pl.BlockSpec((1, TILE, ATTN_WIDTH), lambda b, t: (b, t, 0)),
            pl.BlockSpec((1, TILE, MLSTM_WIDTH), lambda b, t: (b, t + t0, 0)),
            pl.BlockSpec((1, TILE, MLSTM_WIDTH), lambda b, t: (b, t + t0, 0)),
            pl.BlockSpec((1, TILE, MLSTM_WIDTH), lambda b, t: (b, t + t0, 0)),
            _mod_spec(d, t0, n_ctx_tiles),
            pl.BlockSpec((None, 1, ATTN_WIDTH), lambda b, t: (layer, 0, 0)),
            pl.BlockSpec((None, 1, MLSTM_WIDTH), lambda b, t: (layer, 0, 0)),
            pl.BlockSpec((None, None, 1, d), lambda b, t: (layer, 1, 0, 0)),
            _const_spec((None, ATTN_WIDTH + MLSTM_WIDTH, d), lambda b, t: (layer, 0, 0)),
        ],
        out_specs=pl.BlockSpec((1, TILE, d), lambda b, t: (b, t, 0)),
        out_shape=jax.ShapeDtypeStruct((bsz, n_tiles * TILE, d), F32),
        compiler_params=_params(2),
        name=f"merge_l{layer}",
    )(x, a, h_fwd, h_bwd, o, mod_all, g_attn, g_mlstm, g_post, wout)


def _half_swap_perm():
    idx = np.arange(ROPE_DIM)
    axis, half, freq = idx // (2 * AXIS_FREQS), (idx // AXIS_FREQS) % 2, idx % AXIS_FREQS
    return axis * 2 * AXIS_FREQS + (1 - half) * AXIS_FREQS + freq


def _prep_w_in(w_in):
    sizes = (Q_LORA, KV_LORA, ROPE_DIM, 2 * MLSTM_WIDTH, MLSTM_WIDTH, MLSTM_WIDTH, N_GATES)
    offs = np.concatenate([[0], np.cumsum(sizes)])
    cq, ckv, kr, qk, v, o, g = (w_in[..., offs[i]:offs[i + 1]] for i in range(len(sizes)))
    g = jnp.pad(g, ((0, 0), (0, 0), (0, LANES - N_GATES)))
    return jnp.concatenate([cq, ckv, kr, kr[..., _half_swap_perm()], qk, v, o, g], axis=-1).astype(BF16)


def _prep_w_uq(w_uq):
    n_layers, q_lora, _ = w_uq.shape
    w = w_uq.reshape(n_layers, q_lora, ATTN_HEADS, QK_DIM)
    nope = w[..., :NOPE_DIM].reshape(n_layers, q_lora, -1)
    rope = w[..., NOPE_DIM:]
    rope_sw = rope[..., _half_swap_perm()]
    return jnp.concatenate([nope, rope.reshape(n_layers, q_lora, -1), rope_sw.reshape(n_layers, q_lora, -1)],
                           axis=-1).astype(BF16)


def _prep_w_ukv(w_ukv):
    n_layers, kv_lora, _ = w_ukv.shape
    w = w_ukv.reshape(n_layers, kv_lora, ATTN_HEADS, NOPE_DIM + V_DIM)
    return jnp.concatenate([w[..., :NOPE_DIM].reshape(n_layers, kv_lora, -1),
                            w[..., NOPE_DIM:].reshape(n_layers, kv_lora, -1)], axis=-1).astype(BF16)


def _rope_tables(n_ctx, n_tok):
    rows = n_tok // GRID_W
    t_row = jnp.repeat(jnp.arange(rows), GRID_W).astype(F32)
    t_col = jnp.tile(jnp.arange(GRID_W), rows).astype(F32)
    inv = ROPE_BASE ** (-jnp.arange(AXIS_FREQS, dtype=F32) / AXIS_FREQS)
    ang_r = t_row[:, None] * inv
    ang_c = t_col[:, None] * inv
    cos = jnp.concatenate([jnp.cos(ang_r), jnp.cos(ang_r), jnp.cos(ang_c), jnp.cos(ang_c)], axis=-1)
    sin = jnp.concatenate([-jnp.sin(ang_r), jnp.sin(ang_r), -jnp.sin(ang_c), jnp.sin(ang_c)], axis=-1)
    cos = jnp.concatenate([jnp.ones((n_ctx, ROPE_DIM), F32), cos], axis=0)
    sin = jnp.concatenate([jnp.zeros((n_ctx, ROPE_DIM), F32), sin], axis=0)
    return jnp.tile(cos, (1, ATTN_HEADS)), jnp.tile(sin, (1, ATTN_HEADS))


def kernel(x, c, ctx, c_ctx, w_mod, b_mod, norm_pre, norm_post, ffn_w_gate, ffn_w_up, ffn_w_down,
           w_in, q_norm, w_uq, kv_norm, w_ukv, attn_out_norm, conv_w, conv_b, gate_b, mlstm_norm, w_out):
    bsz, n_tok, d = x.shape
    n_ctx = ctx.shape[1]
    n_layers = w_mod.shape[0]
    assert n_tok % TILE == 0 and n_ctx % TILE == 0 and n_tok % GRID_W == 0
    n_ctx_tiles = n_ctx // TILE

    ff = ffn_w_gate.shape[-1]
    ff_pad = _round_up(ff, MXU_DIM) - ff
    wg = jnp.pad(ffn_w_gate, ((0, 0), (0, 0), (0, 0), (0, ff_pad))).astype(BF16)
    wu = jnp.pad(ffn_w_up, ((0, 0), (0, 0), (0, 0), (0, ff_pad))).astype(BF16)
    wd = jnp.pad(ffn_w_down, ((0, 0), (0, 0), (0, ff_pad), (0, 0))).astype(BF16)
    win = _prep_w_in(w_in)
    wuq = _prep_w_uq(w_uq)
    wukv = _prep_w_ukv(w_ukv)
    wout = w_out.astype(BF16)
    g_pre = norm_pre[:, :, None, :]
    g_post = norm_post[:, :, None, :]
    q_gain = q_norm[:, None, :]
    kv_gain = kv_norm[:, None, :]
    a_gain = attn_out_norm[:, None, :]
    m_gain = mlstm_norm[:, None, :]
    conv_bias = conv_b[:, None, :]
    gate_bias = jnp.pad(gate_b.reshape(n_layers, 1, N_GATES), ((0, 0), (0, 0), (0, LANES - N_GATES)))
    cos4, sin4 = _rope_tables(n_ctx, n_tok)

    c_rows = jnp.zeros((_round_up(bsz + 1, SUBLANES), d), F32).at[:bsz].set(c).at[bsz].set(c_ctx)
    mod = _modulation(c_rows, w_mod, b_mod).reshape(n_layers, -1, N_MOD, d)
    mod_all = jnp.stack([jnp.broadcast_to(mod[:, bsz:bsz + 1], (n_layers, bsz, N_MOD, d)), mod[:, :bsz]], axis=2)

    h = jnp.concatenate([ctx, x], axis=1)
    for l in range(n_layers):
        t0 = n_ctx_tiles if l == n_layers - 1 else 0
        ml = mod_all[l]
        h = _ffn(h, ml, g_pre, g_post, wg, wu, wd, layer=l, j=0, sub=0, n_ctx_tiles=n_ctx_tiles, mod_t0=0)
        q, k, vt, qk_pre, vt_m, o_pre, gates_t = _mixin(h, ml, g_pre, win, q_gain, wuq, kv_gain, wukv,
                                                        cos4, sin4, gate_bias, layer=l, n_ctx_tiles=n_ctx_tiles)
        a = _attention(q, k, vt, n_ctx_tiles=n_ctx_tiles, t0=t0, layer=l)
        qt_m, k_m = _qkprep(qk_pre, conv_w, conv_bias, n_ctx_tiles=n_ctx_tiles, layer=l)
        h_fwd, h_bwd = _mlstm(k_m, qt_m, vt_m, gates_t, n_ctx_tiles=n_ctx_tiles, layer=l)
        h = _merge(h, a, h_fwd, h_bwd, o_pre, ml, a_gain, m_gain, g_post, wout,
                   layer=l, n_ctx_tiles=n_ctx_tiles, t0=t0)
        h = _ffn(h, ml, g_pre, g_post, wg, wu, wd, layer=l, j=1, sub=2, n_ctx_tiles=n_ctx_tiles, mod_t0=t0)
    return h
```

```python
import functools

import jax
import jax.numpy as jnp
import numpy as np
from jax import lax
from jax.experimental import pallas as pl
from jax.experimental.pallas import tpu as pltpu

F32 = jnp.float32
BF16 = jnp.bfloat16

N_MOD = 9
EPS = 1e-6
GRID_W = 64
ATTN_HEADS = 4
Q_LORA = 256
KV_LORA = 128
NOPE_DIM = 128
ROPE_DIM = 64
AXIS_FREQS = ROPE_DIM // 4
V_DIM = 128
QK_DIM = NOPE_DIM + ROPE_DIM
ATTN_WIDTH = ATTN_HEADS * V_DIM
ROPE_BASE = 10000.0
SM_SCALE = QK_DIM ** -0.5
Q_SCALE = SM_SCALE * float(np.log2(np.e))
MLSTM_HEADS = 4
MLSTM_DH = 128
MLSTM_WIDTH = MLSTM_HEADS * MLSTM_DH
CONV_K = 3
N_GATES = 4 * MLSTM_HEADS

LANES = 128
SUBLANES = 8
MXU_DIM = 256
TILE = MXU_DIM
VMEM_LIMIT = 56 * 1024 * 1024

A_W = Q_LORA + KV_LORA + 2 * ROPE_DIM
QK_OFF = A_W
V_OFF = QK_OFF + 2 * MLSTM_WIDTH
O_OFF = V_OFF + MLSTM_WIDTH
G_OFF = O_OFF + MLSTM_WIDTH
IN_W = G_OFF + LANES


def _round_up(n, m):
    return (n + m - 1) // m * m


def _rms(x, g):
    return x * lax.rsqrt(jnp.mean(x * x, axis=-1, keepdims=True) + EPS) * g


def _sandwich_in(x, mod, sub, g):
    return _rms(x, g) * (1.0 + mod[3 * sub + 1:3 * sub + 2]) + mod[3 * sub:3 * sub + 1]


def _params(n_grid):
    return pltpu.CompilerParams(dimension_semantics=("arbitrary",) * n_grid,
                                vmem_limit_bytes=VMEM_LIMIT)


def _const_spec(shape, index_map):
    return pl.BlockSpec(shape, index_map, pipeline_mode=pl.Buffered(1))


def _mod_spec(d, mod_t0, n_ctx_tiles):
    return pl.BlockSpec((1, 1, N_MOD, d),
                        lambda b, t: (b, (t + mod_t0 >= n_ctx_tiles).astype(jnp.int32), 0, 0))


def _mod_kernel(c_ref, w_ref, b_ref, o_ref):
    c = c_ref[...]
    h = (c * jax.nn.sigmoid(c)).astype(BF16)
    o_ref[0] = jnp.dot(h, w_ref[0].astype(BF16), preferred_element_type=F32) + b_ref[0]


def _modulation(c_rows, w_mod, b_mod):
    n_layers, d, n_out = w_mod.shape
    rows = c_rows.shape[0]
    n_blk = N_MOD * LANES
    return pl.pallas_call(
        _mod_kernel,
        grid=(n_layers, n_out // n_blk),
        in_specs=[pl.BlockSpec((rows, d), lambda l, j: (0, 0)),
                  pl.BlockSpec((1, d, n_blk), lambda l, j: (l, 0, j)),
                  pl.BlockSpec((1, 1, n_blk), lambda l, j: (l, 0, j))],
        out_specs=pl.BlockSpec((1, rows, n_blk), lambda l, j: (l, 0, j)),
        out_shape=jax.ShapeDtypeStruct((n_layers, rows, n_out), F32),
        compiler_params=_params(2),
        name="modulation",
    )(c_rows, w_mod, b_mod.reshape(n_layers, 1, n_out))


def _ffn_kernel(x_ref, mod_ref, gpre_ref, gpost_ref, wg_ref, wu_ref, wd_ref, o_ref, a_ref, *, sub):
    x = x_ref[0]
    mod = mod_ref[0, 0]
    h = _sandwich_in(x, mod, sub, gpre_ref[...]).astype(BF16)
    ff = a_ref.shape[1]
    for lo in range(0, ff, MXU_DIM):
        hi = min(lo + MXU_DIM, ff)
        g = jnp.dot(h, wg_ref[:, lo:hi], preferred_element_type=F32)
        u = jnp.dot(h, wu_ref[:, lo:hi], preferred_element_type=F32)
        a_ref[:, lo:hi] = (g * jax.nn.sigmoid(g) * u).astype(BF16)
    ff_main = ff // MXU_DIM * MXU_DIM
    y = jnp.dot(a_ref[:, 0:ff_main], wd_ref[0:ff_main, :], preferred_element_type=F32)
    if ff_main < ff:
        y = y + jnp.dot(a_ref[:, ff_main:ff], wd_ref[ff_main:ff, :], preferred_element_type=F32)
    o_ref[0] = x + 0.5 * mod[3 * sub + 2:3 * sub + 3] * _rms(y, gpost_ref[...])


def _ffn(x, mod_all, g_pre, g_post, wg, wu, wd, *, layer, j, sub, n_ctx_tiles, mod_t0):
    bsz, t_in, d = x.shape
    ff = wg.shape[-1]
    return pl.pallas_call(
        functools.partial(_ffn_kernel, sub=sub),
        grid=(bsz, t_in // TILE),
        in_specs=[
            pl.BlockSpec((1, TILE, d), lambda b, t: (b, t, 0)),
            _mod_spec(d, mod_t0, n_ctx_tiles),
            pl.BlockSpec((None, None, 1, d), lambda b, t: (layer, sub, 0, 0)),
            pl.BlockSpec((None, None, 1, d), lambda b, t: (layer, sub, 0, 0)),
            _const_spec((None, None, d, ff), lambda b, t: (layer, j, 0, 0)),
            _const_spec((None, None, d, ff), lambda b, t: (layer, j, 0, 0)),
            _const_spec((None, None, ff, d), lambda b, t: (layer, j, 0, 0)),
        ],
        out_specs=pl.BlockSpec((1, TILE, d), lambda b, t: (b, t, 0)),
        out_shape=jax.ShapeDtypeStruct((bsz, t_in, d), F32),
        scratch_shapes=[pltpu.VMEM((TILE, ff), BF16)],
        compiler_params=_params(2),
        name=f"ffn_l{layer}_{j}",
    )(x, mod_all, g_pre, g_post, wg, wu, wd)


def _mixin_kernel(x_ref, mod_ref, gpre_ref, win_ref, qn_ref, wuq_ref, kvn_ref, wukv_ref, cos_ref, sin_ref, gb_ref,
                  q_ref, k_ref, vt_ref, qk_ref, vmt_ref, o_ref, gt_ref):
    h = _sandwich_in(x_ref[0], mod_ref[0, 0], 1, gpre_ref[...]).astype(BF16)
    cos4 = cos_ref[...]
    sin4 = sin_ref[...]

    gates = jnp.dot(h, win_ref[:, G_OFF:IN_W], preferred_element_type=F32) + gb_ref[...]
    g_t = gates.T[0:N_GATES, :]
    half = N_GATES // 2
    gt_ref[0] = jnp.concatenate(_gate_rows(g_t[0:half], False) + _gate_rows(g_t[half:N_GATES], True), axis=0)

    pa = jnp.dot(h, win_ref[:, 0:A_W], preferred_element_type=F32)
    cq = pa[:, 0:Q_LORA]
    ckv = pa[:, Q_LORA:Q_LORA + KV_LORA]
    kr = pa[:, Q_LORA + KV_LORA:Q_LORA + KV_LORA + ROPE_DIM]
    kr_sw = pa[:, Q_LORA + KV_LORA + ROPE_DIM:A_W]
    k_rope = (kr * cos4[:, 0:ROPE_DIM] + kr_sw * sin4[:, 0:ROPE_DIM]).astype(BF16)

    nope_w = ATTN_HEADS * NOPE_DIM
    rope_w = ATTN_HEADS * ROPE_DIM
    qa = jnp.dot(_rms(cq, qn_ref[...]).astype(BF16), wuq_ref[...], preferred_element_type=F32)
    q_rope = qa[:, nope_w:nope_w + rope_w] * cos4 + qa[:, nope_w + rope_w:nope_w + 2 * rope_w] * sin4
    kva = jnp.dot(_rms(ckv, kvn_ref[...]).astype(BF16), wukv_ref[...], preferred_element_type=F32)
    for hd in range(ATTN_HEADS):
        q_ref[0, hd, :, 0:NOPE_DIM] = (qa[:, hd * NOPE_DIM:(hd + 1) * NOPE_DIM] * Q_SCALE).astype(BF16)
        q_ref[0, hd, :, NOPE_DIM:QK_DIM] = (q_rope[:, hd * ROPE_DIM:(hd + 1) * ROPE_DIM] * Q_SCALE).astype(BF16)
        k_ref[0, hd, :, 0:NOPE_DIM] = kva[:, hd * NOPE_DIM:(hd + 1) * NOPE_DIM].astype(BF16)
        k_ref[0, hd, :, NOPE_DIM:QK_DIM] = k_rope
        vt_ref[0, hd] = kva[:, nope_w + hd * V_DIM:nope_w + (hd + 1) * V_DIM].T.astype(BF16)

    qk_ref[0] = jnp.dot(h, win_ref[:, QK_OFF:V_OFF], preferred_element_type=F32)
    vmt_ref[0] = jnp.dot(h, win_ref[:, V_OFF:O_OFF], preferred_element_type=F32).T.astype(BF16)
    o_ref[0] = jnp.dot(h, win_ref[:, O_OFF:G_OFF], preferred_element_type=F32)


def _mixin(x, mod_all, g_pre, win, q_norm, wuq, kv_norm, wukv, cos4, sin4, gate_b, *, layer, n_ctx_tiles):
    bsz, t_all, d = x.shape
    rope_w = ATTN_HEADS * ROPE_DIM
    tile_map = lambda b, t: (b, t, 0)
    head_map = lambda b, t: (b, 0, t, 0)
    time_last_map = lambda b, t: (b, 0, t)
    out_shape = (
        jax.ShapeDtypeStruct((bsz, ATTN_HEADS, t_all, QK_DIM), BF16),
        jax.ShapeDtypeStruct((bsz, ATTN_HEADS, t_all, QK_DIM), BF16),
        jax.ShapeDtypeStruct((bsz, ATTN_HEADS, V_DIM, t_all), BF16),
        jax.ShapeDtypeStruct((bsz, t_all, 2 * MLSTM_WIDTH), F32),
        jax.ShapeDtypeStruct((bsz, MLSTM_WIDTH, t_all), BF16),
        jax.ShapeDtypeStruct((bsz, t_all, MLSTM_WIDTH), F32),
        jax.ShapeDtypeStruct((bsz, GATE_ROWS, t_all), F32),
    )
    return pl.pallas_call(
        _mixin_kernel,
        grid=(bsz, t_all // TILE),
        in_specs=[
            pl.BlockSpec((1, TILE, d), tile_map),
            _mod_spec(d, 0, n_ctx_tiles),
            pl.BlockSpec((None, None, 1, d), lambda b, t: (layer, 1, 0, 0)),
            _const_spec((None, d, IN_W), lambda b, t: (layer, 0, 0)),
            pl.BlockSpec((None, 1, Q_LORA), lambda b, t: (layer, 0, 0)),
            _const_spec((None, Q_LORA, wuq.shape[-1]), lambda b, t: (layer, 0, 0)),
            pl.BlockSpec((None, 1, KV_LORA), lambda b, t: (layer, 0, 0)),
            _const_spec((None, KV_LORA, wukv.shape[-1]), lambda b, t: (layer, 0, 0)),
            pl.BlockSpec((TILE, rope_w), lambda b, t: (t, 0)),
            pl.BlockSpec((TILE, rope_w), lambda b, t: (t, 0)),
            pl.BlockSpec((None, 1, LANES), lambda b, t: (layer, 0, 0)),
        ],
        out_specs=(
            pl.BlockSpec((1, ATTN_HEADS, TILE, QK_DIM), head_map),
            pl.BlockSpec((1, ATTN_HEADS, TILE, QK_DIM), head_map),
            pl.BlockSpec((1, ATTN_HEADS, V_DIM, TILE), lambda b, t: (b, 0, 0, t)),
            pl.BlockSpec((1, TILE, 2 * MLSTM_WIDTH), tile_map),
            pl.BlockSpec((1, MLSTM_WIDTH, TILE), time_last_map),
            pl.BlockSpec((1, TILE, MLSTM_WIDTH), tile_map),
            pl.BlockSpec((1, GATE_ROWS, TILE), time_last_map),
        ),
        out_shape=out_shape,
        compiler_params=_params(2),
        name=f"mixin_l{layer}",
    )(x, mod_all, g_pre, win, q_norm, wuq, kv_norm, wukv, cos4, sin4, gate_b)


KEY_CHUNK = 2 * TILE
ATTN_BUFFERS = 2


def _attn_kernel(q_ref, k_ref, vt_ref, *rest, chunks):
    o_ref, s_scr, p_scr = rest[-3:]
    n_heads = q_ref.shape[1]
    lo, hi = chunks[0][0], chunks[-1][1]
    fold = lambda v: v.reshape(-1, SUBLANES, TILE)
    m8, l8 = {}, {}
    for stage in range(n_heads + 2):
        h_s, h_p, h_v = stage, stage - 1, stage - 2
        do_s, do_p, do_v = h_s < n_heads, 0 <= h_p < n_heads, 0 <= h_v < n_heads
        if do_v:
            acc = jnp.dot(vt_ref[0, h_v, :, lo:hi], p_scr[h_v % ATTN_BUFFERS, lo:hi, :],
                          preferred_element_type=F32)
            l = jnp.sum(l8.pop(h_v), axis=0, keepdims=True)
            o_ref[0, :, h_v * V_DIM:(h_v + 1) * V_DIM] = (acc / l).T
        if do_s:
            q = q_ref[0, h_s]
            m8[h_s] = jnp.full((SUBLANES, TILE), -jnp.inf, F32)
        if do_p:
            m = jnp.max(m8.pop(h_p), axis=0, keepdims=True)
            l8[h_p] = jnp.zeros((SUBLANES, TILE), F32)
        for c_lo, c_hi in chunks:
            if do_s:
                s = lax.dot_general(k_ref[0, h_s, c_lo:c_hi, :], q, (((1,), (1,)), ((), ())),
                                    preferred_element_type=F32)
                s_scr[h_s % ATTN_BUFFERS, c_lo:c_hi, :] = s
                m8[h_s] = jnp.maximum(m8[h_s], jnp.max(fold(s), axis=0))
            if do_p:
                p = jnp.exp2(s_scr[h_p % ATTN_BUFFERS, c_lo:c_hi, :] - m)
                l8[h_p] = l8[h_p] + jnp.sum(fold(p), axis=0)
                p_scr[h_p % ATTN_BUFFERS, c_lo:c_hi, :] = p.astype(BF16)


def _key_chunks(n_keys, width):
    return tuple((lo, min(lo + width, n_keys)) for lo in range(0, n_keys, width))


def _attention(q, k, vt, *, q_tile0, n_q, n_keys, out_tiles, out_tile0, prev, name):
    bsz, n_heads, _, _ = q.shape
    in_specs = [
        pl.BlockSpec((1, n_heads, TILE, QK_DIM), lambda b, i: (b, 0, i + q_tile0, 0)),
        pl.BlockSpec((1, n_heads, n_keys, QK_DIM), lambda b, i: (b, 0, 0, 0)),
        pl.BlockSpec((1, n_heads, V_DIM, n_keys), lambda b, i: (b, 0, 0, 0)),
    ]
    operands = [q, k, vt]
    aliases = {}
    if prev is not None:
        in_specs.append(pl.BlockSpec(memory_space=pl.ANY))
        operands.append(prev)
        aliases = {3: 0}
    return pl.pallas_call(
        functools.partial(_attn_kernel, chunks=_key_chunks(n_keys, KEY_CHUNK)),
        grid=(bsz, n_q),
        in_specs=in_specs,
        out_specs=pl.BlockSpec((1, TILE, n_heads * V_DIM), lambda b, i: (b, i + out_tile0, 0)),
        out_shape=jax.ShapeDtypeStruct((bsz, out_tiles * TILE, n_heads * V_DIM), F32),
        scratch_shapes=[pltpu.VMEM((ATTN_BUFFERS, n_keys, TILE), F32),
                        pltpu.VMEM((ATTN_BUFFERS, n_keys, TILE), BF16)],
        input_output_aliases=aliases,
        compiler_params=_params(2),
        name=name,
    )(*operands)


def _qkprep_kernel(qk_ref, prev_ref, next_ref, cw_ref, cb_ref, qt_ref, k_ref, *, n_ctx_tiles, n_tiles):
    ci = pl.program_id(1)
    x = qk_ref[0]
    row = lax.broadcasted_iota(jnp.int32, (TILE, 1), 0)
    has_prev = jnp.logical_and(ci != 0, ci != n_ctx_tiles)
    has_next = jnp.logical_and(ci != n_ctx_tiles - 1, ci != n_tiles - 1)
    prev_row = jnp.where(has_prev, prev_ref[0, SUBLANES - 1:SUBLANES, :], 0.0)
    next_row = jnp.where(has_next, next_ref[0, 0:1, :], 0.0)
    x_prev = jnp.where(row == 0, prev_row, pltpu.roll(x, 1, axis=0))
    x_next = jnp.where(row == TILE - 1, next_row, pltpu.roll(x, TILE - 1, axis=0))
    u = x_prev * cw_ref[0:1, :] + x * cw_ref[1:2, :] + x_next * cw_ref[2:3, :] + cb_ref[...]
    u = u * jax.nn.sigmoid(u)
    qt_ref[0] = u[:, 0:MLSTM_WIDTH].T.astype(BF16)
    k_ref[0] = (u[:, MLSTM_WIDTH:2 * MLSTM_WIDTH] * (MLSTM_DH ** -0.5)).astype(BF16)


def _qkprep(qk_pre, conv_w, conv_b, *, n_ctx_tiles, layer):
    bsz, t_all, w2 = qk_pre.shape
    n_tiles = t_all // TILE
    halo_per_tile = TILE // SUBLANES
    n_halo = t_all // SUBLANES
    return pl.pallas_call(
        functools.partial(_qkprep_kernel, n_ctx_tiles=n_ctx_tiles, n_tiles=n_tiles),
        grid=(bsz, n_tiles),
        in_specs=[
            pl.BlockSpec((1, TILE, w2), lambda b, t: (b, t, 0)),
            pl.BlockSpec((1, SUBLANES, w2), lambda b, t: (b, jnp.maximum(t * halo_per_tile - 1, 0), 0)),
            pl.BlockSpec((1, SUBLANES, w2), lambda b, t: (b, jnp.minimum((t + 1) * halo_per_tile, n_halo - 1), 0)),
            pl.BlockSpec((None, CONV_K, w2), lambda b, t: (layer, 0, 0)),
            pl.BlockSpec((None, 1, w2), lambda b, t: (layer, 0, 0)),
        ],
        out_specs=(pl.BlockSpec((1, MLSTM_WIDTH, TILE), lambda b, t: (b, 0, t)),
                   pl.BlockSpec((1, TILE, MLSTM_WIDTH), lambda b, t: (b, t, 0))),
        out_shape=(jax.ShapeDtypeStruct((bsz, MLSTM_WIDTH, t_all), BF16),
                   jax.ShapeDtypeStruct((bsz, t_all, MLSTM_WIDTH), BF16)),
        compiler_params=_params(2),
        name=f"qkprep_l{layer}",
    )(qk_pre, qk_pre, qk_pre, conv_w, conv_b)


STATE_ROWS = MLSTM_DH + 2 * SUBLANES
GATE_ROWS = 2 * 3 * SUBLANES
LOG2E = float(np.log2(np.e))


def _log_sigmoid(x):
    return jnp.minimum(x, 0.0) - jnp.log1p(jnp.exp(-jnp.abs(x)))


def _bwd_chunk(j, n_ctx_tiles, n_tiles):
    return jnp.where(j < n_ctx_tiles, n_ctx_tiles - 1 - j, n_tiles - 1 - (j - n_ctx_tiles))


def _lane_scan(x, op, identity, reverse):
    n = x.shape[1]
    lane = lax.broadcasted_iota(jnp.int32, x.shape, 1)
    shift = 1
    while shift < n:
        if reverse:
            moved = jnp.where(lane < n - shift, pltpu.roll(x, n - shift, axis=1), identity)
        else:
            moved = jnp.where(lane >= shift, pltpu.roll(x, shift, axis=1), identity)
        x = op(x, moved)
        shift *= 2
    return x


def _gate_rows(g8, reverse):
    nh = MLSTM_HEADS
    head_rows = lax.broadcasted_iota(jnp.int32, g8.shape, 0) < nh
    b = _lane_scan(_log_sigmoid(g8), jnp.add, 0.0, reverse)
    b = jnp.where(head_rows, pltpu.roll(b, nh, axis=0), 0.0)
    a = jnp.where(head_rows, g8 - b, 0.0)
    return a, b, _lane_scan(a, jnp.maximum, -jnp.inf, reverse)


def _mlstm_direction(reverse, a, b, a_max, k_ref, qt_ref, vt_ref, o_ref, st_scr, m_scr, d):
    nh = MLSTM_HEADS
    last = 0 if reverse else TILE - 1
    m_old = m_scr[d][:, 0:1]
    mx = jnp.maximum(m_old, a_max)
    inter = jnp.exp(m_old - mx)
    e_inv = jnp.exp(-(b + mx))
    mx_last = mx[:, last:last + 1]
    w = jnp.exp(a - mx_last)
    decay = jnp.exp(m_old - mx_last)
    m_scr[d] = jnp.broadcast_to(b[:, last:last + 1] + mx_last, (SUBLANES, LANES))

    s_idx = lax.broadcasted_iota(jnp.int32, (TILE, TILE), 0)
    t_idx = lax.broadcasted_iota(jnp.int32, (TILE, TILE), 1)
    seen = s_idx >= t_idx if reverse else s_idx <= t_idx
    pad_row = lax.broadcasted_iota(jnp.int32, (2 * SUBLANES, TILE), 0) == 0
    ones_rows = jnp.where(pad_row, 1.0, 0.0).astype(BF16)

    for hd in range(nh):
        sl = slice(hd * MLSTM_DH, (hd + 1) * MLSTM_DH)
        k_h = k_ref[0, :, sl]
        qt_h = qt_ref[0, sl, :]
        vt_h = vt_ref[0, sl, :]
        state = st_scr[d * nh + hd]
        a_rep = jnp.broadcast_to(a[hd:hd + 1] * LOG2E, (LANES, TILE)).T
        log_d = jnp.concatenate([a_rep] * (TILE // LANES), axis=1) - mx[hd:hd + 1] * LOG2E
        d_t = jnp.exp2(jnp.where(seen, log_d, -jnp.inf))
        r = jnp.dot(jnp.concatenate([k_h, state.astype(BF16)], axis=0), qt_h, preferred_element_type=F32)
        p = (r[0:TILE] * d_t).astype(BF16)
        nd = jnp.dot(jnp.concatenate([vt_h, ones_rows], axis=0), p, preferred_element_type=F32)
        inter_h = inter[hd:hd + 1]
        den = nd[MLSTM_DH:MLSTM_DH + 1] + inter_h * r[TILE + MLSTM_DH:TILE + MLSTM_DH + 1]
        scale = 1.0 / jnp.maximum(jnp.abs(den), e_inv[hd:hd + 1])
        h_t = (nd[0:MLSTM_DH] + inter_h * r[TILE:TILE + MLSTM_DH]) * scale
        o_ref[0, :, sl] = h_t.T

        w_h = w[hd:hd + 1]
        vw = jnp.concatenate([(vt_h.astype(F32) * w_h).astype(BF16), jnp.where(pad_row, w_h, 0.0).astype(BF16)],
                             axis=0)
        st_scr[d * nh + hd] = decay[hd:hd + 1] * state + jnp.dot(vw, k_h, preferred_element_type=F32)


def _mlstm_kernel(kf_ref, qtf_ref, vtf_ref, gf_ref, kb_ref, qtb_ref, vtb_ref, gb_ref, of_ref, ob_ref,
                  st_scr, m_scr):
    @pl.when(pl.program_id(1) == 0)
    def _():
        st_scr[...] = jnp.zeros_like(st_scr)
        m_scr[...] = jnp.zeros_like(m_scr)

    rows = [slice(i * SUBLANES, (i + 1) * SUBLANES) for i in range(GATE_ROWS // SUBLANES)]
    _mlstm_direction(False, gf_ref[0, rows[0], :], gf_ref[0, rows[1], :], gf_ref[0, rows[2], :],
                     kf_ref, qtf_ref, vtf_ref, of_ref, st_scr, m_scr, 0)
    _mlstm_direction(True, gb_ref[0, rows[3], :], gb_ref[0, rows[4], :], gb_ref[0, rows[5], :],
                     kb_ref, qtb_ref, vtb_ref, ob_ref, st_scr, m_scr, 1)


def _mlstm(k_m, qt_m, vt_m, gates_t, *, n_ctx_tiles, layer):
    bsz, t_all, width = k_m.shape
    n_tiles = t_all // TILE
    bwd = functools.partial(_bwd_chunk, n_ctx_tiles=n_ctx_tiles, n_tiles=n_tiles)

    def specs(chunk):
        return [pl.BlockSpec((1, TILE, width), lambda b, j: (b, chunk(j), 0)),
                pl.BlockSpec((1, width, TILE), lambda b, j: (b, 0, chunk(j))),
                pl.BlockSpec((1, width, TILE), lambda b, j: (b, 0, chunk(j))),
                pl.BlockSpec((1, GATE_ROWS, TILE), lambda b, j: (b, 0, chunk(j)))]

    out_sds = jax.ShapeDtypeStruct((bsz, t_all, width), F32)
    return pl.pallas_call(
        _mlstm_kernel,
        grid=(bsz, n_tiles),
        in_specs=specs(lambda j: j) + specs(bwd),
        out_specs=(pl.BlockSpec((1, TILE, width), lambda b, j: (b, j, 0)),
                   pl.BlockSpec((1, TILE, width), lambda b, j: (b, bwd(j), 0))),
        out_shape=(out_sds, out_sds),
        scratch_shapes=[pltpu.VMEM((2 * MLSTM_HEADS, STATE_ROWS, MLSTM_DH), F32),
                        pltpu.VMEM((2, SUBLANES, LANES), F32)],
        compiler_params=_params(2),
        name=f"mlstm_l{layer}",
    )(k_m, qt_m, vt_m, gates_t, k_m, qt_m, vt_m, gates_t)


def _merge_kernel(x_ref, a_ref, hf_ref, hb_ref, o_ref, mod_ref, ga_ref, gm_ref, gpost_ref, wout_ref, out_ref):
    x = x_ref[0]
    mod = mod_ref[0, 0]
    hm = (hf_ref[0] + hb_ref[0]) * jax.nn.sigmoid(o_ref[0])
    normed = []
    for hd in range(MLSTM_HEADS):
        seg = hm[:, hd * MLSTM_DH:(hd + 1) * MLSTM_DH]
        cen = seg - jnp.mean(seg, axis=-1, keepdims=True)
        normed.append(cen * lax.rsqrt(jnp.mean(cen * cen, axis=-1, keepdims=True) + EPS))
    hm_n = (jnp.concatenate(normed, axis=-1) * gm_ref[...]).astype(BF16)
    a_n = _rms(a_ref[0], ga_ref[...]).astype(BF16)
    y = jnp.dot(a_n, wout_ref[0:ATTN_WIDTH, :], preferred_element_type=F32)
    y = y + jnp.dot(hm_n, wout_ref[ATTN_WIDTH:ATTN_WIDTH + MLSTM_WIDTH, :], preferred_element_type=F32)
    out_ref[0] = x + mod[5:6] * _rms(y, gpost_ref[...])


def _merge(x, a, h_fwd, h_bwd, o, mod_all, g_attn, g_mlstm, g_post, wout, *, layer, n_ctx_tiles, t0):
    bsz, t_all, d = x.shape
    n_tiles = t_all // TILE - t0
    return pl.pallas_call(
        _merge_kernel,
        grid=(bsz, n_tiles),
        in_specs=[
            pl.BlockSpec((1, TILE, d), lambda b, t: (b, t + t0, 0)),
            pl.BlockSpec((1, TILE, ATTN_WIDTH), lambda b, t: (b, t, 0)),
            pl.BlockSpec((1, TILE, MLSTM_WIDTH), lambda b, t: (b, t + t0, 0)),
            pl.BlockSpec((1, TILE, MLSTM_WIDTH), lambda b, t: (b, t + t0, 0)),
            pl.BlockSpec((1, TILE, MLSTM_WIDTH), lambda b, t: (b, t + t0, 0)),
            _mod_spec(d, t0, n_ctx_tiles),
            pl.BlockSpec((None, 1, ATTN_WIDTH), lambda b, t: (layer, 0, 0)),
            pl.BlockSpec((None, 1, MLSTM_WIDTH), lambda b, t: (layer, 0, 0)),
            pl.BlockSpec((None, None, 1, d), lambda b, t: (layer, 1, 0, 0)),
            _const_spec((None, ATTN_WIDTH + MLSTM_WIDTH, d), lambda b, t: (layer, 0, 0)),
        ],
        out_specs=pl.BlockSpec((1, TILE, d), lambda b, t: (b, t, 0)),
        out_shape=jax.ShapeDtypeStruct((bsz, n_tiles * TILE, d), F32),
        compiler_params=_params(2),
        name=f"merge_l{layer}",
    )(x, a, h_fwd, h_bwd, o, mod_all, g_attn, g_mlstm, g_post, wout)


def _half_swap_perm():
    idx = np.arange(ROPE_DIM)
    axis, half, freq = idx // (2 * AXIS_FREQS), (idx // AXIS_FREQS) % 2, idx % AXIS_FREQS
    return axis * 2 * AXIS_FREQS + (1 - half) * AXIS_FREQS + freq


def _prep_w_in(w_in):
    sizes = (Q_LORA, KV_LORA, ROPE_DIM, 2 * MLSTM_WIDTH, MLSTM_WIDTH, MLSTM_WIDTH, N_GATES)
    offs = np.concatenate([[0], np.cumsum(sizes)])
    cq, ckv, kr, qk, v, o, g = (w_in[..., offs[i]:offs[i + 1]] for i in range(len(sizes)))
    g = jnp.pad(g, ((0, 0), (0, 0), (0, LANES - N_GATES)))
    return jnp.concatenate([cq, ckv, kr, kr[..., _half_swap_perm()], qk, v, o, g], axis=-1).astype(BF16)


def _prep_w_uq(w_uq):
    n_layers, q_lora, _ = w_uq.shape
    w = w_uq.reshape(n_layers, q_lora, ATTN_HEADS, QK_DIM)
    nope = w[..., :NOPE_DIM].reshape(n_layers, q_lora, -1)
    rope = w[..., NOPE_DIM:]
    rope_sw = rope[..., _half_swap_perm()]
    return jnp.concatenate([nope, rope.reshape(n_layers, q_lora, -1), rope_sw.reshape(n_layers, q_lora, -1)],
                           axis=-1).astype(BF16)


def _prep_w_ukv(w_ukv):
    n_layers, kv_lora, _ = w_ukv.shape
    w = w_ukv.reshape(n_layers, kv_lora, ATTN_HEADS, NOPE_DIM + V_DIM)
    return jnp.concatenate([w[..., :NOPE_DIM].reshape(n_layers, kv_lora, -1),
                            w[..., NOPE_DIM:].reshape(n_layers, kv_lora, -1)], axis=-1).astype(BF16)


def _rope_tables(n_ctx, n_tok):
    rows = n_tok // GRID_W
    t_row = jnp.repeat(jnp.arange(rows), GRID_W).astype(F32)
    t_col = jnp.tile(jnp.arange(GRID_W), rows).astype(F32)
    inv = ROPE_BASE ** (-jnp.arange(AXIS_FREQS, dtype=F32) / AXIS_FREQS)
    ang_r = t_row[:, None] * inv
    ang_c = t_col[:, None] * inv
    cos = jnp.concatenate([jnp.cos(ang_r), jnp.cos(ang_r), jnp.cos(ang_c), jnp.cos(ang_c)], axis=-1)
    sin = jnp.concatenate([-jnp.sin(ang_r), jnp.sin(ang_r), -jnp.sin(ang_c), jnp.sin(ang_c)], axis=-1)
    cos = jnp.concatenate([jnp.ones((n_ctx, ROPE_DIM), F32), cos], axis=0)
    sin = jnp.concatenate([jnp.zeros((n_ctx, ROPE_DIM), F32), sin], axis=0)
    return jnp.tile(cos, (1, ATTN_HEADS)), jnp.tile(sin, (1, ATTN_HEADS))


def kernel(x, c, ctx, c_ctx, w_mod, b_mod, norm_pre, norm_post, ffn_w_gate, ffn_w_up, ffn_w_down,
           w_in, q_norm, w_uq, kv_norm, w_ukv, attn_out_norm, conv_w, conv_b, gate_b, mlstm_norm, w_out):
    bsz, n_tok, d = x.shape
    n_ctx = ctx.shape[1]
    n_layers = w_mod.shape[0]
    assert n_tok % TILE == 0 and n_ctx % TILE == 0 and n_tok % GRID_W == 0
    n_ctx_tiles = n_ctx // TILE
    n_lat_tiles = n_tok // TILE

    wg = ffn_w_gate.astype(BF16)
    wu = ffn_w_up.astype(BF16)
    wd = ffn_w_down.astype(BF16)
    win = _prep_w_in(w_in)
    wuq = _prep_w_uq(w_uq)
    wukv = _prep_w_ukv(w_ukv)
    wout = w_out.astype(BF16)
    g_pre = norm_pre[:, :, None, :]
    g_post = norm_post[:, :, None, :]
    q_gain = q_norm[:, None, :]
    kv_gain = kv_norm[:, None, :]
    a_gain = attn_out_norm[:, None, :]
    m_gain = mlstm_norm[:, None, :]
    conv_bias = conv_b[:, None, :]
    gate_bias = jnp.pad(gate_b.reshape(n_layers, 1, N_GATES), ((0, 0), (0, 0), (0, LANES - N_GATES)))
    cos4, sin4 = _rope_tables(n_ctx, n_tok)

    c_rows = jnp.zeros((_round_up(bsz + 1, SUBLANES), d), F32).at[:bsz].set(c).at[bsz].set(c_ctx)
    mod = _modulation(c_rows, w_mod, b_mod).reshape(n_layers, -1, N_MOD, d)
    mod_all = jnp.stack([jnp.broadcast_to(mod[:, bsz:bsz + 1], (n_layers, bsz, N_MOD, d)), mod[:, :bsz]], axis=2)

    h = jnp.concatenate([ctx, x], axis=1)
    for l in range(n_layers):
        t0 = n_ctx_tiles if l == n_layers - 1 else 0
        ml = mod_all[l]
        h = _ffn(h, ml, g_pre, g_post, wg, wu, wd, layer=l, j=0, sub=0, n_ctx_tiles=n_ctx_tiles, mod_t0=0)
        q, k, vt, qk_pre, vt_m, o_pre, gates_t = _mixin(h, ml, g_pre, win, q_gain, wuq, kv_gain, wukv,
                                                        cos4, sin4, gate_bias, layer=l, n_ctx_tiles=n_ctx_tiles)
        a = _attention(q, k, vt, q_tile0=n_ctx_tiles, n_q=n_lat_tiles, n_keys=n_ctx + n_tok,
                       out_tiles=n_ctx_tiles + n_lat_tiles - t0, out_tile0=n_ctx_tiles - t0, prev=None,
                       name=f"attention_l{l}")
        if t0 == 0:
            a = _attention(q, k, vt, q_tile0=0, n_q=n_ctx_tiles, n_keys=n_ctx, out_tiles=n_ctx_tiles + n_lat_tiles,
                           out_tile0=0, prev=a, name=f"attention_ctx_l{l}")
        qt_m, k_m = _qkprep(qk_pre, conv_w, conv_bias, n_ctx_tiles=n_ctx_tiles, layer=l)
        h_fwd, h_bwd = _mlstm(k_m, qt_m, vt_m, gates_t, n_ctx_tiles=n_ctx_tiles, layer=l)
        h = _merge(h, a, h_fwd, h_bwd, o_pre, ml, a_gain, m_gain, g_post, wout,
                   layer=l, n_ctx_tiles=n_ctx_tiles, t0=t0)
        h = _ffn(h, ml, g_pre, g_post, wg, wu, wd, layer=l, j=1, sub=2, n_ctx_tiles=n_ctx_tiles, mod_t0=t0)
    return h
```

```python
import functools

import jax
import jax.numpy as jnp
import numpy as np
from jax import lax
from jax.experimental import pallas as pl
from jax.experimental.pallas import tpu as pltpu

F32 = jnp.float32
BF16 = jnp.bfloat16

N_MOD = 9
EPS = 1e-6
GRID_W = 64
ATTN_HEADS = 4
Q_LORA = 256
KV_LORA = 128
NOPE_DIM = 128
ROPE_DIM = 64
AXIS_FREQS = ROPE_DIM // 4
V_DIM = 128
QK_DIM = NOPE_DIM + ROPE_DIM
ATTN_WIDTH = ATTN_HEADS * V_DIM
ROPE_BASE = 10000.0
SM_SCALE = QK_DIM ** -0.5
LOG2E = float(np.log2(np.e))
Q_SCALE = SM_SCALE * LOG2E
MLSTM_HEADS = 4
MLSTM_DH = 128
MLSTM_WIDTH = MLSTM_HEADS * MLSTM_DH
CONV_K = 3
N_GATES = 4 * MLSTM_HEADS

LANES = 128
SUBLANES = 8
MXU_DIM = 256
TILE = MXU_DIM
VMEM_LIMIT = 56 * 1024 * 1024

A_W = Q_LORA + KV_LORA + 2 * ROPE_DIM
QK_OFF = A_W
V_OFF = QK_OFF + 2 * MLSTM_WIDTH
O_OFF = V_OFF + MLSTM_WIDTH
G_OFF = O_OFF + MLSTM_WIDTH
IN_W = G_OFF + LANES


def _round_up(n, m):
    return (n + m - 1) // m * m


def _rms(x, g):
    return x * lax.rsqrt(jnp.mean(x * x, axis=-1, keepdims=True) + EPS) * g


def _sandwich_in(x, mod, sub, g):
    return _rms(x, g) * (1.0 + mod[3 * sub + 1:3 * sub + 2]) + mod[3 * sub:3 * sub + 1]


def _params(n_grid):
    return pltpu.CompilerParams(dimension_semantics=("arbitrary",) * n_grid,
                                vmem_limit_bytes=VMEM_LIMIT)


def _const_spec(shape, index_map):
    return pl.BlockSpec(shape, index_map, pipeline_mode=pl.Buffered(1))


def _mod_spec(d, tile0, n_ctx_tiles):
    return pl.BlockSpec((1, 1, N_MOD, d),
                        lambda b, t: (b, (t + tile0 >= n_ctx_tiles).astype(jnp.int32), 0, 0))


def _split_specs(width, n_ctx_tiles):
    return [pl.BlockSpec((1, TILE, width), lambda b, t: (b, jnp.minimum(t, n_ctx_tiles - 1), 0)),
            pl.BlockSpec((1, TILE, width), lambda b, t: (b, jnp.maximum(t - n_ctx_tiles, 0), 0))]


def _pick_tile(ctx_ref, lat_ref, n_ctx_tiles):
    return jnp.where(pl.program_id(1) < n_ctx_tiles, ctx_ref[0], lat_ref[0])


def _mod_kernel(c_ref, w_ref, b_ref, o_ref):
    c = c_ref[...]
    h = (c * jax.nn.sigmoid(c)).astype(BF16)
    o_ref[0] = jnp.dot(h, w_ref[0].astype(BF16), preferred_element_type=F32) + b_ref[0]


def _modulation(c_rows, w_mod, b_mod):
    n_layers, d, n_out = w_mod.shape
    rows = c_rows.shape[0]
    n_blk = N_MOD * LANES
    return pl.pallas_call(
        _mod_kernel,
        grid=(n_layers, n_out // n_blk),
        in_specs=[pl.BlockSpec((rows, d), lambda l, j: (0, 0)),
                  pl.BlockSpec((1, d, n_blk), lambda l, j: (l, 0, j)),
                  pl.BlockSpec((1, 1, n_blk), lambda l, j: (l, 0, j))],
        out_specs=pl.BlockSpec((1, rows, n_blk), lambda l, j: (l, 0, j)),
        out_shape=jax.ShapeDtypeStruct((n_layers, rows, n_out), F32),
        compiler_params=_params(2),
        name="modulation",
    )(c_rows, w_mod, b_mod.reshape(n_layers, 1, n_out))


def _ffn_body(x, mod, sub, g_pre, g_post, wg_ref, wu_ref, wd_ref, a_ref):
    h = _sandwich_in(x, mod, sub, g_pre).astype(BF16)
    ff = a_ref.shape[1]
    for lo in range(0, ff, MXU_DIM):
        hi = min(lo + MXU_DIM, ff)
        g = jnp.dot(h, wg_ref[:, lo:hi], preferred_element_type=F32)
        u = jnp.dot(h, wu_ref[:, lo:hi], preferred_element_type=F32)
        a_ref[:, lo:hi] = (g * jax.nn.sigmoid(g) * u).astype(BF16)
    ff_main = ff // MXU_DIM * MXU_DIM
    y = jnp.dot(a_ref[:, 0:ff_main], wd_ref[0:ff_main, :], preferred_element_type=F32)
    if ff_main < ff:
        y = y + jnp.dot(a_ref[:, ff_main:ff], wd_ref[ff_main:ff, :], preferred_element_type=F32)
    return x + 0.5 * mod[3 * sub + 2:3 * sub + 3] * _rms(y, g_post)


def _ffn_kernel(*refs, sub, split_ctx_tiles):
    n_src = 2 if split_ctx_tiles else 1
    mod_ref, gpre_ref, gpost_ref, wg_ref, wu_ref, wd_ref, o_ref, a_ref = refs[n_src:]
    x = _pick_tile(refs[0], refs[1], split_ctx_tiles) if split_ctx_tiles else refs[0][0]
    o_ref[0] = _ffn_body(x, mod_ref[0, 0], sub, gpre_ref[...], gpost_ref[...], wg_ref, wu_ref, wd_ref, a_ref)


def _ffn_weight_specs(layer, j, sub, d, ff):
    return [
        pl.BlockSpec((None, None, 1, d), lambda b, t: (layer, sub, 0, 0)),
        pl.BlockSpec((None, None, 1, d), lambda b, t: (layer, sub, 0, 0)),
        _const_spec((None, None, d, ff), lambda b, t: (layer, j, 0, 0)),
        _const_spec((None, None, d, ff), lambda b, t: (layer, j, 0, 0)),
        _const_spec((None, None, ff, d), lambda b, t: (layer, j, 0, 0)),
    ]


def _ffn(src, mod_all, g_pre, g_post, wg, wu, wd, *, layer, j, sub, n_ctx_tiles):
    split = isinstance(src, tuple)
    srcs = src if split else (src,)
    bsz, _, d = srcs[-1].shape
    t_all = sum(s.shape[1] for s in srcs)
    ff = wg.shape[-1]
    src_specs = _split_specs(d, n_ctx_tiles) if split else [pl.BlockSpec((1, TILE, d), lambda b, t: (b, t, 0))]
    return pl.pallas_call(
        functools.partial(_ffn_kernel, sub=sub, split_ctx_tiles=n_ctx_tiles if split else 0),
        grid=(bsz, t_all // TILE),
        in_specs=src_specs + [_mod_spec(d, 0, n_ctx_tiles)] + _ffn_weight_specs(layer, j, sub, d, ff),
        out_specs=pl.BlockSpec((1, TILE, d), lambda b, t: (b, t, 0)),
        out_shape=jax.ShapeDtypeStruct((bsz, t_all, d), F32),
        scratch_shapes=[pltpu.VMEM((TILE, ff), BF16)],
        compiler_params=_params(2),
        name=f"ffn_l{layer}_{j}",
    )(*srcs, mod_all, g_pre, g_post, wg, wu, wd)


GATE_ROWS = 2 * 3 * SUBLANES


def _log_sigmoid(x):
    return jnp.minimum(x, 0.0) - jnp.log1p(jnp.exp(-jnp.abs(x)))


def _lane_scan(x, op, identity, reverse):
    n = x.shape[1]
    lane = lax.broadcasted_iota(jnp.int32, x.shape, 1)
    shift = 1
    while shift < n:
        if reverse:
            moved = jnp.where(lane < n - shift, pltpu.roll(x, n - shift, axis=1), identity)
        else:
            moved = jnp.where(lane >= shift, pltpu.roll(x, shift, axis=1), identity)
        x = op(x, moved)
        shift *= 2
    return x


def _gate_rows(g8, reverse):
    nh = MLSTM_HEADS
    head_rows = lax.broadcasted_iota(jnp.int32, g8.shape, 0) < nh
    b = _lane_scan(_log_sigmoid(g8), jnp.add, 0.0, reverse)
    b = jnp.where(head_rows, pltpu.roll(b, nh, axis=0), 0.0)
    a = jnp.where(head_rows, g8 - b, 0.0)
    return a, b, _lane_scan(a, jnp.maximum, -jnp.inf, reverse)


def _mixin_kernel(x_ref, xp_ref, xn_ref, mod_ref, gpre_ref, win_ref, qn_ref, wuq_ref, kvn_ref, wukv_ref,
                  cos_ref, sin_ref, gb_ref, cw_ref, cb_ref,
                  q_ref, k_ref, vt_ref, qtm_ref, km_ref, vtm_ref, o_ref, gt_ref, *, n_ctx_tiles, n_tiles):
    ci = pl.program_id(1)
    mod = mod_ref[0, 0]
    h = _sandwich_in(x_ref[0], mod, 1, gpre_ref[...]).astype(BF16)
    cos4 = cos_ref[...]
    sin4 = sin_ref[...]

    gates = jnp.dot(h, win_ref[:, G_OFF:IN_W], preferred_element_type=F32) + gb_ref[...]
    g_t = gates.T[0:N_GATES, :]
    half = N_GATES // 2
    gt_ref[0] = jnp.concatenate(_gate_rows(g_t[0:half], False) + _gate_rows(g_t[half:N_GATES], True), axis=0)

    pa = jnp.dot(h, win_ref[:, 0:A_W], preferred_element_type=F32)
    cq = pa[:, 0:Q_LORA]
    ckv = pa[:, Q_LORA:Q_LORA + KV_LORA]
    kr = pa[:, Q_LORA + KV_LORA:Q_LORA + KV_LORA + ROPE_DIM]
    kr_sw = pa[:, Q_LORA + KV_LORA + ROPE_DIM:A_W]
    k_rope = (kr * cos4[:, 0:ROPE_DIM] + kr_sw * sin4[:, 0:ROPE_DIM]).astype(BF16)

    nope_w = ATTN_HEADS * NOPE_DIM
    rope_w = ATTN_HEADS * ROPE_DIM
    qa = jnp.dot(_rms(cq, qn_ref[...]).astype(BF16), wuq_ref[...], preferred_element_type=F32)
    q_rope = qa[:, nope_w:nope_w + rope_w] * cos4 + qa[:, nope_w + rope_w:nope_w + 2 * rope_w] * sin4
    kva = jnp.dot(_rms(ckv, kvn_ref[...]).astype(BF16), wukv_ref[...], preferred_element_type=F32)
    for hd in range(ATTN_HEADS):
        q_ref[0, hd, :, 0:NOPE_DIM] = (qa[:, hd * NOPE_DIM:(hd + 1) * NOPE_DIM] * Q_SCALE).astype(BF16)
        q_ref[0, hd, :, NOPE_DIM:QK_DIM] = (q_rope[:, hd * ROPE_DIM:(hd + 1) * ROPE_DIM] * Q_SCALE).astype(BF16)
        k_ref[0, hd, :, 0:NOPE_DIM] = kva[:, hd * NOPE_DIM:(hd + 1) * NOPE_DIM].astype(BF16)
        k_ref[0, hd, :, NOPE_DIM:QK_DIM] = k_rope
        vt_ref[0, hd] = kva[:, nope_w + hd * V_DIM:nope_w + (hd + 1) * V_DIM].T.astype(BF16)

    vtm_ref[0] = jnp.dot(h, win_ref[:, V_OFF:O_OFF], preferred_element_type=F32).T.astype(BF16)
    o_ref[0] = jnp.dot(h, win_ref[:, O_OFF:G_OFF], preferred_element_type=F32)

    halo = jnp.concatenate([xp_ref[0], xn_ref[0]], axis=0)
    h_ext = jnp.concatenate([h, _sandwich_in(halo, mod, 1, gpre_ref[...]).astype(BF16)], axis=0)
    pqk = jnp.dot(h_ext, win_ref[:, QK_OFF:V_OFF], preferred_element_type=F32)
    x = pqk[0:TILE]
    row = lax.broadcasted_iota(jnp.int32, (TILE, 1), 0)
    has_prev = jnp.logical_and(ci != 0, ci != n_ctx_tiles)
    has_next = jnp.logical_and(ci != n_ctx_tiles - 1, ci != n_tiles - 1)
    prev_row = jnp.where(has_prev, pqk[TILE + SUBLANES - 1:TILE + SUBLANES], 0.0)
    next_row = jnp.where(has_next, pqk[TILE + SUBLANES:TILE + SUBLANES + 1], 0.0)
    x_prev = jnp.where(row == 0, prev_row, pltpu.roll(x, 1, axis=0))
    x_next = jnp.where(row == TILE - 1, next_row, pltpu.roll(x, TILE - 1, axis=0))
    u = x_prev * cw_ref[0:1, :] + x * cw_ref[1:2, :] + x_next * cw_ref[2:3, :] + cb_ref[...]
    u = u * jax.nn.sigmoid(u)
    qtm_ref[0] = u[:, 0:MLSTM_WIDTH].T.astype(BF16)
    km_ref[0] = (u[:, MLSTM_WIDTH:2 * MLSTM_WIDTH] * (MLSTM_DH ** -0.5)).astype(BF16)


def _mixin(x, mod_all, g_pre, win, q_norm, wuq, kv_norm, wukv, cos4, sin4, gate_b, conv_w, conv_b,
           *, layer, n_ctx_tiles):
    bsz, t_all, d = x.shape
    n_tiles = t_all // TILE
    rope_w = ATTN_HEADS * ROPE_DIM
    halo_per_tile = TILE // SUBLANES
    n_halo = t_all // SUBLANES
    tile_map = lambda b, t: (b, t, 0)
    head_map = lambda b, t: (b, 0, t, 0)
    time_last_map = lambda b, t: (b, 0, t)
    out_shape = (
        jax.ShapeDtypeStruct((bsz, ATTN_HEADS, t_all, QK_DIM), BF16),
        jax.ShapeDtypeStruct((bsz, ATTN_HEADS, t_all, QK_DIM), BF16),
        jax.ShapeDtypeStruct((bsz, ATTN_HEADS, V_DIM, t_all), BF16),
        jax.ShapeDtypeStruct((bsz, MLSTM_WIDTH, t_all), BF16),
        jax.ShapeDtypeStruct((bsz, t_all, MLSTM_WIDTH), BF16),
        jax.ShapeDtypeStruct((bsz, MLSTM_WIDTH, t_all), BF16),
        jax.ShapeDtypeStruct((bsz, t_all, MLSTM_WIDTH), F32),
        jax.ShapeDtypeStruct((bsz, GATE_ROWS, t_all), F32),
    )
    return pl.pallas_call(
        functools.partial(_mixin_kernel, n_ctx_tiles=n_ctx_tiles, n_tiles=n_tiles),
        grid=(bsz, n_tiles),
        in_specs=[
            pl.BlockSpec((1, TILE, d), tile_map),
            pl.BlockSpec((1, SUBLANES, d), lambda b, t: (b, jnp.maximum(t * halo_per_tile - 1, 0), 0)),
            pl.BlockSpec((1, SUBLANES, d), lambda b, t: (b, jnp.minimum((t + 1) * halo_per_tile, n_halo - 1), 0)),
            _mod_spec(d, 0, n_ctx_tiles),
            pl.BlockSpec((None, None, 1, d), lambda b, t: (layer, 1, 0, 0)),
            _const_spec((None, d, IN_W), lambda b, t: (layer, 0, 0)),
            pl.BlockSpec((None, 1, Q_LORA), lambda b, t: (layer, 0, 0)),
            _const_spec((None, Q_LORA, wuq.shape[-1]), lambda b, t: (layer, 0, 0)),
            pl.BlockSpec((None, 1, KV_LORA), lambda b, t: (layer, 0, 0)),
            _const_spec((None, KV_LORA, wukv.shape[-1]), lambda b, t: (layer, 0, 0)),
            pl.BlockSpec((TILE, rope_w), lambda b, t: (t, 0)),
            pl.BlockSpec((TILE, rope_w), lambda b, t: (t, 0)),
            pl.BlockSpec((None, 1, LANES), lambda b, t: (layer, 0, 0)),
            pl.BlockSpec((None, CONV_K, 2 * MLSTM_WIDTH), lambda b, t: (layer, 0, 0)),
            pl.BlockSpec((None, 1, 2 * MLSTM_WIDTH), lambda b, t: (layer, 0, 0)),
        ],
        out_specs=(
            pl.BlockSpec((1, ATTN_HEADS, TILE, QK_DIM), head_map),
            pl.BlockSpec((1, ATTN_HEADS, TILE, QK_DIM), head_map),
            pl.BlockSpec((1, ATTN_HEADS, V_DIM, TILE), lambda b, t: (b, 0, 0, t)),
            pl.BlockSpec((1, MLSTM_WIDTH, TILE), time_last_map),
            pl.BlockSpec((1, TILE, MLSTM_WIDTH), tile_map),
            pl.BlockSpec((1, MLSTM_WIDTH, TILE), time_last_map),
            pl.BlockSpec((1, TILE, MLSTM_WIDTH), tile_map),
            pl.BlockSpec((1, GATE_ROWS, TILE), time_last_map),
        ),
        out_shape=out_shape,
        compiler_params=_params(2),
        name=f"mixin_l{layer}",
    )(x, x, x, mod_all, g_pre, win, q_norm, wuq, kv_norm, wukv, cos4, sin4, gate_b, conv_w, conv_b)


KEY_CHUNK = 2 * TILE
ATTN_BUFFERS = 2


def _attn_kernel(q_ref, k_ref, vt_ref, o_ref, s_scr, p_scr, *, chunks):
    n_heads = q_ref.shape[1]
    lo, hi = chunks[0][0], chunks[-1][1]
    fold = lambda v: v.reshape(-1, SUBLANES, TILE)
    m8, l8 = {}, {}
    for stage in range(n_heads + 2):
        h_s, h_p, h_v = stage, stage - 1, stage - 2
        do_s, do_p, do_v = h_s < n_heads, 0 <= h_p < n_heads, 0 <= h_v < n_heads
        if do_v:
            acc = jnp.dot(vt_ref[0, h_v, :, lo:hi], p_scr[h_v % ATTN_BUFFERS, lo:hi, :],
                          preferred_element_type=F32)
            l = jnp.sum(l8.pop(h_v), axis=0, keepdims=True)
            o_ref[0, :, h_v * V_DIM:(h_v + 1) * V_DIM] = (acc / l).T
        if do_s:
            q = q_ref[0, h_s]
            m8[h_s] = jnp.full((SUBLANES, TILE), -jnp.inf, F32)
        if do_p:
            m = jnp.max(m8.pop(h_p), axis=0, keepdims=True)
            l8[h_p] = jnp.zeros((SUBLANES, TILE), F32)
        for c_lo, c_hi in chunks:
            if do_s:
                s = lax.dot_general(k_ref[0, h_s, c_lo:c_hi, :], q, (((1,), (1,)), ((), ())),
                                    preferred_element_type=F32)
                s_scr[h_s % ATTN_BUFFERS, c_lo:c_hi, :] = s
                m8[h_s] = jnp.maximum(m8[h_s], jnp.max(fold(s), axis=0))
            if do_p:
                p = jnp.exp2(s_scr[h_p % ATTN_BUFFERS, c_lo:c_hi, :] - m)
                l8[h_p] = l8[h_p] + jnp.sum(fold(p), axis=0)
                p_scr[h_p % ATTN_BUFFERS, c_lo:c_hi, :] = p.astype(BF16)


def _key_chunks(n_keys, width):
    return tuple((lo, min(lo + width, n_keys)) for lo in range(0, n_keys, width))


def _attention(q, k, vt, *, q_tile0, n_q, n_keys, name):
    bsz, n_heads, _, _ = q.shape
    return pl.pallas_call(
        functools.partial(_attn_kernel, chunks=_key_chunks(n_keys, KEY_CHUNK)),
        grid=(bsz, n_q),
        in_specs=[
            pl.BlockSpec((1, n_heads, TILE, QK_DIM), lambda b, i: (b, 0, i + q_tile0, 0)),
            pl.BlockSpec((1, n_heads, n_keys, QK_DIM), lambda b, i: (b, 0, 0, 0)),
            pl.BlockSpec((1, n_heads, V_DIM, n_keys), lambda b, i: (b, 0, 0, 0)),
        ],
        out_specs=pl.BlockSpec((1, TILE, n_heads * V_DIM), lambda b, i: (b, i, 0)),
        out_shape=jax.ShapeDtypeStruct((bsz, n_q * TILE, n_heads * V_DIM), F32),
        scratch_shapes=[pltpu.VMEM((ATTN_BUFFERS, n_keys, TILE), F32),
                        pltpu.VMEM((ATTN_BUFFERS, n_keys, TILE), BF16)],
        compiler_params=_params(2),
        name=name,
    )(q, k, vt)


STATE_ROWS = MLSTM_DH + 2 * SUBLANES


def _bwd_chunk(j, n_ctx_tiles, n_tiles):
    return jnp.where(j < n_ctx_tiles, n_ctx_tiles - 1 - j, n_tiles - 1 - (j - n_ctx_tiles))


def _mlstm_direction(reverse, a, b, a_max, k_ref, qt_ref, vt_ref, o_ref, st_scr, m_scr, d):
    nh = MLSTM_HEADS
    last = 0 if reverse else TILE - 1
    m_old = m_scr[d][:, 0:1]
    mx = jnp.maximum(m_old, a_max)
    inter = jnp.exp(m_old - mx)
    e_inv = jnp.exp(-(b + mx))
    mx_last = mx[:, last:last + 1]
    w = jnp.exp(a - mx_last)
    decay = jnp.exp(m_old - mx_last)
    m_scr[d] = jnp.broadcast_to(b[:, last:last + 1] + mx_last, (SUBLANES, LANES))

    s_idx = lax.broadcasted_iota(jnp.int32, (TILE, TILE), 0)
    t_idx = lax.broadcasted_iota(jnp.int32, (TILE, TILE), 1)
    seen = s_idx >= t_idx if reverse else s_idx <= t_idx
    pad_row = lax.broadcasted_iota(jnp.int32, (2 * SUBLANES, TILE), 0) == 0
    ones_rows = jnp.where(pad_row, 1.0, 0.0).astype(BF16)

    for hd in range(nh):
        sl = slice(hd * MLSTM_DH, (hd + 1) * MLSTM_DH)
        k_h = k_ref[0, :, sl]
        qt_h = qt_ref[0, sl, :]
        vt_h = vt_ref[0, sl, :]
        state = st_scr[d * nh + hd]
        a_rep = jnp.broadcast_to(a[hd:hd + 1] * LOG2E, (LANES, TILE)).T
        log_d = jnp.concatenate([a_rep] * (TILE // LANES), axis=1) - mx[hd:hd + 1] * LOG2E
        d_t = jnp.exp2(jnp.where(seen, log_d, -jnp.inf))
        r = jnp.dot(jnp.concatenate([k_h, state.astype(BF16)], axis=0), qt_h, preferred_element_type=F32)
        p = (r[0:TILE] * d_t).astype(BF16)
        nd = jnp.dot(jnp.concatenate([vt_h, ones_rows], axis=0), p, preferred_element_type=F32)
        inter_h = inter[hd:hd + 1]
        den = nd[MLSTM_DH:MLSTM_DH + 1] + inter_h * r[TILE + MLSTM_DH:TILE + MLSTM_DH + 1]
        scale = 1.0 / jnp.maximum(jnp.abs(den), e_inv[hd:hd + 1])
        h_t = (nd[0:MLSTM_DH] + inter_h * r[TILE:TILE + MLSTM_DH]) * scale
        o_ref[0, :, sl] = h_t.T

        w_h = w[hd:hd + 1]
        vw = jnp.concatenate([(vt_h.astype(F32) * w_h).astype(BF16), jnp.where(pad_row, w_h, 0.0).astype(BF16)],
                             axis=0)
        st_scr[d * nh + hd] = decay[hd:hd + 1] * state + jnp.dot(vw, k_h, preferred_element_type=F32)


def _mlstm_kernel(kf_ref, qtf_ref, vtf_ref, gf_ref, kb_ref, qtb_ref, vtb_ref, gb_ref, of_ref, ob_ref,
                  st_scr, m_scr):
    @pl.when(pl.program_id(1) == 0)
    def _():
        st_scr[...] = jnp.zeros_like(st_scr)
        m_scr[...] = jnp.zeros_like(m_scr)

    rows = [slice(i * SUBLANES, (i + 1) * SUBLANES) for i in range(GATE_ROWS // SUBLANES)]
    _mlstm_direction(False, gf_ref[0, rows[0], :], gf_ref[0, rows[1], :], gf_ref[0, rows[2], :],
                     kf_ref, qtf_ref, vtf_ref, of_ref, st_scr, m_scr, 0)
    _mlstm_direction(True, gb_ref[0, rows[3], :], gb_ref[0, rows[4], :], gb_ref[0, rows[5], :],
                     kb_ref, qtb_ref, vtb_ref, ob_ref, st_scr, m_scr, 1)


def _mlstm(k_m, qt_m, vt_m, gates_t, *, n_ctx_tiles, layer):
    bsz, t_all, width = k_m.shape
    n_tiles = t_all // TILE
    bwd = functools.partial(_bwd_chunk, n_ctx_tiles=n_ctx_tiles, n_tiles=n_tiles)

    def specs(chunk):
        return [pl.BlockSpec((1, TILE, width), lambda b, j: (b, chunk(j), 0)),
                pl.BlockSpec((1, width, TILE), lambda b, j: (b, 0, chunk(j))),
                pl.BlockSpec((1, width, TILE), lambda b, j: (b, 0, chunk(j))),
                pl.BlockSpec((1, GATE_ROWS, TILE), lambda b, j: (b, 0, chunk(j)))]

    out_sds = jax.ShapeDtypeStruct((bsz, t_all, width), F32)
    return pl.pallas_call(
        _mlstm_kernel,
        grid=(bsz, n_tiles),
        in_specs=specs(lambda j: j) + specs(bwd),
        out_specs=(pl.BlockSpec((1, TILE, width), lambda b, j: (b, j, 0)),
                   pl.BlockSpec((1, TILE, width), lambda b, j: (b, bwd(j), 0))),
        out_shape=(out_sds, out_sds),
        scratch_shapes=[pltpu.VMEM((2 * MLSTM_HEADS, STATE_ROWS, MLSTM_DH), F32),
                        pltpu.VMEM((2, SUBLANES, LANES), F32)],
        compiler_params=_params(2),
        name=f"mlstm_l{layer}",
    )(k_m, qt_m, vt_m, gates_t, k_m, qt_m, vt_m, gates_t)


def _merge_body(x, a, hm, o, gate, g_attn, g_mlstm, g_post, wout_ref):
    hm = hm * jax.nn.sigmoid(o)
    normed = []
    for hd in range(MLSTM_HEADS):
        seg = hm[:, hd * MLSTM_DH:(hd + 1) * MLSTM_DH]
        cen = seg - jnp.mean(seg, axis=-1, keepdims=True)
        normed.append(cen * lax.rsqrt(jnp.mean(cen * cen, axis=-1, keepdims=True) + EPS))
    hm_n = (jnp.concatenate(normed, axis=-1) * g_mlstm).astype(BF16)
    a_n = _rms(a, g_attn).astype(BF16)
    y = jnp.dot(a_n, wout_ref[0:ATTN_WIDTH, :], preferred_element_type=F32)
    y = y + jnp.dot(hm_n, wout_ref[ATTN_WIDTH:ATTN_WIDTH + MLSTM_WIDTH, :], preferred_element_type=F32)
    return x + gate * _rms(y, g_post)


def _merge_ffn_kernel(*refs, split_ctx_tiles):
    n_a = 2 if split_ctx_tiles else 1
    x_ref = refs[0]
    (hf_ref, hb_ref, o_ref, mod_ref, ga_ref, gm_ref, gpost1_ref, wout_ref,
     gpre2_ref, gpost2_ref, wg_ref, wu_ref, wd_ref, out_ref, a_scr) = refs[1 + n_a:]
    a = _pick_tile(refs[1], refs[2], split_ctx_tiles) if split_ctx_tiles else refs[1][0]
    mod = mod_ref[0, 0]
    x1 = _merge_body(x_ref[0], a, hf_ref[0] + hb_ref[0], o_ref[0], mod[5:6], ga_ref[...], gm_ref[...],
                     gpost1_ref[...], wout_ref)
    out_ref[0] = _ffn_body(x1, mod, 2, gpre2_ref[...], gpost2_ref[...], wg_ref, wu_ref, wd_ref, a_scr)


def _merge_ffn(x, a, h_fwd, h_bwd, o, mod_all, g_attn, g_mlstm, g_pre, g_post, wout, wg, wu, wd,
               *, layer, n_ctx_tiles, t0):
    bsz, t_all, d = x.shape
    n_tiles = t_all // TILE - t0
    ff = wg.shape[-1]
    split = isinstance(a, tuple)
    a_srcs = a if split else (a,)
    a_specs = (_split_specs(ATTN_WIDTH, n_ctx_tiles) if split
               else [pl.BlockSpec((1, TILE, ATTN_WIDTH), lambda b, t: (b, t, 0))])
    stream_map = lambda b, t: (b, t + t0, 0)
    return pl.pallas_call(
        functools.partial(_merge_ffn_kernel, split_ctx_tiles=n_ctx_tiles if split else 0),
        grid=(bsz, n_tiles),
        in_specs=[pl.BlockSpec((1, TILE, d), stream_map)] + a_specs + [
            pl.BlockSpec((1, TILE, MLSTM_WIDTH), stream_map),
            pl.BlockSpec((1, TILE, MLSTM_WIDTH), stream_map),
            pl.BlockSpec((1, TILE, MLSTM_WIDTH), stream_map),
            _mod_spec(d, t0, n_ctx_tiles),
            pl.BlockSpec((None, 1, ATTN_WIDTH), lambda b, t: (layer, 0, 0)),
            pl.BlockSpec((None, 1, MLSTM_WIDTH), lambda b, t: (layer, 0, 0)),
            pl.BlockSpec((None, None, 1, d), lambda b, t: (layer, 1, 0, 0)),
            _const_spec((None, ATTN_WIDTH + MLSTM_WIDTH, d), lambda b, t: (layer, 0, 0)),
        ] + _ffn_weight_specs(layer, 1, 2, d, ff),
        out_specs=pl.BlockSpec((1, TILE, d), lambda b, t: (b, t, 0)),
        out_shape=jax.ShapeDtypeStruct((bsz, n_tiles * TILE, d), F32),
        scratch_shapes=[pltpu.VMEM((TILE, ff), BF16)],
        compiler_params=_params(2),
        name=f"merge_ffn_l{layer}",
    )(x, *a_srcs, h_fwd, h_bwd, o, mod_all, g_attn, g_mlstm, g_post, wout, g_pre, g_post, wg, wu, wd)


def _half_swap_perm():
    idx = np.arange(ROPE_DIM)
    axis, half, freq = idx // (2 * AXIS_FREQS), (idx // AXIS_FREQS) % 2, idx % AXIS_FREQS
    return axis * 2 * AXIS_FREQS + (1 - half) * AXIS_FREQS + freq


def _prep_w_in(w_in):
    sizes = (Q_LORA, KV_LORA, ROPE_DIM, 2 * MLSTM_WIDTH, MLSTM_WIDTH, MLSTM_WIDTH, N_GATES)
    offs = np.concatenate([[0], np.cumsum(sizes)])
    cq, ckv, kr, qk, v, o, g = (w_in[..., offs[i]:offs[i + 1]] for i in range(len(sizes)))
    g = jnp.pad(g, ((0, 0), (0, 0), (0, LANES - N_GATES)))
    return jnp.concatenate([cq, ckv, kr, kr[..., _half_swap_perm()], qk, v, o, g], axis=-1).astype(BF16)


def _prep_w_uq(w_uq):
    n_layers, q_lora, _ = w_uq.shape
    w = w_uq.reshape(n_layers, q_lora, ATTN_HEADS, QK_DIM)
    nope = w[..., :NOPE_DIM].reshape(n_layers, q_lora, -1)
    rope = w[..., NOPE_DIM:]
    rope_sw = rope[..., _half_swap_perm()]
    return jnp.concatenate([nope, rope.reshape(n_layers, q_lora, -1), rope_sw.reshape(n_layers, q_lora, -1)],
                           axis=-1).astype(BF16)


def _prep_w_ukv(w_ukv):
    n_layers, kv_lora, _ = w_ukv.shape
    w = w_ukv.reshape(n_layers, kv_lora, ATTN_HEADS, NOPE_DIM + V_DIM)
    return jnp.concatenate([w[..., :NOPE_DIM].reshape(n_layers, kv_lora, -1),
                            w[..., NOPE_DIM:].reshape(n_layers, kv_lora, -1)], axis=-1).astype(BF16)


def _rope_tables(n_ctx, n_tok):
    rows = n_tok // GRID_W
    t_row = jnp.repeat(jnp.arange(rows), GRID_W).astype(F32)
    t_col = jnp.tile(jnp.arange(GRID_W), rows).astype(F32)
    inv = ROPE_BASE ** (-jnp.arange(AXIS_FREQS, dtype=F32) / AXIS_FREQS)
    ang_r = t_row[:, None] * inv
    ang_c = t_col[:, None] * inv
    cos = jnp.concatenate([jnp.cos(ang_r), jnp.cos(ang_r), jnp.cos(ang_c), jnp.cos(ang_c)], axis=-1)
    sin = jnp.concatenate([-jnp.sin(ang_r), jnp.sin(ang_r), -jnp.sin(ang_c), jnp.sin(ang_c)], axis=-1)
    cos = jnp.concatenate([jnp.ones((n_ctx, ROPE_DIM), F32), cos], axis=0)
    sin = jnp.concatenate([jnp.zeros((n_ctx, ROPE_DIM), F32), sin], axis=0)
    return jnp.tile(cos, (1, ATTN_HEADS)), jnp.tile(sin, (1, ATTN_HEADS))


def kernel(x, c, ctx, c_ctx, w_mod, b_mod, norm_pre, norm_post, ffn_w_gate, ffn_w_up, ffn_w_down,
           w_in, q_norm, w_uq, kv_norm, w_ukv, attn_out_norm, conv_w, conv_b, gate_b, mlstm_norm, w_out):
    bsz, n_tok, d = x.shape
    n_ctx = ctx.shape[1]
    n_layers = w_mod.shape[0]
    assert n_tok % TILE == 0 and n_ctx % TILE == 0 and n_tok % GRID_W == 0
    n_ctx_tiles = n_ctx // TILE
    n_lat_tiles = n_tok // TILE

    wg = ffn_w_gate.astype(BF16)
    wu = ffn_w_up.astype(BF16)
    wd = ffn_w_down.astype(BF16)
    win = _prep_w_in(w_in)
    wuq = _prep_w_uq(w_uq)
    wukv = _prep_w_ukv(w_ukv)
    wout = w_out.astype(BF16)
    g_pre = norm_pre[:, :, None, :]
    g_post = norm_post[:, :, None, :]
    q_gain = q_norm[:, None, :]
    kv_gain = kv_norm[:, None, :]
    a_gain = attn_out_norm[:, None, :]
    m_gain = mlstm_norm[:, None, :]
    conv_bias = conv_b[:, None, :]
    gate_bias = jnp.pad(gate_b.reshape(n_layers, 1, N_GATES), ((0, 0), (0, 0), (0, LANES - N_GATES)))
    cos4, sin4 = _rope_tables(n_ctx, n_tok)

    c_rows = jnp.zeros((_round_up(bsz + 1, SUBLANES), d), F32).at[:bsz].set(c).at[bsz].set(c_ctx)
    mod = _modulation(c_rows, w_mod, b_mod).reshape(n_layers, -1, N_MOD, d)
    mod_all = jnp.stack([jnp.broadcast_to(mod[:, bsz:bsz + 1], (n_layers, bsz, N_MOD, d)), mod[:, :bsz]], axis=2)

    h = (ctx, x)
    for l in range(n_layers):
        t0 = n_ctx_tiles if l == n_layers - 1 else 0
        ml = mod_all[l]
        h = _ffn(h, ml, g_pre, g_post, wg, wu, wd, layer=l, j=0, sub=0, n_ctx_tiles=n_ctx_tiles)
        q, k, vt, qt_m, k_m, vt_m, o_pre, gates_t = _mixin(
            h, ml, g_pre, win, q_gain, wuq, kv_gain, wukv, cos4, sin4, gate_bias, conv_w, conv_bias,
            layer=l, n_ctx_tiles=n_ctx_tiles)
        a = _attention(q, k, vt, q_tile0=n_ctx_tiles, n_q=n_lat_tiles, n_keys=n_ctx + n_tok,
                       name=f"attention_l{l}")
        if t0 == 0:
            a = (_attention(q, k, vt, q_tile0=0, n_q=n_ctx_tiles, n_keys=n_ctx, name=f"attention_ctx_l{l}"), a)
        h_fwd, h_bwd = _mlstm(k_m, qt_m, vt_m, gates_t, n_ctx_tiles=n_ctx_tiles, layer=l)
        h = _merge_ffn(h, a, h_fwd, h_bwd, o_pre, ml, a_gain, m_gain, g_pre, g_post, wout, wg, wu, wd,
                       layer=l, n_ctx_tiles=n_ctx_tiles, t0=t0)
    return h
```

```python
import functools

import jax
import jax.numpy as jnp
import numpy as np
from jax import lax
from jax.experimental import pallas as pl
from jax.experimental.pallas import tpu as pltpu

F32 = jnp.float32
BF16 = jnp.bfloat16

N_MOD = 9
EPS = 1e-6
GRID_W = 64
ATTN_HEADS = 4
Q_LORA = 256
KV_LORA = 128
NOPE_DIM = 128
ROPE_DIM = 64
AXIS_FREQS = ROPE_DIM // 4
V_DIM = 128
QK_DIM = NOPE_DIM + ROPE_DIM
ATTN_WIDTH = ATTN_HEADS * V_DIM
ROPE_BASE = 10000.0
SM_SCALE = QK_DIM ** -0.5
LOG2E = float(np.log2(np.e))
Q_SCALE = SM_SCALE * LOG2E
MLSTM_HEADS = 4
MLSTM_DH = 128
MLSTM_WIDTH = MLSTM_HEADS * MLSTM_DH
CONV_K = 3
N_GATES = 4 * MLSTM_HEADS

LANES = 128
SUBLANES = 8
MXU_DIM = 256
TILE = MXU_DIM
VMEM_LIMIT = 56 * 1024 * 1024

A_W = Q_LORA + KV_LORA + 2 * ROPE_DIM
QK_OFF = A_W
V_OFF = QK_OFF + 2 * MLSTM_WIDTH
O_OFF = V_OFF + MLSTM_WIDTH
G_OFF = O_OFF + MLSTM_WIDTH
IN_W = G_OFF + LANES


def _round_up(n, m):
    return (n + m - 1) // m * m


def _rms(x, g):
    return x * lax.rsqrt(jnp.mean(x * x, axis=-1, keepdims=True) + EPS) * g


def _sandwich_in(x, mod, sub, g):
    return _rms(x, g) * (1.0 + mod[3 * sub + 1:3 * sub + 2]) + mod[3 * sub:3 * sub + 1]


def _params(n_grid):
    return pltpu.CompilerParams(dimension_semantics=("arbitrary",) * n_grid,
                                vmem_limit_bytes=VMEM_LIMIT)


def _const_spec(shape, index_map):
    return pl.BlockSpec(shape, index_map, pipeline_mode=pl.Buffered(1))


def _mod_spec(d, tile0, n_ctx_tiles):
    return pl.BlockSpec((1, 1, N_MOD, d),
                        lambda b, t: (b, (t + tile0 >= n_ctx_tiles).astype(jnp.int32), 0, 0))


TILES_PER_STEP = 2


class _TilePlan:
    def __init__(self, bsz, n, t0, n_ctx_tiles):
        assert (bsz * n) % TILES_PER_STEP == 0
        self.bsz, self.n, self.t0, self.n_ctx_tiles = bsz, n, t0, n_ctx_tiles
        self.grid = (bsz * n // TILES_PER_STEP,)

    def coords(self, i, k):
        g = i * TILES_PER_STEP + k
        return g // self.n, g % self.n + self.t0

    def is_ctx(self, k):
        return self.coords(pl.program_id(0), k)[1] < self.n_ctx_tiles

    def tile_spec(self, arr, k, arr_t0=0, clamp=False):
        arr_tiles = arr.shape[0] // self.bsz

        def index(i):
            b, r = self.coords(i, k)
            r = r - arr_t0
            if clamp:
                r = jnp.clip(r, 0, arr_tiles - 1)
            return (b * arr_tiles + r, 0, 0)

        return pl.BlockSpec((1, TILE, arr.shape[2]), index)

    def mod_spec(self, d, k):
        def index(i):
            b, r = self.coords(i, k)
            return (b, (r >= self.n_ctx_tiles).astype(jnp.int32), 0, 0)

        return pl.BlockSpec((1, 1, N_MOD, d), index)


def _as_tiles(x):
    return x.reshape(-1, TILE, x.shape[-1])


def _const1(shape, index):
    return _const_spec(shape, lambda i: index)


def _mod_kernel(c_ref, w_ref, b_ref, o_ref):
    c = c_ref[...]
    h = (c * jax.nn.sigmoid(c)).astype(BF16)
    o_ref[0] = jnp.dot(h, w_ref[0].astype(BF16), preferred_element_type=F32) + b_ref[0]


def _modulation(c_rows, w_mod, b_mod):
    n_layers, d, n_out = w_mod.shape
    rows = c_rows.shape[0]
    n_blk = N_MOD * LANES
    return pl.pallas_call(
        _mod_kernel,
        grid=(n_layers, n_out // n_blk),
        in_specs=[pl.BlockSpec((rows, d), lambda l, j: (0, 0)),
                  pl.BlockSpec((1, d, n_blk), lambda l, j: (l, 0, j)),
                  pl.BlockSpec((1, 1, n_blk), lambda l, j: (l, 0, j))],
        out_specs=pl.BlockSpec((1, rows, n_blk), lambda l, j: (l, 0, j)),
        out_shape=jax.ShapeDtypeStruct((n_layers, rows, n_out), F32),
        compiler_params=_params(2),
        name="modulation",
    )(c_rows, w_mod, b_mod.reshape(n_layers, 1, n_out))


def _ffn_body(xs, mods, sub, g_pre, g_post, wg_ref, wu_ref, wd_ref, a_ref):
    h = jnp.concatenate([_sandwich_in(x, m, sub, g_pre).astype(BF16) for x, m in zip(xs, mods)], axis=0)
    ff = a_ref.shape[1]
    for lo in range(0, ff, MXU_DIM):
        hi = min(lo + MXU_DIM, ff)
        g = jnp.dot(h, wg_ref[:, lo:hi], preferred_element_type=F32)
        u = jnp.dot(h, wu_ref[:, lo:hi], preferred_element_type=F32)
        a_ref[:, lo:hi] = (g * jax.nn.sigmoid(g) * u).astype(BF16)
    ff_main = ff // MXU_DIM * MXU_DIM
    y = jnp.dot(a_ref[:, 0:ff_main], wd_ref[0:ff_main, :], preferred_element_type=F32)
    if ff_main < ff:
        y = y + jnp.dot(a_ref[:, ff_main:ff], wd_ref[ff_main:ff, :], preferred_element_type=F32)
    return [x + 0.5 * m[3 * sub + 2:3 * sub + 3] * _rms(y[k * TILE:(k + 1) * TILE], g_post)
            for k, (x, m) in enumerate(zip(xs, mods))]


def _slot_tile(plan, k, refs, split):
    return jnp.where(plan.is_ctx(k), refs[0][0], refs[1][0]) if split else refs[0][0]


def _ffn_kernel(*refs, plan, sub, split):
    per_slot = 3 if split else 2
    slots = [refs[k * per_slot:(k + 1) * per_slot] for k in range(TILES_PER_STEP)]
    gpre_ref, gpost_ref, wg_ref, wu_ref, wd_ref, o_ref, a_ref = refs[TILES_PER_STEP * per_slot:]
    xs = [_slot_tile(plan, k, s[:-1], split) for k, s in enumerate(slots)]
    mods = [s[-1][0, 0] for s in slots]
    outs = _ffn_body(xs, mods, sub, gpre_ref[...], gpost_ref[...], wg_ref, wu_ref, wd_ref, a_ref)
    for k, out in enumerate(outs):
        o_ref[k] = out


def _ffn_weight_specs(layer, j, sub, d, ff):
    return [
        pl.BlockSpec((None, None, 1, d), lambda i: (layer, sub, 0, 0)),
        pl.BlockSpec((None, None, 1, d), lambda i: (layer, sub, 0, 0)),
        _const1((None, None, d, ff), (layer, j, 0, 0)),
        _const1((None, None, d, ff), (layer, j, 0, 0)),
        _const1((None, None, ff, d), (layer, j, 0, 0)),
    ]


def _ffn(src, mod_all, g_pre, g_post, wg, wu, wd, *, layer, j, sub, n_ctx_tiles):
    split = isinstance(src, tuple)
    srcs = [_as_tiles(s) for s in (src if split else (src,))]
    bsz, d = mod_all.shape[0], srcs[0].shape[-1]
    n_tiles = sum(s.shape[0] for s in srcs) // bsz
    ff = wg.shape[-1]
    plan = _TilePlan(bsz, n_tiles, 0, n_ctx_tiles)
    slot_specs, slot_args = [], []
    for k in range(TILES_PER_STEP):
        if split:
            slot_specs += [plan.tile_spec(srcs[0], k, 0, clamp=True), plan.tile_spec(srcs[1], k, n_ctx_tiles, clamp=True)]
        else:
            slot_specs += [plan.tile_spec(srcs[0], k)]
        slot_specs += [plan.mod_spec(d, k)]
        slot_args += srcs + [mod_all]
    out = pl.pallas_call(
        functools.partial(_ffn_kernel, plan=plan, sub=sub, split=split),
        grid=plan.grid,
        in_specs=slot_specs + _ffn_weight_specs(layer, j, sub, d, ff),
        out_specs=pl.BlockSpec((TILES_PER_STEP, TILE, d), lambda i: (i, 0, 0)),
        out_shape=jax.ShapeDtypeStruct((bsz * n_tiles, TILE, d), F32),
        scratch_shapes=[pltpu.VMEM((TILES_PER_STEP * TILE, ff), BF16)],
        compiler_params=_params(1),
        name=f"ffn_l{layer}_{j}",
    )(*slot_args, g_pre, g_post, wg, wu, wd)
    return out.reshape(bsz, n_tiles * TILE, d)


GATE_ROWS = 2 * 3 * SUBLANES


def _log_sigmoid(x):
    return jnp.minimum(x, 0.0) - jnp.log1p(jnp.exp(-jnp.abs(x)))


def _lane_scan(x, op, identity, reverse):
    n = x.shape[1]
    lane = lax.broadcasted_iota(jnp.int32, x.shape, 1)
    shift = 1
    while shift < n:
        if reverse:
            moved = jnp.where(lane < n - shift, pltpu.roll(x, n - shift, axis=1), identity)
        else:
            moved = jnp.where(lane >= shift, pltpu.roll(x, shift, axis=1), identity)
        x = op(x, moved)
        shift *= 2
    return x


def _gate_rows(g8, reverse):
    nh = MLSTM_HEADS
    head_rows = lax.broadcasted_iota(jnp.int32, g8.shape, 0) < nh
    b = _lane_scan(_log_sigmoid(g8), jnp.add, 0.0, reverse)
    b = jnp.where(head_rows, pltpu.roll(b, nh, axis=0), 0.0)
    a = jnp.where(head_rows, g8 - b, 0.0)
    return a, b, _lane_scan(a, jnp.maximum, -jnp.inf, reverse)


def _mixin_kernel(x_ref, xp_ref, xn_ref, mod_ref, gpre_ref, win_ref, qn_ref, wuq_ref, kvn_ref, wukv_ref,
                  cos_ref, sin_ref, gb_ref, cw_ref, cb_ref,
                  q_ref, k_ref, vt_ref, qtm_ref, km_ref, vtm_ref, o_ref, gt_ref, *, n_ctx_tiles, n_tiles):
    ci = pl.program_id(1)
    mod = mod_ref[0, 0]
    h = _sandwich_in(x_ref[0], mod, 1, gpre_ref[...]).astype(BF16)
    cos4 = cos_ref[...]
    sin4 = sin_ref[...]

    gates = jnp.dot(h, win_ref[:, G_OFF:IN_W], preferred_element_type=F32) + gb_ref[...]
    g_t = gates.T[0:N_GATES, :]
    half = N_GATES // 2
    gt_ref[0] = jnp.concatenate(_gate_rows(g_t[0:half], False) + _gate_rows(g_t[half:N_GATES], True), axis=0)

    pa = jnp.dot(h, win_ref[:, 0:A_W], preferred_element_type=F32)
    cq = pa[:, 0:Q_LORA]
    ckv = pa[:, Q_LORA:Q_LORA + KV_LORA]
    kr = pa[:, Q_LORA + KV_LORA:Q_LORA + KV_LORA + ROPE_DIM]
    kr_sw = pa[:, Q_LORA + KV_LORA + ROPE_DIM:A_W]
    k_rope = (kr * cos4[:, 0:ROPE_DIM] + kr_sw * sin4[:, 0:ROPE_DIM]).astype(BF16)

    nope_w = ATTN_HEADS * NOPE_DIM
    rope_w = ATTN_HEADS * ROPE_DIM
    qa = jnp.dot(_rms(cq, qn_ref[...]).astype(BF16), wuq_ref[...], preferred_element_type=F32)
    q_rope = qa[:, nope_w:nope_w + rope_w] * cos4 + qa[:, nope_w + rope_w:nope_w + 2 * rope_w] * sin4
    kva = jnp.dot(_rms(ckv, kvn_ref[...]).astype(BF16), wukv_ref[...], preferred_element_type=F32)
    for hd in range(ATTN_HEADS):
        q_ref[0, hd, :, 0:NOPE_DIM] = (qa[:, hd * NOPE_DIM:(hd + 1) * NOPE_DIM] * Q_SCALE).astype(BF16)
        q_ref[0, hd, :, NOPE_DIM:QK_DIM] = (q_rope[:, hd * ROPE_DIM:(hd + 1) * ROPE_DIM] * Q_SCALE).astype(BF16)
        k_ref[0, hd, :, 0:NOPE_DIM] = kva[:, hd * NOPE_DIM:(hd + 1) * NOPE_DIM].astype(BF16)
        k_ref[0, hd, :, NOPE_DIM:QK_DIM] = k_rope
        vt_ref[0, hd] = kva[:, nope_w + hd * V_DIM:nope_w + (hd + 1) * V_DIM].T.astype(BF16)

    vtm_ref[0] = jnp.dot(h, win_ref[:, V_OFF:O_OFF], preferred_element_type=F32).T.astype(BF16)
    o_ref[0] = jnp.dot(h, win_ref[:, O_OFF:G_OFF], preferred_element_type=F32)

    halo = jnp.concatenate([xp_ref[0], xn_ref[0]], axis=0)
    h_ext = jnp.concatenate([h, _sandwich_in(halo, mod, 1, gpre_ref[...]).astype(BF16)], axis=0)
    pqk = jnp.dot(h_ext, win_ref[:, QK_OFF:V_OFF], preferred_element_type=F32)
    x = pqk[0:TILE]
    row = lax.broadcasted_iota(jnp.int32, (TILE, 1), 0)
    has_prev = jnp.logical_and(ci != 0, ci != n_ctx_tiles)
    has_next = jnp.logical_and(ci != n_ctx_tiles - 1, ci != n_tiles - 1)
    prev_row = jnp.where(has_prev, pqk[TILE + SUBLANES - 1:TILE + SUBLANES], 0.0)
    next_row = jnp.where(has_next, pqk[TILE + SUBLANES:TILE + SUBLANES + 1], 0.0)
    x_prev = jnp.where(row == 0, prev_row, pltpu.roll(x, 1, axis=0))
    x_next = jnp.where(row == TILE - 1, next_row, pltpu.roll(x, TILE - 1, axis=0))
    u = x_prev * cw_ref[0:1, :] + x * cw_ref[1:2, :] + x_next * cw_ref[2:3, :] + cb_ref[...]
    u = u * jax.nn.sigmoid(u)
    qtm_ref[0] = u[:, 0:MLSTM_WIDTH].T.astype(BF16)
    km_ref[0] = (u[:, MLSTM_WIDTH:2 * MLSTM_WIDTH] * (MLSTM_DH ** -0.5)).astype(BF16)


def _mixin(x, mod_all, g_pre, win, q_norm, wuq, kv_norm, wukv, cos4, sin4, gate_b, conv_w, conv_b,
           *, layer, n_ctx_tiles):
    bsz, t_all, d = x.shape
    n_tiles = t_all // TILE
    rope_w = ATTN_HEADS * ROPE_DIM
    halo_per_tile = TILE // SUBLANES
    n_halo = t_all // SUBLANES
    tile_map = lambda b, t: (b, t, 0)
    head_map = lambda b, t: (b, 0, t, 0)
    time_last_map = lambda b, t: (b, 0, t)
    out_shape = (
        jax.ShapeDtypeStruct((bsz, ATTN_HEADS, t_all, QK_DIM), BF16),
        jax.ShapeDtypeStruct((bsz, ATTN_HEADS, t_all, QK_DIM), BF16),
        jax.ShapeDtypeStruct((bsz, ATTN_HEADS, V_DIM, t_all), BF16),
        jax.ShapeDtypeStruct((bsz, MLSTM_WIDTH, t_all), BF16),
        jax.ShapeDtypeStruct((bsz, t_all, MLSTM_WIDTH), BF16),
        jax.ShapeDtypeStruct((bsz, MLSTM_WIDTH, t_all), BF16),
        jax.ShapeDtypeStruct((bsz, t_all, MLSTM_WIDTH), F32),
        jax.ShapeDtypeStruct((bsz, GATE_ROWS, t_all), F32),
    )
    return pl.pallas_call(
        functools.partial(_mixin_kernel, n_ctx_tiles=n_ctx_tiles, n_tiles=n_tiles),
        grid=(bsz, n_tiles),
        in_specs=[
            pl.BlockSpec((1, TILE, d), tile_map),
            pl.BlockSpec((1, SUBLANES, d), lambda b, t: (b, jnp.maximum(t * halo_per_tile - 1, 0), 0)),
            pl.BlockSpec((1, SUBLANES, d), lambda b, t: (b, jnp.minimum((t + 1) * halo_per_tile, n_halo - 1), 0)),
            _mod_spec(d, 0, n_ctx_tiles),
            pl.BlockSpec((None, None, 1, d), lambda b, t: (layer, 1, 0, 0)),
            _const_spec((None, d, IN_W), lambda b, t: (layer, 0, 0)),
            pl.BlockSpec((None, 1, Q_LORA), lambda b, t: (layer, 0, 0)),
            _const_spec((None, Q_LORA, wuq.shape[-1]), lambda b, t: (layer, 0, 0)),
            pl.BlockSpec((None, 1, KV_LORA), lambda b, t: (layer, 0, 0)),
            _const_spec((None, KV_LORA, wukv.shape[-1]), lambda b, t: (layer, 0, 0)),
            pl.BlockSpec((TILE, rope_w), lambda b, t: (t, 0)),
            pl.BlockSpec((TILE, rope_w), lambda b, t: (t, 0)),
            pl.BlockSpec((None, 1, LANES), lambda b, t: (layer, 0, 0)),
            pl.BlockSpec((None, CONV_K, 2 * MLSTM_WIDTH), lambda b, t: (layer, 0, 0)),
            pl.BlockSpec((None, 1, 2 * MLSTM_WIDTH), lambda b, t: (layer, 0, 0)),
        ],
        out_specs=(
            pl.BlockSpec((1, ATTN_HEADS, TILE, QK_DIM), head_map),
            pl.BlockSpec((1, ATTN_HEADS, TILE, QK_DIM), head_map),
            pl.BlockSpec((1, ATTN_HEADS, V_DIM, TILE), lambda b, t: (b, 0, 0, t)),
            pl.BlockSpec((1, MLSTM_WIDTH, TILE), time_last_map),
            pl.BlockSpec((1, TILE, MLSTM_WIDTH), tile_map),
            pl.BlockSpec((1, MLSTM_WIDTH, TILE), time_last_map),
            pl.BlockSpec((1, TILE, MLSTM_WIDTH), tile_map),
            pl.BlockSpec((1, GATE_ROWS, TILE), time_last_map),
        ),
        out_shape=out_shape,
        compiler_params=_params(2),
        name=f"mixin_l{layer}",
    )(x, x, x, mod_all, g_pre, win, q_norm, wuq, kv_norm, wukv, cos4, sin4, gate_b, conv_w, conv_b)


KEY_CHUNK = 2 * TILE
ATTN_BUFFERS = 2


def _attn_kernel(q_ref, k_ref, vt_ref, o_ref, s_scr, p_scr, *, chunks):
    n_heads = q_ref.shape[1]
    lo, hi = chunks[0][0], chunks[-1][1]
    fold = lambda v: v.reshape(-1, SUBLANES, TILE)
    m8, l8 = {}, {}
    for stage in range(n_heads + 2):
        h_s, h_p, h_v = stage, stage - 1, stage - 2
        do_s, do_p, do_v = h_s < n_heads, 0 <= h_p < n_heads, 0 <= h_v < n_heads
        if do_v:
            acc = jnp.dot(vt_ref[0, h_v, :, lo:hi], p_scr[h_v % ATTN_BUFFERS, lo:hi, :],
                          preferred_element_type=F32)
            l = jnp.sum(l8.pop(h_v), axis=0, keepdims=True)
            o_ref[0, :, h_v * V_DIM:(h_v + 1) * V_DIM] = (acc / l).T
        if do_s:
            q = q_ref[0, h_s]
            m8[h_s] = jnp.full((SUBLANES, TILE), -jnp.inf, F32)
        if do_p:
            m = jnp.max(m8.pop(h_p), axis=0, keepdims=True)
            l8[h_p] = jnp.zeros((SUBLANES, TILE), F32)
        for c_lo, c_hi in chunks:
            if do_s:
                s = lax.dot_general(k_ref[0, h_s, c_lo:c_hi, :], q, (((1,), (1,)), ((), ())),
                                    preferred_element_type=F32)
                s_scr[h_s % ATTN_BUFFERS, c_lo:c_hi, :] = s
                m8[h_s] = jnp.maximum(m8[h_s], jnp.max(fold(s), axis=0))
            if do_p:
                p = jnp.exp2(s_scr[h_p % ATTN_BUFFERS, c_lo:c_hi, :] - m)
                l8[h_p] = l8[h_p] + jnp.sum(fold(p), axis=0)
                p_scr[h_p % ATTN_BUFFERS, c_lo:c_hi, :] = p.astype(BF16)


def _key_chunks(n_keys, width):
    return tuple((lo, min(lo + width, n_keys)) for lo in range(0, n_keys, width))


def _attention(q, k, vt, *, q_tile0, n_q, n_keys, name):
    bsz, n_heads, _, _ = q.shape
    return pl.pallas_call(
        functools.partial(_attn_kernel, chunks=_key_chunks(n_keys, KEY_CHUNK)),
        grid=(bsz, n_q),
        in_specs=[
            pl.BlockSpec((1, n_heads, TILE, QK_DIM), lambda b, i: (b, 0, i + q_tile0, 0)),
            pl.BlockSpec((1, n_heads, n_keys, QK_DIM), lambda b, i: (b, 0, 0, 0)),
            pl.BlockSpec((1, n_heads, V_DIM, n_keys), lambda b, i: (b, 0, 0, 0)),
        ],
        out_specs=pl.BlockSpec((1, TILE, n_heads * V_DIM), lambda b, i: (b, i, 0)),
        out_shape=jax.ShapeDtypeStruct((bsz, n_q * TILE, n_heads * V_DIM), F32),
        scratch_shapes=[pltpu.VMEM((ATTN_BUFFERS, n_keys, TILE), F32),
                        pltpu.VMEM((ATTN_BUFFERS, n_keys, TILE), BF16)],
        compiler_params=_params(2),
        name=name,
    )(q, k, vt)


STATE_ROWS = MLSTM_DH + 2 * SUBLANES


def _bwd_chunk(j, n_ctx_tiles, n_tiles):
    return jnp.where(j < n_ctx_tiles, n_ctx_tiles - 1 - j, n_tiles - 1 - (j - n_ctx_tiles))


def _mlstm_direction(reverse, a, b, a_max, k_ref, qt_ref, vt_ref, o_ref, st_scr, m_scr, d):
    nh = MLSTM_HEADS
    last = 0 if reverse else TILE - 1
    m_old = m_scr[d][:, 0:1]
    mx = jnp.maximum(m_old, a_max)
    inter = jnp.exp(m_old - mx)
    e_inv = jnp.exp(-(b + mx))
    mx_last = mx[:, last:last + 1]
    w = jnp.exp(a - mx_last)
    decay = jnp.exp(m_old - mx_last)
    m_scr[d] = jnp.broadcast_to(b[:, last:last + 1] + mx_last, (SUBLANES, LANES))

    s_idx = lax.broadcasted_iota(jnp.int32, (TILE, TILE), 0)
    t_idx = lax.broadcasted_iota(jnp.int32, (TILE, TILE), 1)
    seen = s_idx >= t_idx if reverse else s_idx <= t_idx
    pad_row = lax.broadcasted_iota(jnp.int32, (2 * SUBLANES, TILE), 0) == 0
    ones_rows = jnp.where(pad_row, 1.0, 0.0).astype(BF16)

    for hd in range(nh):
        sl = slice(hd * MLSTM_DH, (hd + 1) * MLSTM_DH)
        k_h = k_ref[0, :, sl]
        qt_h = qt_ref[0, sl, :]
        vt_h = vt_ref[0, sl, :]
        state = st_scr[d * nh + hd]
        a_rep = jnp.broadcast_to(a[hd:hd + 1] * LOG2E, (LANES, TILE)).T
        log_d = jnp.concatenate([a_rep] * (TILE // LANES), axis=1) - mx[hd:hd + 1] * LOG2E
        d_t = jnp.exp2(jnp.where(seen, log_d, -jnp.inf))
        r = jnp.dot(jnp.concatenate([k_h, state.astype(BF16)], axis=0), qt_h, preferred_element_type=F32)
        p = (r[0:TILE] * d_t).astype(BF16)
        nd = jnp.dot(jnp.concatenate([vt_h, ones_rows], axis=0), p, preferred_element_type=F32)
        inter_h = inter[hd:hd + 1]
        den = nd[MLSTM_DH:MLSTM_DH + 1] + inter_h * r[TILE + MLSTM_DH:TILE + MLSTM_DH + 1]
        scale = 1.0 / jnp.maximum(jnp.abs(den), e_inv[hd:hd + 1])
        h_t = (nd[0:MLSTM_DH] + inter_h * r[TILE:TILE + MLSTM_DH]) * scale
        o_ref[0, :, sl] = h_t.T

        w_h = w[hd:hd + 1]
        vw = jnp.concatenate([(vt_h.astype(F32) * w_h).astype(BF16), jnp.where(pad_row, w_h, 0.0).astype(BF16)],
                             axis=0)
        st_scr[d * nh + hd] = decay[hd:hd + 1] * state + jnp.dot(vw, k_h, preferred_element_type=F32)


def _mlstm_kernel(kf_ref, qtf_ref, vtf_ref, gf_ref, kb_ref, qtb_ref, vtb_ref, gb_ref, of_ref, ob_ref,
                  st_scr, m_scr):
    @pl.when(pl.program_id(1) == 0)
    def _():
        st_scr[...] = jnp.zeros_like(st_scr)
        m_scr[...] = jnp.zeros_like(m_scr)

    rows = [slice(i * SUBLANES, (i + 1) * SUBLANES) for i in range(GATE_ROWS // SUBLANES)]
    _mlstm_direction(False, gf_ref[0, rows[0], :], gf_ref[0, rows[1], :], gf_ref[0, rows[2], :],
                     kf_ref, qtf_ref, vtf_ref, of_ref, st_scr, m_scr, 0)
    _mlstm_direction(True, gb_ref[0, rows[3], :], gb_ref[0, rows[4], :], gb_ref[0, rows[5], :],
                     kb_ref, qtb_ref, vtb_ref, ob_ref, st_scr, m_scr, 1)


def _mlstm(k_m, qt_m, vt_m, gates_t, *, n_ctx_tiles, layer):
    bsz, t_all, width = k_m.shape
    n_tiles = t_all // TILE
    bwd = functools.partial(_bwd_chunk, n_ctx_tiles=n_ctx_tiles, n_tiles=n_tiles)

    def specs(chunk):
        return [pl.BlockSpec((1, TILE, width), lambda b, j: (b, chunk(j), 0)),
                pl.BlockSpec((1, width, TILE), lambda b, j: (b, 0, chunk(j))),
                pl.BlockSpec((1, width, TILE), lambda b, j: (b, 0, chunk(j))),
                pl.BlockSpec((1, GATE_ROWS, TILE), lambda b, j: (b, 0, chunk(j)))]

    out_sds = jax.ShapeDtypeStruct((bsz, t_all, width), F32)
    return pl.pallas_call(
        _mlstm_kernel,
        grid=(bsz, n_tiles),
        in_specs=specs(lambda j: j) + specs(bwd),
        out_specs=(pl.BlockSpec((1, TILE, width), lambda b, j: (b, j, 0)),
                   pl.BlockSpec((1, TILE, width), lambda b, j: (b, bwd(j), 0))),
        out_shape=(out_sds, out_sds),
        scratch_shapes=[pltpu.VMEM((2 * MLSTM_HEADS, STATE_ROWS, MLSTM_DH), F32),
                        pltpu.VMEM((2, SUBLANES, LANES), F32)],
        compiler_params=_params(2),
        name=f"mlstm_l{layer}",
    )(k_m, qt_m, vt_m, gates_t, k_m, qt_m, vt_m, gates_t)


def _merge_body(xs, attn, hms, os, gates, g_attn, g_mlstm, g_post, wout_ref):
    a_n, hm_n = [], []
    for a, hm, o in zip(attn, hms, os):
        hm = hm * jax.nn.sigmoid(o)
        normed = []
        for hd in range(MLSTM_HEADS):
            seg = hm[:, hd * MLSTM_DH:(hd + 1) * MLSTM_DH]
            cen = seg - jnp.mean(seg, axis=-1, keepdims=True)
            normed.append(cen * lax.rsqrt(jnp.mean(cen * cen, axis=-1, keepdims=True) + EPS))
        hm_n.append((jnp.concatenate(normed, axis=-1) * g_mlstm).astype(BF16))
        a_n.append(_rms(a, g_attn).astype(BF16))
    y = jnp.dot(jnp.concatenate(a_n, axis=0), wout_ref[0:ATTN_WIDTH, :], preferred_element_type=F32)
    y = y + jnp.dot(jnp.concatenate(hm_n, axis=0), wout_ref[ATTN_WIDTH:ATTN_WIDTH + MLSTM_WIDTH, :],
                    preferred_element_type=F32)
    return [x + gate * _rms(y[k * TILE:(k + 1) * TILE], g_post) for k, (x, gate) in enumerate(zip(xs, gates))]


def _merge_ffn_kernel(*refs, plan, split):
    per_slot = 7 if split else 6
    slots = [refs[k * per_slot:(k + 1) * per_slot] for k in range(TILES_PER_STEP)]
    (ga_ref, gm_ref, gpost1_ref, wout_ref, gpre2_ref, gpost2_ref, wg_ref, wu_ref, wd_ref,
     out_ref, a_scr) = refs[TILES_PER_STEP * per_slot:]
    xs = [s[0][0] for s in slots]
    attn = [_slot_tile(plan, k, s[1:-4], split) for k, s in enumerate(slots)]
    hms = [s[-4][0] + s[-3][0] for s in slots]
    os = [s[-2][0] for s in slots]
    mods = [s[-1][0, 0] for s in slots]
    x1 = _merge_body(xs, attn, hms, os, [m[5:6] for m in mods], ga_ref[...], gm_ref[...], gpost1_ref[...], wout_ref)
    outs = _ffn_body(x1, mods, 2, gpre2_ref[...], gpost2_ref[...], wg_ref, wu_ref, wd_ref, a_scr)
    for k, out in enumerate(outs):
        out_ref[k] = out


def _merge_ffn(x, a, h_fwd, h_bwd, o, mod_all, g_attn, g_mlstm, g_pre, g_post, wout, wg, wu, wd,
               *, layer, n_ctx_tiles, t0):
    bsz, t_all, d = x.shape
    n_tiles = t_all // TILE - t0
    ff = wg.shape[-1]
    split = isinstance(a, tuple)
    a_srcs = [_as_tiles(s) for s in (a if split else (a,))]
    stream = [_as_tiles(s) for s in (h_fwd, h_bwd, o)]
    xt = _as_tiles(x)
    plan = _TilePlan(bsz, n_tiles, t0, n_ctx_tiles)
    slot_specs, slot_args = [], []
    for k in range(TILES_PER_STEP):
        slot_specs += [plan.tile_spec(xt, k)]
        if split:
            slot_specs += [plan.tile_spec(a_srcs[0], k, 0, clamp=True),
                           plan.tile_spec(a_srcs[1], k, n_ctx_tiles, clamp=True)]
        else:
            slot_specs += [plan.tile_spec(a_srcs[0], k, t0)]
        slot_specs += [plan.tile_spec(s, k) for s in stream] + [plan.mod_spec(d, k)]
        slot_args += [xt] + a_srcs + stream + [mod_all]
    out = pl.pallas_call(
        functools.partial(_merge_ffn_kernel, plan=plan, split=split),
        grid=plan.grid,
        in_specs=slot_specs + [
            pl.BlockSpec((None, 1, ATTN_WIDTH), lambda i: (layer, 0, 0)),
            pl.BlockSpec((None, 1, MLSTM_WIDTH), lambda i: (layer, 0, 0)),
            pl.BlockSpec((None, None, 1, d), lambda i: (layer, 1, 0, 0)),
            _const1((None, ATTN_WIDTH + MLSTM_WIDTH, d), (layer, 0, 0)),
        ] + _ffn_weight_specs(layer, 1, 2, d, ff),
        out_specs=pl.BlockSpec((TILES_PER_STEP, TILE, d), lambda i: (i, 0, 0)),
        out_shape=jax.ShapeDtypeStruct((bsz * n_tiles, TILE, d), F32),
        scratch_shapes=[pltpu.VMEM((TILES_PER_STEP * TILE, ff), BF16)],
        compiler_params=_params(1),
        name=f"merge_ffn_l{layer}",
    )(*slot_args, g_attn, g_mlstm, g_post, wout, g_pre, g_post, wg, wu, wd)
    return out.reshape(bsz, n_tiles * TILE, d)


def _half_swap_perm():
    idx = np.arange(ROPE_DIM)
    axis, half, freq = idx // (2 * AXIS_FREQS), (idx // AXIS_FREQS) % 2, idx % AXIS_FREQS
    return axis * 2 * AXIS_FREQS + (1 - half) * AXIS_FREQS + freq


def _prep_w_in(w_in):
    sizes = (Q_LORA, KV_LORA, ROPE_DIM, 2 * MLSTM_WIDTH, MLSTM_WIDTH, MLSTM_WIDTH, N_GATES)
    offs = np.concatenate([[0], np.cumsum(sizes)])
    cq, ckv, kr, qk, v, o, g = (w_in[..., offs[i]:offs[i + 1]] for i in range(len(sizes)))
    g = jnp.pad(g, ((0, 0), (0, 0), (0, LANES - N_GATES)))
    return jnp.concatenate([cq, ckv, kr, kr[..., _half_swap_perm()], qk, v, o, g], axis=-1).astype(BF16)


def _prep_w_uq(w_uq):
    n_layers, q_lora, _ = w_uq.shape
    w = w_uq.reshape(n_layers, q_lora, ATTN_HEADS, QK_DIM)
    nope = w[..., :NOPE_DIM].reshape(n_layers, q_lora, -1)
    rope = w[..., NOPE_DIM:]
    rope_sw = rope[..., _half_swap_perm()]
    return jnp.concatenate([nope, rope.reshape(n_layers, q_lora, -1), rope_sw.reshape(n_layers, q_lora, -1)],
                           axis=-1).astype(BF16)


def _prep_w_ukv(w_ukv):
    n_layers, kv_lora, _ = w_ukv.shape
    w = w_ukv.reshape(n_layers, kv_lora, ATTN_HEADS, NOPE_DIM + V_DIM)
    return jnp.concatenate([w[..., :NOPE_DIM].reshape(n_layers, kv_lora, -1),
                            w[..., NOPE_DIM:].reshape(n_layers, kv_lora, -1)], axis=-1).astype(BF16)


def _rope_tables(n_ctx, n_tok):
    rows = n_tok // GRID_W
    t_row = jnp.repeat(jnp.arange(rows), GRID_W).astype(F32)
    t_col = jnp.tile(jnp.arange(GRID_W), rows).astype(F32)
    inv = ROPE_BASE ** (-jnp.arange(AXIS_FREQS, dtype=F32) / AXIS_FREQS)
    ang_r = t_row[:, None] * inv
    ang_c = t_col[:, None] * inv
    cos = jnp.concatenate([jnp.cos(ang_r), jnp.cos(ang_r), jnp.cos(ang_c), jnp.cos(ang_c)], axis=-1)
    sin = jnp.concatenate([-jnp.sin(ang_r), jnp.sin(ang_r), -jnp.sin(ang_c), jnp.sin(ang_c)], axis=-1)
    cos = jnp.concatenate([jnp.ones((n_ctx, ROPE_DIM), F32), cos], axis=0)
    sin = jnp.concatenate([jnp.zeros((n_ctx, ROPE_DIM), F32), sin], axis=0)
    return jnp.tile(cos, (1, ATTN_HEADS)), jnp.tile(sin, (1, ATTN_HEADS))


def kernel(x, c, ctx, c_ctx, w_mod, b_mod, norm_pre, norm_post, ffn_w_gate, ffn_w_up, ffn_w_down,
           w_in, q_norm, w_uq, kv_norm, w_ukv, attn_out_norm, conv_w, conv_b, gate_b, mlstm_norm, w_out):
    bsz, n_tok, d = x.shape
    n_ctx = ctx.shape[1]
    n_layers = w_mod.shape[0]
    assert n_tok % TILE == 0 and n_ctx % TILE == 0 and n_tok % GRID_W == 0
    n_ctx_tiles = n_ctx // TILE
    n_lat_tiles = n_tok // TILE

    wg = ffn_w_gate.astype(BF16)
    wu = ffn_w_up.astype(BF16)
    wd = ffn_w_down.astype(BF16)
    win = _prep_w_in(w_in)
    wuq = _prep_w_uq(w_uq)
    wukv = _prep_w_ukv(w_ukv)
    wout = w_out.astype(BF16)
    g_pre = norm_pre[:, :, None, :]
    g_post = norm_post[:, :, None, :]
    q_gain = q_norm[:, None, :]
    kv_gain = kv_norm[:, None, :]
    a_gain = attn_out_norm[:, None, :]
    m_gain = mlstm_norm[:, None, :]
    conv_bias = conv_b[:, None, :]
    gate_bias = jnp.pad(gate_b.reshape(n_layers, 1, N_GATES), ((0, 0), (0, 0), (0, LANES - N_GATES)))
    cos4, sin4 = _rope_tables(n_ctx, n_tok)

    c_rows = jnp.zeros((_round_up(bsz + 1, SUBLANES), d), F32).at[:bsz].set(c).at[bsz].set(c_ctx)
    mod = _modulation(c_rows, w_mod, b_mod).reshape(n_layers, -1, N_MOD, d)
    mod_all = jnp.stack([jnp.broadcast_to(mod[:, bsz:bsz + 1], (n_layers, bsz, N_MOD, d)), mod[:, :bsz]], axis=2)

    h = (ctx, x)
    for l in range(n_layers):
        t0 = n_ctx_tiles if l == n_layers - 1 else 0
        ml = mod_all[l]
        h = _ffn(h, ml, g_pre, g_post, wg, wu, wd, layer=l, j=0, sub=0, n_ctx_tiles=n_ctx_tiles)
        q, k, vt, qt_m, k_m, vt_m, o_pre, gates_t = _mixin(
            h, ml, g_pre, win, q_gain, wuq, kv_gain, wukv, cos4, sin4, gate_bias, conv_w, conv_bias,
            layer=l, n_ctx_tiles=n_ctx_tiles)
        a = _attention(q, k, vt, q_tile0=n_ctx_tiles, n_q=n_lat_tiles, n_keys=n_ctx + n_tok,
                       name=f"attention_l{l}")
        if t0 == 0:
            a = (_attention(q, k, vt, q_tile0=0, n_q=n_ctx_tiles, n_keys=n_ctx, name=f"attention_ctx_l{l}"), a)
        h_fwd, h_bwd = _mlstm(k_m, qt_m, vt_m, gates_t, n_ctx_tiles=n_ctx_tiles, layer=l)
        h = _merge_ffn(h, a, h_fwd, h_bwd, o_pre, ml, a_gain, m_gain, g_pre, g_post, wout, wg, wu, wd,
                       layer=l, n_ctx_tiles=n_ctx_tiles, t0=t0)
    return h
```

```python
import functools

import jax
import jax.numpy as jnp
import numpy as np
from jax import lax
from jax.experimental import pallas as pl
from jax.experimental.pallas import tpu as pltpu

F32 = jnp.float32
BF16 = jnp.bfloat16

N_MOD = 9
EPS = 1e-6
GRID_W = 64
ATTN_HEADS = 4
Q_LORA = 256
KV_LORA = 128
NOPE_DIM = 128
ROPE_DIM = 64
AXIS_FREQS = ROPE_DIM // 4
V_DIM = 128
QK_DIM = NOPE_DIM + ROPE_DIM
ATTN_WIDTH = ATTN_HEADS * V_DIM
ROPE_BASE = 10000.0
SM_SCALE = QK_DIM ** -0.5
LOG2E = float(np.log2(np.e))
Q_SCALE = SM_SCALE * LOG2E
MLSTM_HEADS = 4
MLSTM_DH = 128
MLSTM_WIDTH = MLSTM_HEADS * MLSTM_DH
CONV_K = 3
N_GATES = 4 * MLSTM_HEADS

LANES = 128
SUBLANES = 8
MXU_DIM = 256
TILE = MXU_DIM
VMEM_LIMIT = 56 * 1024 * 1024

A_W = Q_LORA + KV_LORA + 2 * ROPE_DIM
QK_OFF = A_W
V_OFF = QK_OFF + 2 * MLSTM_WIDTH
O_OFF = V_OFF + MLSTM_WIDTH
G_OFF = O_OFF + MLSTM_WIDTH
IN_W = G_OFF + LANES


def _round_up(n, m):
    return (n + m - 1) // m * m


def _rms(x, g):
    return x * lax.rsqrt(jnp.mean(x * x, axis=-1, keepdims=True) + EPS) * g


def _sandwich_in(x, mod, sub, g):
    return _rms(x, g) * (1.0 + mod[3 * sub + 1:3 * sub + 2]) + mod[3 * sub:3 * sub + 1]


def _params(n_grid):
    return pltpu.CompilerParams(dimension_semantics=("arbitrary",) * n_grid,
                                vmem_limit_bytes=VMEM_LIMIT)


def _const_spec(shape, index_map):
    return pl.BlockSpec(shape, index_map, pipeline_mode=pl.Buffered(1))


def _mod_spec(d, tile0, n_ctx_tiles):
    return pl.BlockSpec((1, 1, N_MOD, d),
                        lambda b, t: (b, (t + tile0 >= n_ctx_tiles).astype(jnp.int32), 0, 0))


TILES_PER_STEP = 2


class _TilePlan:
    def __init__(self, bsz, n, t0, n_ctx_tiles):
        assert (bsz * n) % TILES_PER_STEP == 0
        self.bsz, self.n, self.t0, self.n_ctx_tiles = bsz, n, t0, n_ctx_tiles
        self.grid = (bsz * n // TILES_PER_STEP,)

    def coords(self, i, k):
        g = i * TILES_PER_STEP + k
        return g // self.n, g % self.n + self.t0

    def is_ctx(self, k):
        return self.coords(pl.program_id(0), k)[1] < self.n_ctx_tiles

    def tile_spec(self, arr, k, arr_t0=0, clamp=False):
        arr_tiles = arr.shape[0] // self.bsz

        def index(i):
            b, r = self.coords(i, k)
            r = r - arr_t0
            if clamp:
                r = jnp.clip(r, 0, arr_tiles - 1)
            return (b * arr_tiles + r, 0, 0)

        return pl.BlockSpec((1, TILE, arr.shape[2]), index)

    def mod_spec(self, d, k):
        def index(i):
            b, r = self.coords(i, k)
            return (b, (r >= self.n_ctx_tiles).astype(jnp.int32), 0, 0)

        return pl.BlockSpec((1, 1, N_MOD, d), index)


def _as_tiles(x):
    return x.reshape(-1, TILE, x.shape[-1])


def _const1(shape, index):
    return _const_spec(shape, lambda i: index)


def _mod_kernel(c_ref, w_ref, b_ref, o_ref):
    c = c_ref[...]
    h = (c * jax.nn.sigmoid(c)).astype(BF16)
    o_ref[0] = jnp.dot(h, w_ref[0].astype(BF16), preferred_element_type=F32) + b_ref[0]


def _modulation(c_rows, w_mod, b_mod):
    n_layers, d, n_out = w_mod.shape
    rows = c_rows.shape[0]
    n_blk = N_MOD * LANES
    return pl.pallas_call(
        _mod_kernel,
        grid=(n_layers, n_out // n_blk),
        in_specs=[pl.BlockSpec((rows, d), lambda l, j: (0, 0)),
                  pl.BlockSpec((1, d, n_blk), lambda l, j: (l, 0, j)),
                  pl.BlockSpec((1, 1, n_blk), lambda l, j: (l, 0, j))],
        out_specs=pl.BlockSpec((1, rows, n_blk), lambda l, j: (l, 0, j)),
        out_shape=jax.ShapeDtypeStruct((n_layers, rows, n_out), F32),
        compiler_params=_params(2),
        name="modulation",
    )(c_rows, w_mod, b_mod.reshape(n_layers, 1, n_out))


def _ffn_body(xs, mods, sub, g_pre, g_post, wg_ref, wu_ref, wd_ref, a_ref):
    h = jnp.concatenate([_sandwich_in(x, m, sub, g_pre).astype(BF16) for x, m in zip(xs, mods)], axis=0)
    ff = a_ref.shape[1]
    for lo in range(0, ff, MXU_DIM):
        hi = min(lo + MXU_DIM, ff)
        g = lax.dot_general(h, wg_ref[lo:hi, :], (((1,), (1,)), ((), ())), preferred_element_type=F32)
        u = lax.dot_general(h, wu_ref[lo:hi, :], (((1,), (1,)), ((), ())), preferred_element_type=F32)
        a_ref[:, lo:hi] = (g * jax.nn.sigmoid(g) * u).astype(BF16)
    ff_main = ff // MXU_DIM * MXU_DIM
    y = jnp.dot(a_ref[:, 0:ff_main], wd_ref[0:ff_main, :], preferred_element_type=F32)
    if ff_main < ff:
        y = y + jnp.dot(a_ref[:, ff_main:ff], wd_ref[ff_main:ff, :], preferred_element_type=F32)
    return [x + 0.5 * m[3 * sub + 2:3 * sub + 3] * _rms(y[k * TILE:(k + 1) * TILE], g_post)
            for k, (x, m) in enumerate(zip(xs, mods))]


def _slot_tile(plan, k, refs, split):
    return jnp.where(plan.is_ctx(k), refs[0][0], refs[1][0]) if split else refs[0][0]


def _ffn_kernel(*refs, plan, sub, split):
    per_slot = 3 if split else 2
    slots = [refs[k * per_slot:(k + 1) * per_slot] for k in range(TILES_PER_STEP)]
    gpre_ref, gpost_ref, wg_ref, wu_ref, wd_ref, o_ref, a_ref = refs[TILES_PER_STEP * per_slot:]
    xs = [_slot_tile(plan, k, s[:-1], split) for k, s in enumerate(slots)]
    mods = [s[-1][0, 0] for s in slots]
    outs = _ffn_body(xs, mods, sub, gpre_ref[...], gpost_ref[...], wg_ref, wu_ref, wd_ref, a_ref)
    for k, out in enumerate(outs):
        o_ref[k] = out


def _ffn_weight_specs(layer, j, sub, d, ff):
    return [
        pl.BlockSpec((None, None, 1, d), lambda i: (layer, sub, 0, 0)),
        pl.BlockSpec((None, None, 1, d), lambda i: (layer, sub, 0, 0)),
        _const1((None, None, ff, d), (layer, j, 0, 0)),
        _const1((None, None, ff, d), (layer, j, 0, 0)),
        _const1((None, None, ff, d), (layer, j, 0, 0)),
    ]


def _ffn(src, mod_all, g_pre, g_post, wg, wu, wd, *, layer, j, sub, n_ctx_tiles):
    split = isinstance(src, tuple)
    srcs = [_as_tiles(s) for s in (src if split else (src,))]
    bsz, d = mod_all.shape[0], srcs[0].shape[-1]
    n_tiles = sum(s.shape[0] for s in srcs) // bsz
    ff = wd.shape[-2]
    plan = _TilePlan(bsz, n_tiles, 0, n_ctx_tiles)
    slot_specs, slot_args = [], []
    for k in range(TILES_PER_STEP):
        if split:
            slot_specs += [plan.tile_spec(srcs[0], k, 0, clamp=True), plan.tile_spec(srcs[1], k, n_ctx_tiles, clamp=True)]
        else:
            slot_specs += [plan.tile_spec(srcs[0], k)]
        slot_specs += [plan.mod_spec(d, k)]
        slot_args += srcs + [mod_all]
    out = pl.pallas_call(
        functools.partial(_ffn_kernel, plan=plan, sub=sub, split=split),
        grid=plan.grid,
        in_specs=slot_specs + _ffn_weight_specs(layer, j, sub, d, ff),
        out_specs=pl.BlockSpec((TILES_PER_STEP, TILE, d), lambda i: (i, 0, 0)),
        out_shape=jax.ShapeDtypeStruct((bsz * n_tiles, TILE, d), F32),
        scratch_shapes=[pltpu.VMEM((TILES_PER_STEP * TILE, ff), BF16)],
        compiler_params=_params(1),
        name=f"ffn_l{layer}_{j}",
    )(*slot_args, g_pre, g_post, wg, wu, wd)
    return out.reshape(bsz, n_tiles * TILE, d)


GATE_ROWS = 2 * 3 * SUBLANES


def _log_sigmoid(x):
    return jnp.minimum(x, 0.0) - jnp.log1p(jnp.exp(-jnp.abs(x)))


def _lane_scan(x, op, identity, reverse):
    n = x.shape[1]
    lane = lax.broadcasted_iota(jnp.int32, x.shape, 1)
    shift = 1
    while shift < n:
        if reverse:
            moved = jnp.where(lane < n - shift, pltpu.roll(x, n - shift, axis=1), identity)
        else:
            moved = jnp.where(lane >= shift, pltpu.roll(x, shift, axis=1), identity)
        x = op(x, moved)
        shift *= 2
    return x


def _gate_rows(g8, reverse):
    nh = MLSTM_HEADS
    head_rows = lax.broadcasted_iota(jnp.int32, g8.shape, 0) < nh
    b = _lane_scan(_log_sigmoid(g8), jnp.add, 0.0, reverse)
    b = jnp.where(head_rows, pltpu.roll(b, nh, axis=0), 0.0)
    a = jnp.where(head_rows, g8 - b, 0.0)
    return a, b, _lane_scan(a, jnp.maximum, -jnp.inf, reverse)


def _mixin_kernel(x_ref, xp_ref, xn_ref, mod_ref, gpre_ref, win_ref, qn_ref, wuq_ref, kvn_ref, wukv_ref,
                  cos_ref, sin_ref, gb_ref, cw_ref, cb_ref,
                  q_ref, k_ref, vt_ref, qtm_ref, km_ref, vtm_ref, o_ref, gt_ref, *, n_ctx_tiles, n_tiles):
    ci = pl.program_id(1)
    mod = mod_ref[0, 0]
    h = _sandwich_in(x_ref[0], mod, 1, gpre_ref[...]).astype(BF16)
    cos4 = cos_ref[...]
    sin4 = sin_ref[...]

    gates = jnp.dot(h, win_ref[:, G_OFF:IN_W], preferred_element_type=F32) + gb_ref[...]
    g_t = gates.T[0:N_GATES, :]
    half = N_GATES // 2
    gt_ref[0] = jnp.concatenate(_gate_rows(g_t[0:half], False) + _gate_rows(g_t[half:N_GATES], True), axis=0)

    halo = jnp.concatenate([xp_ref[0], xn_ref[0]], axis=0)
    h_ext = jnp.concatenate([h, _sandwich_in(halo, mod, 1, gpre_ref[...]).astype(BF16)], axis=0)
    row = lax.broadcasted_iota(jnp.int32, (TILE, 1), 0)
    has_prev = jnp.logical_and(ci != 0, ci != n_ctx_tiles)
    has_next = jnp.logical_and(ci != n_ctx_tiles - 1, ci != n_tiles - 1)

    def conv_silu(lo):
        cols = slice(lo, lo + MLSTM_WIDTH)
        pqk = jnp.dot(h_ext, win_ref[:, QK_OFF + lo:QK_OFF + lo + MLSTM_WIDTH], preferred_element_type=F32)
        x = pqk[0:TILE]
        prev_row = jnp.where(has_prev, pqk[TILE + SUBLANES - 1:TILE + SUBLANES], 0.0)
        next_row = jnp.where(has_next, pqk[TILE + SUBLANES:TILE + SUBLANES + 1], 0.0)
        x_prev = jnp.where(row == 0, prev_row, pltpu.roll(x, 1, axis=0))
        x_next = jnp.where(row == TILE - 1, next_row, pltpu.roll(x, TILE - 1, axis=0))
        u = x_prev * cw_ref[0:1, cols] + x * cw_ref[1:2, cols] + x_next * cw_ref[2:3, cols] + cb_ref[:, cols]
        return u * jax.nn.sigmoid(u)

    pa = jnp.dot(h, win_ref[:, 0:A_W], preferred_element_type=F32)
    cq = pa[:, 0:Q_LORA]
    ckv = pa[:, Q_LORA:Q_LORA + KV_LORA]
    kr = pa[:, Q_LORA + KV_LORA:Q_LORA + KV_LORA + ROPE_DIM]
    kr_sw = pa[:, Q_LORA + KV_LORA + ROPE_DIM:A_W]
    k_rope = (kr * cos4[:, 0:ROPE_DIM] + kr_sw * sin4[:, 0:ROPE_DIM]).astype(BF16)

    nope_w = ATTN_HEADS * NOPE_DIM
    rope_w = ATTN_HEADS * ROPE_DIM
    qa = jnp.dot(_rms(cq, qn_ref[...]).astype(BF16), wuq_ref[...], preferred_element_type=F32)
    q_rope = qa[:, nope_w:nope_w + rope_w] * cos4 + qa[:, nope_w + rope_w:nope_w + 2 * rope_w] * sin4
    kva = jnp.dot(_rms(ckv, kvn_ref[...]).astype(BF16), wukv_ref[...], preferred_element_type=F32)
    for hd in range(ATTN_HEADS):
        q_ref[0, hd, :, 0:NOPE_DIM] = (qa[:, hd * NOPE_DIM:(hd + 1) * NOPE_DIM] * Q_SCALE).astype(BF16)
        q_ref[0, hd, :, NOPE_DIM:QK_DIM] = (q_rope[:, hd * ROPE_DIM:(hd + 1) * ROPE_DIM] * Q_SCALE).astype(BF16)
        k_ref[0, hd, :, 0:NOPE_DIM] = kva[:, hd * NOPE_DIM:(hd + 1) * NOPE_DIM].astype(BF16)
        k_ref[0, hd, :, NOPE_DIM:QK_DIM] = k_rope
        vt_ref[0, hd] = kva[:, nope_w + hd * V_DIM:nope_w + (hd + 1) * V_DIM].T.astype(BF16)

    qtm_ref[0] = conv_silu(0).T.astype(BF16)
    km_ref[0] = (conv_silu(MLSTM_WIDTH) * (MLSTM_DH ** -0.5)).astype(BF16)
    vtm_ref[0] = jnp.dot(h, win_ref[:, V_OFF:O_OFF], preferred_element_type=F32).T.astype(BF16)
    o_ref[0] = jnp.dot(h, win_ref[:, O_OFF:G_OFF], preferred_element_type=F32)


def _mixin(x, mod_all, g_pre, win, q_norm, wuq, kv_norm, wukv, cos4, sin4, gate_b, conv_w, conv_b,
           *, layer, n_ctx_tiles):
    bsz, t_all, d = x.shape
    n_tiles = t_all // TILE
    rope_w = ATTN_HEADS * ROPE_DIM
    halo_per_tile = TILE // SUBLANES
    n_halo = t_all // SUBLANES
    tile_map = lambda b, t: (b, t, 0)
    head_map = lambda b, t: (b, 0, t, 0)
    time_last_map = lambda b, t: (b, 0, t)
    out_shape = (
        jax.ShapeDtypeStruct((bsz, ATTN_HEADS, t_all, QK_DIM), BF16),
        jax.ShapeDtypeStruct((bsz, ATTN_HEADS, t_all, QK_DIM), BF16),
        jax.ShapeDtypeStruct((bsz, ATTN_HEADS, V_DIM, t_all), BF16),
        jax.ShapeDtypeStruct((bsz, MLSTM_WIDTH, t_all), BF16),
        jax.ShapeDtypeStruct((bsz, t_all, MLSTM_WIDTH), BF16),
        jax.ShapeDtypeStruct((bsz, MLSTM_WIDTH, t_all), BF16),
        jax.ShapeDtypeStruct((bsz, t_all, MLSTM_WIDTH), F32),
        jax.ShapeDtypeStruct((bsz, GATE_ROWS, t_all), F32),
    )
    return pl.pallas_call(
        functools.partial(_mixin_kernel, n_ctx_tiles=n_ctx_tiles, n_tiles=n_tiles),
        grid=(bsz, n_tiles),
        in_specs=[
            pl.BlockSpec((1, TILE, d), tile_map),
            pl.BlockSpec((1, SUBLANES, d), lambda b, t: (b, jnp.maximum(t * halo_per_tile - 1, 0), 0)),
            pl.BlockSpec((1, SUBLANES, d), lambda b, t: (b, jnp.minimum((t + 1) * halo_per_tile, n_halo - 1), 0)),
            _mod_spec(d, 0, n_ctx_tiles),
            pl.BlockSpec((None, None, 1, d), lambda b, t: (layer, 1, 0, 0)),
            _const_spec((None, d, IN_W), lambda b, t: (layer, 0, 0)),
            pl.BlockSpec((None, 1, Q_LORA), lambda b, t: (layer, 0, 0)),
            _const_spec((None, Q_LORA, wuq.shape[-1]), lambda b, t: (layer, 0, 0)),
            pl.BlockSpec((None, 1, KV_LORA), lambda b, t: (layer, 0, 0)),
            _const_spec((None, KV_LORA, wukv.shape[-1]), lambda b, t: (layer, 0, 0)),
            pl.BlockSpec((TILE, rope_w), lambda b, t: (t, 0)),
            pl.BlockSpec((TILE, rope_w), lambda b, t: (t, 0)),
            pl.BlockSpec((None, 1, LANES), lambda b, t: (layer, 0, 0)),
            pl.BlockSpec((None, CONV_K, 2 * MLSTM_WIDTH), lambda b, t: (layer, 0, 0)),
            pl.BlockSpec((None, 1, 2 * MLSTM_WIDTH), lambda b, t: (layer, 0, 0)),
        ],
        out_specs=(
            pl.BlockSpec((1, ATTN_HEADS, TILE, QK_DIM), head_map),
            pl.BlockSpec((1, ATTN_HEADS, TILE, QK_DIM), head_map),
            pl.BlockSpec((1, ATTN_HEADS, V_DIM, TILE), lambda b, t: (b, 0, 0, t)),
            pl.BlockSpec((1, MLSTM_WIDTH, TILE), time_last_map),
            pl.BlockSpec((1, TILE, MLSTM_WIDTH), tile_map),
            pl.BlockSpec((1, MLSTM_WIDTH, TILE), time_last_map),
            pl.BlockSpec((1, TILE, MLSTM_WIDTH), tile_map),
            pl.BlockSpec((1, GATE_ROWS, TILE), time_last_map),
        ),
        out_shape=out_shape,
        compiler_params=_params(2),
        name=f"mixin_l{layer}",
    )(x, x, x, mod_all, g_pre, win, q_norm, wuq, kv_norm, wukv, cos4, sin4, gate_b, conv_w, conv_b)


KEY_CHUNK = 2 * TILE
ATTN_BUFFERS = 2


def _attn_kernel(q_ref, k_ref, vt_ref, o_ref, s_scr, p_scr, *, chunks):
    n_heads = q_ref.shape[1]
    lo, hi = chunks[0][0], chunks[-1][1]
    fold = lambda v: v.reshape(-1, SUBLANES, TILE)
    m8, l8 = {}, {}
    for stage in range(n_heads + 2):
        h_s, h_p, h_v = stage, stage - 1, stage - 2
        do_s, do_p, do_v = h_s < n_heads, 0 <= h_p < n_heads, 0 <= h_v < n_heads
        if do_v:
            acc = jnp.dot(vt_ref[0, h_v, :, lo:hi], p_scr[h_v % ATTN_BUFFERS, lo:hi, :],
                          preferred_element_type=F32)
            l = jnp.sum(l8.pop(h_v), axis=0, keepdims=True)
            o_ref[0, :, h_v * V_DIM:(h_v + 1) * V_DIM] = (acc / l).T
        if do_s:
            q = q_ref[0, h_s]
            m8[h_s] = jnp.full((SUBLANES, TILE), -jnp.inf, F32)
        if do_p:
            m = jnp.max(m8.pop(h_p), axis=0, keepdims=True)
            l8[h_p] = jnp.zeros((SUBLANES, TILE), F32)
        for c_lo, c_hi in chunks:
            if do_s:
                s = lax.dot_general(k_ref[0, h_s, c_lo:c_hi, :], q, (((1,), (1,)), ((), ())),
                                    preferred_element_type=F32)
                s_scr[h_s % ATTN_BUFFERS, c_lo:c_hi, :] = s
                m8[h_s] = jnp.maximum(m8[h_s], jnp.max(fold(s), axis=0))
            if do_p:
                p = jnp.exp2(s_scr[h_p % ATTN_BUFFERS, c_lo:c_hi, :] - m)
                l8[h_p] = l8[h_p] + jnp.sum(fold(p), axis=0)
                p_scr[h_p % ATTN_BUFFERS, c_lo:c_hi, :] = p.astype(BF16)


def _key_chunks(n_keys, width):
    return tuple((lo, min(lo + width, n_keys)) for lo in range(0, n_keys, width))


def _attention(q, k, vt, *, q_tile0, n_q, n_keys, name):
    bsz, n_heads, _, _ = q.shape
    return pl.pallas_call(
        functools.partial(_attn_kernel, chunks=_key_chunks(n_keys, KEY_CHUNK)),
        grid=(bsz, n_q),
        in_specs=[
            pl.BlockSpec((1, n_heads, TILE, QK_DIM), lambda b, i: (b, 0, i + q_tile0, 0)),
            pl.BlockSpec((1, n_heads, n_keys, QK_DIM), lambda b, i: (b, 0, 0, 0)),
            pl.BlockSpec((1, n_heads, V_DIM, n_keys), lambda b, i: (b, 0, 0, 0)),
        ],
        out_specs=pl.BlockSpec((1, TILE, n_heads * V_DIM), lambda b, i: (b, i, 0)),
        out_shape=jax.ShapeDtypeStruct((bsz, n_q * TILE, n_heads * V_DIM), F32),
        scratch_shapes=[pltpu.VMEM((ATTN_BUFFERS, n_keys, TILE), F32),
                        pltpu.VMEM((ATTN_BUFFERS, n_keys, TILE), BF16)],
        compiler_params=_params(2),
        name=name,
    )(q, k, vt)


STATE_ROWS = MLSTM_DH + 2 * SUBLANES


def _bwd_chunk(j, n_ctx_tiles, n_tiles):
    return jnp.where(j < n_ctx_tiles, n_ctx_tiles - 1 - j, n_tiles - 1 - (j - n_ctx_tiles))


def _mlstm_kernel(kf_ref, qtf_ref, vtf_ref, gf_ref, kb_ref, qtb_ref, vtb_ref, gb_ref, of_ref, ob_ref,
                  st_scr, m_scr, d_scr, p_scr, r_scr):
    nh = MLSTM_HEADS

    @pl.when(pl.program_id(1) == 0)
    def _():
        st_scr[...] = jnp.zeros_like(st_scr)
        m_scr[...] = jnp.zeros_like(m_scr)

    rows = [slice(i * SUBLANES, (i + 1) * SUBLANES) for i in range(GATE_ROWS // SUBLANES)]
    s_idx = lax.broadcasted_iota(jnp.int32, (TILE, TILE), 0)
    t_idx = lax.broadcasted_iota(jnp.int32, (TILE, TILE), 1)
    pad_row = lax.broadcasted_iota(jnp.int32, (2 * SUBLANES, TILE), 0) == 0
    ones_rows = jnp.where(pad_row, 1.0, 0.0).astype(BF16)

    units = []
    for d, (reverse, g_ref, g0, k_ref, qt_ref, vt_ref, o_ref) in enumerate(
            ((False, gf_ref, 0, kf_ref, qtf_ref, vtf_ref, of_ref), (True, gb_ref, 3, kb_ref, qtb_ref, vtb_ref, ob_ref))):
        a, b, a_max = g_ref[0, rows[g0], :], g_ref[0, rows[g0 + 1], :], g_ref[0, rows[g0 + 2], :]
        last = 0 if reverse else TILE - 1
        m_old = m_scr[d][:, 0:1]
        mx = jnp.maximum(m_old, a_max)
        inter = jnp.exp(m_old - mx)
        e_inv = jnp.exp(-(b + mx))
        mx_last = mx[:, last:last + 1]
        w = jnp.exp(a - mx_last)
        decay = jnp.exp(m_old - mx_last)
        m_scr[d] = jnp.broadcast_to(b[:, last:last + 1] + mx_last, (SUBLANES, LANES))
        seen = s_idx >= t_idx if reverse else s_idx <= t_idx
        for hd in range(nh):
            sl = slice(hd * MLSTM_DH, (hd + 1) * MLSTM_DH)
            row = slice(hd, hd + 1)
            units.append(dict(u=d * nh + hd, sl=sl, seen=seen, a=a[row], mx=mx[row], inter=inter[row],
                              e_inv=e_inv[row], w=w[row], decay=decay[row],
                              k_ref=k_ref, qt_ref=qt_ref, vt_ref=vt_ref, o_ref=o_ref))

    for un in units:
        a_rep = jnp.broadcast_to(un["a"] * LOG2E, (LANES, TILE)).T
        log_d = jnp.concatenate([a_rep] * (TILE // LANES), axis=1) - un["mx"] * LOG2E
        d_scr[un["u"]] = jnp.exp2(jnp.where(un["seen"], log_d, -jnp.inf))

    for un in units:
        u, sl = un["u"], un["sl"]
        lhs = jnp.concatenate([un["k_ref"][0, :, sl], st_scr[u].astype(BF16)], axis=0)
        r = jnp.dot(lhs, un["qt_ref"][0, sl, :], preferred_element_type=F32)
        p_scr[u] = (r[0:TILE] * d_scr[u]).astype(BF16)
        r_scr[u] = r[TILE:TILE + STATE_ROWS]

    for un in units:
        u, sl = un["u"], un["sl"]
        nd = jnp.dot(jnp.concatenate([un["vt_ref"][0, sl, :], ones_rows], axis=0), p_scr[u],
                     preferred_element_type=F32)
        r_state = r_scr[u]
        den = nd[MLSTM_DH:MLSTM_DH + 1] + un["inter"] * r_state[MLSTM_DH:MLSTM_DH + 1]
        scale = 1.0 / jnp.maximum(jnp.abs(den), un["e_inv"])
        h_t = (nd[0:MLSTM_DH] + un["inter"] * r_state[0:MLSTM_DH]) * scale
        un["o_ref"][0, :, sl] = h_t.T

    for un in units:
        u, sl = un["u"], un["sl"]
        vt_h = un["vt_ref"][0, sl, :]
        vw = jnp.concatenate([(vt_h.astype(F32) * un["w"]).astype(BF16),
                              jnp.where(pad_row, un["w"], 0.0).astype(BF16)], axis=0)
        st_scr[u] = un["decay"] * st_scr[u] + jnp.dot(vw, un["k_ref"][0, :, sl], preferred_element_type=F32)


def _mlstm(k_m, qt_m, vt_m, gates_t, *, n_ctx_tiles, layer):
    bsz, t_all, width = k_m.shape
    n_tiles = t_all // TILE
    bwd = functools.partial(_bwd_chunk, n_ctx_tiles=n_ctx_tiles, n_tiles=n_tiles)

    def specs(chunk):
        return [pl.BlockSpec((1, TILE, width), lambda b, j: (b, chunk(j), 0)),
                pl.BlockSpec((1, width, TILE), lambda b, j: (b, 0, chunk(j))),
                pl.BlockSpec((1, width, TILE), lambda b, j: (b, 0, chunk(j))),
                pl.BlockSpec((1, GATE_ROWS, TILE), lambda b, j: (b, 0, chunk(j)))]

    out_sds = jax.ShapeDtypeStruct((bsz, t_all, width), F32)
    return pl.pallas_call(
        _mlstm_kernel,
        grid=(bsz, n_tiles),
        in_specs=specs(lambda j: j) + specs(bwd),
        out_specs=(pl.BlockSpec((1, TILE, width), lambda b, j: (b, j, 0)),
                   pl.BlockSpec((1, TILE, width), lambda b, j: (b, bwd(j), 0))),
        out_shape=(out_sds, out_sds),
        scratch_shapes=[pltpu.VMEM((2 * MLSTM_HEADS, STATE_ROWS, MLSTM_DH), F32),
                        pltpu.VMEM((2, SUBLANES, LANES), F32),
                        pltpu.VMEM((2 * MLSTM_HEADS, TILE, TILE), F32),
                        pltpu.VMEM((2 * MLSTM_HEADS, TILE, TILE), BF16),
                        pltpu.VMEM((2 * MLSTM_HEADS, STATE_ROWS, TILE), F32)],
        compiler_params=_params(2),
        name=f"mlstm_l{layer}",
    )(k_m, qt_m, vt_m, gates_t, k_m, qt_m, vt_m, gates_t)


def _merge_body(xs, attn, hms, os, gates, g_attn, g_mlstm, g_post, wout_ref):
    a_n, hm_n = [], []
    for a, hm, o in zip(attn, hms, os):
        hm = hm * jax.nn.sigmoid(o)
        normed = []
        for hd in range(MLSTM_HEADS):
            seg = hm[:, hd * MLSTM_DH:(hd + 1) * MLSTM_DH]
            cen = seg - jnp.mean(seg, axis=-1, keepdims=True)
            normed.append(cen * lax.rsqrt(jnp.mean(cen * cen, axis=-1, keepdims=True) + EPS))
        hm_n.append((jnp.concatenate(normed, axis=-1) * g_mlstm).astype(BF16))
        a_n.append(_rms(a, g_attn).astype(BF16))
    y = jnp.dot(jnp.concatenate(a_n, axis=0), wout_ref[0:ATTN_WIDTH, :], preferred_element_type=F32)
    y = y + jnp.dot(jnp.concatenate(hm_n, axis=0), wout_ref[ATTN_WIDTH:ATTN_WIDTH + MLSTM_WIDTH, :],
                    preferred_element_type=F32)
    return [x + gate * _rms(y[k * TILE:(k + 1) * TILE], g_post) for k, (x, gate) in enumerate(zip(xs, gates))]


def _merge_ffn_kernel(*refs, plan, split):
    per_slot = 7 if split else 6
    slots = [refs[k * per_slot:(k + 1) * per_slot] for k in range(TILES_PER_STEP)]
    (ga_ref, gm_ref, gpost1_ref, wout_ref, gpre2_ref, gpost2_ref, wg_ref, wu_ref, wd_ref,
     out_ref, a_scr) = refs[TILES_PER_STEP * per_slot:]
    xs = [s[0][0] for s in slots]
    attn = [_slot_tile(plan, k, s[1:-4], split) for k, s in enumerate(slots)]
    hms = [s[-4][0] + s[-3][0] for s in slots]
    os = [s[-2][0] for s in slots]
    mods = [s[-1][0, 0] for s in slots]
    x1 = _merge_body(xs, attn, hms, os, [m[5:6] for m in mods], ga_ref[...], gm_ref[...], gpost1_ref[...], wout_ref)
    outs = _ffn_body(x1, mods, 2, gpre2_ref[...], gpost2_ref[...], wg_ref, wu_ref, wd_ref, a_scr)
    for k, out in enumerate(outs):
        out_ref[k] = out


def _merge_ffn(x, a, h_fwd, h_bwd, o, mod_all, g_attn, g_mlstm, g_pre, g_post, wout, wg, wu, wd,
               *, layer, n_ctx_tiles, t0):
    bsz, t_all, d = x.shape
    n_tiles = t_all // TILE - t0
    ff = wd.shape[-2]
    split = isinstance(a, tuple)
    a_srcs = [_as_tiles(s) for s in (a if split else (a,))]
    stream = [_as_tiles(s) for s in (h_fwd, h_bwd, o)]
    xt = _as_tiles(x)
    plan = _TilePlan(bsz, n_tiles, t0, n_ctx_tiles)
    slot_specs, slot_args = [], []
    for k in range(TILES_PER_STEP):
        slot_specs += [plan.tile_spec(xt, k)]
        if split:
            slot_specs += [plan.tile_spec(a_srcs[0], k, 0, clamp=True),
                           plan.tile_spec(a_srcs[1], k, n_ctx_tiles, clamp=True)]
        else:
            slot_specs += [plan.tile_spec(a_srcs[0], k, t0)]
        slot_specs += [plan.tile_spec(s, k) for s in stream] + [plan.mod_spec(d, k)]
        slot_args += [xt] + a_srcs + stream + [mod_all]
    out = pl.pallas_call(
        functools.partial(_merge_ffn_kernel, plan=plan, split=split),
        grid=plan.grid,
        in_specs=slot_specs + [
            pl.BlockSpec((None, 1, ATTN_WIDTH), lambda i: (layer, 0, 0)),
            pl.BlockSpec((None, 1, MLSTM_WIDTH), lambda i: (layer, 0, 0)),
            pl.BlockSpec((None, None, 1, d), lambda i: (layer, 1, 0, 0)),
            _const1((None, ATTN_WIDTH + MLSTM_WIDTH, d), (layer, 0, 0)),
        ] + _ffn_weight_specs(layer, 1, 2, d, ff),
        out_specs=pl.BlockSpec((TILES_PER_STEP, TILE, d), lambda i: (i, 0, 0)),
        out_shape=jax.ShapeDtypeStruct((bsz * n_tiles, TILE, d), F32),
        scratch_shapes=[pltpu.VMEM((TILES_PER_STEP * TILE, ff), BF16)],
        compiler_params=_params(1),
        name=f"merge_ffn_l{layer}",
    )(*slot_args, g_attn, g_mlstm, g_post, wout, g_pre, g_post, wg, wu, wd)
    return out.reshape(bsz, n_tiles * TILE, d)


def _half_swap_perm():
    idx = np.arange(ROPE_DIM)
    axis, half, freq = idx // (2 * AXIS_FREQS), (idx // AXIS_FREQS) % 2, idx % AXIS_FREQS
    return axis * 2 * AXIS_FREQS + (1 - half) * AXIS_FREQS + freq


def _prep_w_in(w_in):
    sizes = (Q_LORA, KV_LORA, ROPE_DIM, 2 * MLSTM_WIDTH, MLSTM_WIDTH, MLSTM_WIDTH, N_GATES)
    offs = np.concatenate([[0], np.cumsum(sizes)])
    cq, ckv, kr, qk, v, o, g = (w_in[..., offs[i]:offs[i + 1]] for i in range(len(sizes)))
    g = jnp.pad(g, ((0, 0), (0, 0), (0, LANES - N_GATES)))
    return jnp.concatenate([cq, ckv, kr, kr[..., _half_swap_perm()], qk, v, o, g], axis=-1).astype(BF16)


def _prep_w_uq(w_uq):
    n_layers, q_lora, _ = w_uq.shape
    w = w_uq.reshape(n_layers, q_lora, ATTN_HEADS, QK_DIM)
    nope = w[..., :NOPE_DIM].reshape(n_layers, q_lora, -1)
    rope = w[..., NOPE_DIM:]
    rope_sw = rope[..., _half_swap_perm()]
    return jnp.concatenate([nope, rope.reshape(n_layers, q_lora, -1), rope_sw.reshape(n_layers, q_lora, -1)],
                           axis=-1).astype(BF16)


def _prep_w_ukv(w_ukv):
    n_layers, kv_lora, _ = w_ukv.shape
    w = w_ukv.reshape(n_layers, kv_lora, ATTN_HEADS, NOPE_DIM + V_DIM)
    return jnp.concatenate([w[..., :NOPE_DIM].reshape(n_layers, kv_lora, -1),
                            w[..., NOPE_DIM:].reshape(n_layers, kv_lora, -1)], axis=-1).astype(BF16)


def _rope_tables(n_ctx, n_tok):
    rows = n_tok // GRID_W
    t_row = jnp.repeat(jnp.arange(rows), GRID_W).astype(F32)
    t_col = jnp.tile(jnp.arange(GRID_W), rows).astype(F32)
    inv = ROPE_BASE ** (-jnp.arange(AXIS_FREQS, dtype=F32) / AXIS_FREQS)
    ang_r = t_row[:, None] * inv
    ang_c = t_col[:, None] * inv
    cos = jnp.concatenate([jnp.cos(ang_r), jnp.cos(ang_r), jnp.cos(ang_c), jnp.cos(ang_c)], axis=-1)
    sin = jnp.concatenate([-jnp.sin(ang_r), jnp.sin(ang_r), -jnp.sin(ang_c), jnp.sin(ang_c)], axis=-1)
    cos = jnp.concatenate([jnp.ones((n_ctx, ROPE_DIM), F32), cos], axis=0)
    sin = jnp.concatenate([jnp.zeros((n_ctx, ROPE_DIM), F32), sin], axis=0)
    return jnp.tile(cos, (1, ATTN_HEADS)), jnp.tile(sin, (1, ATTN_HEADS))


def kernel(x, c, ctx, c_ctx, w_mod, b_mod, norm_pre, norm_post, ffn_w_gate, ffn_w_up, ffn_w_down,
           w_in, q_norm, w_uq, kv_norm, w_ukv, attn_out_norm, conv_w, conv_b, gate_b, mlstm_norm, w_out):
    bsz, n_tok, d = x.shape
    n_ctx = ctx.shape[1]
    n_layers = w_mod.shape[0]
    assert n_tok % TILE == 0 and n_ctx % TILE == 0 and n_tok % GRID_W == 0
    n_ctx_tiles = n_ctx // TILE
    n_lat_tiles = n_tok // TILE

    wg = jnp.swapaxes(ffn_w_gate, 2, 3).astype(BF16)
    wu = jnp.swapaxes(ffn_w_up, 2, 3).astype(BF16)
    wd = ffn_w_down.astype(BF16)
    win = _prep_w_in(w_in)
    wuq = _prep_w_uq(w_uq)
    wukv = _prep_w_ukv(w_ukv)
    wout = w_out.astype(BF16)
    g_pre = norm_pre[:, :, None, :]
    g_post = norm_post[:, :, None, :]
    q_gain = q_norm[:, None, :]
    kv_gain = kv_norm[:, None, :]
    a_gain = attn_out_norm[:, None, :]
    m_gain = mlstm_norm[:, None, :]
    conv_bias = conv_b[:, None, :]
    gate_bias = jnp.pad(gate_b.reshape(n_layers, 1, N_GATES), ((0, 0), (0, 0), (0, LANES - N_GATES)))
    cos4, sin4 = _rope_tables(n_ctx, n_tok)

    c_rows = jnp.zeros((_round_up(bsz + 1, SUBLANES), d), F32).at[:bsz].set(c).at[bsz].set(c_ctx)
    mod = _modulation(c_rows, w_mod, b_mod).reshape(n_layers, -1, N_MOD, d)
    mod_all = jnp.stack([jnp.broadcast_to(mod[:, bsz:bsz + 1], (n_layers, bsz, N_MOD, d)), mod[:, :bsz]], axis=2)

    h = (ctx, x)
    for l in range(n_layers):
        t0 = n_ctx_tiles if l == n_layers - 1 else 0
        ml = mod_all[l]
        h = _ffn(h, ml, g_pre, g_post, wg, wu, wd, layer=l, j=0, sub=0, n_ctx_tiles=n_ctx_tiles)
        q, k, vt, qt_m, k_m, vt_m, o_pre, gates_t = _mixin(
            h, ml, g_pre, win, q_gain, wuq, kv_gain, wukv, cos4, sin4, gate_bias, conv_w, conv_bias,
            layer=l, n_ctx_tiles=n_ctx_tiles)
        a = _attention(q, k, vt, q_tile0=n_ctx_tiles, n_q=n_lat_tiles, n_keys=n_ctx + n_tok,
                       name=f"attention_l{l}")
        if t0 == 0:
            a = (_attention(q, k, vt, q_tile0=0, n_q=n_ctx_tiles, n_keys=n_ctx, name=f"attention_ctx_l{l}"), a)
        h_fwd, h_bwd = _mlstm(k_m, qt_m, vt_m, gates_t, n_ctx_tiles=n_ctx_tiles, layer=l)
        h = _merge_ffn(h, a, h_fwd, h_bwd, o_pre, ml, a_gain, m_gain, g_pre, g_post, wout, wg, wu, wd,
                       layer=l, n_ctx_tiles=n_ctx_tiles, t0=t0)
    return h
```

```python
import functools

import jax
import jax.numpy as jnp
import numpy as np
from jax import lax
from jax.experimental import pallas as pl
from jax.experimental.pallas import tpu as pltpu

F32 = jnp.float32
BF16 = jnp.bfloat16

N_MOD = 9
EPS = 1e-6
GRID_W = 64
ATTN_HEADS = 4
Q_LORA = 256
KV_LORA = 128
NOPE_DIM = 128
ROPE_DIM = 64
AXIS_FREQS = ROPE_DIM // 4
V_DIM = 128
QK_DIM = NOPE_DIM + ROPE_DIM
ATTN_WIDTH = ATTN_HEADS * V_DIM
ROPE_BASE = 10000.0
SM_SCALE = QK_DIM ** -0.5
LOG2E = float(np.log2(np.e))
Q_SCALE = SM_SCALE * LOG2E
MLSTM_HEADS = 4
MLSTM_DH = 128
MLSTM_WIDTH = MLSTM_HEADS * MLSTM_DH
CONV_K = 3
N_GATES = 4 * MLSTM_HEADS

LANES = 128
SUBLANES = 8
MXU_DIM = 256
TILE = MXU_DIM
VMEM_LIMIT = 56 * 1024 * 1024

A_W = Q_LORA + KV_LORA + 2 * ROPE_DIM
QK_OFF = A_W
V_OFF = QK_OFF + 2 * MLSTM_WIDTH
O_OFF = V_OFF + MLSTM_WIDTH
G_OFF = O_OFF + MLSTM_WIDTH
IN_W = G_OFF + LANES


def _round_up(n, m):
    return (n + m - 1) // m * m


def _rms(x, g):
    return x * lax.rsqrt(jnp.mean(x * x, axis=-1, keepdims=True) + EPS) * g


def _dot_nt(x, w_t):
    return lax.dot_general(x, w_t, (((1,), (1,)), ((), ())), preferred_element_type=F32)


def _sandwich_in(x, mod, sub, g):
    return _rms(x, g) * (1.0 + mod[3 * sub + 1:3 * sub + 2]) + mod[3 * sub:3 * sub + 1]


def _params(n_grid):
    return pltpu.CompilerParams(dimension_semantics=("arbitrary",) * n_grid,
                                vmem_limit_bytes=VMEM_LIMIT)


def _const_spec(shape, index_map):
    return pl.BlockSpec(shape, index_map, pipeline_mode=pl.Buffered(1))


def _mod_spec(d, tile0, n_ctx_tiles):
    return pl.BlockSpec((1, 1, N_MOD, d),
                        lambda b, t: (b, (t + tile0 >= n_ctx_tiles).astype(jnp.int32), 0, 0))


TILES_PER_STEP = 2


class _TilePlan:
    def __init__(self, bsz, n, t0, n_ctx_tiles):
        assert (bsz * n) % TILES_PER_STEP == 0
        self.bsz, self.n, self.t0, self.n_ctx_tiles = bsz, n, t0, n_ctx_tiles
        self.grid = (bsz * n // TILES_PER_STEP,)

    def coords(self, i, k):
        g = i * TILES_PER_STEP + k
        return g // self.n, g % self.n + self.t0

    def is_ctx(self, k):
        return self.coords(pl.program_id(0), k)[1] < self.n_ctx_tiles

    def tile_spec(self, arr, k, arr_t0=0, clamp=False):
        arr_tiles = arr.shape[0] // self.bsz

        def index(i):
            b, r = self.coords(i, k)
            r = r - arr_t0
            if clamp:
                r = jnp.clip(r, 0, arr_tiles - 1)
            return (b * arr_tiles + r, 0, 0)

        return pl.BlockSpec((1, TILE, arr.shape[2]), index)

    def mod_spec(self, d, k):
        def index(i):
            b, r = self.coords(i, k)
            return (b, (r >= self.n_ctx_tiles).astype(jnp.int32), 0, 0)

        return pl.BlockSpec((1, 1, N_MOD, d), index)


def _as_tiles(x):
    return x.reshape(-1, TILE, x.shape[-1])


def _const1(shape, index):
    return _const_spec(shape, lambda i: index)


def _mod_kernel(c_ref, w_ref, b_ref, o_ref):
    c = c_ref[...]
    h = (c * jax.nn.sigmoid(c)).astype(BF16)
    o_ref[0] = jnp.dot(h, w_ref[0].astype(BF16), preferred_element_type=F32) + b_ref[0]


def _modulation(c_rows, w_mod, b_mod):
    n_layers, d, n_out = w_mod.shape
    rows = c_rows.shape[0]
    n_blk = N_MOD * LANES
    return pl.pallas_call(
        _mod_kernel,
        grid=(n_layers, n_out // n_blk),
        in_specs=[pl.BlockSpec((rows, d), lambda l, j: (0, 0)),
                  pl.BlockSpec((1, d, n_blk), lambda l, j: (l, 0, j)),
                  pl.BlockSpec((1, 1, n_blk), lambda l, j: (l, 0, j))],
        out_specs=pl.BlockSpec((1, rows, n_blk), lambda l, j: (l, 0, j)),
        out_shape=jax.ShapeDtypeStruct((n_layers, rows, n_out), F32),
        compiler_params=_params(2),
        name="modulation",
    )(c_rows, w_mod, b_mod.reshape(n_layers, 1, n_out))


def _ffn_body(xs, mods, sub, g_pre, g_post, wg_ref, wu_ref, wd_ref, a_ref):
    h = jnp.concatenate([_sandwich_in(x, m, sub, g_pre).astype(BF16) for x, m in zip(xs, mods)], axis=0)
    ff = a_ref.shape[1]
    for lo in range(0, ff, MXU_DIM):
        hi = min(lo + MXU_DIM, ff)
        g = _dot_nt(h, wg_ref[lo:hi, :])
        u = _dot_nt(h, wu_ref[lo:hi, :])
        a_ref[:, lo:hi] = (g * jax.nn.sigmoid(g) * u).astype(BF16)
    ff_main = ff // MXU_DIM * MXU_DIM
    y = jnp.dot(a_ref[:, 0:ff_main], wd_ref[0:ff_main, :], preferred_element_type=F32)
    if ff_main < ff:
        y = y + jnp.dot(a_ref[:, ff_main:ff], wd_ref[ff_main:ff, :], preferred_element_type=F32)
    return [x + 0.5 * m[3 * sub + 2:3 * sub + 3] * _rms(y[k * TILE:(k + 1) * TILE], g_post)
            for k, (x, m) in enumerate(zip(xs, mods))]


def _slot_tile(plan, k, refs, split):
    return jnp.where(plan.is_ctx(k), refs[0][0], refs[1][0]) if split else refs[0][0]


def _ffn_kernel(*refs, plan, sub, split):
    per_slot = 3 if split else 2
    slots = [refs[k * per_slot:(k + 1) * per_slot] for k in range(TILES_PER_STEP)]
    gpre_ref, gpost_ref, wg_ref, wu_ref, wd_ref, o_ref, a_ref = refs[TILES_PER_STEP * per_slot:]
    xs = [_slot_tile(plan, k, s[:-1], split) for k, s in enumerate(slots)]
    mods = [s[-1][0, 0] for s in slots]
    outs = _ffn_body(xs, mods, sub, gpre_ref[...], gpost_ref[...], wg_ref, wu_ref, wd_ref, a_ref)
    for k, out in enumerate(outs):
        o_ref[k] = out


def _ffn_weight_specs(layer, j, sub, d, ff):
    return [
        pl.BlockSpec((None, None, 1, d), lambda i: (layer, sub, 0, 0)),
        pl.BlockSpec((None, None, 1, d), lambda i: (layer, sub, 0, 0)),
        _const1((None, None, ff, d), (layer, j, 0, 0)),
        _const1((None, None, ff, d), (layer, j, 0, 0)),
        _const1((None, None, ff, d), (layer, j, 0, 0)),
    ]


def _ffn(src, mod_all, g_pre, g_post, wg, wu, wd, *, layer, j, sub, n_ctx_tiles):
    split = isinstance(src, tuple)
    srcs = [_as_tiles(s) for s in (src if split else (src,))]
    bsz, d = mod_all.shape[0], srcs[0].shape[-1]
    n_tiles = sum(s.shape[0] for s in srcs) // bsz
    ff = wd.shape[-2]
    plan = _TilePlan(bsz, n_tiles, 0, n_ctx_tiles)
    slot_specs, slot_args = [], []
    for k in range(TILES_PER_STEP):
        if split:
            slot_specs += [plan.tile_spec(srcs[0], k, 0, clamp=True), plan.tile_spec(srcs[1], k, n_ctx_tiles, clamp=True)]
        else:
            slot_specs += [plan.tile_spec(srcs[0], k)]
        slot_specs += [plan.mod_spec(d, k)]
        slot_args += srcs + [mod_all]
    out = pl.pallas_call(
        functools.partial(_ffn_kernel, plan=plan, sub=sub, split=split),
        grid=plan.grid,
        in_specs=slot_specs + _ffn_weight_specs(layer, j, sub, d, ff),
        out_specs=pl.BlockSpec((TILES_PER_STEP, TILE, d), lambda i: (i, 0, 0)),
        out_shape=jax.ShapeDtypeStruct((bsz * n_tiles, TILE, d), F32),
        scratch_shapes=[pltpu.VMEM((TILES_PER_STEP * TILE, ff), BF16)],
        compiler_params=_params(1),
        name=f"ffn_l{layer}_{j}",
    )(*slot_args, g_pre, g_post, wg, wu, wd)
    return out.reshape(bsz, n_tiles * TILE, d)


GATE_ROWS = 2 * 3 * SUBLANES


def _log_sigmoid(x):
    return jnp.minimum(x, 0.0) - jnp.log1p(jnp.exp(-jnp.abs(x)))


def _lane_scan(x, op, identity, reverse):
    n = x.shape[1]
    lane = lax.broadcasted_iota(jnp.int32, x.shape, 1)
    shift = 1
    while shift < n:
        if reverse:
            moved = jnp.where(lane < n - shift, pltpu.roll(x, n - shift, axis=1), identity)
        else:
            moved = jnp.where(lane >= shift, pltpu.roll(x, shift, axis=1), identity)
        x = op(x, moved)
        shift *= 2
    return x


def _gate_rows(g8, reverse):
    nh = MLSTM_HEADS
    head_rows = lax.broadcasted_iota(jnp.int32, g8.shape, 0) < nh
    b = _lane_scan(_log_sigmoid(g8), jnp.add, 0.0, reverse)
    b = jnp.where(head_rows, pltpu.roll(b, nh, axis=0), 0.0)
    a = jnp.where(head_rows, g8 - b, 0.0)
    return a, b, _lane_scan(a, jnp.maximum, -jnp.inf, reverse)


def _mixin_kernel(x_ref, xp_ref, xn_ref, mod_ref, gpre_ref, win_ref, qn_ref, wuq_ref, kvn_ref, wukv_ref,
                  cos_ref, sin_ref, gb_ref, cw_ref, cb_ref,
                  q_ref, k_ref, vt_ref, qtm_ref, km_ref, vtm_ref, o_ref, gt_ref, *, n_ctx_tiles, n_tiles):
    ci = pl.program_id(1)
    mod = mod_ref[0, 0]
    h = _sandwich_in(x_ref[0], mod, 1, gpre_ref[...]).astype(BF16)
    cos4 = cos_ref[...]
    sin4 = sin_ref[...]

    gates = _dot_nt(h, win_ref[G_OFF:IN_W, :]) + gb_ref[...]
    g_t = gates.T[0:N_GATES, :]
    half = N_GATES // 2
    gt_ref[0] = jnp.concatenate(_gate_rows(g_t[0:half], False) + _gate_rows(g_t[half:N_GATES], True), axis=0)

    halo = jnp.concatenate([xp_ref[0], xn_ref[0]], axis=0)
    h_ext = jnp.concatenate([h, _sandwich_in(halo, mod, 1, gpre_ref[...]).astype(BF16)], axis=0)
    row = lax.broadcasted_iota(jnp.int32, (TILE, 1), 0)
    has_prev = jnp.logical_and(ci != 0, ci != n_ctx_tiles)
    has_next = jnp.logical_and(ci != n_ctx_tiles - 1, ci != n_tiles - 1)

    def conv_silu(lo):
        cols = slice(lo, lo + MLSTM_WIDTH)
        pqk = _dot_nt(h_ext, win_ref[QK_OFF + lo:QK_OFF + lo + MLSTM_WIDTH, :])
        x = pqk[0:TILE]
        prev_row = jnp.where(has_prev, pqk[TILE + SUBLANES - 1:TILE + SUBLANES], 0.0)
        next_row = jnp.where(has_next, pqk[TILE + SUBLANES:TILE + SUBLANES + 1], 0.0)
        x_prev = jnp.where(row == 0, prev_row, pltpu.roll(x, 1, axis=0))
        x_next = jnp.where(row == TILE - 1, next_row, pltpu.roll(x, TILE - 1, axis=0))
        u = x_prev * cw_ref[0:1, cols] + x * cw_ref[1:2, cols] + x_next * cw_ref[2:3, cols] + cb_ref[:, cols]
        return u * jax.nn.sigmoid(u)

    pa = _dot_nt(h, win_ref[0:A_W, :])
    cq = pa[:, 0:Q_LORA]
    ckv = pa[:, Q_LORA:Q_LORA + KV_LORA]
    kr = pa[:, Q_LORA + KV_LORA:Q_LORA + KV_LORA + ROPE_DIM]
    kr_sw = pa[:, Q_LORA + KV_LORA + ROPE_DIM:A_W]
    k_rope = (kr * cos4[:, 0:ROPE_DIM] + kr_sw * sin4[:, 0:ROPE_DIM]).astype(BF16)

    nope_w = ATTN_HEADS * NOPE_DIM
    rope_w = ATTN_HEADS * ROPE_DIM
    qa = jnp.dot(_rms(cq, qn_ref[...]).astype(BF16), wuq_ref[...], preferred_element_type=F32)
    q_rope = qa[:, nope_w:nope_w + rope_w] * cos4 + qa[:, nope_w + rope_w:nope_w + 2 * rope_w] * sin4
    kva = jnp.dot(_rms(ckv, kvn_ref[...]).astype(BF16), wukv_ref[...], preferred_element_type=F32)
    for hd in range(ATTN_HEADS):
        q_ref[0, hd, :, 0:NOPE_DIM] = (qa[:, hd * NOPE_DIM:(hd + 1) * NOPE_DIM] * Q_SCALE).astype(BF16)
        q_ref[0, hd, :, NOPE_DIM:QK_DIM] = (q_rope[:, hd * ROPE_DIM:(hd + 1) * ROPE_DIM] * Q_SCALE).astype(BF16)
        k_ref[0, hd, :, 0:NOPE_DIM] = kva[:, hd * NOPE_DIM:(hd + 1) * NOPE_DIM].astype(BF16)
        k_ref[0, hd, :, NOPE_DIM:QK_DIM] = k_rope
        vt_ref[0, hd] = kva[:, nope_w + hd * V_DIM:nope_w + (hd + 1) * V_DIM].T.astype(BF16)

    vtm_ref[0] = _dot_nt(h, win_ref[V_OFF:O_OFF, :]).T.astype(BF16)
    o_ref[0] = _dot_nt(h, win_ref[O_OFF:G_OFF, :])
    qtm_ref[0] = conv_silu(0).T.astype(BF16)
    km_ref[0] = (conv_silu(MLSTM_WIDTH) * (MLSTM_DH ** -0.5)).astype(BF16)


def _mixin(x, mod_all, g_pre, win, q_norm, wuq, kv_norm, wukv, cos4, sin4, gate_b, conv_w, conv_b,
           *, layer, n_ctx_tiles):
    bsz, t_all, d = x.shape
    n_tiles = t_all // TILE
    rope_w = ATTN_HEADS * ROPE_DIM
    halo_per_tile = TILE // SUBLANES
    n_halo = t_all // SUBLANES
    tile_map = lambda b, t: (b, t, 0)
    head_map = lambda b, t: (b, 0, t, 0)
    time_last_map = lambda b, t: (b, 0, t)
    out_shape = (
        jax.ShapeDtypeStruct((bsz, ATTN_HEADS, t_all, QK_DIM), BF16),
        jax.ShapeDtypeStruct((bsz, ATTN_HEADS, t_all, QK_DIM), BF16),
        jax.ShapeDtypeStruct((bsz, ATTN_HEADS, V_DIM, t_all), BF16),
        jax.ShapeDtypeStruct((bsz, MLSTM_WIDTH, t_all), BF16),
        jax.ShapeDtypeStruct((bsz, t_all, MLSTM_WIDTH), BF16),
        jax.ShapeDtypeStruct((bsz, MLSTM_WIDTH, t_all), BF16),
        jax.ShapeDtypeStruct((bsz, t_all, MLSTM_WIDTH), F32),
        jax.ShapeDtypeStruct((bsz, GATE_ROWS, t_all), F32),
    )
    return pl.pallas_call(
        functools.partial(_mixin_kernel, n_ctx_tiles=n_ctx_tiles, n_tiles=n_tiles),
        grid=(bsz, n_tiles),
        in_specs=[
            pl.BlockSpec((1, TILE, d), tile_map),
            pl.BlockSpec((1, SUBLANES, d), lambda b, t: (b, jnp.maximum(t * halo_per_tile - 1, 0), 0)),
            pl.BlockSpec((1, SUBLANES, d), lambda b, t: (b, jnp.minimum((t + 1) * halo_per_tile, n_halo - 1), 0)),
            _mod_spec(d, 0, n_ctx_tiles),
            pl.BlockSpec((None, None, 1, d), lambda b, t: (layer, 1, 0, 0)),
            _const_spec((None, IN_W, d), lambda b, t: (layer, 0, 0)),
            pl.BlockSpec((None, 1, Q_LORA), lambda b, t: (layer, 0, 0)),
            _const_spec((None, Q_LORA, wuq.shape[-1]), lambda b, t: (layer, 0, 0)),
            pl.BlockSpec((None, 1, KV_LORA), lambda b, t: (layer, 0, 0)),
            _const_spec((None, KV_LORA, wukv.shape[-1]), lambda b, t: (layer, 0, 0)),
            pl.BlockSpec((TILE, rope_w), lambda b, t: (t, 0)),
            pl.BlockSpec((TILE, rope_w), lambda b, t: (t, 0)),
            pl.BlockSpec((None, 1, LANES), lambda b, t: (layer, 0, 0)),
            pl.BlockSpec((None, CONV_K, 2 * MLSTM_WIDTH), lambda b, t: (layer, 0, 0)),
            pl.BlockSpec((None, 1, 2 * MLSTM_WIDTH), lambda b, t: (layer, 0, 0)),
        ],
        out_specs=(
            pl.BlockSpec((1, ATTN_HEADS, TILE, QK_DIM), head_map),
            pl.BlockSpec((1, ATTN_HEADS, TILE, QK_DIM), head_map),
            pl.BlockSpec((1, ATTN_HEADS, V_DIM, TILE), lambda b, t: (b, 0, 0, t)),
            pl.BlockSpec((1, MLSTM_WIDTH, TILE), time_last_map),
            pl.BlockSpec((1, TILE, MLSTM_WIDTH), tile_map),
            pl.BlockSpec((1, MLSTM_WIDTH, TILE), time_last_map),
            pl.BlockSpec((1, TILE, MLSTM_WIDTH), tile_map),
            pl.BlockSpec((1, GATE_ROWS, TILE), time_last_map),
        ),
        out_shape=out_shape,
        compiler_params=_params(2),
        name=f"mixin_l{layer}",
    )(x, x, x, mod_all, g_pre, win, q_norm, wuq, kv_norm, wukv, cos4, sin4, gate_b, conv_w, conv_b)


KEY_CHUNK = 2 * TILE
ATTN_BUFFERS = 2


def _attn_kernel(*refs, chunks):
    q_refs = refs[:-5]
    k_ref, vt_ref, o_ref, s_scr, p_scr = refs[-5:]
    n_heads = k_ref.shape[1]
    units = [(j, h) for j in range(len(q_refs)) for h in range(n_heads)]
    lo, hi = chunks[0][0], chunks[-1][1]
    fold = lambda v: v.reshape(-1, SUBLANES, TILE)
    m8, l8 = {}, {}
    for stage in range(len(units) + 2):
        u_s, u_p, u_v = stage, stage - 1, stage - 2
        do_s, do_p, do_v = u_s < len(units), 0 <= u_p < len(units), 0 <= u_v < len(units)
        if do_v:
            j, h = units[u_v]
            acc = jnp.dot(vt_ref[0, h, :, lo:hi], p_scr[u_v % ATTN_BUFFERS, lo:hi, :], preferred_element_type=F32)
            l = jnp.sum(l8.pop(u_v), axis=0, keepdims=True)
            o_ref[0, j * TILE:(j + 1) * TILE, h * V_DIM:(h + 1) * V_DIM] = (acc / l).T
        if do_s:
            j, h_s = units[u_s]
            q = q_refs[j][0, h_s]
            m8[u_s] = jnp.full((SUBLANES, TILE), -jnp.inf, F32)
        if do_p:
            m = jnp.max(m8.pop(u_p), axis=0, keepdims=True)
            l8[u_p] = jnp.zeros((SUBLANES, TILE), F32)
        for c_lo, c_hi in chunks:
            if do_s:
                s = lax.dot_general(k_ref[0, h_s, c_lo:c_hi, :], q, (((1,), (1,)), ((), ())),
                                    preferred_element_type=F32)
                s_scr[u_s % ATTN_BUFFERS, c_lo:c_hi, :] = s
                m8[u_s] = jnp.maximum(m8[u_s], jnp.max(fold(s), axis=0))
            if do_p:
                p = jnp.exp2(s_scr[u_p % ATTN_BUFFERS, c_lo:c_hi, :] - m)
                l8[u_p] = l8[u_p] + jnp.sum(fold(p), axis=0)
                p_scr[u_p % ATTN_BUFFERS, c_lo:c_hi, :] = p.astype(BF16)


def _key_chunks(n_keys, width):
    return tuple((lo, min(lo + width, n_keys)) for lo in range(0, n_keys, width))


def _attention(q, k, vt, *, q_tile0, n_q, n_keys, name):
    bsz, n_heads, _, _ = q.shape
    qts = 2 if n_q % 2 == 0 else 1
    q_specs = [pl.BlockSpec((1, n_heads, TILE, QK_DIM), lambda b, i, j=j: (b, 0, i * qts + j + q_tile0, 0))
               for j in range(qts)]
    return pl.pallas_call(
        functools.partial(_attn_kernel, chunks=_key_chunks(n_keys, KEY_CHUNK)),
        grid=(bsz, n_q // qts),
        in_specs=q_specs + [
            pl.BlockSpec((1, n_heads, n_keys, QK_DIM), lambda b, i: (b, 0, 0, 0)),
            pl.BlockSpec((1, n_heads, V_DIM, n_keys), lambda b, i: (b, 0, 0, 0)),
        ],
        out_specs=pl.BlockSpec((1, qts * TILE, n_heads * V_DIM), lambda b, i: (b, i, 0)),
        out_shape=jax.ShapeDtypeStruct((bsz, n_q * TILE, n_heads * V_DIM), F32),
        scratch_shapes=[pltpu.VMEM((ATTN_BUFFERS, n_keys, TILE), F32),
                        pltpu.VMEM((ATTN_BUFFERS, n_keys, TILE), BF16)],
        compiler_params=_params(2),
        name=name,
    )(*([q] * qts), k, vt)


STATE_ROWS = MLSTM_DH + 2 * SUBLANES


def _bwd_chunk(j, n_ctx_tiles, n_tiles):
    return jnp.where(j < n_ctx_tiles, n_ctx_tiles - 1 - j, n_tiles - 1 - (j - n_ctx_tiles))


def _mlstm_kernel(kf_ref, qtf_ref, vtf_ref, gf_ref, kb_ref, qtb_ref, vtb_ref, gb_ref, of_ref, ob_ref,
                  st_scr, m_scr, d_scr, p_scr, r_scr):
    nh = MLSTM_HEADS

    @pl.when(pl.program_id(1) == 0)
    def _():
        st_scr[...] = jnp.zeros_like(st_scr)
        m_scr[...] = jnp.zeros_like(m_scr)

    rows = [slice(i * SUBLANES, (i + 1) * SUBLANES) for i in range(GATE_ROWS // SUBLANES)]
    s_idx = lax.broadcasted_iota(jnp.int32, (TILE, TILE), 0)
    t_idx = lax.broadcasted_iota(jnp.int32, (TILE, TILE), 1)
    pad_row = lax.broadcasted_iota(jnp.int32, (2 * SUBLANES, TILE), 0) == 0
    ones_rows = jnp.where(pad_row, 1.0, 0.0).astype(BF16)

    units = []
    for d, (reverse, g_ref, g0, k_ref, qt_ref, vt_ref, o_ref) in enumerate(
            ((False, gf_ref, 0, kf_ref, qtf_ref, vtf_ref, of_ref), (True, gb_ref, 3, kb_ref, qtb_ref, vtb_ref, ob_ref))):
        a, b, a_max = g_ref[0, rows[g0], :], g_ref[0, rows[g0 + 1], :], g_ref[0, rows[g0 + 2], :]
        last = 0 if reverse else TILE - 1
        m_old = m_scr[d][:, 0:1]
        mx = jnp.maximum(m_old, a_max)
        inter = jnp.exp(m_old - mx)
        e_inv = jnp.exp(-(b + mx))
        mx_last = mx[:, last:last + 1]
        w = jnp.exp(a - mx_last)
        decay = jnp.exp(m_old - mx_last)
        m_scr[d] = jnp.broadcast_to(b[:, last:last + 1] + mx_last, (SUBLANES, LANES))
        seen = s_idx >= t_idx if reverse else s_idx <= t_idx
        for hd in range(nh):
            sl = slice(hd * MLSTM_DH, (hd + 1) * MLSTM_DH)
            row = slice(hd, hd + 1)
            units.append(dict(u=d * nh + hd, sl=sl, seen=seen, a=a[row], mx=mx[row], inter=inter[row],
                              e_inv=e_inv[row], w=w[row], decay=decay[row],
                              k_ref=k_ref, qt_ref=qt_ref, vt_ref=vt_ref, o_ref=o_ref))

    for un in units:
        a_rep = jnp.broadcast_to(un["a"] * LOG2E, (LANES, TILE)).T
        log_d = jnp.concatenate([a_rep] * (TILE // LANES), axis=1) - un["mx"] * LOG2E
        d_scr[un["u"]] = jnp.exp2(jnp.where(un["seen"], log_d, -jnp.inf))

    for un in units:
        u, sl = un["u"], un["sl"]
        lhs = jnp.concatenate([un["k_ref"][0, :, sl], st_scr[u].astype(BF16)], axis=0)
        r = jnp.dot(lhs, un["qt_ref"][0, sl, :], preferred_element_type=F32)
        p_scr[u] = (r[0:TILE] * d_scr[u]).astype(BF16)
        r_scr[u] = r[TILE:TILE + STATE_ROWS]

    for un in units:
        u, sl = un["u"], un["sl"]
        nd = jnp.dot(jnp.concatenate([un["vt_ref"][0, sl, :], ones_rows], axis=0), p_scr[u],
                     preferred_element_type=F32)
        r_state = r_scr[u]
        den = nd[MLSTM_DH:MLSTM_DH + 1] + un["inter"] * r_state[MLSTM_DH:MLSTM_DH + 1]
        scale = 1.0 / jnp.maximum(jnp.abs(den), un["e_inv"])
        h_t = (nd[0:MLSTM_DH] + un["inter"] * r_state[0:MLSTM_DH]) * scale
        un["o_ref"][0, :, sl] = h_t.T

    for un in units:
        u, sl = un["u"], un["sl"]
        vt_h = un["vt_ref"][0, sl, :]
        vw = jnp.concatenate([(vt_h.astype(F32) * un["w"]).astype(BF16),
                              jnp.where(pad_row, un["w"], 0.0).astype(BF16)], axis=0)
        st_scr[u] = un["decay"] * st_scr[u] + jnp.dot(vw, un["k_ref"][0, :, sl], preferred_element_type=F32)


def _mlstm(k_m, qt_m, vt_m, gates_t, *, n_ctx_tiles, layer):
    bsz, t_all, width = k_m.shape
    n_tiles = t_all // TILE
    bwd = functools.partial(_bwd_chunk, n_ctx_tiles=n_ctx_tiles, n_tiles=n_tiles)

    def specs(chunk):
        return [pl.BlockSpec((1, TILE, width), lambda b, j: (b, chunk(j), 0)),
                pl.BlockSpec((1, width, TILE), lambda b, j: (b, 0, chunk(j))),
                pl.BlockSpec((1, width, TILE), lambda b, j: (b, 0, chunk(j))),
                pl.BlockSpec((1, GATE_ROWS, TILE), lambda b, j: (b, 0, chunk(j)))]

    out_sds = jax.ShapeDtypeStruct((bsz, t_all, width), F32)
    return pl.pallas_call(
        _mlstm_kernel,
        grid=(bsz, n_tiles),
        in_specs=specs(lambda j: j) + specs(bwd),
        out_specs=(pl.BlockSpec((1, TILE, width), lambda b, j: (b, j, 0)),
                   pl.BlockSpec((1, TILE, width), lambda b, j: (b, bwd(j), 0))),
        out_shape=(out_sds, out_sds),
        scratch_shapes=[pltpu.VMEM((2 * MLSTM_HEADS, STATE_ROWS, MLSTM_DH), F32),
                        pltpu.VMEM((2, SUBLANES, LANES), F32),
                        pltpu.VMEM((2 * MLSTM_HEADS, TILE, TILE), F32),
                        pltpu.VMEM((2 * MLSTM_HEADS, TILE, TILE), BF16),
                        pltpu.VMEM((2 * MLSTM_HEADS, STATE_ROWS, TILE), F32)],
        compiler_params=_params(2),
        name=f"mlstm_l{layer}",
    )(k_m, qt_m, vt_m, gates_t, k_m, qt_m, vt_m, gates_t)


def _merge_body(xs, attn, hms, os, gates, g_attn, g_mlstm, g_post, wout_ref):
    a_n, hm_n = [], []
    for a, hm, o in zip(attn, hms, os):
        hm = hm * jax.nn.sigmoid(o)
        normed = []
        for hd in range(MLSTM_HEADS):
            seg = hm[:, hd * MLSTM_DH:(hd + 1) * MLSTM_DH]
            cen = seg - jnp.mean(seg, axis=-1, keepdims=True)
            normed.append(cen * lax.rsqrt(jnp.mean(cen * cen, axis=-1, keepdims=True) + EPS))
        hm_n.append((jnp.concatenate(normed, axis=-1) * g_mlstm).astype(BF16))
        a_n.append(_rms(a, g_attn).astype(BF16))
    y = jnp.dot(jnp.concatenate(a_n, axis=0), wout_ref[0:ATTN_WIDTH, :], preferred_element_type=F32)
    y = y + jnp.dot(jnp.concatenate(hm_n, axis=0), wout_ref[ATTN_WIDTH:ATTN_WIDTH + MLSTM_WIDTH, :],
                    preferred_element_type=F32)
    return [x + gate * _rms(y[k * TILE:(k + 1) * TILE], g_post) for k, (x, gate) in enumerate(zip(xs, gates))]


def _merge_ffn_kernel(*refs, plan, split):
    per_slot = 7 if split else 6
    slots = [refs[k * per_slot:(k + 1) * per_slot] for k in range(TILES_PER_STEP)]
    (ga_ref, gm_ref, gpost1_ref, wout_ref, gpre2_ref, gpost2_ref, wg_ref, wu_ref, wd_ref,
     out_ref, a_scr) = refs[TILES_PER_STEP * per_slot:]
    xs = [s[0][0] for s in slots]
    attn = [_slot_tile(plan, k, s[1:-4], split) for k, s in enumerate(slots)]
    hms = [s[-4][0] + s[-3][0] for s in slots]
    os = [s[-2][0] for s in slots]
    mods = [s[-1][0, 0] for s in slots]
    x1 = _merge_body(xs, attn, hms, os, [m[5:6] for m in mods], ga_ref[...], gm_ref[...], gpost1_ref[...], wout_ref)
    outs = _ffn_body(x1, mods, 2, gpre2_ref[...], gpost2_ref[...], wg_ref, wu_ref, wd_ref, a_scr)
    for k, out in enumerate(outs):
        out_ref[k] = out


def _merge_ffn(x, a, h_fwd, h_bwd, o, mod_all, g_attn, g_mlstm, g_pre, g_post, wout, wg, wu, wd,
               *, layer, n_ctx_tiles, t0):
    bsz, t_all, d = x.shape
    n_tiles = t_all // TILE - t0
    ff = wd.shape[-2]
    split = isinstance(a, tuple)
    a_srcs = [_as_tiles(s) for s in (a if split else (a,))]
    stream = [_as_tiles(s) for s in (h_fwd, h_bwd, o)]
    xt = _as_tiles(x)
    plan = _TilePlan(bsz, n_tiles, t0, n_ctx_tiles)
    slot_specs, slot_args = [], []
    for k in range(TILES_PER_STEP):
        slot_specs += [plan.tile_spec(xt, k)]
        if split:
            slot_specs += [plan.tile_spec(a_srcs[0], k, 0, clamp=True),
                           plan.tile_spec(a_srcs[1], k, n_ctx_tiles, clamp=True)]
        else:
            slot_specs += [plan.tile_spec(a_srcs[0], k, t0)]
        slot_specs += [plan.tile_spec(s, k) for s in stream] + [plan.mod_spec(d, k)]
        slot_args += [xt] + a_srcs + stream + [mod_all]
    out = pl.pallas_call(
        functools.partial(_merge_ffn_kernel, plan=plan, split=split),
        grid=plan.grid,
        in_specs=slot_specs + [
            pl.BlockSpec((None, 1, ATTN_WIDTH), lambda i: (layer, 0, 0)),
            pl.BlockSpec((None, 1, MLSTM_WIDTH), lambda i: (layer, 0, 0)),
            pl.BlockSpec((None, None, 1, d), lambda i: (layer, 1, 0, 0)),
            _const1((None, ATTN_WIDTH + MLSTM_WIDTH, d), (layer, 0, 0)),
        ] + _ffn_weight_specs(layer, 1, 2, d, ff),
        out_specs=pl.BlockSpec((TILES_PER_STEP, TILE, d), lambda i: (i, 0, 0)),
        out_shape=jax.ShapeDtypeStruct((bsz * n_tiles, TILE, d), F32),
        scratch_shapes=[pltpu.VMEM((TILES_PER_STEP * TILE, ff), BF16)],
        compiler_params=_params(1),
        name=f"merge_ffn_l{layer}",
    )(*slot_args, g_attn, g_mlstm, g_post, wout, g_pre, g_post, wg, wu, wd)
    return out.reshape(bsz, n_tiles * TILE, d)


def _half_swap_perm():
    idx = np.arange(ROPE_DIM)
    axis, half, freq = idx // (2 * AXIS_FREQS), (idx // AXIS_FREQS) % 2, idx % AXIS_FREQS
    return axis * 2 * AXIS_FREQS + (1 - half) * AXIS_FREQS + freq


def _prep_w_in(w_in):
    sizes = (Q_LORA, KV_LORA, ROPE_DIM, 2 * MLSTM_WIDTH, MLSTM_WIDTH, MLSTM_WIDTH, N_GATES)
    offs = np.concatenate([[0], np.cumsum(sizes)])
    w_t = jnp.swapaxes(w_in, 1, 2)
    cq, ckv, kr, qk, v, o, g = (w_t[:, offs[i]:offs[i + 1], :] for i in range(len(sizes)))
    g = jnp.pad(g, ((0, 0), (0, LANES - N_GATES), (0, 0)))
    return jnp.concatenate([cq, ckv, kr, kr[:, _half_swap_perm(), :], qk, v, o, g], axis=1).astype(BF16)


def _prep_w_uq(w_uq):
    n_layers, q_lora, _ = w_uq.shape
    w = w_uq.reshape(n_layers, q_lora, ATTN_HEADS, QK_DIM)
    nope = w[..., :NOPE_DIM].reshape(n_layers, q_lora, -1)
    rope = w[..., NOPE_DIM:]
    rope_sw = rope[..., _half_swap_perm()]
    return jnp.concatenate([nope, rope.reshape(n_layers, q_lora, -1), rope_sw.reshape(n_layers, q_lora, -1)],
                           axis=-1).astype(BF16)


def _prep_w_ukv(w_ukv):
    n_layers, kv_lora, _ = w_ukv.shape
    w = w_ukv.reshape(n_layers, kv_lora, ATTN_HEADS, NOPE_DIM + V_DIM)
    return jnp.concatenate([w[..., :NOPE_DIM].reshape(n_layers, kv_lora, -1),
                            w[..., NOPE_DIM:].reshape(n_layers, kv_lora, -1)], axis=-1).astype(BF16)


def _rope_tables(n_ctx, n_tok):
    rows = n_tok // GRID_W
    t_row = jnp.repeat(jnp.arange(rows), GRID_W).astype(F32)
    t_col = jnp.tile(jnp.arange(GRID_W), rows).astype(F32)
    inv = ROPE_BASE ** (-jnp.arange(AXIS_FREQS, dtype=F32) / AXIS_FREQS)
    ang_r = t_row[:, None] * inv
    ang_c = t_col[:, None] * inv
    cos = jnp.concatenate([jnp.cos(ang_r), jnp.cos(ang_r), jnp.cos(ang_c), jnp.cos(ang_c)], axis=-1)
    sin = jnp.concatenate([-jnp.sin(ang_r), jnp.sin(ang_r), -jnp.sin(ang_c), jnp.sin(ang_c)], axis=-1)
    cos = jnp.concatenate([jnp.ones((n_ctx, ROPE_DIM), F32), cos], axis=0)
    sin = jnp.concatenate([jnp.zeros((n_ctx, ROPE_DIM), F32), sin], axis=0)
    return jnp.tile(cos, (1, ATTN_HEADS)), jnp.tile(sin, (1, ATTN_HEADS))


def kernel(x, c, ctx, c_ctx, w_mod, b_mod, norm_pre, norm_post, ffn_w_gate, ffn_w_up, ffn_w_down,
           w_in, q_norm, w_uq, kv_norm, w_ukv, attn_out_norm, conv_w, conv_b, gate_b, mlstm_norm, w_out):
    bsz, n_tok, d = x.shape
    n_ctx = ctx.shape[1]
    n_layers = w_mod.shape[0]
    assert n_tok % TILE == 0 and n_ctx % TILE == 0 and n_tok % GRID_W == 0
    n_ctx_tiles = n_ctx // TILE
    n_lat_tiles = n_tok // TILE

    wg = jnp.swapaxes(ffn_w_gate, 2, 3).astype(BF16)
    wu = jnp.swapaxes(ffn_w_up, 2, 3).astype(BF16)
    wd = ffn_w_down.astype(BF16)
    win = _prep_w_in(w_in)
    wuq = _prep_w_uq(w_uq)
    wukv = _prep_w_ukv(w_ukv)
    wout = w_out.astype(BF16)
    g_pre = norm_pre[:, :, None, :]
    g_post = norm_post[:, :, None, :]
    q_gain = q_norm[:, None, :]
    kv_gain = kv_norm[:, None, :]
    a_gain = attn_out_norm[:, None, :]
    m_gain = mlstm_norm[:, None, :]
    conv_bias = conv_b[:, None, :]
    gate_bias = jnp.pad(gate_b.reshape(n_layers, 1, N_GATES), ((0, 0), (0, 0), (0, LANES - N_GATES)))
    cos4, sin4 = _rope_tables(n_ctx, n_tok)

    c_rows = jnp.zeros((_round_up(bsz + 1, SUBLANES), d), F32).at[:bsz].set(c).at[bsz].set(c_ctx)
    mod = _modulation(c_rows, w_mod, b_mod).reshape(n_layers, -1, N_MOD, d)
    mod_all = jnp.stack([jnp.broadcast_to(mod[:, bsz:bsz + 1], (n_layers, bsz, N_MOD, d)), mod[:, :bsz]], axis=2)

    h = (ctx, x)
    for l in range(n_layers):
        t0 = n_ctx_tiles if l == n_layers - 1 else 0
        ml = mod_all[l]
        h = _ffn(h, ml, g_pre, g_post, wg, wu, wd, layer=l, j=0, sub=0, n_ctx_tiles=n_ctx_tiles)
        q, k, vt, qt_m, k_m, vt_m, o_pre, gates_t = _mixin(
            h, ml, g_pre, win, q_gain, wuq, kv_gain, wukv, cos4, sin4, gate_bias, conv_w, conv_bias,
            layer=l, n_ctx_tiles=n_ctx_tiles)
        a = _attention(q, k, vt, q_tile0=n_ctx_tiles, n_q=n_lat_tiles, n_keys=n_ctx + n_tok,
                       name=f"attention_l{l}")
        if t0 == 0:
            a = (_attention(q, k, vt, q_tile0=0, n_q=n_ctx_tiles, n_keys=n_ctx, name=f"attention_ctx_l{l}"), a)
        h_fwd, h_bwd = _mlstm(k_m, qt_m, vt_m, gates_t, n_ctx_tiles=n_ctx_tiles, layer=l)
        h = _merge_ffn(h, a, h_fwd, h_bwd, o_pre, ml, a_gain, m_gain, g_pre, g_post, wout, wg, wu, wd,
                       layer=l, n_ctx_tiles=n_ctx_tiles, t0=t0)
    return h
```

```python
import functools

import jax
import jax.numpy as jnp
import numpy as np
from jax import lax
from jax.experimental import pallas as pl
from jax.experimental.pallas import tpu as pltpu

F32 = jnp.float32
BF16 = jnp.bfloat16

N_MOD = 9
EPS = 1e-6
GRID_W = 64
ATTN_HEADS = 4
Q_LORA = 256
KV_LORA = 128
NOPE_DIM = 128
ROPE_DIM = 64
AXIS_FREQS = ROPE_DIM // 4
V_DIM = 128
QK_DIM = NOPE_DIM + ROPE_DIM
ATTN_WIDTH = ATTN_HEADS * V_DIM
ROPE_BASE = 10000.0
SM_SCALE = QK_DIM ** -0.5
LOG2E = float(np.log2(np.e))
Q_SCALE = SM_SCALE * LOG2E
MLSTM_HEADS = 4
MLSTM_DH = 128
MLSTM_WIDTH = MLSTM_HEADS * MLSTM_DH
CONV_K = 3
N_GATES = 4 * MLSTM_HEADS

LANES = 128
SUBLANES = 8
MXU_DIM = 256
TILE = MXU_DIM
VMEM_LIMIT = 56 * 1024 * 1024

A_W = Q_LORA + KV_LORA + 2 * ROPE_DIM
QK_OFF = A_W
V_OFF = QK_OFF + 2 * MLSTM_WIDTH
O_OFF = V_OFF + MLSTM_WIDTH
G_OFF = O_OFF + MLSTM_WIDTH
IN_W = G_OFF + LANES


def _round_up(n, m):
    return (n + m - 1) // m * m


def _rms(x, g):
    return x * lax.rsqrt(jnp.mean(x * x, axis=-1, keepdims=True) + EPS) * g


def _dot_nt(x, w_t):
    return lax.dot_general(x, w_t, (((1,), (1,)), ((), ())), preferred_element_type=F32)


def _sandwich_in(x, mod, sub, g):
    return _rms(x, g) * (1.0 + mod[3 * sub + 1:3 * sub + 2]) + mod[3 * sub:3 * sub + 1]


def _params(n_grid):
    return pltpu.CompilerParams(dimension_semantics=("arbitrary",) * n_grid,
                                vmem_limit_bytes=VMEM_LIMIT)


def _const_spec(shape, index_map):
    return pl.BlockSpec(shape, index_map, pipeline_mode=pl.Buffered(1))


def _mod_spec(d, tile0, n_ctx_tiles):
    return pl.BlockSpec((1, 1, N_MOD, d),
                        lambda b, t: (b, (t + tile0 >= n_ctx_tiles).astype(jnp.int32), 0, 0))


TILES_PER_STEP = 2


class _TilePlan:
    def __init__(self, bsz, n, t0, n_ctx_tiles):
        assert (bsz * n) % TILES_PER_STEP == 0
        self.bsz, self.n, self.t0, self.n_ctx_tiles = bsz, n, t0, n_ctx_tiles
        self.grid = (bsz * n // TILES_PER_STEP,)

    def coords(self, i, k):
        g = i * TILES_PER_STEP + k
        return g // self.n, g % self.n + self.t0

    def is_ctx(self, k):
        return self.coords(pl.program_id(0), k)[1] < self.n_ctx_tiles

    def tile_spec(self, arr, k, arr_t0=0, clamp=False):
        arr_tiles = arr.shape[0] // self.bsz

        def index(i):
            b, r = self.coords(i, k)
            r = r - arr_t0
            if clamp:
                r = jnp.clip(r, 0, arr_tiles - 1)
            return (b * arr_tiles + r, 0, 0)

        return pl.BlockSpec((1, TILE, arr.shape[2]), index)

    def mod_spec(self, d, k):
        def index(i):
            b, r = self.coords(i, k)
            return (b, (r >= self.n_ctx_tiles).astype(jnp.int32), 0, 0)

        return pl.BlockSpec((1, 1, N_MOD, d), index)


def _as_tiles(x):
    return x.reshape(-1, TILE, x.shape[-1])


def _const1(shape, index):
    return _const_spec(shape, lambda i: index)


def _mod_kernel(c_ref, w_ref, b_ref, o_ref):
    c = c_ref[...]
    h = (c * jax.nn.sigmoid(c)).astype(BF16)
    o_ref[0] = jnp.dot(h, w_ref[0].astype(BF16), preferred_element_type=F32) + b_ref[0]


def _modulation(c_rows, w_mod, b_mod):
    n_layers, d, n_out = w_mod.shape
    rows = c_rows.shape[0]
    n_blk = N_MOD * LANES
    return pl.pallas_call(
        _mod_kernel,
        grid=(n_layers, n_out // n_blk),
        in_specs=[pl.BlockSpec((rows, d), lambda l, j: (0, 0)),
                  pl.BlockSpec((1, d, n_blk), lambda l, j: (l, 0, j)),
                  pl.BlockSpec((1, 1, n_blk), lambda l, j: (l, 0, j))],
        out_specs=pl.BlockSpec((1, rows, n_blk), lambda l, j: (l, 0, j)),
        out_shape=jax.ShapeDtypeStruct((n_layers, rows, n_out), F32),
        compiler_params=_params(2),
        name="modulation",
    )(c_rows, w_mod, b_mod.reshape(n_layers, 1, n_out))


def _ffn_body(xs, mods, sub, g_pre, g_post, wg_ref, wu_ref, wd_ref, a_ref):
    h = jnp.concatenate([_sandwich_in(x, m, sub, g_pre).astype(BF16) for x, m in zip(xs, mods)], axis=0)
    ff = a_ref.shape[1]
    for lo in range(0, ff, MXU_DIM):
        hi = min(lo + MXU_DIM, ff)
        g = _dot_nt(h, wg_ref[lo:hi, :])
        u = _dot_nt(h, wu_ref[lo:hi, :])
        a_ref[:, lo:hi] = (g * jax.nn.sigmoid(g) * u).astype(BF16)
    ff_main = ff // MXU_DIM * MXU_DIM
    y = jnp.dot(a_ref[:, 0:ff_main], wd_ref[0:ff_main, :], preferred_element_type=F32)
    if ff_main < ff:
        y = y + jnp.dot(a_ref[:, ff_main:ff], wd_ref[ff_main:ff, :], preferred_element_type=F32)
    return [x + 0.5 * m[3 * sub + 2:3 * sub + 3] * _rms(y[k * TILE:(k + 1) * TILE], g_post)
            for k, (x, m) in enumerate(zip(xs, mods))]


def _slot_tile(plan, k, refs, split):
    return jnp.where(plan.is_ctx(k), refs[0][0], refs[1][0]) if split else refs[0][0]


def _ffn_kernel(*refs, plan, sub, split):
    per_slot = 3 if split else 2
    slots = [refs[k * per_slot:(k + 1) * per_slot] for k in range(TILES_PER_STEP)]
    gpre_ref, gpost_ref, wg_ref, wu_ref, wd_ref, o_ref, a_ref = refs[TILES_PER_STEP * per_slot:]
    xs = [_slot_tile(plan, k, s[:-1], split) for k, s in enumerate(slots)]
    mods = [s[-1][0, 0] for s in slots]
    outs = _ffn_body(xs, mods, sub, gpre_ref[...], gpost_ref[...], wg_ref, wu_ref, wd_ref, a_ref)
    for k, out in enumerate(outs):
        o_ref[k] = out


def _ffn_weight_specs(layer, j, sub, d, ff):
    return [
        pl.BlockSpec((None, None, 1, d), lambda i: (layer, sub, 0, 0)),
        pl.BlockSpec((None, None, 1, d), lambda i: (layer, sub, 0, 0)),
        _const1((None, None, ff, d), (layer, j, 0, 0)),
        _const1((None, None, ff, d), (layer, j, 0, 0)),
        _const1((None, None, ff, d), (layer, j, 0, 0)),
    ]


def _ffn(src, mod_all, g_pre, g_post, wg, wu, wd, *, layer, j, sub, n_ctx_tiles):
    split = isinstance(src, tuple)
    srcs = [_as_tiles(s) for s in (src if split else (src,))]
    bsz, d = mod_all.shape[0], srcs[0].shape[-1]
    n_tiles = sum(s.shape[0] for s in srcs) // bsz
    ff = wd.shape[-2]
    plan = _TilePlan(bsz, n_tiles, 0, n_ctx_tiles)
    slot_specs, slot_args = [], []
    for k in range(TILES_PER_STEP):
        if split:
            slot_specs += [plan.tile_spec(srcs[0], k, 0, clamp=True), plan.tile_spec(srcs[1], k, n_ctx_tiles, clamp=True)]
        else:
            slot_specs += [plan.tile_spec(srcs[0], k)]
        slot_specs += [plan.mod_spec(d, k)]
        slot_args += srcs + [mod_all]
    out = pl.pallas_call(
        functools.partial(_ffn_kernel, plan=plan, sub=sub, split=split),
        grid=plan.grid,
        in_specs=slot_specs + _ffn_weight_specs(layer, j, sub, d, ff),
        out_specs=pl.BlockSpec((TILES_PER_STEP, TILE, d), lambda i: (i, 0, 0)),
        out_shape=jax.ShapeDtypeStruct((bsz * n_tiles, TILE, d), F32),
        scratch_shapes=[pltpu.VMEM((TILES_PER_STEP * TILE, ff), BF16)],
        compiler_params=_params(1),
        name=f"ffn_l{layer}_{j}",
    )(*slot_args, g_pre, g_post, wg, wu, wd)
    return out.reshape(bsz, n_tiles * TILE, d)


GATE_ROWS = 2 * 3 * SUBLANES


def _log_sigmoid(x):
    return jnp.minimum(x, 0.0) - jnp.log1p(jnp.exp(-jnp.abs(x)))


def _lane_scan(x, op, identity, reverse):
    n = x.shape[1]
    lane = lax.broadcasted_iota(jnp.int32, x.shape, 1)
    shift = 1
    while shift < n:
        if reverse:
            moved = jnp.where(lane < n - shift, pltpu.roll(x, n - shift, axis=1), identity)
        else:
            moved = jnp.where(lane >= shift, pltpu.roll(x, shift, axis=1), identity)
        x = op(x, moved)
        shift *= 2
    return x


def _gate_rows(g8, reverse):
    nh = MLSTM_HEADS
    head_rows = lax.broadcasted_iota(jnp.int32, g8.shape, 0) < nh
    b = _lane_scan(_log_sigmoid(g8), jnp.add, 0.0, reverse)
    b = jnp.where(head_rows, pltpu.roll(b, nh, axis=0), 0.0)
    a = jnp.where(head_rows, g8 - b, 0.0)
    return a, b, _lane_scan(a, jnp.maximum, -jnp.inf, reverse)


def _mixin_kernel(x_ref, xp_ref, xn_ref, mod_ref, gpre_ref, win_ref, qn_ref, wuq_ref, kvn_ref, wukv_ref,
                  cos_ref, sin_ref, gb_ref, cw_ref, cb_ref,
                  q_ref, k_ref, vt_ref, qtm_ref, km_ref, vtm_ref, o_ref, gt_ref, *, n_ctx_tiles, n_tiles):
    ci = pl.program_id(1)
    mod = mod_ref[0, 0]
    h = _sandwich_in(x_ref[0], mod, 1, gpre_ref[...]).astype(BF16)
    cos4 = cos_ref[...]
    sin4 = sin_ref[...]

    gates = jnp.dot(h, win_ref[:, G_OFF:IN_W], preferred_element_type=F32) + gb_ref[...]
    g_t = gates.T[0:N_GATES, :]
    half = N_GATES // 2
    gt_ref[0] = jnp.concatenate(_gate_rows(g_t[0:half], False) + _gate_rows(g_t[half:N_GATES], True), axis=0)

    pa = jnp.dot(h, win_ref[:, 0:A_W], preferred_element_type=F32)
    cq = pa[:, 0:Q_LORA]
    ckv = pa[:, Q_LORA:Q_LORA + KV_LORA]
    kr = pa[:, Q_LORA + KV_LORA:Q_LORA + KV_LORA + ROPE_DIM]
    kr_sw = pa[:, Q_LORA + KV_LORA + ROPE_DIM:A_W]
    k_rope = (kr * cos4[:, 0:ROPE_DIM] + kr_sw * sin4[:, 0:ROPE_DIM]).astype(BF16)

    nope_w = ATTN_HEADS * NOPE_DIM
    rope_w = ATTN_HEADS * ROPE_DIM
    qa = jnp.dot(_rms(cq, qn_ref[...]).astype(BF16), wuq_ref[...], preferred_element_type=F32)
    q_rope = qa[:, nope_w:nope_w + rope_w] * cos4 + qa[:, nope_w + rope_w:nope_w + 2 * rope_w] * sin4
    kva = jnp.dot(_rms(ckv, kvn_ref[...]).astype(BF16), wukv_ref[...], preferred_element_type=F32)
    for hd in range(ATTN_HEADS):
        q_ref[0, hd, :, 0:NOPE_DIM] = (qa[:, hd * NOPE_DIM:(hd + 1) * NOPE_DIM] * Q_SCALE).astype(BF16)
        q_ref[0, hd, :, NOPE_DIM:QK_DIM] = (q_rope[:, hd * ROPE_DIM:(hd + 1) * ROPE_DIM] * Q_SCALE).astype(BF16)
        k_ref[0, hd, :, 0:NOPE_DIM] = kva[:, hd * NOPE_DIM:(hd + 1) * NOPE_DIM].astype(BF16)
        k_ref[0, hd, :, NOPE_DIM:QK_DIM] = k_rope
        vt_ref[0, hd] = kva[:, nope_w + hd * V_DIM:nope_w + (hd + 1) * V_DIM].T.astype(BF16)

    vtm_ref[0] = jnp.dot(h, win_ref[:, V_OFF:O_OFF], preferred_element_type=F32).T.astype(BF16)
    o_ref[0] = jnp.dot(h, win_ref[:, O_OFF:G_OFF], preferred_element_type=F32)

    halo = jnp.concatenate([xp_ref[0], xn_ref[0]], axis=0)
    h_ext = jnp.concatenate([h, _sandwich_in(halo, mod, 1, gpre_ref[...]).astype(BF16)], axis=0)
    pqk = jnp.dot(h_ext, win_ref[:, QK_OFF:V_OFF], preferred_element_type=F32)
    x = pqk[0:TILE]
    row = lax.broadcasted_iota(jnp.int32, (TILE, 1), 0)
    has_prev = jnp.logical_and(ci != 0, ci != n_ctx_tiles)
    has_next = jnp.logical_and(ci != n_ctx_tiles - 1, ci != n_tiles - 1)
    prev_row = jnp.where(has_prev, pqk[TILE + SUBLANES - 1:TILE + SUBLANES], 0.0)
    next_row = jnp.where(has_next, pqk[TILE + SUBLANES:TILE + SUBLANES + 1], 0.0)
    x_prev = jnp.where(row == 0, prev_row, pltpu.roll(x, 1, axis=0))
    x_next = jnp.where(row == TILE - 1, next_row, pltpu.roll(x, TILE - 1, axis=0))
    u = x_prev * cw_ref[0:1, :] + x * cw_ref[1:2, :] + x_next * cw_ref[2:3, :] + cb_ref[...]
    u = u * jax.nn.sigmoid(u)
    qtm_ref[0] = u[:, 0:MLSTM_WIDTH].T.astype(BF16)
    km_ref[0] = (u[:, MLSTM_WIDTH:2 * MLSTM_WIDTH] * (MLSTM_DH ** -0.5)).astype(BF16)


def _mixin(x, mod_all, g_pre, win, q_norm, wuq, kv_norm, wukv, cos4, sin4, gate_b, conv_w, conv_b,
           *, layer, n_ctx_tiles):
    bsz, t_all, d = x.shape
    n_tiles = t_all // TILE
    rope_w = ATTN_HEADS * ROPE_DIM
    halo_per_tile = TILE // SUBLANES
    n_halo = t_all // SUBLANES
    tile_map = lambda b, t: (b, t, 0)
    head_map = lambda b, t: (b, 0, t, 0)
    time_last_map = lambda b, t: (b, 0, t)
    out_shape = (
        jax.ShapeDtypeStruct((bsz, ATTN_HEADS, t_all, QK_DIM), BF16),
        jax.ShapeDtypeStruct((bsz, ATTN_HEADS, t_all, QK_DIM), BF16),
        jax.ShapeDtypeStruct((bsz, ATTN_HEADS, V_DIM, t_all), BF16),
        jax.ShapeDtypeStruct((bsz, MLSTM_WIDTH, t_all), BF16),
        jax.ShapeDtypeStruct((bsz, t_all, MLSTM_WIDTH), BF16),
        jax.ShapeDtypeStruct((bsz, MLSTM_WIDTH, t_all), BF16),
        jax.ShapeDtypeStruct((bsz, t_all, MLSTM_WIDTH), F32),
        jax.ShapeDtypeStruct((bsz, GATE_ROWS, t_all), F32),
    )
    return pl.pallas_call(
        functools.partial(_mixin_kernel, n_ctx_tiles=n_ctx_tiles, n_tiles=n_tiles),
        grid=(bsz, n_tiles),
        in_specs=[
            pl.BlockSpec((1, TILE, d), tile_map),
            pl.BlockSpec((1, SUBLANES, d), lambda b, t: (b, jnp.maximum(t * halo_per_tile - 1, 0), 0)),
            pl.BlockSpec((1, SUBLANES, d), lambda b, t: (b, jnp.minimum((t + 1) * halo_per_tile, n_halo - 1), 0)),
            _mod_spec(d, 0, n_ctx_tiles),
            pl.BlockSpec((None, None, 1, d), lambda b, t: (layer, 1, 0, 0)),
            _const_spec((None, d, IN_W), lambda b, t: (layer, 0, 0)),
            pl.BlockSpec((None, 1, Q_LORA), lambda b, t: (layer, 0, 0)),
            _const_spec((None, Q_LORA, wuq.shape[-1]), lambda b, t: (layer, 0, 0)),
            pl.BlockSpec((None, 1, KV_LORA), lambda b, t: (layer, 0, 0)),
            _const_spec((None, KV_LORA, wukv.shape[-1]), lambda b, t: (layer, 0, 0)),
            pl.BlockSpec((TILE, rope_w), lambda b, t: (t, 0)),
            pl.BlockSpec((TILE, rope_w), lambda b, t: (t, 0)),
            pl.BlockSpec((None, 1, LANES), lambda b, t: (layer, 0, 0)),
            pl.BlockSpec((None, CONV_K, 2 * MLSTM_WIDTH), lambda b, t: (layer, 0, 0)),
            pl.BlockSpec((None, 1, 2 * MLSTM_WIDTH), lambda b, t: (layer, 0, 0)),
        ],
        out_specs=(
            pl.BlockSpec((1, ATTN_HEADS, TILE, QK_DIM), head_map),
            pl.BlockSpec((1, ATTN_HEADS, TILE, QK_DIM), head_map),
            pl.BlockSpec((1, ATTN_HEADS, V_DIM, TILE), lambda b, t: (b, 0, 0, t)),
            pl.BlockSpec((1, MLSTM_WIDTH, TILE), time_last_map),
            pl.BlockSpec((1, TILE, MLSTM_WIDTH), tile_map),
            pl.BlockSpec((1, MLSTM_WIDTH, TILE), time_last_map),
            pl.BlockSpec((1, TILE, MLSTM_WIDTH), tile_map),
            pl.BlockSpec((1, GATE_ROWS, TILE), time_last_map),
        ),
        out_shape=out_shape,
        compiler_params=_params(2),
        name=f"mixin_l{layer}",
    )(x, x, x, mod_all, g_pre, win, q_norm, wuq, kv_norm, wukv, cos4, sin4, gate_b, conv_w, conv_b)


KEY_CHUNK = 2 * TILE
ATTN_BUFFERS = 2


def _attn_kernel(*refs, chunks):
    q_refs = refs[:-5]
    k_ref, vt_ref, o_ref, s_scr, p_scr = refs[-5:]
    n_heads = k_ref.shape[1]
    units = [(j, h) for j in range(len(q_refs)) for h in range(n_heads)]
    lo, hi = chunks[0][0], chunks[-1][1]
    fold = lambda v: v.reshape(-1, SUBLANES, TILE)
    m8, l8 = {}, {}
    for stage in range(len(units) + 2):
        u_s, u_p, u_v = stage, stage - 1, stage - 2
        do_s, do_p, do_v = u_s < len(units), 0 <= u_p < len(units), 0 <= u_v < len(units)
        if do_v:
            j, h = units[u_v]
            acc = jnp.dot(vt_ref[0, h, :, lo:hi], p_scr[u_v % ATTN_BUFFERS, lo:hi, :], preferred_element_type=F32)
            l = jnp.sum(l8.pop(u_v), axis=0, keepdims=True)
            o_ref[0, j * TILE:(j + 1) * TILE, h * V_DIM:(h + 1) * V_DIM] = (acc / l).T
        if do_s:
            j, h_s = units[u_s]
            q = q_refs[j][0, h_s]
            m8[u_s] = jnp.full((SUBLANES, TILE), -jnp.inf, F32)
        if do_p:
            m = jnp.max(m8.pop(u_p), axis=0, keepdims=True)
            l8[u_p] = jnp.zeros((SUBLANES, TILE), F32)
        for c_lo, c_hi in chunks:
            if do_s:
                s = lax.dot_general(k_ref[0, h_s, c_lo:c_hi, :], q, (((1,), (1,)), ((), ())),
                                    preferred_element_type=F32)
                s_scr[u_s % ATTN_BUFFERS, c_lo:c_hi, :] = s
                m8[u_s] = jnp.maximum(m8[u_s], jnp.max(fold(s), axis=0))
            if do_p:
                p = jnp.exp2(s_scr[u_p % ATTN_BUFFERS, c_lo:c_hi, :] - m)
                l8[u_p] = l8[u_p] + jnp.sum(fold(p), axis=0)
                p_scr[u_p % ATTN_BUFFERS, c_lo:c_hi, :] = p.astype(BF16)


def _key_chunks(n_keys, width):
    return tuple((lo, min(lo + width, n_keys)) for lo in range(0, n_keys, width))


def _attention(q, k, vt, *, q_tile0, n_q, n_keys, name):
    bsz, n_heads, _, _ = q.shape
    qts = 2 if n_q % 2 == 0 else 1
    q_specs = [pl.BlockSpec((1, n_heads, TILE, QK_DIM), lambda b, i, j=j: (b, 0, i * qts + j + q_tile0, 0))
               for j in range(qts)]
    return pl.pallas_call(
        functools.partial(_attn_kernel, chunks=_key_chunks(n_keys, KEY_CHUNK)),
        grid=(bsz, n_q // qts),
        in_specs=q_specs + [
            pl.BlockSpec((1, n_heads, n_keys, QK_DIM), lambda b, i: (b, 0, 0, 0)),
            pl.BlockSpec((1, n_heads, V_DIM, n_keys), lambda b, i: (b, 0, 0, 0)),
        ],
        out_specs=pl.BlockSpec((1, qts * TILE, n_heads * V_DIM), lambda b, i: (b, i, 0)),
        out_shape=jax.ShapeDtypeStruct((bsz, n_q * TILE, n_heads * V_DIM), F32),
        scratch_shapes=[pltpu.VMEM((ATTN_BUFFERS, n_keys, TILE), F32),
                        pltpu.VMEM((ATTN_BUFFERS, n_keys, TILE), BF16)],
        compiler_params=_params(2),
        name=name,
    )(*([q] * qts), k, vt)


STATE_ROWS = MLSTM_DH + 2 * SUBLANES


def _bwd_chunk(j, n_ctx_tiles, n_tiles):
    return jnp.where(j < n_ctx_tiles, n_ctx_tiles - 1 - j, n_tiles - 1 - (j - n_ctx_tiles))


def _mlstm_kernel(kf_ref, qtf_ref, vtf_ref, gf_ref, kb_ref, qtb_ref, vtb_ref, gb_ref, of_ref, ob_ref,
                  st_scr, m_scr, d_scr, p_scr, r_scr):
    nh = MLSTM_HEADS

    @pl.when(pl.program_id(1) == 0)
    def _():
        st_scr[...] = jnp.zeros_like(st_scr)
        m_scr[...] = jnp.zeros_like(m_scr)

    rows = [slice(i * SUBLANES, (i + 1) * SUBLANES) for i in range(GATE_ROWS // SUBLANES)]
    s_idx = lax.broadcasted_iota(jnp.int32, (TILE, TILE), 0)
    t_idx = lax.broadcasted_iota(jnp.int32, (TILE, TILE), 1)
    pad_row = lax.broadcasted_iota(jnp.int32, (2 * SUBLANES, TILE), 0) == 0
    ones_rows = jnp.where(pad_row, 1.0, 0.0).astype(BF16)

    units = []
    for d, (reverse, g_ref, g0, k_ref, qt_ref, vt_ref, o_ref) in enumerate(
            ((False, gf_ref, 0, kf_ref, qtf_ref, vtf_ref, of_ref), (True, gb_ref, 3, kb_ref, qtb_ref, vtb_ref, ob_ref))):
        a, b, a_max = g_ref[0, rows[g0], :], g_ref[0, rows[g0 + 1], :], g_ref[0, rows[g0 + 2], :]
        last = 0 if reverse else TILE - 1
        m_old = m_scr[d][:, 0:1]
        mx = jnp.maximum(m_old, a_max)
        inter = jnp.exp(m_old - mx)
        e_inv = jnp.exp(-(b + mx))
        mx_last = mx[:, last:last + 1]
        w = jnp.exp(a - mx_last)
        decay = jnp.exp(m_old - mx_last)
        m_scr[d] = jnp.broadcast_to(b[:, last:last + 1] + mx_last, (SUBLANES, LANES))
        seen = s_idx >= t_idx if reverse else s_idx <= t_idx
        for hd in range(nh):
            sl = slice(hd * MLSTM_DH, (hd + 1) * MLSTM_DH)
            row = slice(hd, hd + 1)
            units.append(dict(u=d * nh + hd, sl=sl, seen=seen, a=a[row], mx=mx[row], inter=inter[row],
                              e_inv=e_inv[row], w=w[row], decay=decay[row],
                              k_ref=k_ref, qt_ref=qt_ref, vt_ref=vt_ref, o_ref=o_ref))

    for un in units:
        a_rep = jnp.broadcast_to(un["a"] * LOG2E, (LANES, TILE)).T
        log_d = jnp.concatenate([a_rep] * (TILE // LANES), axis=1) - un["mx"] * LOG2E
        d_scr[un["u"]] = jnp.exp2(jnp.where(un["seen"], log_d, -jnp.inf))

    for un in units:
        u, sl = un["u"], un["sl"]
        lhs = jnp.concatenate([un["k_ref"][0, :, sl], st_scr[u].astype(BF16)], axis=0)
        r = jnp.dot(lhs, un["qt_ref"][0, sl, :], preferred_element_type=F32)
        p_scr[u] = (r[0:TILE] * d_scr[u]).astype(BF16)
        r_scr[u] = r[TILE:TILE + STATE_ROWS]

    for un in units:
        u, sl = un["u"], un["sl"]
        nd = jnp.dot(jnp.concatenate([un["vt_ref"][0, sl, :], ones_rows], axis=0), p_scr[u],
                     preferred_element_type=F32)
        r_state = r_scr[u]
        den = nd[MLSTM_DH:MLSTM_DH + 1] + un["inter"] * r_state[MLSTM_DH:MLSTM_DH + 1]
        scale = 1.0 / jnp.maximum(jnp.abs(den), un["e_inv"])
        h_t = (nd[0:MLSTM_DH] + un["inter"] * r_state[0:MLSTM_DH]) * scale
        un["o_ref"][0, :, sl] = h_t.T

    for un in units:
        u, sl = un["u"], un["sl"]
        vt_h = un["vt_ref"][0, sl, :]
        vw = jnp.concatenate([(vt_h.astype(F32) * un["w"]).astype(BF16),
                              jnp.where(pad_row, un["w"], 0.0).astype(BF16)], axis=0)
        st_scr[u] = un["decay"] * st_scr[u] + jnp.dot(vw, un["k_ref"][0, :, sl], preferred_element_type=F32)


def _mlstm(k_m, qt_m, vt_m, gates_t, *, n_ctx_tiles, layer):
    bsz, t_all, width = k_m.shape
    n_tiles = t_all // TILE
    bwd = functools.partial(_bwd_chunk, n_ctx_tiles=n_ctx_tiles, n_tiles=n_tiles)

    def specs(chunk):
        return [pl.BlockSpec((1, TILE, width), lambda b, j: (b, chunk(j), 0)),
                pl.BlockSpec((1, width, TILE), lambda b, j: (b, 0, chunk(j))),
                pl.BlockSpec((1, width, TILE), lambda b, j: (b, 0, chunk(j))),
                pl.BlockSpec((1, GATE_ROWS, TILE), lambda b, j: (b, 0, chunk(j)))]

    out_sds = jax.ShapeDtypeStruct((bsz, t_all, width), F32)
    return pl.pallas_call(
        _mlstm_kernel,
        grid=(bsz, n_tiles),
        in_specs=specs(lambda j: j) + specs(bwd),
        out_specs=(pl.BlockSpec((1, TILE, width), lambda b, j: (b, j, 0)),
                   pl.BlockSpec((1, TILE, width), lambda b, j: (b, bwd(j), 0))),
        out_shape=(out_sds, out_sds),
        scratch_shapes=[pltpu.VMEM((2 * MLSTM_HEADS, STATE_ROWS, MLSTM_DH), F32),
                        pltpu.VMEM((2, SUBLANES, LANES), F32),
                        pltpu.VMEM((2 * MLSTM_HEADS, TILE, TILE), F32),
                        pltpu.VMEM((2 * MLSTM_HEADS, TILE, TILE), BF16),
                        pltpu.VMEM((2 * MLSTM_HEADS, STATE_ROWS, TILE), F32)],
        compiler_params=_params(2),
        name=f"mlstm_l{layer}",
    )(k_m, qt_m, vt_m, gates_t, k_m, qt_m, vt_m, gates_t)


def _merge_body(xs, attn, hms, os, gates, g_attn, g_mlstm, g_post, wout_ref):
    a_n, hm_n = [], []
    for a, hm, o in zip(attn, hms, os):
        hm = hm * jax.nn.sigmoid(o)
        normed = []
        for hd in range(MLSTM_HEADS):
            seg = hm[:, hd * MLSTM_DH:(hd + 1) * MLSTM_DH]
            cen = seg - jnp.mean(seg, axis=-1, keepdims=True)
            normed.append(cen * lax.rsqrt(jnp.mean(cen * cen, axis=-1, keepdims=True) + EPS))
        hm_n.append((jnp.concatenate(normed, axis=-1) * g_mlstm).astype(BF16))
        a_n.append(_rms(a, g_attn).astype(BF16))
    y = jnp.dot(jnp.concatenate(a_n, axis=0), wout_ref[0:ATTN_WIDTH, :], preferred_element_type=F32)
    y = y + jnp.dot(jnp.concatenate(hm_n, axis=0), wout_ref[ATTN_WIDTH:ATTN_WIDTH + MLSTM_WIDTH, :],
                    preferred_element_type=F32)
    return [x + gate * _rms(y[k * TILE:(k + 1) * TILE], g_post) for k, (x, gate) in enumerate(zip(xs, gates))]


def _merge_ffn_kernel(*refs, plan, split):
    per_slot = 7 if split else 6
    slots = [refs[k * per_slot:(k + 1) * per_slot] for k in range(TILES_PER_STEP)]
    (ga_ref, gm_ref, gpost1_ref, wout_ref, gpre2_ref, gpost2_ref, wg_ref, wu_ref, wd_ref,
     out_ref, a_scr) = refs[TILES_PER_STEP * per_slot:]
    xs = [s[0][0] for s in slots]
    attn = [_slot_tile(plan, k, s[1:-4], split) for k, s in enumerate(slots)]
    hms = [s[-4][0] + s[-3][0] for s in slots]
    os = [s[-2][0] for s in slots]
    mods = [s[-1][0, 0] for s in slots]
    x1 = _merge_body(xs, attn, hms, os, [m[5:6] for m in mods], ga_ref[...], gm_ref[...], gpost1_ref[...], wout_ref)
    outs = _ffn_body(x1, mods, 2, gpre2_ref[...], gpost2_ref[...], wg_ref, wu_ref, wd_ref, a_scr)
    for k, out in enumerate(outs):
        out_ref[k] = out


def _merge_ffn(x, a, h_fwd, h_bwd, o, mod_all, g_attn, g_mlstm, g_pre, g_post, wout, wg, wu, wd,
               *, layer, n_ctx_tiles, t0):
    bsz, t_all, d = x.shape
    n_tiles = t_all // TILE - t0
    ff = wd.shape[-2]
    split = isinstance(a, tuple)
    a_srcs = [_as_tiles(s) for s in (a if split else (a,))]
    stream = [_as_tiles(s) for s in (h_fwd, h_bwd, o)]
    xt = _as_tiles(x)
    plan = _TilePlan(bsz, n_tiles, t0, n_ctx_tiles)
    slot_specs, slot_args = [], []
    for k in range(TILES_PER_STEP):
        slot_specs += [plan.tile_spec(xt, k)]
        if split:
            slot_specs += [plan.tile_spec(a_srcs[0], k, 0, clamp=True),
                           plan.tile_spec(a_srcs[1], k, n_ctx_tiles, clamp=True)]
        else:
            slot_specs += [plan.tile_spec(a_srcs[0], k, t0)]
        slot_specs += [plan.tile_spec(s, k) for s in stream] + [plan.mod_spec(d, k)]
        slot_args += [xt] + a_srcs + stream + [mod_all]
    out = pl.pallas_call(
        functools.partial(_merge_ffn_kernel, plan=plan, split=split),
        grid=plan.grid,
        in_specs=slot_specs + [
            pl.BlockSpec((None, 1, ATTN_WIDTH), lambda i: (layer, 0, 0)),
            pl.BlockSpec((None, 1, MLSTM_WIDTH), lambda i: (layer, 0, 0)),
            pl.BlockSpec((None, None, 1, d), lambda i: (layer, 1, 0, 0)),
            _const1((None, ATTN_WIDTH + MLSTM_WIDTH, d), (layer, 0, 0)),
        ] + _ffn_weight_specs(layer, 1, 2, d, ff),
        out_specs=pl.BlockSpec((TILES_PER_STEP, TILE, d), lambda i: (i, 0, 0)),
        out_shape=jax.ShapeDtypeStruct((bsz * n_tiles, TILE, d), F32),
        scratch_shapes=[pltpu.VMEM((TILES_PER_STEP * TILE, ff), BF16)],
        compiler_params=_params(1),
        name=f"merge_ffn_l{layer}",
    )(*slot_args, g_attn, g_mlstm, g_post, wout, g_pre, g_post, wg, wu, wd)
    return out.reshape(bsz, n_tiles * TILE, d)


def _half_swap_perm():
    idx = np.arange(ROPE_DIM)
    axis, half, freq = idx // (2 * AXIS_FREQS), (idx // AXIS_FREQS) % 2, idx % AXIS_FREQS
    return axis * 2 * AXIS_FREQS + (1 - half) * AXIS_FREQS + freq


def _prep_w_in(w_in):
    sizes = (Q_LORA, KV_LORA, ROPE_DIM, 2 * MLSTM_WIDTH, MLSTM_WIDTH, MLSTM_WIDTH, N_GATES)
    offs = np.concatenate([[0], np.cumsum(sizes)])
    cq, ckv, kr, qk, v, o, g = (w_in[..., offs[i]:offs[i + 1]] for i in range(len(sizes)))
    g = jnp.pad(g, ((0, 0), (0, 0), (0, LANES - N_GATES)))
    return jnp.concatenate([cq, ckv, kr, kr[..., _half_swap_perm()], qk, v, o, g], axis=-1).astype(BF16)


def _prep_w_uq(w_uq):
    n_layers, q_lora, _ = w_uq.shape
    w = w_uq.reshape(n_layers, q_lora, ATTN_HEADS, QK_DIM)
    nope = w[..., :NOPE_DIM].reshape(n_layers, q_lora, -1)
    rope = w[..., NOPE_DIM:]
    rope_sw = rope[..., _half_swap_perm()]
    return jnp.concatenate([nope, rope.reshape(n_layers, q_lora, -1), rope_sw.reshape(n_layers, q_lora, -1)],
                           axis=-1).astype(BF16)


def _prep_w_ukv(w_ukv):
    n_layers, kv_lora, _ = w_ukv.shape
    w = w_ukv.reshape(n_layers, kv_lora, ATTN_HEADS, NOPE_DIM + V_DIM)
    return jnp.concatenate([w[..., :NOPE_DIM].reshape(n_layers, kv_lora, -1),
                            w[..., NOPE_DIM:].reshape(n_layers, kv_lora, -1)], axis=-1).astype(BF16)


def _rope_tables(n_ctx, n_tok):
    rows = n_tok // GRID_W
    t_row = jnp.repeat(jnp.arange(rows), GRID_W).astype(F32)
    t_col = jnp.tile(jnp.arange(GRID_W), rows).astype(F32)
    inv = ROPE_BASE ** (-jnp.arange(AXIS_FREQS, dtype=F32) / AXIS_FREQS)
    ang_r = t_row[:, None] * inv
    ang_c = t_col[:, None] * inv
    cos = jnp.concatenate([jnp.cos(ang_r), jnp.cos(ang_r), jnp.cos(ang_c), jnp.cos(ang_c)], axis=-1)
    sin = jnp.concatenate([-jnp.sin(ang_r), jnp.sin(ang_r), -jnp.sin(ang_c), jnp.sin(ang_c)], axis=-1)
    cos = jnp.concatenate([jnp.ones((n_ctx, ROPE_DIM), F32), cos], axis=0)
    sin = jnp.concatenate([jnp.zeros((n_ctx, ROPE_DIM), F32), sin], axis=0)
    return jnp.tile(cos, (1, ATTN_HEADS)), jnp.tile(sin, (1, ATTN_HEADS))


def kernel(x, c, ctx, c_ctx, w_mod, b_mod, norm_pre, norm_post, ffn_w_gate, ffn_w_up, ffn_w_down,
           w_in, q_norm, w_uq, kv_norm, w_ukv, attn_out_norm, conv_w, conv_b, gate_b, mlstm_norm, w_out):
    bsz, n_tok, d = x.shape
    n_ctx = ctx.shape[1]
    n_layers = w_mod.shape[0]
    assert n_tok % TILE == 0 and n_ctx % TILE == 0 and n_tok % GRID_W == 0
    n_ctx_tiles = n_ctx // TILE
    n_lat_tiles = n_tok // TILE

    wg = jnp.swapaxes(ffn_w_gate, 2, 3).astype(BF16)
    wu = jnp.swapaxes(ffn_w_up, 2, 3).astype(BF16)
    wd = ffn_w_down.astype(BF16)
    win = _prep_w_in(w_in)
    wuq = _prep_w_uq(w_uq)
    wukv = _prep_w_ukv(w_ukv)
    wout = w_out.astype(BF16)
    g_pre = norm_pre[:, :, None, :]
    g_post = norm_post[:, :, None, :]
    q_gain = q_norm[:, None, :]
    kv_gain = kv_norm[:, None, :]
    a_gain = attn_out_norm[:, None, :]
    m_gain = mlstm_norm[:, None, :]
    conv_bias = conv_b[:, None, :]
    gate_bias = jnp.pad(gate_b.reshape(n_layers, 1, N_GATES), ((0, 0), (0, 0), (0, LANES - N_GATES)))
    cos4, sin4 = _rope_tables(n_ctx, n_tok)

    c_rows = jnp.zeros((_round_up(bsz + 1, SUBLANES), d), F32).at[:bsz].set(c).at[bsz].set(c_ctx)
    mod = _modulation(c_rows, w_mod, b_mod).reshape(n_layers, -1, N_MOD, d)
    mod_all = jnp.stack([jnp.broadcast_to(mod[:, bsz:bsz + 1], (n_layers, bsz, N_MOD, d)), mod[:, :bsz]], axis=2)

    h = (ctx, x)
    for l in range(n_layers):
        t0 = n_ctx_tiles if l == n_layers - 1 else 0
        ml = mod_all[l]
        h = _ffn(h, ml, g_pre, g_post, wg, wu, wd, layer=l, j=0, sub=0, n_ctx_tiles=n_ctx_tiles)
        q, k, vt, qt_m, k_m, vt_m, o_pre, gates_t = _mixin(
            h, ml, g_pre, win, q_gain, wuq, kv_gain, wukv, cos4, sin4, gate_bias, conv_w, conv_bias,
            layer=l, n_ctx_tiles=n_ctx_tiles)
        a = _attention(q, k, vt, q_tile0=n_ctx_tiles, n_q=n_lat_tiles, n_keys=n_ctx + n_tok,
                       name=f"attention_l{l}")
        if t0 == 0:
            a = (_attention(q, k, vt, q_tile0=0, n_q=n_ctx_tiles, n_keys=n_ctx, name=f"attention_ctx_l{l}"), a)
        h_fwd, h_bwd = _mlstm(k_m, qt_m, vt_m, gates_t, n_ctx_tiles=n_ctx_tiles, layer=l)
        h = _merge_ffn(h, a, h_fwd, h_bwd, o_pre, ml, a_gain, m_gain, g_pre, g_post, wout, wg, wu, wd,
                       layer=l, n_ctx_tiles=n_ctx_tiles, t0=t0)
    return h
```

```python
import functools

import jax
import jax.numpy as jnp
import numpy as np
from jax import lax
from jax.experimental import pallas as pl
from jax.experimental.pallas import tpu as pltpu

F32 = jnp.float32
BF16 = jnp.bfloat16

N_MOD = 9
EPS = 1e-6
GRID_W = 64
ATTN_HEADS = 4
Q_LORA = 256
KV_LORA = 128
NOPE_DIM = 128
ROPE_DIM = 64
AXIS_FREQS = ROPE_DIM // 4
V_DIM = 128
QK_DIM = NOPE_DIM + ROPE_DIM
ATTN_WIDTH = ATTN_HEADS * V_DIM
ROPE_BASE = 10000.0
SM_SCALE = QK_DIM ** -0.5
LOG2E = float(np.log2(np.e))
Q_SCALE = SM_SCALE * LOG2E
MLSTM_HEADS = 4
MLSTM_DH = 128
MLSTM_WIDTH = MLSTM_HEADS * MLSTM_DH
CONV_K = 3
N_GATES = 4 * MLSTM_HEADS

LANES = 128
SUBLANES = 8
MXU_DIM = 256
TILE = MXU_DIM
VMEM_LIMIT = 56 * 1024 * 1024

A_W = Q_LORA + KV_LORA + 2 * ROPE_DIM
QK_OFF = A_W
V_OFF = QK_OFF + 2 * MLSTM_WIDTH
O_OFF = V_OFF + MLSTM_WIDTH
G_OFF = O_OFF + MLSTM_WIDTH
IN_W = G_OFF + LANES


def _round_up(n, m):
    return (n + m - 1) // m * m


def _rms(x, g):
    return x * lax.rsqrt(jnp.mean(x * x, axis=-1, keepdims=True) + EPS) * g


def _dot_nt(x, w_t):
    return lax.dot_general(x, w_t, (((1,), (1,)), ((), ())), preferred_element_type=F32)


def _sandwich_in(x, mod, sub, g):
    return _rms(x, g) * (1.0 + mod[3 * sub + 1:3 * sub + 2]) + mod[3 * sub:3 * sub + 1]


def _params(n_grid):
    return pltpu.CompilerParams(dimension_semantics=("arbitrary",) * n_grid,
                                vmem_limit_bytes=VMEM_LIMIT)


def _const_spec(shape, index_map):
    return pl.BlockSpec(shape, index_map, pipeline_mode=pl.Buffered(1))


def _mod_spec(d, tile0, n_ctx_tiles):
    return pl.BlockSpec((1, 1, N_MOD, d),
                        lambda b, t: (b, (t + tile0 >= n_ctx_tiles).astype(jnp.int32), 0, 0))


FFN_TILES_PER_STEP = 4
MERGE_FFN_TILES_PER_STEP = 2


class _TilePlan:
    def __init__(self, bsz, n, t0, n_ctx_tiles, tiles_per_step):
        assert (bsz * n) % tiles_per_step == 0
        self.bsz, self.n, self.t0, self.n_ctx_tiles, self.tps = bsz, n, t0, n_ctx_tiles, tiles_per_step
        self.grid = (bsz * n // tiles_per_step,)

    def coords(self, i, k):
        g = i * self.tps + k
        return g // self.n, g % self.n + self.t0

    def is_ctx(self, k):
        return self.coords(pl.program_id(0), k)[1] < self.n_ctx_tiles

    def tile_spec(self, arr, k, arr_t0=0, clamp=False):
        arr_tiles = arr.shape[0] // self.bsz

        def index(i):
            b, r = self.coords(i, k)
            r = r - arr_t0
            if clamp:
                r = jnp.clip(r, 0, arr_tiles - 1)
            return (b * arr_tiles + r, 0, 0)

        return pl.BlockSpec((1, TILE, arr.shape[2]), index)

    def mod_spec(self, d, k):
        def index(i):
            b, r = self.coords(i, k)
            return (b, (r >= self.n_ctx_tiles).astype(jnp.int32), 0, 0)

        return pl.BlockSpec((1, 1, N_MOD, d), index)


def _as_tiles(x):
    return x.reshape(-1, TILE, x.shape[-1])


def _const1(shape, index):
    return _const_spec(shape, lambda i: index)


def _mod_kernel(c_ref, w_ref, b_ref, o_ref):
    c = c_ref[...]
    h = (c * jax.nn.sigmoid(c)).astype(BF16)
    o_ref[0] = jnp.dot(h, w_ref[0].astype(BF16), preferred_element_type=F32) + b_ref[0]


def _modulation(c_rows, w_mod, b_mod):
    n_layers, d, n_out = w_mod.shape
    rows = c_rows.shape[0]
    n_blk = N_MOD * LANES
    return pl.pallas_call(
        _mod_kernel,
        grid=(n_layers, n_out // n_blk),
        in_specs=[pl.BlockSpec((rows, d), lambda l, j: (0, 0)),
                  pl.BlockSpec((1, d, n_blk), lambda l, j: (l, 0, j)),
                  pl.BlockSpec((1, 1, n_blk), lambda l, j: (l, 0, j))],
        out_specs=pl.BlockSpec((1, rows, n_blk), lambda l, j: (l, 0, j)),
        out_shape=jax.ShapeDtypeStruct((n_layers, rows, n_out), F32),
        compiler_params=_params(2),
        name="modulation",
    )(c_rows, w_mod, b_mod.reshape(n_layers, 1, n_out))


def _ffn_body(xs, mods, sub, g_pre, g_post, wg_ref, wu_ref, wd_ref, a_ref):
    h = jnp.concatenate([_sandwich_in(x, m, sub, g_pre).astype(BF16) for x, m in zip(xs, mods)], axis=0)
    ff = a_ref.shape[1]
    for lo in range(0, ff, MXU_DIM):
        hi = min(lo + MXU_DIM, ff)
        g = _dot_nt(h, wg_ref[lo:hi, :])
        u = _dot_nt(h, wu_ref[lo:hi, :])
        a_ref[:, lo:hi] = (g * jax.nn.sigmoid(g) * u).astype(BF16)
    ff_main = ff // MXU_DIM * MXU_DIM
    y = jnp.dot(a_ref[:, 0:ff_main], wd_ref[0:ff_main, :], preferred_element_type=F32)
    if ff_main < ff:
        y = y + jnp.dot(a_ref[:, ff_main:ff], wd_ref[ff_main:ff, :], preferred_element_type=F32)
    return [x + 0.5 * m[3 * sub + 2:3 * sub + 3] * _rms(y[k * TILE:(k + 1) * TILE], g_post)
            for k, (x, m) in enumerate(zip(xs, mods))]


def _slot_tile(plan, k, refs, split):
    return jnp.where(plan.is_ctx(k), refs[0][0], refs[1][0]) if split else refs[0][0]


def _ffn_kernel(*refs, plan, sub, split):
    per_slot = 3 if split else 2
    slots = [refs[k * per_slot:(k + 1) * per_slot] for k in range(plan.tps)]
    gpre_ref, gpost_ref, wg_ref, wu_ref, wd_ref, o_ref, a_ref = refs[plan.tps * per_slot:]
    xs = [_slot_tile(plan, k, s[:-1], split) for k, s in enumerate(slots)]
    mods = [s[-1][0, 0] for s in slots]
    outs = _ffn_body(xs, mods, sub, gpre_ref[...], gpost_ref[...], wg_ref, wu_ref, wd_ref, a_ref)
    for k, out in enumerate(outs):
        o_ref[k] = out


def _ffn_weight_specs(layer, j, sub, d, ff):
    return [
        pl.BlockSpec((None, None, 1, d), lambda i: (layer, sub, 0, 0)),
        pl.BlockSpec((None, None, 1, d), lambda i: (layer, sub, 0, 0)),
        _const1((None, None, ff, d), (layer, j, 0, 0)),
        _const1((None, None, ff, d), (layer, j, 0, 0)),
        _const1((None, None, ff, d), (layer, j, 0, 0)),
    ]


def _ffn(src, mod_all, g_pre, g_post, wg, wu, wd, *, layer, j, sub, n_ctx_tiles):
    split = isinstance(src, tuple)
    srcs = [_as_tiles(s) for s in (src if split else (src,))]
    bsz, d = mod_all.shape[0], srcs[0].shape[-1]
    n_tiles = sum(s.shape[0] for s in srcs) // bsz
    ff = wd.shape[-2]
    plan = _TilePlan(bsz, n_tiles, 0, n_ctx_tiles, FFN_TILES_PER_STEP)
    slot_specs, slot_args = [], []
    for k in range(plan.tps):
        if split:
            slot_specs += [plan.tile_spec(srcs[0], k, 0, clamp=True), plan.tile_spec(srcs[1], k, n_ctx_tiles, clamp=True)]
        else:
            slot_specs += [plan.tile_spec(srcs[0], k)]
        slot_specs += [plan.mod_spec(d, k)]
        slot_args += srcs + [mod_all]
    out = pl.pallas_call(
        functools.partial(_ffn_kernel, plan=plan, sub=sub, split=split),
        grid=plan.grid,
        in_specs=slot_specs + _ffn_weight_specs(layer, j, sub, d, ff),
        out_specs=pl.BlockSpec((plan.tps, TILE, d), lambda i: (i, 0, 0)),
        out_shape=jax.ShapeDtypeStruct((bsz * n_tiles, TILE, d), F32),
        scratch_shapes=[pltpu.VMEM((plan.tps * TILE, ff), BF16)],
        compiler_params=_params(1),
        name=f"ffn_l{layer}_{j}",
    )(*slot_args, g_pre, g_post, wg, wu, wd)
    return out.reshape(bsz, n_tiles * TILE, d)


GATE_ROWS = 2 * 3 * SUBLANES


def _log_sigmoid(x):
    return jnp.minimum(x, 0.0) - jnp.log1p(jnp.exp(-jnp.abs(x)))


def _lane_scan(x, op, identity, reverse):
    n = x.shape[1]
    lane = lax.broadcasted_iota(jnp.int32, x.shape, 1)
    shift = 1
    while shift < n:
        if reverse:
            moved = jnp.where(lane < n - shift, pltpu.roll(x, n - shift, axis=1), identity)
        else:
            moved = jnp.where(lane >= shift, pltpu.roll(x, shift, axis=1), identity)
        x = op(x, moved)
        shift *= 2
    return x


def _gate_rows(g8, reverse):
    nh = MLSTM_HEADS
    head_rows = lax.broadcasted_iota(jnp.int32, g8.shape, 0) < nh
    b = _lane_scan(_log_sigmoid(g8), jnp.add, 0.0, reverse)
    b = jnp.where(head_rows, pltpu.roll(b, nh, axis=0), 0.0)
    a = jnp.where(head_rows, g8 - b, 0.0)
    return a, b, _lane_scan(a, jnp.maximum, -jnp.inf, reverse)


def _mixin_kernel(x_ref, xp_ref, xn_ref, mod_ref, gpre_ref, win_ref, qn_ref, wuq_ref, kvn_ref, wukv_ref,
                  cos_ref, sin_ref, gb_ref, cw_ref, cb_ref,
                  q_ref, k_ref, vt_ref, qtm_ref, km_ref, vtm_ref, o_ref, gt_ref, *, n_ctx_tiles, n_tiles):
    ci = pl.program_id(1)
    mod = mod_ref[0, 0]
    h = _sandwich_in(x_ref[0], mod, 1, gpre_ref[...]).astype(BF16)
    cos4 = cos_ref[...]
    sin4 = sin_ref[...]

    gates = jnp.dot(h, win_ref[:, G_OFF:IN_W], preferred_element_type=F32) + gb_ref[...]
    g_t = gates.T[0:N_GATES, :]
    half = N_GATES // 2
    gt_ref[0] = jnp.concatenate(_gate_rows(g_t[0:half], False) + _gate_rows(g_t[half:N_GATES], True), axis=0)

    pa = jnp.dot(h, win_ref[:, 0:A_W], preferred_element_type=F32)
    cq = pa[:, 0:Q_LORA]
    ckv = pa[:, Q_LORA:Q_LORA + KV_LORA]
    kr = pa[:, Q_LORA + KV_LORA:Q_LORA + KV_LORA + ROPE_DIM]
    kr_sw = pa[:, Q_LORA + KV_LORA + ROPE_DIM:A_W]
    k_rope = (kr * cos4[:, 0:ROPE_DIM] + kr_sw * sin4[:, 0:ROPE_DIM]).astype(BF16)

    nope_w = ATTN_HEADS * NOPE_DIM
    rope_w = ATTN_HEADS * ROPE_DIM
    qa = jnp.dot(_rms(cq, qn_ref[...]).astype(BF16), wuq_ref[...], preferred_element_type=F32)
    q_rope = qa[:, nope_w:nope_w + rope_w] * cos4 + qa[:, nope_w + rope_w:nope_w + 2 * rope_w] * sin4
    kva = jnp.dot(_rms(ckv, kvn_ref[...]).astype(BF16), wukv_ref[...], preferred_element_type=F32)
    for hd in range(ATTN_HEADS):
        q_ref[0, hd, :, 0:NOPE_DIM] = (qa[:, hd * NOPE_DIM:(hd + 1) * NOPE_DIM] * Q_SCALE).astype(BF16)
        q_ref[0, hd, :, NOPE_DIM:QK_DIM] = (q_rope[:, hd * ROPE_DIM:(hd + 1) * ROPE_DIM] * Q_SCALE).astype(BF16)
        k_ref[0, hd, :, 0:NOPE_DIM] = kva[:, hd * NOPE_DIM:(hd + 1) * NOPE_DIM].astype(BF16)
        k_ref[0, hd, :, NOPE_DIM:QK_DIM] = k_rope
        vt_ref[0, hd] = kva[:, nope_w + hd * V_DIM:nope_w + (hd + 1) * V_DIM].T.astype(BF16)

    vtm_ref[0] = jnp.dot(h, win_ref[:, V_OFF:O_OFF], preferred_element_type=F32).T.astype(BF16)
    o_ref[0] = jnp.dot(h, win_ref[:, O_OFF:G_OFF], preferred_element_type=F32)

    halo = jnp.concatenate([xp_ref[0], xn_ref[0]], axis=0)
    h_ext = jnp.concatenate([h, _sandwich_in(halo, mod, 1, gpre_ref[...]).astype(BF16)], axis=0)
    pqk = jnp.dot(h_ext, win_ref[:, QK_OFF:V_OFF], preferred_element_type=F32)
    x = pqk[0:TILE]
    row = lax.broadcasted_iota(jnp.int32, (TILE, 1), 0)
    has_prev = jnp.logical_and(ci != 0, ci != n_ctx_tiles)
    has_next = jnp.logical_and(ci != n_ctx_tiles - 1, ci != n_tiles - 1)
    prev_row = jnp.where(has_prev, pqk[TILE + SUBLANES - 1:TILE + SUBLANES], 0.0)
    next_row = jnp.where(has_next, pqk[TILE + SUBLANES:TILE + SUBLANES + 1], 0.0)
    x_prev = jnp.where(row == 0, prev_row, pltpu.roll(x, 1, axis=0))
    x_next = jnp.where(row == TILE - 1, next_row, pltpu.roll(x, TILE - 1, axis=0))
    u = x_prev * cw_ref[0:1, :] + x * cw_ref[1:2, :] + x_next * cw_ref[2:3, :] + cb_ref[...]
    u = u * jax.nn.sigmoid(u)
    qtm_ref[0] = u[:, 0:MLSTM_WIDTH].T.astype(BF16)
    km_ref[0] = (u[:, MLSTM_WIDTH:2 * MLSTM_WIDTH] * (MLSTM_DH ** -0.5)).astype(BF16)


def _mixin(x, mod_all, g_pre, win, q_norm, wuq, kv_norm, wukv, cos4, sin4, gate_b, conv_w, conv_b,
           *, layer, n_ctx_tiles):
    bsz, t_all, d = x.shape
    n_tiles = t_all // TILE
    rope_w = ATTN_HEADS * ROPE_DIM
    halo_per_tile = TILE // SUBLANES
    n_halo = t_all // SUBLANES
    tile_map = lambda b, t: (b, t, 0)
    head_map = lambda b, t: (b, 0, t, 0)
    time_last_map = lambda b, t: (b, 0, t)
    out_shape = (
        jax.ShapeDtypeStruct((bsz, ATTN_HEADS, t_all, QK_DIM), BF16),
        jax.ShapeDtypeStruct((bsz, ATTN_HEADS, t_all, QK_DIM), BF16),
        jax.ShapeDtypeStruct((bsz, ATTN_HEADS, V_DIM, t_all), BF16),
        jax.ShapeDtypeStruct((bsz, MLSTM_WIDTH, t_all), BF16),
        jax.ShapeDtypeStruct((bsz, t_all, MLSTM_WIDTH), BF16),
        jax.ShapeDtypeStruct((bsz, MLSTM_WIDTH, t_all), BF16),
        jax.ShapeDtypeStruct((bsz, t_all, MLSTM_WIDTH), F32),
        jax.ShapeDtypeStruct((bsz, GATE_ROWS, t_all), F32),
    )
    return pl.pallas_call(
        functools.partial(_mixin_kernel, n_ctx_tiles=n_ctx_tiles, n_tiles=n_tiles),
        grid=(bsz, n_tiles),
        in_specs=[
            pl.BlockSpec((1, TILE, d), tile_map),
            pl.BlockSpec((1, SUBLANES, d), lambda b, t: (b, jnp.maximum(t * halo_per_tile - 1, 0), 0)),
            pl.BlockSpec((1, SUBLANES, d), lambda b, t: (b, jnp.minimum((t + 1) * halo_per_tile, n_halo - 1), 0)),
            _mod_spec(d, 0, n_ctx_tiles),
            pl.BlockSpec((None, None, 1, d), lambda b, t: (layer, 1, 0, 0)),
            _const_spec((None, d, IN_W), lambda b, t: (layer, 0, 0)),
            pl.BlockSpec((None, 1, Q_LORA), lambda b, t: (layer, 0, 0)),
            _const_spec((None, Q_LORA, wuq.shape[-1]), lambda b, t: (layer, 0, 0)),
            pl.BlockSpec((None, 1, KV_LORA), lambda b, t: (layer, 0, 0)),
            _const_spec((None, KV_LORA, wukv.shape[-1]), lambda b, t: (layer, 0, 0)),
            pl.BlockSpec((TILE, rope_w), lambda b, t: (t, 0)),
            pl.BlockSpec((TILE, rope_w), lambda b, t: (t, 0)),
            pl.BlockSpec((None, 1, LANES), lambda b, t: (layer, 0, 0)),
            pl.BlockSpec((None, CONV_K, 2 * MLSTM_WIDTH), lambda b, t: (layer, 0, 0)),
            pl.BlockSpec((None, 1, 2 * MLSTM_WIDTH), lambda b, t: (layer, 0, 0)),
        ],
        out_specs=(
            pl.BlockSpec((1, ATTN_HEADS, TILE, QK_DIM), head_map),
            pl.BlockSpec((1, ATTN_HEADS, TILE, QK_DIM), head_map),
            pl.BlockSpec((1, ATTN_HEADS, V_DIM, TILE), lambda b, t: (b, 0, 0, t)),
            pl.BlockSpec((1, MLSTM_WIDTH, TILE), time_last_map),
            pl.BlockSpec((1, TILE, MLSTM_WIDTH), tile_map),
            pl.BlockSpec((1, MLSTM_WIDTH, TILE), time_last_map),
            pl.BlockSpec((1, TILE, MLSTM_WIDTH), tile_map),
            pl.BlockSpec((1, GATE_ROWS, TILE), time_last_map),
        ),
        out_shape=out_shape,
        compiler_params=_params(2),
        name=f"mixin_l{layer}",
    )(x, x, x, mod_all, g_pre, win, q_norm, wuq, kv_norm, wukv, cos4, sin4, gate_b, conv_w, conv_b)


KEY_CHUNK = 2 * TILE
ATTN_BUFFERS = 2


def _attn_kernel(*refs, chunks):
    q_refs = refs[:-5]
    k_ref, vt_ref, o_ref, s_scr, p_scr = refs[-5:]
    n_heads = k_ref.shape[1]
    units = [(j, h) for j in range(len(q_refs)) for h in range(n_heads)]
    lo, hi = chunks[0][0], chunks[-1][1]
    fold = lambda v: v.reshape(-1, SUBLANES, TILE)
    m8, l8 = {}, {}
    for stage in range(len(units) + 2):
        u_s, u_p, u_v = stage, stage - 1, stage - 2
        do_s, do_p, do_v = u_s < len(units), 0 <= u_p < len(units), 0 <= u_v < len(units)
        if do_v:
            j, h = units[u_v]
            acc = jnp.dot(vt_ref[0, h, :, lo:hi], p_scr[u_v % ATTN_BUFFERS, lo:hi, :], preferred_element_type=F32)
            l = jnp.sum(l8.pop(u_v), axis=0, keepdims=True)
            o_ref[0, j * TILE:(j + 1) * TILE, h * V_DIM:(h + 1) * V_DIM] = (acc / l).T
        if do_s:
            j, h_s = units[u_s]
            q = q_refs[j][0, h_s]
            m8[u_s] = jnp.full((SUBLANES, TILE), -jnp.inf, F32)
        if do_p:
            m = jnp.max(m8.pop(u_p), axis=0, keepdims=True)
            l8[u_p] = jnp.zeros((SUBLANES, TILE), F32)
        for c_lo, c_hi in chunks:
            if do_s:
                s = lax.dot_general(k_ref[0, h_s, c_lo:c_hi, :], q, (((1,), (1,)), ((), ())),
                                    preferred_element_type=F32)
                s_scr[u_s % ATTN_BUFFERS, c_lo:c_hi, :] = s
                m8[u_s] = jnp.maximum(m8[u_s], jnp.max(fold(s), axis=0))
            if do_p:
                p = jnp.exp2(s_scr[u_p % ATTN_BUFFERS, c_lo:c_hi, :] - m)
                l8[u_p] = l8[u_p] + jnp.sum(fold(p), axis=0)
                p_scr[u_p % ATTN_BUFFERS, c_lo:c_hi, :] = p.astype(BF16)


def _key_chunks(n_keys, width):
    return tuple((lo, min(lo + width, n_keys)) for lo in range(0, n_keys, width))


def _attention(q, k, vt, *, q_tile0, n_q, n_keys, name):
    bsz, n_heads, _, _ = q.shape
    qts = max(t for t in (1, 2, 4) if n_q % t == 0)
    q_specs = [pl.BlockSpec((1, n_heads, TILE, QK_DIM), lambda b, i, j=j: (b, 0, i * qts + j + q_tile0, 0))
               for j in range(qts)]
    return pl.pallas_call(
        functools.partial(_attn_kernel, chunks=_key_chunks(n_keys, KEY_CHUNK)),
        grid=(bsz, n_q // qts),
        in_specs=q_specs + [
            pl.BlockSpec((1, n_heads, n_keys, QK_DIM), lambda b, i: (b, 0, 0, 0)),
            pl.BlockSpec((1, n_heads, V_DIM, n_keys), lambda b, i: (b, 0, 0, 0)),
        ],
        out_specs=pl.BlockSpec((1, qts * TILE, n_heads * V_DIM), lambda b, i: (b, i, 0)),
        out_shape=jax.ShapeDtypeStruct((bsz, n_q * TILE, n_heads * V_DIM), F32),
        scratch_shapes=[pltpu.VMEM((ATTN_BUFFERS, n_keys, TILE), F32),
                        pltpu.VMEM((ATTN_BUFFERS, n_keys, TILE), BF16)],
        compiler_params=_params(2),
        name=name,
    )(*([q] * qts), k, vt)


STATE_ROWS = MLSTM_DH + 2 * SUBLANES


def _bwd_chunk(j, n_ctx_tiles, n_tiles):
    return jnp.where(j < n_ctx_tiles, n_ctx_tiles - 1 - j, n_tiles - 1 - (j - n_ctx_tiles))


def _mlstm_kernel(kf_ref, qtf_ref, vtf_ref, gf_ref, kb_ref, qtb_ref, vtb_ref, gb_ref, of_ref, ob_ref,
                  st_scr, m_scr, d_scr, p_scr, r_scr):
    nh = MLSTM_HEADS

    @pl.when(pl.program_id(1) == 0)
    def _():
        st_scr[...] = jnp.zeros_like(st_scr)
        m_scr[...] = jnp.zeros_like(m_scr)

    rows = [slice(i * SUBLANES, (i + 1) * SUBLANES) for i in range(GATE_ROWS // SUBLANES)]
    s_idx = lax.broadcasted_iota(jnp.int32, (TILE, TILE), 0)
    t_idx = lax.broadcasted_iota(jnp.int32, (TILE, TILE), 1)
    pad_row = lax.broadcasted_iota(jnp.int32, (2 * SUBLANES, TILE), 0) == 0
    ones_rows = jnp.where(pad_row, 1.0, 0.0).astype(BF16)

    units = []
    for d, (reverse, g_ref, g0, k_ref, qt_ref, vt_ref, o_ref) in enumerate(
            ((False, gf_ref, 0, kf_ref, qtf_ref, vtf_ref, of_ref), (True, gb_ref, 3, kb_ref, qtb_ref, vtb_ref, ob_ref))):
        a, b, a_max = g_ref[0, rows[g0], :], g_ref[0, rows[g0 + 1], :], g_ref[0, rows[g0 + 2], :]
        last = 0 if reverse else TILE - 1
        m_old = m_scr[d][:, 0:1]
        mx = jnp.maximum(m_old, a_max)
        inter = jnp.exp(m_old - mx)
        e_inv = jnp.exp(-(b + mx))
        mx_last = mx[:, last:last + 1]
        w = jnp.exp(a - mx_last)
        decay = jnp.exp(m_old - mx_last)
        m_scr[d] = jnp.broadcast_to(b[:, last:last + 1] + mx_last, (SUBLANES, LANES))
        seen = s_idx >= t_idx if reverse else s_idx <= t_idx
        for hd in range(nh):
            sl = slice(hd * MLSTM_DH, (hd + 1) * MLSTM_DH)
            row = slice(hd, hd + 1)
            units.append(dict(u=d * nh + hd, sl=sl, seen=seen, a=a[row], mx=mx[row], inter=inter[row],
                              e_inv=e_inv[row], w=w[row], decay=decay[row],
                              k_ref=k_ref, qt_ref=qt_ref, vt_ref=vt_ref, o_ref=o_ref))

    for un in units:
        a_rep = jnp.broadcast_to(un["a"] * LOG2E, (LANES, TILE)).T
        log_d = jnp.concatenate([a_rep] * (TILE // LANES), axis=1) - un["mx"] * LOG2E
        d_scr[un["u"]] = jnp.exp2(jnp.where(un["seen"], log_d, -jnp.inf))

    for un in units:
        u, sl = un["u"], un["sl"]
        lhs = jnp.concatenate([un["k_ref"][0, :, sl], st_scr[u].astype(BF16)], axis=0)
        r = jnp.dot(lhs, un["qt_ref"][0, sl, :], preferred_element_type=F32)
        p_scr[u] = (r[0:TILE] * d_scr[u]).astype(BF16)
        r_scr[u] = r[TILE:TILE + STATE_ROWS]

    for un in units:
        u, sl = un["u"], un["sl"]
        nd = jnp.dot(jnp.concatenate([un["vt_ref"][0, sl, :], ones_rows], axis=0), p_scr[u],
                     preferred_element_type=F32)
        r_state = r_scr[u]
        den = nd[MLSTM_DH:MLSTM_DH + 1] + un["inter"] * r_state[MLSTM_DH:MLSTM_DH + 1]
        scale = 1.0 / jnp.maximum(jnp.abs(den), un["e_inv"])
        h_t = (nd[0:MLSTM_DH] + un["inter"] * r_state[0:MLSTM_DH]) * scale
        un["o_ref"][0, :, sl] = h_t.T

    for un in units:
        u, sl = un["u"], un["sl"]
        vt_h = un["vt_ref"][0, sl, :]
        vw = jnp.concatenate([(vt_h.astype(F32) * un["w"]).astype(BF16),
                              jnp.where(pad_row, un["w"], 0.0).astype(BF16)], axis=0)
        st_scr[u] = un["decay"] * st_scr[u] + jnp.dot(vw, un["k_ref"][0, :, sl], preferred_element_type=F32)


def _mlstm(k_m, qt_m, vt_m, gates_t, *, n_ctx_tiles, layer):
    bsz, t_all, width = k_m.shape
    n_tiles = t_all // TILE
    bwd = functools.partial(_bwd_chunk, n_ctx_tiles=n_ctx_tiles, n_tiles=n_tiles)

    def specs(chunk):
        return [pl.BlockSpec((1, TILE, width), lambda b, j: (b, chunk(j), 0)),
                pl.BlockSpec((1, width, TILE), lambda b, j: (b, 0, chunk(j))),
                pl.BlockSpec((1, width, TILE), lambda b, j: (b, 0, chunk(j))),
                pl.BlockSpec((1, GATE_ROWS, TILE), lambda b, j: (b, 0, chunk(j)))]

    out_sds = jax.ShapeDtypeStruct((bsz, t_all, width), F32)
    return pl.pallas_call(
        _mlstm_kernel,
        grid=(bsz, n_tiles),
        in_specs=specs(lambda j: j) + specs(bwd),
        out_specs=(pl.BlockSpec((1, TILE, width), lambda b, j: (b, j, 0)),
                   pl.BlockSpec((1, TILE, width), lambda b, j: (b, bwd(j), 0))),
        out_shape=(out_sds, out_sds),
        scratch_shapes=[pltpu.VMEM((2 * MLSTM_HEADS, STATE_ROWS, MLSTM_DH), F32),
                        pltpu.VMEM((2, SUBLANES, LANES), F32),
                        pltpu.VMEM((2 * MLSTM_HEADS, TILE, TILE), F32),
                        pltpu.VMEM((2 * MLSTM_HEADS, TILE, TILE), BF16),
                        pltpu.VMEM((2 * MLSTM_HEADS, STATE_ROWS, TILE), F32)],
        compiler_params=_params(2),
        name=f"mlstm_l{layer}",
    )(k_m, qt_m, vt_m, gates_t, k_m, qt_m, vt_m, gates_t)


def _merge_body(xs, attn, hms, os, gates, g_attn, g_mlstm, g_post, wout_ref):
    a_n, hm_n = [], []
    for a, hm, o in zip(attn, hms, os):
        hm = hm * jax.nn.sigmoid(o)
        normed = []
        for hd in range(MLSTM_HEADS):
            seg = hm[:, hd * MLSTM_DH:(hd + 1) * MLSTM_DH]
            cen = seg - jnp.mean(seg, axis=-1, keepdims=True)
            normed.append(cen * lax.rsqrt(jnp.mean(cen * cen, axis=-1, keepdims=True) + EPS))
        hm_n.append((jnp.concatenate(normed, axis=-1) * g_mlstm).astype(BF16))
        a_n.append(_rms(a, g_attn).astype(BF16))
    y = jnp.dot(jnp.concatenate(a_n, axis=0), wout_ref[0:ATTN_WIDTH, :], preferred_element_type=F32)
    y = y + jnp.dot(jnp.concatenate(hm_n, axis=0), wout_ref[ATTN_WIDTH:ATTN_WIDTH + MLSTM_WIDTH, :],
                    preferred_element_type=F32)
    return [x + gate * _rms(y[k * TILE:(k + 1) * TILE], g_post) for k, (x, gate) in enumerate(zip(xs, gates))]


def _merge_ffn_kernel(*refs, plan, split):
    per_slot = 7 if split else 6
    slots = [refs[k * per_slot:(k + 1) * per_slot] for k in range(plan.tps)]
    (ga_ref, gm_ref, gpost1_ref, wout_ref, gpre2_ref, gpost2_ref, wg_ref, wu_ref, wd_ref,
     out_ref, a_scr) = refs[plan.tps * per_slot:]
    xs = [s[0][0] for s in slots]
    attn = [_slot_tile(plan, k, s[1:-4], split) for k, s in enumerate(slots)]
    hms = [s[-4][0] + s[-3][0] for s in slots]
    os = [s[-2][0] for s in slots]
    mods = [s[-1][0, 0] for s in slots]
    x1 = _merge_body(xs, attn, hms, os, [m[5:6] for m in mods], ga_ref[...], gm_ref[...], gpost1_ref[...], wout_ref)
    outs = _ffn_body(x1, mods, 2, gpre2_ref[...], gpost2_ref[...], wg_ref, wu_ref, wd_ref, a_scr)
    for k, out in enumerate(outs):
        out_ref[k] = out


def _merge_ffn(x, a, h_fwd, h_bwd, o, mod_all, g_attn, g_mlstm, g_pre, g_post, wout, wg, wu, wd,
               *, layer, n_ctx_tiles, t0):
    bsz, t_all, d = x.shape
    n_tiles = t_all // TILE - t0
    ff = wd.shape[-2]
    split = isinstance(a, tuple)
    a_srcs = [_as_tiles(s) for s in (a if split else (a,))]
    stream = [_as_tiles(s) for s in (h_fwd, h_bwd, o)]
    xt = _as_tiles(x)
    plan = _TilePlan(bsz, n_tiles, t0, n_ctx_tiles, MERGE_FFN_TILES_PER_STEP)
    slot_specs, slot_args = [], []
    for k in range(plan.tps):
        slot_specs += [plan.tile_spec(xt, k)]
        if split:
            slot_specs += [plan.tile_spec(a_srcs[0], k, 0, clamp=True),
                           plan.tile_spec(a_srcs[1], k, n_ctx_tiles, clamp=True)]
        else:
            slot_specs += [plan.tile_spec(a_srcs[0], k, t0)]
        slot_specs += [plan.tile_spec(s, k) for s in stream] + [plan.mod_spec(d, k)]
        slot_args += [xt] + a_srcs + stream + [mod_all]
    out = pl.pallas_call(
        functools.partial(_merge_ffn_kernel, plan=plan, split=split),
        grid=plan.grid,
        in_specs=slot_specs + [
            pl.BlockSpec((None, 1, ATTN_WIDTH), lambda i: (layer, 0, 0)),
            pl.BlockSpec((None, 1, MLSTM_WIDTH), lambda i: (layer, 0, 0)),
            pl.BlockSpec((None, None, 1, d), lambda i: (layer, 1, 0, 0)),
            _const1((None, ATTN_WIDTH + MLSTM_WIDTH, d), (layer, 0, 0)),
        ] + _ffn_weight_specs(layer, 1, 2, d, ff),
        out_specs=pl.BlockSpec((plan.tps, TILE, d), lambda i: (i, 0, 0)),
        out_shape=jax.ShapeDtypeStruct((bsz * n_tiles, TILE, d), F32),
        scratch_shapes=[pltpu.VMEM((plan.tps * TILE, ff), BF16)],
        compiler_params=_params(1),
        name=f"merge_ffn_l{layer}",
    )(*slot_args, g_attn, g_mlstm, g_post, wout, g_pre, g_post, wg, wu, wd)
    return out.reshape(bsz, n_tiles * TILE, d)


def _half_swap_perm():
    idx = np.arange(ROPE_DIM)
    axis, half, freq = idx // (2 * AXIS_FREQS), (idx // AXIS_FREQS) % 2, idx % AXIS_FREQS
    return axis * 2 * AXIS_FREQS + (1 - half) * AXIS_FREQS + freq


def _prep_w_in(w_in):
    sizes = (Q_LORA, KV_LORA, ROPE_DIM, 2 * MLSTM_WIDTH, MLSTM_WIDTH, MLSTM_WIDTH, N_GATES)
    offs = np.concatenate([[0], np.cumsum(sizes)])
    cq, ckv, kr, qk, v, o, g = (w_in[..., offs[i]:offs[i + 1]] for i in range(len(sizes)))
    g = jnp.pad(g, ((0, 0), (0, 0), (0, LANES - N_GATES)))
    return jnp.concatenate([cq, ckv, kr, kr[..., _half_swap_perm()], qk, v, o, g], axis=-1).astype(BF16)


def _prep_w_uq(w_uq):
    n_layers, q_lora, _ = w_uq.shape
    w = w_uq.reshape(n_layers, q_lora, ATTN_HEADS, QK_DIM)
    nope = w[..., :NOPE_DIM].reshape(n_layers, q_lora, -1)
    rope = w[..., NOPE_DIM:]
    rope_sw = rope[..., _half_swap_perm()]
    return jnp.concatenate([nope, rope.reshape(n_layers, q_lora, -1), rope_sw.reshape(n_layers, q_lora, -1)],
                           axis=-1).astype(BF16)


def _prep_w_ukv(w_ukv):
    n_layers, kv_lora, _ = w_ukv.shape
    w = w_ukv.reshape(n_layers, kv_lora, ATTN_HEADS, NOPE_DIM + V_DIM)
    return jnp.concatenate([w[..., :NOPE_DIM].reshape(n_layers, kv_lora, -1),
                            w[..., NOPE_DIM:].reshape(n_layers, kv_lora, -1)], axis=-1).astype(BF16)


def _rope_tables(n_ctx, n_tok):
    rows = n_tok // GRID_W
    t_row = jnp.repeat(jnp.arange(rows), GRID_W).astype(F32)
    t_col = jnp.tile(jnp.arange(GRID_W), rows).astype(F32)
    inv = ROPE_BASE ** (-jnp.arange(AXIS_FREQS, dtype=F32) / AXIS_FREQS)
    ang_r = t_row[:, None] * inv
    ang_c = t_col[:, None] * inv
    cos = jnp.concatenate([jnp.cos(ang_r), jnp.cos(ang_r), jnp.cos(ang_c), jnp.cos(ang_c)], axis=-1)
    sin = jnp.concatenate([-jnp.sin(ang_r), jnp.sin(ang_r), -jnp.sin(ang_c), jnp.sin(ang_c)], axis=-1)
    cos = jnp.concatenate([jnp.ones((n_ctx, ROPE_DIM), F32), cos], axis=0)
    sin = jnp.concatenate([jnp.zeros((n_ctx, ROPE_DIM), F32), sin], axis=0)
    return jnp.tile(cos, (1, ATTN_HEADS)), jnp.tile(sin, (1, ATTN_HEADS))


def kernel(x, c, ctx, c_ctx, w_mod, b_mod, norm_pre, norm_post, ffn_w_gate, ffn_w_up, ffn_w_down,
           w_in, q_norm, w_uq, kv_norm, w_ukv, attn_out_norm, conv_w, conv_b, gate_b, mlstm_norm, w_out):
    bsz, n_tok, d = x.shape
    n_ctx = ctx.shape[1]
    n_layers = w_mod.shape[0]
    assert n_tok % TILE == 0 and n_ctx % TILE == 0 and n_tok % GRID_W == 0
    n_ctx_tiles = n_ctx // TILE
    n_lat_tiles = n_tok // TILE

    wg = jnp.swapaxes(ffn_w_gate, 2, 3).astype(BF16)
    wu = jnp.swapaxes(ffn_w_up, 2, 3).astype(BF16)
    wd = ffn_w_down.astype(BF16)
    win = _prep_w_in(w_in)
    wuq = _prep_w_uq(w_uq)
    wukv = _prep_w_ukv(w_ukv)
    wout = w_out.astype(BF16)
    g_pre = norm_pre[:, :, None, :]
    g_post = norm_post[:, :, None, :]
    q_gain = q_norm[:, None, :]
    kv_gain = kv_norm[:, None, :]
    a_gain = attn_out_norm[:, None, :]
    m_gain = mlstm_norm[:, None, :]
    conv_bias = conv_b[:, None, :]
    gate_bias = jnp.pad(gate_b.reshape(n_layers, 1, N_GATES), ((0, 0), (0, 0), (0, LANES - N_GATES)))
    cos4, sin4 = _rope_tables(n_ctx, n_tok)

    c_rows = jnp.zeros((_round_up(bsz + 1, SUBLANES), d), F32).at[:bsz].set(c).at[bsz].set(c_ctx)
    mod = _modulation(c_rows, w_mod, b_mod).reshape(n_layers, -1, N_MOD, d)
    mod_all = jnp.stack([jnp.broadcast_to(mod[:, bsz:bsz + 1], (n_layers, bsz, N_MOD, d)), mod[:, :bsz]], axis=2)

    h = (ctx, x)
    for l in range(n_layers):
        t0 = n_ctx_tiles if l == n_layers - 1 else 0
        ml = mod_all[l]
        h = _ffn(h, ml, g_pre, g_post, wg, wu, wd, layer=l, j=0, sub=0, n_ctx_tiles=n_ctx_tiles)
        q, k, vt, qt_m, k_m, vt_m, o_pre, gates_t = _mixin(
            h, ml, g_pre, win, q_gain, wuq, kv_gain, wukv, cos4, sin4, gate_bias, conv_w, conv_bias,
            layer=l, n_ctx_tiles=n_ctx_tiles)
        a = _attention(q, k, vt, q_tile0=n_ctx_tiles, n_q=n_lat_tiles, n_keys=n_ctx + n_tok,
                       name=f"attention_l{l}")
        if t0 == 0:
            a = (_attention(q, k, vt, q_tile0=0, n_q=n_ctx_tiles, n_keys=n_ctx, name=f"attention_ctx_l{l}"), a)
        h_fwd, h_bwd = _mlstm(k_m, qt_m, vt_m, gates_t, n_ctx_tiles=n_ctx_tiles, layer=l)
        h = _merge_ffn(h, a, h_fwd, h_bwd, o_pre, ml, a_gain, m_gain, g_pre, g_post, wout, wg, wu, wd,
                       layer=l, n_ctx_tiles=n_ctx_tiles, t0=t0)
    return h
```

```python
import functools
import itertools

import jax
import jax.numpy as jnp
import numpy as np
from jax import lax
from jax.experimental import pallas as pl
from jax.experimental.pallas import tpu as pltpu

F32 = jnp.float32
BF16 = jnp.bfloat16

N_MOD = 9
EPS = 1e-6
GRID_W = 64
ATTN_HEADS = 4
Q_LORA = 256
KV_LORA = 128
NOPE_DIM = 128
ROPE_DIM = 64
AXIS_FREQS = ROPE_DIM // 4
V_DIM = 128
QK_DIM = NOPE_DIM + ROPE_DIM
ATTN_WIDTH = ATTN_HEADS * V_DIM
ROPE_BASE = 10000.0
SM_SCALE = QK_DIM ** -0.5
LOG2E = float(np.log2(np.e))
Q_SCALE = SM_SCALE * LOG2E
MLSTM_HEADS = 4
MLSTM_DH = 128
MLSTM_WIDTH = MLSTM_HEADS * MLSTM_DH
CONV_K = 3
N_GATES = 4 * MLSTM_HEADS

LANES = 128
SUBLANES = 8
MXU_DIM = 256
TILE = MXU_DIM
VMEM_LIMIT = 56 * 1024 * 1024

A_W = Q_LORA + KV_LORA + 2 * ROPE_DIM
QK_OFF = A_W
V_OFF = QK_OFF + 2 * MLSTM_WIDTH
O_OFF = V_OFF + MLSTM_WIDTH
G_OFF = O_OFF + MLSTM_WIDTH
IN_W = G_OFF + LANES


def _round_up(n, m):
    return (n + m - 1) // m * m


def _rms(x, g):
    return x * lax.rsqrt(jnp.mean(x * x, axis=-1, keepdims=True) + EPS) * g


def _dot_nt(x, w_t):
    return lax.dot_general(x, w_t, (((1,), (1,)), ((), ())), preferred_element_type=F32)


def _sandwich_in(x, mod, sub, g):
    return _rms(x, g) * (1.0 + mod[3 * sub + 1:3 * sub + 2]) + mod[3 * sub:3 * sub + 1]


def _params(n_grid):
    return pltpu.CompilerParams(dimension_semantics=("arbitrary",) * n_grid,
                                vmem_limit_bytes=VMEM_LIMIT)


def _const_spec(shape, index_map):
    return pl.BlockSpec(shape, index_map, pipeline_mode=pl.Buffered(1))


def _mod_spec(d, tile0, n_ctx_tiles):
    return pl.BlockSpec((1, 1, N_MOD, d),
                        lambda b, t: (b, (t + tile0 >= n_ctx_tiles).astype(jnp.int32), 0, 0))


FFN_TILES_PER_STEP = 4
MERGE_FFN_TILES_PER_STEP = 2


class _TilePlan:
    def __init__(self, bsz, n, t0, n_ctx_tiles, tiles_per_step):
        assert (bsz * n) % tiles_per_step == 0
        self.bsz, self.n, self.t0, self.n_ctx_tiles, self.tps = bsz, n, t0, n_ctx_tiles, tiles_per_step
        self.grid = (bsz * n // tiles_per_step,)

    def coords(self, i, k):
        g = i * self.tps + k
        return g // self.n, g % self.n + self.t0

    def is_ctx(self, k):
        return self.coords(pl.program_id(0), k)[1] < self.n_ctx_tiles

    def tile_spec(self, arr, k, arr_t0=0, clamp=False):
        arr_tiles = arr.shape[0] // self.bsz

        def index(i):
            b, r = self.coords(i, k)
            r = r - arr_t0
            if clamp:
                r = jnp.clip(r, 0, arr_tiles - 1)
            return (b * arr_tiles + r, 0, 0)

        return pl.BlockSpec((1, TILE, arr.shape[2]), index)

    def mod_spec(self, d, k):
        def index(i):
            b, r = self.coords(i, k)
            return (b, (r >= self.n_ctx_tiles).astype(jnp.int32), 0, 0)

        return pl.BlockSpec((1, 1, N_MOD, d), index)


def _as_tiles(x):
    return x.reshape(-1, TILE, x.shape[-1])


def _const1(shape, index):
    return _const_spec(shape, lambda i: index)


def _mod_kernel(c_ref, w_ref, b_ref, o_ref):
    c = c_ref[...]
    h = (c * jax.nn.sigmoid(c)).astype(BF16)
    o_ref[0] = jnp.dot(h, w_ref[0].astype(BF16), preferred_element_type=F32) + b_ref[0]


def _modulation(c_rows, w_mod, b_mod):
    n_layers, d, n_out = w_mod.shape
    rows = c_rows.shape[0]
    n_blk = N_MOD * LANES
    return pl.pallas_call(
        _mod_kernel,
        grid=(n_layers, n_out // n_blk),
        in_specs=[pl.BlockSpec((rows, d), lambda l, j: (0, 0)),
                  pl.BlockSpec((1, d, n_blk), lambda l, j: (l, 0, j)),
                  pl.BlockSpec((1, 1, n_blk), lambda l, j: (l, 0, j))],
        out_specs=pl.BlockSpec((1, rows, n_blk), lambda l, j: (l, 0, j)),
        out_shape=jax.ShapeDtypeStruct((n_layers, rows, n_out), F32),
        compiler_params=_params(2),
        name="modulation",
    )(c_rows, w_mod, b_mod.reshape(n_layers, 1, n_out))


def _ffn_body(xs, mods, sub, g_pre, g_post, wg_ref, wu_ref, wd_ref, a_ref):
    h = jnp.concatenate([_sandwich_in(x, m, sub, g_pre).astype(BF16) for x, m in zip(xs, mods)], axis=0)
    ff = a_ref.shape[1]
    for lo in range(0, ff, MXU_DIM):
        hi = min(lo + MXU_DIM, ff)
        g = _dot_nt(h, wg_ref[lo:hi, :])
        u = _dot_nt(h, wu_ref[lo:hi, :])
        a_ref[:, lo:hi] = (g * jax.nn.sigmoid(g) * u).astype(BF16)
    ff_main = ff // MXU_DIM * MXU_DIM
    y = jnp.dot(a_ref[:, 0:ff_main], wd_ref[0:ff_main, :], preferred_element_type=F32)
    if ff_main < ff:
        y = y + jnp.dot(a_ref[:, ff_main:ff], wd_ref[ff_main:ff, :], preferred_element_type=F32)
    return [x + 0.5 * m[3 * sub + 2:3 * sub + 3] * _rms(y[k * TILE:(k + 1) * TILE], g_post)
            for k, (x, m) in enumerate(zip(xs, mods))]


def _slot_tile(plan, k, refs, split):
    return jnp.where(plan.is_ctx(k), refs[0][0], refs[1][0]) if split else refs[0][0]


def _ffn_kernel(*refs, plan, sub, split):
    per_slot = 3 if split else 2
    slots = [refs[k * per_slot:(k + 1) * per_slot] for k in range(plan.tps)]
    gpre_ref, gpost_ref, wg_ref, wu_ref, wd_ref, o_ref, a_ref = refs[plan.tps * per_slot:]
    xs = [_slot_tile(plan, k, s[:-1], split) for k, s in enumerate(slots)]
    mods = [s[-1][0, 0] for s in slots]
    outs = _ffn_body(xs, mods, sub, gpre_ref[...], gpost_ref[...], wg_ref, wu_ref, wd_ref, a_ref)
    for k, out in enumerate(outs):
        o_ref[k] = out


def _ffn_weight_specs(layer, j, sub, d, ff):
    return [
        pl.BlockSpec((None, None, 1, d), lambda i: (layer, sub, 0, 0)),
        pl.BlockSpec((None, None, 1, d), lambda i: (layer, sub, 0, 0)),
        _const1((None, None, ff, d), (layer, j, 0, 0)),
        _const1((None, None, ff, d), (layer, j, 0, 0)),
        _const1((None, None, ff, d), (layer, j, 0, 0)),
    ]


def _ffn(src, mod_all, g_pre, g_post, wg, wu, wd, *, layer, j, sub, n_ctx_tiles):
    split = isinstance(src, tuple)
    srcs = [_as_tiles(s) for s in (src if split else (src,))]
    bsz, d = mod_all.shape[0], srcs[0].shape[-1]
    n_tiles = sum(s.shape[0] for s in srcs) // bsz
    ff = wd.shape[-2]
    plan = _TilePlan(bsz, n_tiles, 0, n_ctx_tiles, FFN_TILES_PER_STEP)
    slot_specs, slot_args = [], []
    for k in range(plan.tps):
        if split:
            slot_specs += [plan.tile_spec(srcs[0], k, 0, clamp=True), plan.tile_spec(srcs[1], k, n_ctx_tiles, clamp=True)]
        else:
            slot_specs += [plan.tile_spec(srcs[0], k)]
        slot_specs += [plan.mod_spec(d, k)]
        slot_args += srcs + [mod_all]
    out = pl.pallas_call(
        functools.partial(_ffn_kernel, plan=plan, sub=sub, split=split),
        grid=plan.grid,
        in_specs=slot_specs + _ffn_weight_specs(layer, j, sub, d, ff),
        out_specs=pl.BlockSpec((plan.tps, TILE, d), lambda i: (i, 0, 0)),
        out_shape=jax.ShapeDtypeStruct((bsz * n_tiles, TILE, d), F32),
        scratch_shapes=[pltpu.VMEM((plan.tps * TILE, ff), BF16)],
        compiler_params=_params(1),
        name=f"ffn_l{layer}_{j}",
    )(*slot_args, g_pre, g_post, wg, wu, wd)
    return out.reshape(bsz, n_tiles * TILE, d)


GATE_ROWS = 2 * 3 * SUBLANES


def _log_sigmoid(x):
    return jnp.minimum(x, 0.0) - jnp.log1p(jnp.exp(-jnp.abs(x)))


def _lane_scan(x, op, identity, reverse):
    n = x.shape[1]
    lane = lax.broadcasted_iota(jnp.int32, x.shape, 1)
    shift = 1
    while shift < n:
        if reverse:
            moved = jnp.where(lane < n - shift, pltpu.roll(x, n - shift, axis=1), identity)
        else:
            moved = jnp.where(lane >= shift, pltpu.roll(x, shift, axis=1), identity)
        x = op(x, moved)
        shift *= 2
    return x


def _gate_rows(g8, reverse):
    nh = MLSTM_HEADS
    head_rows = lax.broadcasted_iota(jnp.int32, g8.shape, 0) < nh
    b = _lane_scan(_log_sigmoid(g8), jnp.add, 0.0, reverse)
    b = jnp.where(head_rows, pltpu.roll(b, nh, axis=0), 0.0)
    a = jnp.where(head_rows, g8 - b, 0.0)
    return a, b, _lane_scan(a, jnp.maximum, -jnp.inf, reverse)


def _mixin_kernel(x_ref, xp_ref, xn_ref, mod_ref, gpre_ref, win_ref, qn_ref, wuq_ref, kvn_ref, wukv_ref,
                  cos_ref, sin_ref, gb_ref, cw_ref, cb_ref,
                  q_ref, k_ref, vt_ref, qtm_ref, km_ref, vtm_ref, o_ref, gt_ref, *, n_ctx_tiles, n_tiles):
    ci = pl.program_id(1)
    mod = mod_ref[0, 0]
    h = _sandwich_in(x_ref[0], mod, 1, gpre_ref[...]).astype(BF16)
    cos4 = cos_ref[...]
    sin4 = sin_ref[...]

    gates = jnp.dot(h, win_ref[:, G_OFF:IN_W], preferred_element_type=F32) + gb_ref[...]
    g_t = gates.T[0:N_GATES, :]
    half = N_GATES // 2
    gt_ref[0] = jnp.concatenate(_gate_rows(g_t[0:half], False) + _gate_rows(g_t[half:N_GATES], True), axis=0)

    pa = jnp.dot(h, win_ref[:, 0:A_W], preferred_element_type=F32)
    cq = pa[:, 0:Q_LORA]
    ckv = pa[:, Q_LORA:Q_LORA + KV_LORA]
    kr = pa[:, Q_LORA + KV_LORA:Q_LORA + KV_LORA + ROPE_DIM]
    kr_sw = pa[:, Q_LORA + KV_LORA + ROPE_DIM:A_W]
    k_rope = (kr * cos4[:, 0:ROPE_DIM] + kr_sw * sin4[:, 0:ROPE_DIM]).astype(BF16)

    nope_w = ATTN_HEADS * NOPE_DIM
    rope_w = ATTN_HEADS * ROPE_DIM
    qa = jnp.dot(_rms(cq, qn_ref[...]).astype(BF16), wuq_ref[...], preferred_element_type=F32)
    q_rope = qa[:, nope_w:nope_w + rope_w] * cos4 + qa[:, nope_w + rope_w:nope_w + 2 * rope_w] * sin4
    kva = jnp.dot(_rms(ckv, kvn_ref[...]).astype(BF16), wukv_ref[...], preferred_element_type=F32)
    for hd in range(ATTN_HEADS):
        q_ref[0, hd, :, 0:NOPE_DIM] = (qa[:, hd * NOPE_DIM:(hd + 1) * NOPE_DIM] * Q_SCALE).astype(BF16)
        q_ref[0, hd, :, NOPE_DIM:QK_DIM] = (q_rope[:, hd * ROPE_DIM:(hd + 1) * ROPE_DIM] * Q_SCALE).astype(BF16)
        k_ref[0, hd, :, 0:NOPE_DIM] = kva[:, hd * NOPE_DIM:(hd + 1) * NOPE_DIM].astype(BF16)
        k_ref[0, hd, :, NOPE_DIM:QK_DIM] = k_rope
        vt_ref[0, hd] = kva[:, nope_w + hd * V_DIM:nope_w + (hd + 1) * V_DIM].T.astype(BF16)

    vtm_ref[0] = jnp.dot(h, win_ref[:, V_OFF:O_OFF], preferred_element_type=F32).T.astype(BF16)
    o_ref[0] = jnp.dot(h, win_ref[:, O_OFF:G_OFF], preferred_element_type=F32)

    halo = jnp.concatenate([xp_ref[0], xn_ref[0]], axis=0)
    h_ext = jnp.concatenate([h, _sandwich_in(halo, mod, 1, gpre_ref[...]).astype(BF16)], axis=0)
    pqk = jnp.dot(h_ext, win_ref[:, QK_OFF:V_OFF], preferred_element_type=F32)
    x = pqk[0:TILE]
    row = lax.broadcasted_iota(jnp.int32, (TILE, 1), 0)
    has_prev = jnp.logical_and(ci != 0, ci != n_ctx_tiles)
    has_next = jnp.logical_and(ci != n_ctx_tiles - 1, ci != n_tiles - 1)
    prev_row = jnp.where(has_prev, pqk[TILE + SUBLANES - 1:TILE + SUBLANES], 0.0)
    next_row = jnp.where(has_next, pqk[TILE + SUBLANES:TILE + SUBLANES + 1], 0.0)
    x_prev = jnp.where(row == 0, prev_row, pltpu.roll(x, 1, axis=0))
    x_next = jnp.where(row == TILE - 1, next_row, pltpu.roll(x, TILE - 1, axis=0))
    u = x_prev * cw_ref[0:1, :] + x * cw_ref[1:2, :] + x_next * cw_ref[2:3, :] + cb_ref[...]
    u = u * jax.nn.sigmoid(u)
    qtm_ref[0] = u[:, 0:MLSTM_WIDTH].T.astype(BF16)
    km_ref[0] = (u[:, MLSTM_WIDTH:2 * MLSTM_WIDTH] * (MLSTM_DH ** -0.5)).astype(BF16)


def _mixin(x, mod_all, g_pre, win, q_norm, wuq, kv_norm, wukv, cos4, sin4, gate_b, conv_w, conv_b,
           *, layer, n_ctx_tiles):
    bsz, t_all, d = x.shape
    n_tiles = t_all // TILE
    rope_w = ATTN_HEADS * ROPE_DIM
    halo_per_tile = TILE // SUBLANES
    n_halo = t_all // SUBLANES
    tile_map = lambda b, t: (b, t, 0)
    head_map = lambda b, t: (b, 0, t, 0)
    time_last_map = lambda b, t: (b, 0, t)
    out_shape = (
        jax.ShapeDtypeStruct((bsz, ATTN_HEADS, t_all, QK_DIM), BF16),
        jax.ShapeDtypeStruct((bsz, ATTN_HEADS, t_all, QK_DIM), BF16),
        jax.ShapeDtypeStruct((bsz, ATTN_HEADS, V_DIM, t_all), BF16),
        jax.ShapeDtypeStruct((bsz, MLSTM_WIDTH, t_all), BF16),
        jax.ShapeDtypeStruct((bsz, t_all, MLSTM_WIDTH), BF16),
        jax.ShapeDtypeStruct((bsz, MLSTM_WIDTH, t_all), BF16),
        jax.ShapeDtypeStruct((bsz, t_all, MLSTM_WIDTH), F32),
        jax.ShapeDtypeStruct((bsz, GATE_ROWS, t_all), F32),
    )
    return pl.pallas_call(
        functools.partial(_mixin_kernel, n_ctx_tiles=n_ctx_tiles, n_tiles=n_tiles),
        grid=(bsz, n_tiles),
        in_specs=[
            pl.BlockSpec((1, TILE, d), tile_map),
            pl.BlockSpec((1, SUBLANES, d), lambda b, t: (b, jnp.maximum(t * halo_per_tile - 1, 0), 0)),
            pl.BlockSpec((1, SUBLANES, d), lambda b, t: (b, jnp.minimum((t + 1) * halo_per_tile, n_halo - 1), 0)),
            _mod_spec(d, 0, n_ctx_tiles),
            pl.BlockSpec((None, None, 1, d), lambda b, t: (layer, 1, 0, 0)),
            _const_spec((None, d, IN_W), lambda b, t: (layer, 0, 0)),
            pl.BlockSpec((None, 1, Q_LORA), lambda b, t: (layer, 0, 0)),
            _const_spec((None, Q_LORA, wuq.shape[-1]), lambda b, t: (layer, 0, 0)),
            pl.BlockSpec((None, 1, KV_LORA), lambda b, t: (layer, 0, 0)),
            _const_spec((None, KV_LORA, wukv.shape[-1]), lambda b, t: (layer, 0, 0)),
            pl.BlockSpec((TILE, rope_w), lambda b, t: (t, 0)),
            pl.BlockSpec((TILE, rope_w), lambda b, t: (t, 0)),
            pl.BlockSpec((None, 1, LANES), lambda b, t: (layer, 0, 0)),
            pl.BlockSpec((None, CONV_K, 2 * MLSTM_WIDTH), lambda b, t: (layer, 0, 0)),
            pl.BlockSpec((None, 1, 2 * MLSTM_WIDTH), lambda b, t: (layer, 0, 0)),
        ],
        out_specs=(
            pl.BlockSpec((1, ATTN_HEADS, TILE, QK_DIM), head_map),
            pl.BlockSpec((1, ATTN_HEADS, TILE, QK_DIM), head_map),
            pl.BlockSpec((1, ATTN_HEADS, V_DIM, TILE), lambda b, t: (b, 0, 0, t)),
            pl.BlockSpec((1, MLSTM_WIDTH, TILE), time_last_map),
            pl.BlockSpec((1, TILE, MLSTM_WIDTH), tile_map),
            pl.BlockSpec((1, MLSTM_WIDTH, TILE), time_last_map),
            pl.BlockSpec((1, TILE, MLSTM_WIDTH), tile_map),
            pl.BlockSpec((1, GATE_ROWS, TILE), time_last_map),
        ),
        out_shape=out_shape,
        compiler_params=_params(2),
        name=f"mixin_l{layer}",
    )(x, x, x, mod_all, g_pre, win, q_norm, wuq, kv_norm, wukv, cos4, sin4, gate_b, conv_w, conv_b)


KEY_CHUNK = 2 * TILE
ATTN_BUFFERS = 2


def _attn_kernel(*refs, chunks):
    q_refs = refs[:-5]
    k_ref, vt_ref, o_ref, s_scr, p_scr = refs[-5:]
    n_heads = k_ref.shape[1]
    units = [(j, h) for j in range(len(q_refs)) for h in range(n_heads)]
    lo, hi = chunks[0][0], chunks[-1][1]
    fold = lambda v: v.reshape(-1, SUBLANES, TILE)
    m8, l8 = {}, {}
    for stage in range(len(units) + 2):
        u_s, u_p, u_v = stage, stage - 1, stage - 2
        do_s, do_p, do_v = u_s < len(units), 0 <= u_p < len(units), 0 <= u_v < len(units)
        if do_v:
            j, h = units[u_v]
            acc = jnp.dot(vt_ref[0, h, :, lo:hi], p_scr[u_v % ATTN_BUFFERS, lo:hi, :], preferred_element_type=F32)
            l = jnp.sum(l8.pop(u_v), axis=0, keepdims=True)
            o_ref[0, j * TILE:(j + 1) * TILE, h * V_DIM:(h + 1) * V_DIM] = (acc / l).T
        if do_s:
            j, h_s = units[u_s]
            q = q_refs[j][0, h_s]
            m8[u_s] = jnp.full((SUBLANES, TILE), -jnp.inf, F32)
        if do_p:
            m = jnp.max(m8.pop(u_p), axis=0, keepdims=True)
            l8[u_p] = jnp.zeros((SUBLANES, TILE), F32)
        for c_lo, c_hi in chunks:
            if do_s:
                s = lax.dot_general(k_ref[0, h_s, c_lo:c_hi, :], q, (((1,), (1,)), ((), ())),
                                    preferred_element_type=F32)
                s_scr[u_s % ATTN_BUFFERS, c_lo:c_hi, :] = s
                m8[u_s] = jnp.maximum(m8[u_s], jnp.max(fold(s), axis=0))
            if do_p:
                p = jnp.exp2(s_scr[u_p % ATTN_BUFFERS, c_lo:c_hi, :] - m)
                l8[u_p] = l8[u_p] + jnp.sum(fold(p), axis=0)
                p_scr[u_p % ATTN_BUFFERS, c_lo:c_hi, :] = p.astype(BF16)


def _key_chunks(n_keys, width):
    return tuple((lo, min(lo + width, n_keys)) for lo in range(0, n_keys, width))


def _attention(q, k, vt, *, q_tile0, n_q, n_keys, name):
    bsz, n_heads, _, _ = q.shape
    qts = max(t for t in (1, 2, 4) if n_q % t == 0)
    q_specs = [pl.BlockSpec((1, n_heads, TILE, QK_DIM), lambda b, i, j=j: (b, 0, i * qts + j + q_tile0, 0))
               for j in range(qts)]
    return pl.pallas_call(
        functools.partial(_attn_kernel, chunks=_key_chunks(n_keys, KEY_CHUNK)),
        grid=(bsz, n_q // qts),
        in_specs=q_specs + [
            pl.BlockSpec((1, n_heads, n_keys, QK_DIM), lambda b, i: (b, 0, 0, 0)),
            pl.BlockSpec((1, n_heads, V_DIM, n_keys), lambda b, i: (b, 0, 0, 0)),
        ],
        out_specs=pl.BlockSpec((1, qts * TILE, n_heads * V_DIM), lambda b, i: (b, i, 0)),
        out_shape=jax.ShapeDtypeStruct((bsz, n_q * TILE, n_heads * V_DIM), F32),
        scratch_shapes=[pltpu.VMEM((ATTN_BUFFERS, n_keys, TILE), F32),
                        pltpu.VMEM((ATTN_BUFFERS, n_keys, TILE), BF16)],
        compiler_params=_params(2),
        name=name,
    )(*([q] * qts), k, vt)


STATE_ROWS = MLSTM_DH + 2 * SUBLANES
MLSTM_ROWS_PER_STEP = 4


def _bwd_chunk(j, n_ctx_tiles, n_tiles):
    return jnp.where(j < n_ctx_tiles, n_ctx_tiles - 1 - j, n_tiles - 1 - (j - n_ctx_tiles))


def _mlstm_kernel(kf_ref, qtf_ref, vtf_ref, gf_ref, kb_ref, qtb_ref, vtb_ref, gb_ref, of_ref, ob_ref,
                  st_scr, m_scr, d_scr, p_scr, r_scr):
    nh = MLSTM_HEADS
    rows_per_step = kf_ref.shape[0]

    @pl.when(pl.program_id(1) == 0)
    def _():
        st_scr[...] = jnp.zeros_like(st_scr)
        m_scr[...] = jnp.zeros_like(m_scr)

    rows = [slice(i * SUBLANES, (i + 1) * SUBLANES) for i in range(GATE_ROWS // SUBLANES)]
    s_idx = lax.broadcasted_iota(jnp.int32, (TILE, TILE), 0)
    t_idx = lax.broadcasted_iota(jnp.int32, (TILE, TILE), 1)
    pad_row = lax.broadcasted_iota(jnp.int32, (2 * SUBLANES, TILE), 0) == 0
    ones_rows = jnp.where(pad_row, 1.0, 0.0).astype(BF16)

    units = []
    scans = ((False, gf_ref, 0, kf_ref, qtf_ref, vtf_ref, of_ref), (True, gb_ref, 3, kb_ref, qtb_ref, vtb_ref, ob_ref))
    for bb, (d, (reverse, g_ref, g0, k_ref, qt_ref, vt_ref, o_ref)) in itertools.product(
            range(rows_per_step), enumerate(scans)):
        scan = bb * len(scans) + d
        a, b, a_max = g_ref[bb, rows[g0], :], g_ref[bb, rows[g0 + 1], :], g_ref[bb, rows[g0 + 2], :]
        last = 0 if reverse else TILE - 1
        m_old = m_scr[scan][:, 0:1]
        mx = jnp.maximum(m_old, a_max)
        inter = jnp.exp(m_old - mx)
        e_inv = jnp.exp(-(b + mx))
        mx_last = mx[:, last:last + 1]
        w = jnp.exp(a - mx_last)
        decay = jnp.exp(m_old - mx_last)
        m_scr[scan] = jnp.broadcast_to(b[:, last:last + 1] + mx_last, (SUBLANES, LANES))
        seen = s_idx >= t_idx if reverse else s_idx <= t_idx
        for hd in range(nh):
            sl = slice(hd * MLSTM_DH, (hd + 1) * MLSTM_DH)
            row = slice(hd, hd + 1)
            units.append(dict(u=scan * nh + hd, bb=bb, sl=sl, seen=seen, a=a[row], mx=mx[row], inter=inter[row],
                              e_inv=e_inv[row], w=w[row], decay=decay[row],
                              k_ref=k_ref, qt_ref=qt_ref, vt_ref=vt_ref, o_ref=o_ref))

    for un in units:
        a_rep = jnp.broadcast_to(un["a"] * LOG2E, (LANES, TILE)).T
        log_d = jnp.concatenate([a_rep] * (TILE // LANES), axis=1) - un["mx"] * LOG2E
        d_scr[un["u"]] = jnp.exp2(jnp.where(un["seen"], log_d, -jnp.inf))

    for un in units:
        u, bb, sl = un["u"], un["bb"], un["sl"]
        lhs = jnp.concatenate([un["k_ref"][bb, :, sl], st_scr[u].astype(BF16)], axis=0)
        r = jnp.dot(lhs, un["qt_ref"][bb, sl, :], preferred_element_type=F32)
        p_scr[u] = (r[0:TILE] * d_scr[u]).astype(BF16)
        r_scr[u] = r[TILE:TILE + STATE_ROWS]

    for un in units:
        u, bb, sl = un["u"], un["bb"], un["sl"]
        nd = jnp.dot(jnp.concatenate([un["vt_ref"][bb, sl, :], ones_rows], axis=0), p_scr[u],
                     preferred_element_type=F32)
        r_state = r_scr[u]
        den = nd[MLSTM_DH:MLSTM_DH + 1] + un["inter"] * r_state[MLSTM_DH:MLSTM_DH + 1]
        scale = 1.0 / jnp.maximum(jnp.abs(den), un["e_inv"])
        h_t = (nd[0:MLSTM_DH] + un["inter"] * r_state[0:MLSTM_DH]) * scale
        un["o_ref"][bb, :, sl] = h_t.T

    for un in units:
        u, bb, sl = un["u"], un["bb"], un["sl"]
        vt_h = un["vt_ref"][bb, sl, :]
        vw = jnp.concatenate([(vt_h.astype(F32) * un["w"]).astype(BF16),
                              jnp.where(pad_row, un["w"], 0.0).astype(BF16)], axis=0)
        st_scr[u] = un["decay"] * st_scr[u] + jnp.dot(vw, un["k_ref"][bb, :, sl], preferred_element_type=F32)


def _mlstm(k_m, qt_m, vt_m, gates_t, *, n_ctx_tiles, layer):
    bsz, t_all, width = k_m.shape
    n_tiles = t_all // TILE
    bps = MLSTM_ROWS_PER_STEP if bsz % MLSTM_ROWS_PER_STEP == 0 else 1
    n_units = bps * 2 * MLSTM_HEADS
    bwd = functools.partial(_bwd_chunk, n_ctx_tiles=n_ctx_tiles, n_tiles=n_tiles)

    def specs(chunk):
        return [pl.BlockSpec((bps, TILE, width), lambda b, j: (b, chunk(j), 0)),
                pl.BlockSpec((bps, width, TILE), lambda b, j: (b, 0, chunk(j))),
                pl.BlockSpec((bps, width, TILE), lambda b, j: (b, 0, chunk(j))),
                pl.BlockSpec((bps, GATE_ROWS, TILE), lambda b, j: (b, 0, chunk(j)))]

    out_sds = jax.ShapeDtypeStruct((bsz, t_all, width), F32)
    return pl.pallas_call(
        _mlstm_kernel,
        grid=(bsz // bps, n_tiles),
        in_specs=specs(lambda j: j) + specs(bwd),
        out_specs=(pl.BlockSpec((bps, TILE, width), lambda b, j: (b, j, 0)),
                   pl.BlockSpec((bps, TILE, width), lambda b, j: (b, bwd(j), 0))),
        out_shape=(out_sds, out_sds),
        scratch_shapes=[pltpu.VMEM((n_units, STATE_ROWS, MLSTM_DH), F32),
                        pltpu.VMEM((n_units // MLSTM_HEADS, SUBLANES, LANES), F32),
                        pltpu.VMEM((n_units, TILE, TILE), F32),
                        pltpu.VMEM((n_units, TILE, TILE), BF16),
                        pltpu.VMEM((n_units, STATE_ROWS, TILE), F32)],
        compiler_params=_params(2),
        name=f"mlstm_l{layer}",
    )(k_m, qt_m, vt_m, gates_t, k_m, qt_m, vt_m, gates_t)


def _merge_body(xs, attn, hms, os, gates, g_attn, g_mlstm, g_post, wout_ref):
    a_n, hm_n = [], []
    for a, hm, o in zip(attn, hms, os):
        hm = hm * jax.nn.sigmoid(o)
        normed = []
        for hd in range(MLSTM_HEADS):
            seg = hm[:, hd * MLSTM_DH:(hd + 1) * MLSTM_DH]
            cen = seg - jnp.mean(seg, axis=-1, keepdims=True)
            normed.append(cen * lax.rsqrt(jnp.mean(cen * cen, axis=-1, keepdims=True) + EPS))
        hm_n.append((jnp.concatenate(normed, axis=-1) * g_mlstm).astype(BF16))
        a_n.append(_rms(a, g_attn).astype(BF16))
    y = jnp.dot(jnp.concatenate(a_n, axis=0), wout_ref[0:ATTN_WIDTH, :], preferred_element_type=F32)
    y = y + jnp.dot(jnp.concatenate(hm_n, axis=0), wout_ref[ATTN_WIDTH:ATTN_WIDTH + MLSTM_WIDTH, :],
                    preferred_element_type=F32)
    return [x + gate * _rms(y[k * TILE:(k + 1) * TILE], g_post) for k, (x, gate) in enumerate(zip(xs, gates))]


def _merge_ffn_kernel(*refs, plan, split):
    per_slot = 7 if split else 6
    slots = [refs[k * per_slot:(k + 1) * per_slot] for k in range(plan.tps)]
    (ga_ref, gm_ref, gpost1_ref, wout_ref, gpre2_ref, gpost2_ref, wg_ref, wu_ref, wd_ref,
     out_ref, a_scr) = refs[plan.tps * per_slot:]
    xs = [s[0][0] for s in slots]
    attn = [_slot_tile(plan, k, s[1:-4], split) for k, s in enumerate(slots)]
    hms = [s[-4][0] + s[-3][0] for s in slots]
    os = [s[-2][0] for s in slots]
    mods = [s[-1][0, 0] for s in slots]
    x1 = _merge_body(xs, attn, hms, os, [m[5:6] for m in mods], ga_ref[...], gm_ref[...], gpost1_ref[...], wout_ref)
    outs = _ffn_body(x1, mods, 2, gpre2_ref[...], gpost2_ref[...], wg_ref, wu_ref, wd_ref, a_scr)
    for k, out in enumerate(outs):
        out_ref[k] = out


def _merge_ffn(x, a, h_fwd, h_bwd, o, mod_all, g_attn, g_mlstm, g_pre, g_post, wout, wg, wu, wd,
               *, layer, n_ctx_tiles, t0):
    bsz, t_all, d = x.shape
    n_tiles = t_all // TILE - t0
    ff = wd.shape[-2]
    split = isinstance(a, tuple)
    a_srcs = [_as_tiles(s) for s in (a if split else (a,))]
    stream = [_as_tiles(s) for s in (h_fwd, h_bwd, o)]
    xt = _as_tiles(x)
    plan = _TilePlan(bsz, n_tiles, t0, n_ctx_tiles, MERGE_FFN_TILES_PER_STEP)
    slot_specs, slot_args = [], []
    for k in range(plan.tps):
        slot_specs += [plan.tile_spec(xt, k)]
        if split:
            slot_specs += [plan.tile_spec(a_srcs[0], k, 0, clamp=True),
                           plan.tile_spec(a_srcs[1], k, n_ctx_tiles, clamp=True)]
        else:
            slot_specs += [plan.tile_spec(a_srcs[0], k, t0)]
        slot_specs += [plan.tile_spec(s, k) for s in stream] + [plan.mod_spec(d, k)]
        slot_args += [xt] + a_srcs + stream + [mod_all]
    out = pl.pallas_call(
        functools.partial(_merge_ffn_kernel, plan=plan, split=split),
        grid=plan.grid,
        in_specs=slot_specs + [
            pl.BlockSpec((None, 1, ATTN_WIDTH), lambda i: (layer, 0, 0)),
            pl.BlockSpec((None, 1, MLSTM_WIDTH), lambda i: (layer, 0, 0)),
            pl.BlockSpec((None, None, 1, d), lambda i: (layer, 1, 0, 0)),
            _const1((None, ATTN_WIDTH + MLSTM_WIDTH, d), (layer, 0, 0)),
        ] + _ffn_weight_specs(layer, 1, 2, d, ff),
        out_specs=pl.BlockSpec((plan.tps, TILE, d), lambda i: (i, 0, 0)),
        out_shape=jax.ShapeDtypeStruct((bsz * n_tiles, TILE, d), F32),
        scratch_shapes=[pltpu.VMEM((plan.tps * TILE, ff), BF16)],
        compiler_params=_params(1),
        name=f"merge_ffn_l{layer}",
    )(*slot_args, g_attn, g_mlstm, g_post, wout, g_pre, g_post, wg, wu, wd)
    return out.reshape(bsz, n_tiles * TILE, d)


def _half_swap_perm():
    idx = np.arange(ROPE_DIM)
    axis, half, freq = idx // (2 * AXIS_FREQS), (idx // AXIS_FREQS) % 2, idx % AXIS_FREQS
    return axis * 2 * AXIS_FREQS + (1 - half) * AXIS_FREQS + freq


def _prep_w_in(w_in):
    sizes = (Q_LORA, KV_LORA, ROPE_DIM, 2 * MLSTM_WIDTH, MLSTM_WIDTH, MLSTM_WIDTH, N_GATES)
    offs = np.concatenate([[0], np.cumsum(sizes)])
    cq, ckv, kr, qk, v, o, g = (w_in[..., offs[i]:offs[i + 1]] for i in range(len(sizes)))
    g = jnp.pad(g, ((0, 0), (0, 0), (0, LANES - N_GATES)))
    return jnp.concatenate([cq, ckv, kr, kr[..., _half_swap_perm()], qk, v, o, g], axis=-1).astype(BF16)


def _prep_w_uq(w_uq):
    n_layers, q_lora, _ = w_uq.shape
    w = w_uq.reshape(n_layers, q_lora, ATTN_HEADS, QK_DIM)
    nope = w[..., :NOPE_DIM].reshape(n_layers, q_lora, -1)
    rope = w[..., NOPE_DIM:]
    rope_sw = rope[..., _half_swap_perm()]
    return jnp.concatenate([nope, rope.reshape(n_layers, q_lora, -1), rope_sw.reshape(n_layers, q_lora, -1)],
                           axis=-1).astype(BF16)


def _prep_w_ukv(w_ukv):
    n_layers, kv_lora, _ = w_ukv.shape
    w = w_ukv.reshape(n_layers, kv_lora, ATTN_HEADS, NOPE_DIM + V_DIM)
    return jnp.concatenate([w[..., :NOPE_DIM].reshape(n_layers, kv_lora, -1),
                            w[..., NOPE_DIM:].reshape(n_layers, kv_lora, -1)], axis=-1).astype(BF16)


def _rope_tables(n_ctx, n_tok):
    rows = n_tok // GRID_W
    t_row = jnp.repeat(jnp.arange(rows), GRID_W).astype(F32)
    t_col = jnp.tile(jnp.arange(GRID_W), rows).astype(F32)
    inv = ROPE_BASE ** (-jnp.arange(AXIS_FREQS, dtype=F32) / AXIS_FREQS)
    ang_r = t_row[:, None] * inv
    ang_c = t_col[:, None] * inv
    cos = jnp.concatenate([jnp.cos(ang_r), jnp.cos(ang_r), jnp.cos(ang_c), jnp.cos(ang_c)], axis=-1)
    sin = jnp.concatenate([-jnp.sin(ang_r), jnp.sin(ang_r), -jnp.sin(ang_c), jnp.sin(ang_c)], axis=-1)
    cos = jnp.concatenate([jnp.ones((n_ctx, ROPE_DIM), F32), cos], axis=0)
    sin = jnp.concatenate([jnp.zeros((n_ctx, ROPE_DIM), F32), sin], axis=0)
    return jnp.tile(cos, (1, ATTN_HEADS)), jnp.tile(sin, (1, ATTN_HEADS))


def kernel(x, c, ctx, c_ctx, w_mod, b_mod, norm_pre, norm_post, ffn_w_gate, ffn_w_up, ffn_w_down,
           w_in, q_norm, w_uq, kv_norm, w_ukv, attn_out_norm, conv_w, conv_b, gate_b, mlstm_norm, w_out):
    bsz, n_tok, d = x.shape
    n_ctx = ctx.shape[1]
    n_layers = w_mod.shape[0]
    assert n_tok % TILE == 0 and n_ctx % TILE == 0 and n_tok % GRID_W == 0
    n_ctx_tiles = n_ctx // TILE
    n_lat_tiles = n_tok // TILE

    wg = jnp.swapaxes(ffn_w_gate, 2, 3).astype(BF16)
    wu = jnp.swapaxes(ffn_w_up, 2, 3).astype(BF16)
    wd = ffn_w_down.astype(BF16)
    win = _prep_w_in(w_in)
    wuq = _prep_w_uq(w_uq)
    wukv = _prep_w_ukv(w_ukv)
    wout = w_out.astype(BF16)
    g_pre = norm_pre[:, :, None, :]
    g_post = norm_post[:, :, None, :]
    q_gain = q_norm[:, None, :]
    kv_gain = kv_norm[:, None, :]
    a_gain = attn_out_norm[:, None, :]
    m_gain = mlstm_norm[:, None, :]
    conv_bias = conv_b[:, None, :]
    gate_bias = jnp.pad(gate_b.reshape(n_layers, 1, N_GATES), ((0, 0), (0, 0), (0, LANES - N_GATES)))
    cos4, sin4 = _rope_tables(n_ctx, n_tok)

    c_rows = jnp.zeros((_round_up(bsz + 1, SUBLANES), d), F32).at[:bsz].set(c).at[bsz].set(c_ctx)
    mod = _modulation(c_rows, w_mod, b_mod).reshape(n_layers, -1, N_MOD, d)
    mod_all = jnp.stack([jnp.broadcast_to(mod[:, bsz:bsz + 1], (n_layers, bsz, N_MOD, d)), mod[:, :bsz]], axis=2)

    h = (ctx, x)
    for l in range(n_layers):
        t0 = n_ctx_tiles if l == n_layers - 1 else 0
        ml = mod_all[l]
        h = _ffn(h, ml, g_pre, g_post, wg, wu, wd, layer=l, j=0, sub=0, n_ctx_tiles=n_ctx_tiles)
        q, k, vt, qt_m, k_m, vt_m, o_pre, gates_t = _mixin(
            h, ml, g_pre, win, q_gain, wuq, kv_gain, wukv, cos4, sin4, gate_bias, conv_w, conv_bias,
            layer=l, n_ctx_tiles=n_ctx_tiles)
        a = _attention(q, k, vt, q_tile0=n_ctx_tiles, n_q=n_lat_tiles, n_keys=n_ctx + n_tok,
                       name=f"attention_l{l}")
        if t0 == 0:
            a = (_attention(q, k, vt, q_tile0=0, n_q=n_ctx_tiles, n_keys=n_ctx, name=f"attention_ctx_l{l}"), a)
        h_fwd, h_bwd = _mlstm(k_m, qt_m, vt_m, gates_t, n_ctx_tiles=n_ctx_tiles, layer=l)
        h = _merge_ffn(h, a, h_fwd, h_bwd, o_pre, ml, a_gain, m_gain, g_pre, g_post, wout, wg, wu, wd,
                       layer=l, n_ctx_tiles=n_ctx_tiles, t0=t0)
    return h
```

```python
import functools
import itertools

import jax
import jax.numpy as jnp
import numpy as np
from jax import lax
from jax.experimental import pallas as pl
from jax.experimental.pallas import tpu as pltpu

F32 = jnp.float32
BF16 = jnp.bfloat16

N_MOD = 9
EPS = 1e-6
GRID_W = 64
ATTN_HEADS = 4
Q_LORA = 256
KV_LORA = 128
NOPE_DIM = 128
ROPE_DIM = 64
AXIS_FREQS = ROPE_DIM // 4
V_DIM = 128
QK_DIM = NOPE_DIM + ROPE_DIM
ATTN_WIDTH = ATTN_HEADS * V_DIM
ROPE_BASE = 10000.0
SM_SCALE = QK_DIM ** -0.5
LOG2E = float(np.log2(np.e))
Q_SCALE = SM_SCALE * LOG2E
MLSTM_HEADS = 4
MLSTM_DH = 128
MLSTM_WIDTH = MLSTM_HEADS * MLSTM_DH
CONV_K = 3
N_GATES = 4 * MLSTM_HEADS

LANES = 128
SUBLANES = 8
MXU_DIM = 256
TILE = MXU_DIM
VMEM_LIMIT = 56 * 1024 * 1024

A_W = Q_LORA + KV_LORA + 2 * ROPE_DIM
QK_OFF = A_W
V_OFF = QK_OFF + 2 * MLSTM_WIDTH
O_OFF = V_OFF + MLSTM_WIDTH
G_OFF = O_OFF + MLSTM_WIDTH
IN_W = G_OFF + LANES


def _round_up(n, m):
    return (n + m - 1) // m * m


def _rms(x, g):
    return x * lax.rsqrt(jnp.mean(x * x, axis=-1, keepdims=True) + EPS) * g


def _dot_nt(x, w_t):
    return lax.dot_general(x, w_t, (((1,), (1,)), ((), ())), preferred_element_type=F32)


def _sandwich_in(x, mod, sub, g):
    return _rms(x, g) * (1.0 + mod[3 * sub + 1:3 * sub + 2]) + mod[3 * sub:3 * sub + 1]


def _params(n_grid):
    return pltpu.CompilerParams(dimension_semantics=("arbitrary",) * n_grid,
                                vmem_limit_bytes=VMEM_LIMIT)


def _const_spec(shape, index_map):
    return pl.BlockSpec(shape, index_map, pipeline_mode=pl.Buffered(1))


def _mod_spec(d, tile0, n_ctx_tiles):
    return pl.BlockSpec((1, 1, N_MOD, d),
                        lambda b, t: (b, (t + tile0 >= n_ctx_tiles).astype(jnp.int32), 0, 0))


FFN_TILES_PER_STEP = 4
MERGE_FFN_TILES_PER_STEP = 2


class _TilePlan:
    def __init__(self, bsz, n, t0, n_ctx_tiles, tiles_per_step):
        assert (bsz * n) % tiles_per_step == 0
        self.bsz, self.n, self.t0, self.n_ctx_tiles, self.tps = bsz, n, t0, n_ctx_tiles, tiles_per_step
        self.grid = (bsz * n // tiles_per_step,)

    def coords(self, i, k):
        g = i * self.tps + k
        return g // self.n, g % self.n + self.t0

    def is_ctx(self, k):
        return self.coords(pl.program_id(0), k)[1] < self.n_ctx_tiles

    def tile_spec(self, arr, k, arr_t0=0, clamp=False):
        arr_tiles = arr.shape[0] // self.bsz

        def index(i):
            b, r = self.coords(i, k)
            r = r - arr_t0
            if clamp:
                r = jnp.clip(r, 0, arr_tiles - 1)
            return (b * arr_tiles + r, 0, 0)

        return pl.BlockSpec((1, TILE, arr.shape[2]), index)

    def mod_spec(self, d, k):
        def index(i):
            b, r = self.coords(i, k)
            return (b, (r >= self.n_ctx_tiles).astype(jnp.int32), 0, 0)

        return pl.BlockSpec((1, 1, N_MOD, d), index)


def _as_tiles(x):
    return x.reshape(-1, TILE, x.shape[-1])


def _const1(shape, index):
    return _const_spec(shape, lambda i: index)


def _mod_kernel(c_ref, w_ref, b_ref, o_ref):
    c = c_ref[...]
    h = (c * jax.nn.sigmoid(c)).astype(BF16)
    o_ref[0] = jnp.dot(h, w_ref[0].astype(BF16), preferred_element_type=F32) + b_ref[0]


def _modulation(c_rows, w_mod, b_mod):
    n_layers, d, n_out = w_mod.shape
    rows = c_rows.shape[0]
    n_blk = N_MOD * LANES
    return pl.pallas_call(
        _mod_kernel,
        grid=(n_layers, n_out // n_blk),
        in_specs=[pl.BlockSpec((rows, d), lambda l, j: (0, 0)),
                  pl.BlockSpec((1, d, n_blk), lambda l, j: (l, 0, j)),
                  pl.BlockSpec((1, 1, n_blk), lambda l, j: (l, 0, j))],
        out_specs=pl.BlockSpec((1, rows, n_blk), lambda l, j: (l, 0, j)),
        out_shape=jax.ShapeDtypeStruct((n_layers, rows, n_out), F32),
        compiler_params=_params(2),
        name="modulation",
    )(c_rows, w_mod, b_mod.reshape(n_layers, 1, n_out))


def _ffn_body(xs, mods, sub, g_pre, g_post, wg_ref, wu_ref, wd_ref, a_ref):
    h = jnp.concatenate([_sandwich_in(x, m, sub, g_pre).astype(BF16) for x, m in zip(xs, mods)], axis=0)
    ff = a_ref.shape[1]
    for lo in range(0, ff, MXU_DIM):
        hi = min(lo + MXU_DIM, ff)
        g = _dot_nt(h, wg_ref[lo:hi, :])
        u = _dot_nt(h, wu_ref[lo:hi, :])
        a_ref[:, lo:hi] = (g * jax.nn.sigmoid(g) * u).astype(BF16)
    ff_main = ff // MXU_DIM * MXU_DIM
    y = jnp.dot(a_ref[:, 0:ff_main], wd_ref[0:ff_main, :], preferred_element_type=F32)
    if ff_main < ff:
        y = y + jnp.dot(a_ref[:, ff_main:ff], wd_ref[ff_main:ff, :], preferred_element_type=F32)
    return [x + 0.5 * m[3 * sub + 2:3 * sub + 3] * _rms(y[k * TILE:(k + 1) * TILE], g_post)
            for k, (x, m) in enumerate(zip(xs, mods))]


def _slot_tile(plan, k, refs, split):
    return jnp.where(plan.is_ctx(k), refs[0][0], refs[1][0]) if split else refs[0][0]


def _ffn_kernel(*refs, plan, sub, split):
    per_slot = 3 if split else 2
    slots = [refs[k * per_slot:(k + 1) * per_slot] for k in range(plan.tps)]
    gpre_ref, gpost_ref, wg_ref, wu_ref, wd_ref, o_ref, a_ref = refs[plan.tps * per_slot:]
    xs = [_slot_tile(plan, k, s[:-1], split) for k, s in enumerate(slots)]
    mods = [s[-1][0, 0] for s in slots]
    outs = _ffn_body(xs, mods, sub, gpre_ref[...], gpost_ref[...], wg_ref, wu_ref, wd_ref, a_ref)
    for k, out in enumerate(outs):
        o_ref[k] = out


def _ffn_weight_specs(layer, j, sub, d, ff):
    return [
        pl.BlockSpec((None, None, 1, d), lambda i: (layer, sub, 0, 0)),
        pl.BlockSpec((None, None, 1, d), lambda i: (layer, sub, 0, 0)),
        _const1((None, None, ff, d), (layer, j, 0, 0)),
        _const1((None, None, ff, d), (layer, j, 0, 0)),
        _const1((None, None, ff, d), (layer, j, 0, 0)),
    ]


def _ffn(src, mod_all, g_pre, g_post, wg, wu, wd, *, layer, j, sub, n_ctx_tiles):
    split = isinstance(src, tuple)
    srcs = [_as_tiles(s) for s in (src if split else (src,))]
    bsz, d = mod_all.shape[0], srcs[0].shape[-1]
    n_tiles = sum(s.shape[0] for s in srcs) // bsz
    ff = wd.shape[-2]
    plan = _TilePlan(bsz, n_tiles, 0, n_ctx_tiles, FFN_TILES_PER_STEP)
    slot_specs, slot_args = [], []
    for k in range(plan.tps):
        if split:
            slot_specs += [plan.tile_spec(srcs[0], k, 0, clamp=True), plan.tile_spec(srcs[1], k, n_ctx_tiles, clamp=True)]
        else:
            slot_specs += [plan.tile_spec(srcs[0], k)]
        slot_specs += [plan.mod_spec(d, k)]
        slot_args += srcs + [mod_all]
    out = pl.pallas_call(
        functools.partial(_ffn_kernel, plan=plan, sub=sub, split=split),
        grid=plan.grid,
        in_specs=slot_specs + _ffn_weight_specs(layer, j, sub, d, ff),
        out_specs=pl.BlockSpec((plan.tps, TILE, d), lambda i: (i, 0, 0)),
        out_shape=jax.ShapeDtypeStruct((bsz * n_tiles, TILE, d), F32),
        scratch_shapes=[pltpu.VMEM((plan.tps * TILE, ff), BF16)],
        compiler_params=_params(1),
        name=f"ffn_l{layer}_{j}",
    )(*slot_args, g_pre, g_post, wg, wu, wd)
    return out.reshape(bsz, n_tiles * TILE, d)


GATE_ROWS = 2 * 3 * SUBLANES
MIXIN_ROWS_PER_STEP = 2


def _log_sigmoid(x):
    return jnp.minimum(x, 0.0) - jnp.log1p(jnp.exp(-jnp.abs(x)))


def _lane_scan(x, op, identity, reverse):
    n = x.shape[1]
    lane = lax.broadcasted_iota(jnp.int32, x.shape, 1)
    shift = 1
    while shift < n:
        if reverse:
            moved = jnp.where(lane < n - shift, pltpu.roll(x, n - shift, axis=1), identity)
        else:
            moved = jnp.where(lane >= shift, pltpu.roll(x, shift, axis=1), identity)
        x = op(x, moved)
        shift *= 2
    return x


def _gate_rows(g8, reverse):
    nh = MLSTM_HEADS
    head_rows = lax.broadcasted_iota(jnp.int32, g8.shape, 0) < nh
    b = _lane_scan(_log_sigmoid(g8), jnp.add, 0.0, reverse)
    b = jnp.where(head_rows, pltpu.roll(b, nh, axis=0), 0.0)
    a = jnp.where(head_rows, g8 - b, 0.0)
    return a, b, _lane_scan(a, jnp.maximum, -jnp.inf, reverse)


def _mixin_kernel(x_ref, xp_ref, xn_ref, mod_ref, gpre_ref, win_ref, qn_ref, wuq_ref, kvn_ref, wukv_ref,
                  cos_ref, sin_ref, gb_ref, cw_ref, cb_ref,
                  q_ref, k_ref, vt_ref, qtm_ref, km_ref, vtm_ref, o_ref, gt_ref, *, n_ctx_tiles, n_tiles):
    ci = pl.program_id(1)
    n_rows = x_ref.shape[0]
    mods = [mod_ref[r, 0] for r in range(n_rows)]
    tiles = [slice(r * TILE, (r + 1) * TILE) for r in range(n_rows)]
    h = jnp.concatenate([_sandwich_in(x_ref[r], mods[r], 1, gpre_ref[...]).astype(BF16) for r in range(n_rows)],
                        axis=0)
    cos4 = cos_ref[...]
    sin4 = sin_ref[...]

    gates = jnp.dot(h, win_ref[:, G_OFF:IN_W], preferred_element_type=F32) + gb_ref[...]
    half = N_GATES // 2
    for r in range(n_rows):
        g_t = gates[tiles[r]].T[0:N_GATES, :]
        gt_ref[r] = jnp.concatenate(_gate_rows(g_t[0:half], False) + _gate_rows(g_t[half:N_GATES], True), axis=0)

    pa = jnp.dot(h, win_ref[:, 0:A_W], preferred_element_type=F32)
    cq = pa[:, 0:Q_LORA]
    ckv = pa[:, Q_LORA:Q_LORA + KV_LORA]
    nope_w = ATTN_HEADS * NOPE_DIM
    rope_w = ATTN_HEADS * ROPE_DIM
    qa = jnp.dot(_rms(cq, qn_ref[...]).astype(BF16), wuq_ref[...], preferred_element_type=F32)
    kva = jnp.dot(_rms(ckv, kvn_ref[...]).astype(BF16), wukv_ref[...], preferred_element_type=F32)
    for r in range(n_rows):
        kr = pa[tiles[r], Q_LORA + KV_LORA:Q_LORA + KV_LORA + ROPE_DIM]
        kr_sw = pa[tiles[r], Q_LORA + KV_LORA + ROPE_DIM:A_W]
        k_rope = (kr * cos4[:, 0:ROPE_DIM] + kr_sw * sin4[:, 0:ROPE_DIM]).astype(BF16)
        qa_r = qa[tiles[r]]
        kva_r = kva[tiles[r]]
        q_rope = qa_r[:, nope_w:nope_w + rope_w] * cos4 + qa_r[:, nope_w + rope_w:nope_w + 2 * rope_w] * sin4
        for hd in range(ATTN_HEADS):
            q_ref[r, hd, :, 0:NOPE_DIM] = (qa_r[:, hd * NOPE_DIM:(hd + 1) * NOPE_DIM] * Q_SCALE).astype(BF16)
            q_ref[r, hd, :, NOPE_DIM:QK_DIM] = (q_rope[:, hd * ROPE_DIM:(hd + 1) * ROPE_DIM] * Q_SCALE).astype(BF16)
            k_ref[r, hd, :, 0:NOPE_DIM] = kva_r[:, hd * NOPE_DIM:(hd + 1) * NOPE_DIM].astype(BF16)
            k_ref[r, hd, :, NOPE_DIM:QK_DIM] = k_rope
            vt_ref[r, hd] = kva_r[:, nope_w + hd * V_DIM:nope_w + (hd + 1) * V_DIM].T.astype(BF16)

    vtm = jnp.dot(h, win_ref[:, V_OFF:O_OFF], preferred_element_type=F32)
    o_pre = jnp.dot(h, win_ref[:, O_OFF:G_OFF], preferred_element_type=F32)
    for r in range(n_rows):
        vtm_ref[r] = vtm[tiles[r]].T.astype(BF16)
        o_ref[r] = o_pre[tiles[r]]

    halo = [_sandwich_in(ref[r], mods[r], 1, gpre_ref[...]).astype(BF16) for ref in (xp_ref, xn_ref)
            for r in range(n_rows)]
    pqk = jnp.dot(jnp.concatenate([h] + halo, axis=0), win_ref[:, QK_OFF:V_OFF], preferred_element_type=F32)
    row = lax.broadcasted_iota(jnp.int32, (TILE, 1), 0)
    has_prev = jnp.logical_and(ci != 0, ci != n_ctx_tiles)
    has_next = jnp.logical_and(ci != n_ctx_tiles - 1, ci != n_tiles - 1)
    for r in range(n_rows):
        x = pqk[tiles[r]]
        prev_at = (n_rows * TILE) + r * SUBLANES + SUBLANES - 1
        next_at = (n_rows * TILE) + (n_rows + r) * SUBLANES
        prev_row = jnp.where(has_prev, pqk[prev_at:prev_at + 1], 0.0)
        next_row = jnp.where(has_next, pqk[next_at:next_at + 1], 0.0)
        x_prev = jnp.where(row == 0, prev_row, pltpu.roll(x, 1, axis=0))
        x_next = jnp.where(row == TILE - 1, next_row, pltpu.roll(x, TILE - 1, axis=0))
        u = x_prev * cw_ref[0:1, :] + x * cw_ref[1:2, :] + x_next * cw_ref[2:3, :] + cb_ref[...]
        u = u * jax.nn.sigmoid(u)
        qtm_ref[r] = u[:, 0:MLSTM_WIDTH].T.astype(BF16)
        km_ref[r] = (u[:, MLSTM_WIDTH:2 * MLSTM_WIDTH] * (MLSTM_DH ** -0.5)).astype(BF16)


def _mixin(x, mod_all, g_pre, win, q_norm, wuq, kv_norm, wukv, cos4, sin4, gate_b, conv_w, conv_b,
           *, layer, n_ctx_tiles):
    bsz, t_all, d = x.shape
    n_tiles = t_all // TILE
    rope_w = ATTN_HEADS * ROPE_DIM
    halo_per_tile = TILE // SUBLANES
    n_halo = t_all // SUBLANES
    rps = MIXIN_ROWS_PER_STEP if bsz % MIXIN_ROWS_PER_STEP == 0 else 1
    tile_map = lambda b, t: (b, t, 0)
    head_map = lambda b, t: (b, 0, t, 0)
    time_last_map = lambda b, t: (b, 0, t)
    out_shape = (
        jax.ShapeDtypeStruct((bsz, ATTN_HEADS, t_all, QK_DIM), BF16),
        jax.ShapeDtypeStruct((bsz, ATTN_HEADS, t_all, QK_DIM), BF16),
        jax.ShapeDtypeStruct((bsz, ATTN_HEADS, V_DIM, t_all), BF16),
        jax.ShapeDtypeStruct((bsz, MLSTM_WIDTH, t_all), BF16),
        jax.ShapeDtypeStruct((bsz, t_all, MLSTM_WIDTH), BF16),
        jax.ShapeDtypeStruct((bsz, MLSTM_WIDTH, t_all), BF16),
        jax.ShapeDtypeStruct((bsz, t_all, MLSTM_WIDTH), F32),
        jax.ShapeDtypeStruct((bsz, GATE_ROWS, t_all), F32),
    )
    return pl.pallas_call(
        functools.partial(_mixin_kernel, n_ctx_tiles=n_ctx_tiles, n_tiles=n_tiles),
        grid=(bsz // rps, n_tiles),
        in_specs=[
            pl.BlockSpec((rps, TILE, d), tile_map),
            pl.BlockSpec((rps, SUBLANES, d), lambda b, t: (b, jnp.maximum(t * halo_per_tile - 1, 0), 0)),
            pl.BlockSpec((rps, SUBLANES, d), lambda b, t: (b, jnp.minimum((t + 1) * halo_per_tile, n_halo - 1), 0)),
            pl.BlockSpec((rps, 1, N_MOD, d), lambda b, t: (b, (t >= n_ctx_tiles).astype(jnp.int32), 0, 0)),
            pl.BlockSpec((None, None, 1, d), lambda b, t: (layer, 1, 0, 0)),
            _const_spec((None, d, IN_W), lambda b, t: (layer, 0, 0)),
            pl.BlockSpec((None, 1, Q_LORA), lambda b, t: (layer, 0, 0)),
            _const_spec((None, Q_LORA, wuq.shape[-1]), lambda b, t: (layer, 0, 0)),
            pl.BlockSpec((None, 1, KV_LORA), lambda b, t: (layer, 0, 0)),
            _const_spec((None, KV_LORA, wukv.shape[-1]), lambda b, t: (layer, 0, 0)),
            pl.BlockSpec((TILE, rope_w), lambda b, t: (t, 0)),
            pl.BlockSpec((TILE, rope_w), lambda b, t: (t, 0)),
            pl.BlockSpec((None, 1, LANES), lambda b, t: (layer, 0, 0)),
            pl.BlockSpec((None, CONV_K, 2 * MLSTM_WIDTH), lambda b, t: (layer, 0, 0)),
            pl.BlockSpec((None, 1, 2 * MLSTM_WIDTH), lambda b, t: (layer, 0, 0)),
        ],
        out_specs=(
            pl.BlockSpec((rps, ATTN_HEADS, TILE, QK_DIM), head_map),
            pl.BlockSpec((rps, ATTN_HEADS, TILE, QK_DIM), head_map),
            pl.BlockSpec((rps, ATTN_HEADS, V_DIM, TILE), lambda b, t: (b, 0, 0, t)),
            pl.BlockSpec((rps, MLSTM_WIDTH, TILE), time_last_map),
            pl.BlockSpec((rps, TILE, MLSTM_WIDTH), tile_map),
            pl.BlockSpec((rps, MLSTM_WIDTH, TILE), time_last_map),
            pl.BlockSpec((rps, TILE, MLSTM_WIDTH), tile_map),
            pl.BlockSpec((rps, GATE_ROWS, TILE), time_last_map),
        ),
        out_shape=out_shape,
        compiler_params=_params(2),
        name=f"mixin_l{layer}",
    )(x, x, x, mod_all, g_pre, win, q_norm, wuq, kv_norm, wukv, cos4, sin4, gate_b, conv_w, conv_b)


KEY_CHUNK = 2 * TILE
ATTN_BUFFERS = 2


def _attn_kernel(*refs, chunks):
    q_refs = refs[:-5]
    k_ref, vt_ref, o_ref, s_scr, p_scr = refs[-5:]
    n_heads = k_ref.shape[1]
    units = [(j, h) for j in range(len(q_refs)) for h in range(n_heads)]
    lo, hi = chunks[0][0], chunks[-1][1]
    fold = lambda v: v.reshape(-1, SUBLANES, TILE)
    m8, l8 = {}, {}
    for stage in range(len(units) + 2):
        u_s, u_p, u_v = stage, stage - 1, stage - 2
        do_s, do_p, do_v = u_s < len(units), 0 <= u_p < len(units), 0 <= u_v < len(units)
        if do_v:
            j, h = units[u_v]
            acc = jnp.dot(vt_ref[0, h, :, lo:hi], p_scr[u_v % ATTN_BUFFERS, lo:hi, :], preferred_element_type=F32)
            l = jnp.sum(l8.pop(u_v), axis=0, keepdims=True)
            o_ref[0, j * TILE:(j + 1) * TILE, h * V_DIM:(h + 1) * V_DIM] = (acc / l).T
        if do_s:
            j, h_s = units[u_s]
            q = q_refs[j][0, h_s]
            m8[u_s] = jnp.full((SUBLANES, TILE), -jnp.inf, F32)
        if do_p:
            m = jnp.max(m8.pop(u_p), axis=0, keepdims=True)
            l8[u_p] = jnp.zeros((SUBLANES, TILE), F32)
        for c_lo, c_hi in chunks:
            if do_s:
                s = lax.dot_general(k_ref[0, h_s, c_lo:c_hi, :], q, (((1,), (1,)), ((), ())),
                                    preferred_element_type=F32)
                s_scr[u_s % ATTN_BUFFERS, c_lo:c_hi, :] = s
                m8[u_s] = jnp.maximum(m8[u_s], jnp.max(fold(s), axis=0))
            if do_p:
                p = jnp.exp2(s_scr[u_p % ATTN_BUFFERS, c_lo:c_hi, :] - m)
                l8[u_p] = l8[u_p] + jnp.sum(fold(p), axis=0)
                p_scr[u_p % ATTN_BUFFERS, c_lo:c_hi, :] = p.astype(BF16)


def _key_chunks(n_keys, width):
    return tuple((lo, min(lo + width, n_keys)) for lo in range(0, n_keys, width))


def _attention(q, k, vt, *, q_tile0, n_q, n_keys, name):
    bsz, n_heads, _, _ = q.shape
    qts = max(t for t in (1, 2, 4) if n_q % t == 0)
    q_specs = [pl.BlockSpec((1, n_heads, TILE, QK_DIM), lambda b, i, j=j: (b, 0, i * qts + j + q_tile0, 0))
               for j in range(qts)]
    return pl.pallas_call(
        functools.partial(_attn_kernel, chunks=_key_chunks(n_keys, KEY_CHUNK)),
        grid=(bsz, n_q // qts),
        in_specs=q_specs + [
            pl.BlockSpec((1, n_heads, n_keys, QK_DIM), lambda b, i: (b, 0, 0, 0)),
            pl.BlockSpec((1, n_heads, V_DIM, n_keys), lambda b, i: (b, 0, 0, 0)),
        ],
        out_specs=pl.BlockSpec((1, qts * TILE, n_heads * V_DIM), lambda b, i: (b, i, 0)),
        out_shape=jax.ShapeDtypeStruct((bsz, n_q * TILE, n_heads * V_DIM), F32),
        scratch_shapes=[pltpu.VMEM((ATTN_BUFFERS, n_keys, TILE), F32),
                        pltpu.VMEM((ATTN_BUFFERS, n_keys, TILE), BF16)],
        compiler_params=_params(2),
        name=name,
    )(*([q] * qts), k, vt)


STATE_ROWS = MLSTM_DH + 2 * SUBLANES
MLSTM_ROWS_PER_STEP = 4


def _bwd_chunk(j, n_ctx_tiles, n_tiles):
    return jnp.where(j < n_ctx_tiles, n_ctx_tiles - 1 - j, n_tiles - 1 - (j - n_ctx_tiles))


def _mlstm_kernel(kf_ref, qtf_ref, vtf_ref, gf_ref, kb_ref, qtb_ref, vtb_ref, gb_ref, of_ref, ob_ref,
                  st_scr, m_scr, d_scr, p_scr, r_scr):
    nh = MLSTM_HEADS
    rows_per_step = kf_ref.shape[0]

    @pl.when(pl.program_id(1) == 0)
    def _():
        st_scr[...] = jnp.zeros_like(st_scr)
        m_scr[...] = jnp.zeros_like(m_scr)

    rows = [slice(i * SUBLANES, (i + 1) * SUBLANES) for i in range(GATE_ROWS // SUBLANES)]
    s_idx = lax.broadcasted_iota(jnp.int32, (TILE, TILE), 0)
    t_idx = lax.broadcasted_iota(jnp.int32, (TILE, TILE), 1)
    pad_row = lax.broadcasted_iota(jnp.int32, (2 * SUBLANES, TILE), 0) == 0
    ones_rows = jnp.where(pad_row, 1.0, 0.0).astype(BF16)

    units = []
    scans = ((False, gf_ref, 0, kf_ref, qtf_ref, vtf_ref, of_ref), (True, gb_ref, 3, kb_ref, qtb_ref, vtb_ref, ob_ref))
    for bb, (d, (reverse, g_ref, g0, k_ref, qt_ref, vt_ref, o_ref)) in itertools.product(
            range(rows_per_step), enumerate(scans)):
        scan = bb * len(scans) + d
        a, b, a_max = g_ref[bb, rows[g0], :], g_ref[bb, rows[g0 + 1], :], g_ref[bb, rows[g0 + 2], :]
        last = 0 if reverse else TILE - 1
        m_old = m_scr[scan][:, 0:1]
        mx = jnp.maximum(m_old, a_max)
        inter = jnp.exp(m_old - mx)
        e_inv = jnp.exp(-(b + mx))
        mx_last = mx[:, last:last + 1]
        w = jnp.exp(a - mx_last)
        decay = jnp.exp(m_old - mx_last)
        m_scr[scan] = jnp.broadcast_to(b[:, last:last + 1] + mx_last, (SUBLANES, LANES))
        seen = s_idx >= t_idx if reverse else s_idx <= t_idx
        for hd in range(nh):
            sl = slice(hd * MLSTM_DH, (hd + 1) * MLSTM_DH)
            row = slice(hd, hd + 1)
            units.append(dict(u=scan * nh + hd, bb=bb, sl=sl, seen=seen, a=a[row], mx=mx[row], inter=inter[row],
                              e_inv=e_inv[row], w=w[row], decay=decay[row],
                              k_ref=k_ref, qt_ref=qt_ref, vt_ref=vt_ref, o_ref=o_ref))

    for un in units:
        a_rep = jnp.broadcast_to(un["a"] * LOG2E, (LANES, TILE)).T
        log_d = jnp.concatenate([a_rep] * (TILE // LANES), axis=1) - un["mx"] * LOG2E
        d_scr[un["u"]] = jnp.exp2(jnp.where(un["seen"], log_d, -jnp.inf))

    for un in units:
        u, bb, sl = un["u"], un["bb"], un["sl"]
        lhs = jnp.concatenate([un["k_ref"][bb, :, sl], st_scr[u].astype(BF16)], axis=0)
        r = jnp.dot(lhs, un["qt_ref"][bb, sl, :], preferred_element_type=F32)
        p_scr[u] = (r[0:TILE] * d_scr[u]).astype(BF16)
        r_scr[u] = r[TILE:TILE + STATE_ROWS]

    for un in units:
        u, bb, sl = un["u"], un["bb"], un["sl"]
        nd = jnp.dot(jnp.concatenate([un["vt_ref"][bb, sl, :], ones_rows], axis=0), p_scr[u],
                     preferred_element_type=F32)
        r_state = r_scr[u]
        den = nd[MLSTM_DH:MLSTM_DH + 1] + un["inter"] * r_state[MLSTM_DH:MLSTM_DH + 1]
        scale = 1.0 / jnp.maximum(jnp.abs(den), un["e_inv"])
        h_t = (nd[0:MLSTM_DH] + un["inter"] * r_state[0:MLSTM_DH]) * scale
        un["o_ref"][bb, :, sl] = h_t.T

    for un in units:
        u, bb, sl = un["u"], un["bb"], un["sl"]
        vt_h = un["vt_ref"][bb, sl, :]
        vw = jnp.concatenate([(vt_h.astype(F32) * un["w"]).astype(BF16),
                              jnp.where(pad_row, un["w"], 0.0).astype(BF16)], axis=0)
        st_scr[u] = un["decay"] * st_scr[u] + jnp.dot(vw, un["k_ref"][bb, :, sl], preferred_element_type=F32)


def _mlstm(k_m, qt_m, vt_m, gates_t, *, n_ctx_tiles, layer):
    bsz, t_all, width = k_m.shape
    n_tiles = t_all // TILE
    bps = MLSTM_ROWS_PER_STEP if bsz % MLSTM_ROWS_PER_STEP == 0 else 1
    n_units = bps * 2 * MLSTM_HEADS
    bwd = functools.partial(_bwd_chunk, n_ctx_tiles=n_ctx_tiles, n_tiles=n_tiles)

    def specs(chunk):
        return [pl.BlockSpec((bps, TILE, width), lambda b, j: (b, chunk(j), 0)),
                pl.BlockSpec((bps, width, TILE), lambda b, j: (b, 0, chunk(j))),
                pl.BlockSpec((bps, width, TILE), lambda b, j: (b, 0, chunk(j))),
                pl.BlockSpec((bps, GATE_ROWS, TILE), lambda b, j: (b, 0, chunk(j)))]

    out_sds = jax.ShapeDtypeStruct((bsz, t_all, width), F32)
    return pl.pallas_call(
        _mlstm_kernel,
        grid=(bsz // bps, n_tiles),
        in_specs=specs(lambda j: j) + specs(bwd),
        out_specs=(pl.BlockSpec((bps, TILE, width), lambda b, j: (b, j, 0)),
                   pl.BlockSpec((bps, TILE, width), lambda b, j: (b, bwd(j), 0))),
        out_shape=(out_sds, out_sds),
        scratch_shapes=[pltpu.VMEM((n_units, STATE_ROWS, MLSTM_DH), F32),
                        pltpu.VMEM((n_units // MLSTM_HEADS, SUBLANES, LANES), F32),
                        pltpu.VMEM((n_units, TILE, TILE), F32),
                        pltpu.VMEM((n_units, TILE, TILE), BF16),
                        pltpu.VMEM((n_units, STATE_ROWS, TILE), F32)],
        compiler_params=_params(2),
        name=f"mlstm_l{layer}",
    )(k_m, qt_m, vt_m, gates_t, k_m, qt_m, vt_m, gates_t)


def _merge_body(xs, attn, hms, os, gates, g_attn, g_mlstm, g_post, wout_ref):
    a_n, hm_n = [], []
    for a, hm, o in zip(attn, hms, os):
        hm = hm * jax.nn.sigmoid(o)
        normed = []
        for hd in range(MLSTM_HEADS):
            seg = hm[:, hd * MLSTM_DH:(hd + 1) * MLSTM_DH]
            cen = seg - jnp.mean(seg, axis=-1, keepdims=True)
            normed.append(cen * lax.rsqrt(jnp.mean(cen * cen, axis=-1, keepdims=True) + EPS))
        hm_n.append((jnp.concatenate(normed, axis=-1) * g_mlstm).astype(BF16))
        a_n.append(_rms(a, g_attn).astype(BF16))
    y = jnp.dot(jnp.concatenate(a_n, axis=0), wout_ref[0:ATTN_WIDTH, :], preferred_element_type=F32)
    y = y + jnp.dot(jnp.concatenate(hm_n, axis=0), wout_ref[ATTN_WIDTH:ATTN_WIDTH + MLSTM_WIDTH, :],
                    preferred_element_type=F32)
    return [x + gate * _rms(y[k * TILE:(k + 1) * TILE], g_post) for k, (x, gate) in enumerate(zip(xs, gates))]


def _merge_ffn_kernel(*refs, plan, split):
    per_slot = 7 if split else 6
    slots = [refs[k * per_slot:(k + 1) * per_slot] for k in range(plan.tps)]
    (ga_ref, gm_ref, gpost1_ref, wout_ref, gpre2_ref, gpost2_ref, wg_ref, wu_ref, wd_ref,
     out_ref, a_scr) = refs[plan.tps * per_slot:]
    xs = [s[0][0] for s in slots]
    attn = [_slot_tile(plan, k, s[1:-4], split) for k, s in enumerate(slots)]
    hms = [s[-4][0] + s[-3][0] for s in slots]
    os = [s[-2][0] for s in slots]
    mods = [s[-1][0, 0] for s in slots]
    x1 = _merge_body(xs, attn, hms, os, [m[5:6] for m in mods], ga_ref[...], gm_ref[...], gpost1_ref[...], wout_ref)
    outs = _ffn_body(x1, mods, 2, gpre2_ref[...], gpost2_ref[...], wg_ref, wu_ref, wd_ref, a_scr)
    for k, out in enumerate(outs):
        out_ref[k] = out


def _merge_ffn(x, a, h_fwd, h_bwd, o, mod_all, g_attn, g_mlstm, g_pre, g_post, wout, wg, wu, wd,
               *, layer, n_ctx_tiles, t0):
    bsz, t_all, d = x.shape
    n_tiles = t_all // TILE - t0
    ff = wd.shape[-2]
    split = isinstance(a, tuple)
    a_srcs = [_as_tiles(s) for s in (a if split else (a,))]
    stream = [_as_tiles(s) for s in (h_fwd, h_bwd, o)]
    xt = _as_tiles(x)
    plan = _TilePlan(bsz, n_tiles, t0, n_ctx_tiles, MERGE_FFN_TILES_PER_STEP)
    slot_specs, slot_args = [], []
    for k in range(plan.tps):
        slot_specs += [plan.tile_spec(xt, k)]
        if split:
            slot_specs += [plan.tile_spec(a_srcs[0], k, 0, clamp=True),
                           plan.tile_spec(a_srcs[1], k, n_ctx_tiles, clamp=True)]
        else:
            slot_specs += [plan.tile_spec(a_srcs[0], k, t0)]
        slot_specs += [plan.tile_spec(s, k) for s in stream] + [plan.mod_spec(d, k)]
        slot_args += [xt] + a_srcs + stream + [mod_all]
    out = pl.pallas_call(
        functools.partial(_merge_ffn_kernel, plan=plan, split=split),
        grid=plan.grid,
        in_specs=slot_specs + [
            pl.BlockSpec((None, 1, ATTN_WIDTH), lambda i: (layer, 0, 0)),
            pl.BlockSpec((None, 1, MLSTM_WIDTH), lambda i: (layer, 0, 0)),
            pl.BlockSpec((None, None, 1, d), lambda i: (layer, 1, 0, 0)),
            _const1((None, ATTN_WIDTH + MLSTM_WIDTH, d), (layer, 0, 0)),
        ] + _ffn_weight_specs(layer, 1, 2, d, ff),
        out_specs=pl.BlockSpec((plan.tps, TILE, d), lambda i: (i, 0, 0)),
        out_shape=jax.ShapeDtypeStruct((bsz * n_tiles, TILE, d), F32),
        scratch_shapes=[pltpu.VMEM((plan.tps * TILE, ff), BF16)],
        compiler_params=_params(1),
        name=f"merge_ffn_l{layer}",
    )(*slot_args, g_attn, g_mlstm, g_post, wout, g_pre, g_post, wg, wu, wd)
    return out.reshape(bsz, n_tiles * TILE, d)


def _half_swap_perm():
    idx = np.arange(ROPE_DIM)
    axis, half, freq = idx // (2 * AXIS_FREQS), (idx // AXIS_FREQS) % 2, idx % AXIS_FREQS
    return axis * 2 * AXIS_FREQS + (1 - half) * AXIS_FREQS + freq


def _prep_w_in(w_in):
    sizes = (Q_LORA, KV_LORA, ROPE_DIM, 2 * MLSTM_WIDTH, MLSTM_WIDTH, MLSTM_WIDTH, N_GATES)
    offs = np.concatenate([[0], np.cumsum(sizes)])
    cq, ckv, kr, qk, v, o, g = (w_in[..., offs[i]:offs[i + 1]] for i in range(len(sizes)))
    g = jnp.pad(g, ((0, 0), (0, 0), (0, LANES - N_GATES)))
    return jnp.concatenate([cq, ckv, kr, kr[..., _half_swap_perm()], qk, v, o, g], axis=-1).astype(BF16)


def _prep_w_uq(w_uq):
    n_layers, q_lora, _ = w_uq.shape
    w = w_uq.reshape(n_layers, q_lora, ATTN_HEADS, QK_DIM)
    nope = w[..., :NOPE_DIM].reshape(n_layers, q_lora, -1)
    rope = w[..., NOPE_DIM:]
    rope_sw = rope[..., _half_swap_perm()]
    return jnp.concatenate([nope, rope.reshape(n_layers, q_lora, -1), rope_sw.reshape(n_layers, q_lora, -1)],
                           axis=-1).astype(BF16)


def _prep_w_ukv(w_ukv):
    n_layers, kv_lora, _ = w_ukv.shape
    w = w_ukv.reshape(n_layers, kv_lora, ATTN_HEADS, NOPE_DIM + V_DIM)
    return jnp.concatenate([w[..., :NOPE_DIM].reshape(n_layers, kv_lora, -1),
                            w[..., NOPE_DIM:].reshape(n_layers, kv_lora, -1)], axis=-1).astype(BF16)


def _rope_tables(n_ctx, n_tok):
    rows = n_tok // GRID_W
    t_row = jnp.repeat(jnp.arange(rows), GRID_W).astype(F32)
    t_col = jnp.tile(jnp.arange(GRID_W), rows).astype(F32)
    inv = ROPE_BASE ** (-jnp.arange(AXIS_FREQS, dtype=F32) / AXIS_FREQS)
    ang_r = t_row[:, None] * inv
    ang_c = t_col[:, None] * inv
    cos = jnp.concatenate([jnp.cos(ang_r), jnp.cos(ang_r), jnp.cos(ang_c), jnp.cos(ang_c)], axis=-1)
    sin = jnp.concatenate([-jnp.sin(ang_r), jnp.sin(ang_r), -jnp.sin(ang_c), jnp.sin(ang_c)], axis=-1)
    cos = jnp.concatenate([jnp.ones((n_ctx, ROPE_DIM), F32), cos], axis=0)
    sin = jnp.concatenate([jnp.zeros((n_ctx, ROPE_DIM), F32), sin], axis=0)
    return jnp.tile(cos, (1, ATTN_HEADS)), jnp.tile(sin, (1, ATTN_HEADS))


def kernel(x, c, ctx, c_ctx, w_mod, b_mod, norm_pre, norm_post, ffn_w_gate, ffn_w_up, ffn_w_down,
           w_in, q_norm, w_uq, kv_norm, w_ukv, attn_out_norm, conv_w, conv_b, gate_b, mlstm_norm, w_out):
    bsz, n_tok, d = x.shape
    n_ctx = ctx.shape[1]
    n_layers = w_mod.shape[0]
    assert n_tok % TILE == 0 and n_ctx % TILE == 0 and n_tok % GRID_W == 0
    n_ctx_tiles = n_ctx // TILE
    n_lat_tiles = n_tok // TILE

    wg = jnp.swapaxes(ffn_w_gate, 2, 3).astype(BF16)
    wu = jnp.swapaxes(ffn_w_up, 2, 3).astype(BF16)
    wd = ffn_w_down.astype(BF16)
    win = _prep_w_in(w_in)
    wuq = _prep_w_uq(w_uq)
    wukv = _prep_w_ukv(w_ukv)
    wout = w_out.astype(BF16)
    g_pre = norm_pre[:, :, None, :]
    g_post = norm_post[:, :, None, :]
    q_gain = q_norm[:, None, :]
    kv_gain = kv_norm[:, None, :]
    a_gain = attn_out_norm[:, None, :]
    m_gain = mlstm_norm[:, None, :]
    conv_bias = conv_b[:, None, :]
    gate_bias = jnp.pad(gate_b.reshape(n_layers, 1, N_GATES), ((0, 0), (0, 0), (0, LANES - N_GATES)))
    cos4, sin4 = _rope_tables(n_ctx, n_tok)

    c_rows = jnp.zeros((_round_up(bsz + 1, SUBLANES), d), F32).at[:bsz].set(c).at[bsz].set(c_ctx)
    mod = _modulation(c_rows, w_mod, b_mod).reshape(n_layers, -1, N_MOD, d)
    mod_all = jnp.stack([jnp.broadcast_to(mod[:, bsz:bsz + 1], (n_layers, bsz, N_MOD, d)), mod[:, :bsz]], axis=2)

    h = (ctx, x)
    for l in range(n_layers):
        t0 = n_ctx_tiles if l == n_layers - 1 else 0
        ml = mod_all[l]
        h = _ffn(h, ml, g_pre, g_post, wg, wu, wd, layer=l, j=0, sub=0, n_ctx_tiles=n_ctx_tiles)
        q, k, vt, qt_m, k_m, vt_m, o_pre, gates_t = _mixin(
            h, ml, g_pre, win, q_gain, wuq, kv_gain, wukv, cos4, sin4, gate_bias, conv_w, conv_bias,
            layer=l, n_ctx_tiles=n_ctx_tiles)
        a = _attention(q, k, vt, q_tile0=n_ctx_tiles, n_q=n_lat_tiles, n_keys=n_ctx + n_tok,
                       name=f"attention_l{l}")
        if t0 == 0:
            a = (_attention(q, k, vt, q_tile0=0, n_q=n_ctx_tiles, n_keys=n_ctx, name=f"attention_ctx_l{l}"), a)
        h_fwd, h_bwd = _mlstm(k_m, qt_m, vt_m, gates_t, n_ctx_tiles=n_ctx_tiles, layer=l)
        h = _merge_ffn(h, a, h_fwd, h_bwd, o_pre, ml, a_gain, m_gain, g_pre, g_post, wout, wg, wu, wd,
                       layer=l, n_ctx_tiles=n_ctx_tiles, t0=t0)
    return h
```

```python
import functools
import itertools

import jax
import jax.numpy as jnp
import numpy as np
from jax import lax
from jax.experimental import pallas as pl
from jax.experimental.pallas import tpu as pltpu

F32 = jnp.float32
BF16 = jnp.bfloat16

N_MOD = 9
EPS = 1e-6
GRID_W = 64
ATTN_HEADS = 4
Q_LORA = 256
KV_LORA = 128
NOPE_DIM = 128
ROPE_DIM = 64
AXIS_FREQS = ROPE_DIM // 4
V_DIM = 128
QK_DIM = NOPE_DIM + ROPE_DIM
ATTN_WIDTH = ATTN_HEADS * V_DIM
ROPE_BASE = 10000.0
SM_SCALE = QK_DIM ** -0.5
LOG2E = float(np.log2(np.e))
Q_SCALE = SM_SCALE * LOG2E
MLSTM_HEADS = 4
MLSTM_DH = 128
MLSTM_WIDTH = MLSTM_HEADS * MLSTM_DH
CONV_K = 3
N_GATES = 4 * MLSTM_HEADS

LANES = 128
SUBLANES = 8
MXU_DIM = 256
TILE = MXU_DIM
VMEM_LIMIT = 56 * 1024 * 1024

A_W = Q_LORA + KV_LORA + 2 * ROPE_DIM
QK_OFF = A_W
V_OFF = QK_OFF + 2 * MLSTM_WIDTH
O_OFF = V_OFF + MLSTM_WIDTH
G_OFF = O_OFF + MLSTM_WIDTH
IN_W = G_OFF + LANES


def _round_up(n, m):
    return (n + m - 1) // m * m


def _rms(x, g):
    return x * lax.rsqrt(jnp.mean(x * x, axis=-1, keepdims=True) + EPS) * g


def _dot_nt(x, w_t):
    return lax.dot_general(x, w_t, (((1,), (1,)), ((), ())), preferred_element_type=F32)


def _sandwich_in(x, mod, sub, g):
    return _rms(x, g) * (1.0 + mod[3 * sub + 1:3 * sub + 2]) + mod[3 * sub:3 * sub + 1]


def _params(n_grid):
    return pltpu.CompilerParams(dimension_semantics=("arbitrary",) * n_grid,
                                vmem_limit_bytes=VMEM_LIMIT)


def _const_spec(shape, index_map):
    return pl.BlockSpec(shape, index_map, pipeline_mode=pl.Buffered(1))


def _mod_spec(d, tile0, n_ctx_tiles):
    return pl.BlockSpec((1, 1, N_MOD, d),
                        lambda b, t: (b, (t + tile0 >= n_ctx_tiles).astype(jnp.int32), 0, 0))


FFN_TILES_PER_STEP = 4
MERGE_FFN_TILES_PER_STEP = 2


class _TilePlan:
    def __init__(self, bsz, n, t0, n_ctx_tiles, tiles_per_step):
        assert (bsz * n) % tiles_per_step == 0
        self.bsz, self.n, self.t0, self.n_ctx_tiles, self.tps = bsz, n, t0, n_ctx_tiles, tiles_per_step
        self.grid = (bsz * n // tiles_per_step,)

    def coords(self, i, k):
        g = i * self.tps + k
        return g // self.n, g % self.n + self.t0

    def is_ctx(self, k):
        return self.coords(pl.program_id(0), k)[1] < self.n_ctx_tiles

    def tile_spec(self, arr, k, arr_t0=0, clamp=False):
        arr_tiles = arr.shape[0] // self.bsz

        def index(i):
            b, r = self.coords(i, k)
            r = r - arr_t0
            if clamp:
                r = jnp.clip(r, 0, arr_tiles - 1)
            return (b * arr_tiles + r, 0, 0)

        return pl.BlockSpec((1, TILE, arr.shape[2]), index)

    def mod_spec(self, d, k):
        def index(i):
            b, r = self.coords(i, k)
            return (b, (r >= self.n_ctx_tiles).astype(jnp.int32), 0, 0)

        return pl.BlockSpec((1, 1, N_MOD, d), index)


def _as_tiles(x):
    return x.reshape(-1, TILE, x.shape[-1])


def _const1(shape, index):
    return _const_spec(shape, lambda i: index)


def _mod_kernel(c_ref, w_ref, b_ref, o_ref):
    c = c_ref[...]
    h = (c * jax.nn.sigmoid(c)).astype(BF16)
    o_ref[0] = jnp.dot(h, w_ref[0].astype(BF16), preferred_element_type=F32) + b_ref[0]


def _modulation(c_rows, w_mod, b_mod):
    n_layers, d, n_out = w_mod.shape
    rows = c_rows.shape[0]
    n_blk = N_MOD * LANES
    return pl.pallas_call(
        _mod_kernel,
        grid=(n_layers, n_out // n_blk),
        in_specs=[pl.BlockSpec((rows, d), lambda l, j: (0, 0)),
                  pl.BlockSpec((1, d, n_blk), lambda l, j: (l, 0, j)),
                  pl.BlockSpec((1, 1, n_blk), lambda l, j: (l, 0, j))],
        out_specs=pl.BlockSpec((1, rows, n_blk), lambda l, j: (l, 0, j)),
        out_shape=jax.ShapeDtypeStruct((n_layers, rows, n_out), F32),
        compiler_params=_params(2),
        name="modulation",
    )(c_rows, w_mod, b_mod.reshape(n_layers, 1, n_out))


def _ffn_body(xs, mods, sub, g_pre, g_post, wg_ref, wu_ref, wd_ref, a_ref):
    h = jnp.concatenate([_sandwich_in(x, m, sub, g_pre).astype(BF16) for x, m in zip(xs, mods)], axis=0)
    ff = a_ref.shape[1]
    for lo in range(0, ff, MXU_DIM):
        hi = min(lo + MXU_DIM, ff)
        g = _dot_nt(h, wg_ref[lo:hi, :])
        u = _dot_nt(h, wu_ref[lo:hi, :])
        a_ref[:, lo:hi] = (g * jax.nn.sigmoid(g) * u).astype(BF16)
    ff_main = ff // MXU_DIM * MXU_DIM
    y = jnp.dot(a_ref[:, 0:ff_main], wd_ref[0:ff_main, :], preferred_element_type=F32)
    if ff_main < ff:
        y = y + jnp.dot(a_ref[:, ff_main:ff], wd_ref[ff_main:ff, :], preferred_element_type=F32)
    return [x + 0.5 * m[3 * sub + 2:3 * sub + 3] * _rms(y[k * TILE:(k + 1) * TILE], g_post)
            for k, (x, m) in enumerate(zip(xs, mods))]


def _slot_tile(plan, k, refs, split):
    return jnp.where(plan.is_ctx(k), refs[0][0], refs[1][0]) if split else refs[0][0]


def _ffn_kernel(*refs, plan, sub, split):
    per_slot = 3 if split else 2
    slots = [refs[k * per_slot:(k + 1) * per_slot] for k in range(plan.tps)]
    gpre_ref, gpost_ref, wg_ref, wu_ref, wd_ref, o_ref, a_ref = refs[plan.tps * per_slot:]
    xs = [_slot_tile(plan, k, s[:-1], split) for k, s in enumerate(slots)]
    mods = [s[-1][0, 0] for s in slots]
    outs = _ffn_body(xs, mods, sub, gpre_ref[...], gpost_ref[...], wg_ref, wu_ref, wd_ref, a_ref)
    for k, out in enumerate(outs):
        o_ref[k] = out


def _ffn_weight_specs(layer, j, sub, d, ff):
    return [
        pl.BlockSpec((None, None, 1, d), lambda i: (layer, sub, 0, 0)),
        pl.BlockSpec((None, None, 1, d), lambda i: (layer, sub, 0, 0)),
        _const1((None, None, ff, d), (layer, j, 0, 0)),
        _const1((None, None, ff, d), (layer, j, 0, 0)),
        _const1((None, None, ff, d), (layer, j, 0, 0)),
    ]


def _ffn(src, mod_all, g_pre, g_post, wg, wu, wd, *, layer, j, sub, n_ctx_tiles):
    split = isinstance(src, tuple)
    srcs = [_as_tiles(s) for s in (src if split else (src,))]
    bsz, d = mod_all.shape[0], srcs[0].shape[-1]
    n_tiles = sum(s.shape[0] for s in srcs) // bsz
    ff = wd.shape[-2]
    plan = _TilePlan(bsz, n_tiles, 0, n_ctx_tiles, FFN_TILES_PER_STEP)
    slot_specs, slot_args = [], []
    for k in range(plan.tps):
        if split:
            slot_specs += [plan.tile_spec(srcs[0], k, 0, clamp=True), plan.tile_spec(srcs[1], k, n_ctx_tiles, clamp=True)]
        else:
            slot_specs += [plan.tile_spec(srcs[0], k)]
        slot_specs += [plan.mod_spec(d, k)]
        slot_args += srcs + [mod_all]
    out = pl.pallas_call(
        functools.partial(_ffn_kernel, plan=plan, sub=sub, split=split),
        grid=plan.grid,
        in_specs=slot_specs + _ffn_weight_specs(layer, j, sub, d, ff),
        out_specs=pl.BlockSpec((plan.tps, TILE, d), lambda i: (i, 0, 0)),
        out_shape=jax.ShapeDtypeStruct((bsz * n_tiles, TILE, d), F32),
        scratch_shapes=[pltpu.VMEM((plan.tps * TILE, ff), BF16)],
        compiler_params=_params(1),
        name=f"ffn_l{layer}_{j}",
    )(*slot_args, g_pre, g_post, wg, wu, wd)
    return out.reshape(bsz, n_tiles * TILE, d)


GATE_ROWS = 2 * 3 * SUBLANES
MIXIN_ROWS_PER_STEP = 2


def _log_sigmoid(x):
    return jnp.minimum(x, 0.0) - jnp.log1p(jnp.exp(-jnp.abs(x)))


def _lane_scan(x, op, identity, reverse):
    n = x.shape[1]
    lane = lax.broadcasted_iota(jnp.int32, x.shape, 1)
    shift = 1
    while shift < n:
        if reverse:
            moved = jnp.where(lane < n - shift, pltpu.roll(x, n - shift, axis=1), identity)
        else:
            moved = jnp.where(lane >= shift, pltpu.roll(x, shift, axis=1), identity)
        x = op(x, moved)
        shift *= 2
    return x


def _gate_rows(g8, reverse):
    nh = MLSTM_HEADS
    head_rows = lax.broadcasted_iota(jnp.int32, g8.shape, 0) < nh
    b = _lane_scan(_log_sigmoid(g8), jnp.add, 0.0, reverse)
    b = jnp.where(head_rows, pltpu.roll(b, nh, axis=0), 0.0)
    a = jnp.where(head_rows, g8 - b, 0.0)
    return a, b, _lane_scan(a, jnp.maximum, -jnp.inf, reverse)


def _mixin_kernel(x_ref, xp_ref, xn_ref, mod_ref, gpre_ref, win_ref, qn_ref, wuq_ref, kvn_ref, wukv_ref,
                  cos_ref, sin_ref, gb_ref, cw_ref, cb_ref,
                  q_ref, k_ref, vt_ref, qtm_ref, km_ref, vtm_ref, o_ref, gt_ref, *, n_ctx_tiles, n_tiles):
    ci = pl.program_id(1)
    n_rows = x_ref.shape[0]
    mods = [mod_ref[r, 0] for r in range(n_rows)]
    tiles = [slice(r * TILE, (r + 1) * TILE) for r in range(n_rows)]
    h = jnp.concatenate([_sandwich_in(x_ref[r], mods[r], 1, gpre_ref[...]).astype(BF16) for r in range(n_rows)],
                        axis=0)
    cos4 = cos_ref[...]
    sin4 = sin_ref[...]

    gates = jnp.dot(h, win_ref[:, G_OFF:IN_W], preferred_element_type=F32) + gb_ref[...]
    half = N_GATES // 2
    for r in range(n_rows):
        g_t = gates[tiles[r]].T[0:N_GATES, :]
        gt_ref[r] = jnp.concatenate(_gate_rows(g_t[0:half], False) + _gate_rows(g_t[half:N_GATES], True), axis=0)

    pa = jnp.dot(h, win_ref[:, 0:A_W], preferred_element_type=F32)
    cq = pa[:, 0:Q_LORA]
    ckv = pa[:, Q_LORA:Q_LORA + KV_LORA]
    nope_w = ATTN_HEADS * NOPE_DIM
    rope_w = ATTN_HEADS * ROPE_DIM
    qa = jnp.dot(_rms(cq, qn_ref[...]).astype(BF16), wuq_ref[...], preferred_element_type=F32)
    kva = jnp.dot(_rms(ckv, kvn_ref[...]).astype(BF16), wukv_ref[...], preferred_element_type=F32)
    for r in range(n_rows):
        kr = pa[tiles[r], Q_LORA + KV_LORA:Q_LORA + KV_LORA + ROPE_DIM]
        kr_sw = pa[tiles[r], Q_LORA + KV_LORA + ROPE_DIM:A_W]
        k_rope = (kr * cos4[:, 0:ROPE_DIM] + kr_sw * sin4[:, 0:ROPE_DIM]).astype(BF16)
        qa_r = qa[tiles[r]]
        kva_r = kva[tiles[r]]
        q_rope = qa_r[:, nope_w:nope_w + rope_w] * cos4 + qa_r[:, nope_w + rope_w:nope_w + 2 * rope_w] * sin4
        for hd in range(ATTN_HEADS):
            q_ref[r, hd, :, 0:NOPE_DIM] = (qa_r[:, hd * NOPE_DIM:(hd + 1) * NOPE_DIM] * Q_SCALE).astype(BF16)
            q_ref[r, hd, :, NOPE_DIM:QK_DIM] = (q_rope[:, hd * ROPE_DIM:(hd + 1) * ROPE_DIM] * Q_SCALE).astype(BF16)
            k_ref[r, hd, :, 0:NOPE_DIM] = kva_r[:, hd * NOPE_DIM:(hd + 1) * NOPE_DIM].astype(BF16)
            k_ref[r, hd, :, NOPE_DIM:QK_DIM] = k_rope
            vt_ref[r, hd] = kva_r[:, nope_w + hd * V_DIM:nope_w + (hd + 1) * V_DIM].T.astype(BF16)

    vtm = jnp.dot(h, win_ref[:, V_OFF:O_OFF], preferred_element_type=F32)
    o_pre = jnp.dot(h, win_ref[:, O_OFF:G_OFF], preferred_element_type=F32)
    for r in range(n_rows):
        vtm_ref[r] = vtm[tiles[r]].T.astype(BF16)
        o_ref[r] = o_pre[tiles[r]]

    halo = [_sandwich_in(ref[r], mods[r], 1, gpre_ref[...]).astype(BF16) for ref in (xp_ref, xn_ref)
            for r in range(n_rows)]
    pqk = jnp.dot(jnp.concatenate([h] + halo, axis=0), win_ref[:, QK_OFF:V_OFF], preferred_element_type=F32)
    row = lax.broadcasted_iota(jnp.int32, (TILE, 1), 0)
    has_prev = jnp.logical_and(ci != 0, ci != n_ctx_tiles)
    has_next = jnp.logical_and(ci != n_ctx_tiles - 1, ci != n_tiles - 1)
    for r in range(n_rows):
        x = pqk[tiles[r]]
        prev_at = (n_rows * TILE) + r * SUBLANES + SUBLANES - 1
        next_at = (n_rows * TILE) + (n_rows + r) * SUBLANES
        prev_row = jnp.where(has_prev, pqk[prev_at:prev_at + 1], 0.0)
        next_row = jnp.where(has_next, pqk[next_at:next_at + 1], 0.0)
        x_prev = jnp.where(row == 0, prev_row, pltpu.roll(x, 1, axis=0))
        x_next = jnp.where(row == TILE - 1, next_row, pltpu.roll(x, TILE - 1, axis=0))
        u = x_prev * cw_ref[0:1, :] + x * cw_ref[1:2, :] + x_next * cw_ref[2:3, :] + cb_ref[...]
        u = u * jax.nn.sigmoid(u)
        qtm_ref[r] = u[:, 0:MLSTM_WIDTH].T.astype(BF16)
        km_ref[r] = (u[:, MLSTM_WIDTH:2 * MLSTM_WIDTH] * (MLSTM_DH ** -0.5)).astype(BF16)


def _mixin(x, mod_all, g_pre, win, q_norm, wuq, kv_norm, wukv, cos4, sin4, gate_b, conv_w, conv_b,
           *, layer, n_ctx_tiles):
    bsz, t_all, d = x.shape
    n_tiles = t_all // TILE
    rope_w = ATTN_HEADS * ROPE_DIM
    halo_per_tile = TILE // SUBLANES
    n_halo = t_all // SUBLANES
    rps = MIXIN_ROWS_PER_STEP if bsz % MIXIN_ROWS_PER_STEP == 0 else 1
    tile_map = lambda b, t: (b, t, 0)
    head_map = lambda b, t: (b, 0, t, 0)
    time_last_map = lambda b, t: (b, 0, t)
    out_shape = (
        jax.ShapeDtypeStruct((bsz, ATTN_HEADS, t_all, QK_DIM), BF16),
        jax.ShapeDtypeStruct((bsz, ATTN_HEADS, t_all, QK_DIM), BF16),
        jax.ShapeDtypeStruct((bsz, ATTN_HEADS, V_DIM, t_all), BF16),
        jax.ShapeDtypeStruct((bsz, MLSTM_WIDTH, t_all), BF16),
        jax.ShapeDtypeStruct((bsz, t_all, MLSTM_WIDTH), BF16),
        jax.ShapeDtypeStruct((bsz, MLSTM_WIDTH, t_all), BF16),
        jax.ShapeDtypeStruct((bsz, t_all, MLSTM_WIDTH), F32),
        jax.ShapeDtypeStruct((bsz, GATE_ROWS, t_all), F32),
    )
    return pl.pallas_call(
        functools.partial(_mixin_kernel, n_ctx_tiles=n_ctx_tiles, n_tiles=n_tiles),
        grid=(bsz // rps, n_tiles),
        in_specs=[
            pl.BlockSpec((rps, TILE, d), tile_map),
            pl.BlockSpec((rps, SUBLANES, d), lambda b, t: (b, jnp.maximum(t * halo_per_tile - 1, 0), 0)),
            pl.BlockSpec((rps, SUBLANES, d), lambda b, t: (b, jnp.minimum((t + 1) * halo_per_tile, n_halo - 1), 0)),
            pl.BlockSpec((rps, 1, N_MOD, d), lambda b, t: (b, (t >= n_ctx_tiles).astype(jnp.int32), 0, 0)),
            pl.BlockSpec((None, None, 1, d), lambda b, t: (layer, 1, 0, 0)),
            _const_spec((None, d, IN_W), lambda b, t: (layer, 0, 0)),
            pl.BlockSpec((None, 1, Q_LORA), lambda b, t: (layer, 0, 0)),
            _const_spec((None, Q_LORA, wuq.shape[-1]), lambda b, t: (layer, 0, 0)),
            pl.BlockSpec((None, 1, KV_LORA), lambda b, t: (layer, 0, 0)),
            _const_spec((None, KV_LORA, wukv.shape[-1]), lambda b, t: (layer, 0, 0)),
            pl.BlockSpec((TILE, rope_w), lambda b, t: (t, 0)),
            pl.BlockSpec((TILE, rope_w), lambda b, t: (t, 0)),
            pl.BlockSpec((None, 1, LANES), lambda b, t: (layer, 0, 0)),
            pl.BlockSpec((None, CONV_K, 2 * MLSTM_WIDTH), lambda b, t: (layer, 0, 0)),
            pl.BlockSpec((None, 1, 2 * MLSTM_WIDTH), lambda b, t: (layer, 0, 0)),
        ],
        out_specs=(
            pl.BlockSpec((rps, ATTN_HEADS, TILE, QK_DIM), head_map),
            pl.BlockSpec((rps, ATTN_HEADS, TILE, QK_DIM), head_map),
            pl.BlockSpec((rps, ATTN_HEADS, V_DIM, TILE), lambda b, t: (b, 0, 0, t)),
            pl.BlockSpec((rps, MLSTM_WIDTH, TILE), time_last_map),
            pl.BlockSpec((rps, TILE, MLSTM_WIDTH), tile_map),
            pl.BlockSpec((rps, MLSTM_WIDTH, TILE), time_last_map),
            pl.BlockSpec((rps, TILE, MLSTM_WIDTH), tile_map),
            pl.BlockSpec((rps, GATE_ROWS, TILE), time_last_map),
        ),
        out_shape=out_shape,
        compiler_params=_params(2),
        name=f"mixin_l{layer}",
    )(x, x, x, mod_all, g_pre, win, q_norm, wuq, kv_norm, wukv, cos4, sin4, gate_b, conv_w, conv_b)


KEY_CHUNK = 2 * TILE
ATTN_BUFFERS = 2


def _attn_kernel(*refs, chunks):
    q_refs = refs[:-5]
    k_ref, vt_ref, o_ref, s_scr, p_scr = refs[-5:]
    n_heads = k_ref.shape[1]
    units = [(j, h) for j in range(len(q_refs)) for h in range(n_heads)]
    lo, hi = chunks[0][0], chunks[-1][1]
    fold = lambda v: v.reshape(-1, SUBLANES, TILE)
    m8, l8 = {}, {}
    for stage in range(len(units) + 2):
        u_s, u_p, u_v = stage, stage - 1, stage - 2
        do_s, do_p, do_v = u_s < len(units), 0 <= u_p < len(units), 0 <= u_v < len(units)
        if do_v:
            j, h = units[u_v]
            acc = jnp.dot(vt_ref[0, h, :, lo:hi], p_scr[u_v % ATTN_BUFFERS, lo:hi, :], preferred_element_type=F32)
            l = jnp.sum(l8.pop(u_v), axis=0, keepdims=True)
            o_ref[0, j * TILE:(j + 1) * TILE, h * V_DIM:(h + 1) * V_DIM] = (acc / l).T
        if do_s:
            j, h_s = units[u_s]
            q = q_refs[j][0, h_s]
            m8[u_s] = jnp.full((SUBLANES, TILE), -jnp.inf, F32)
        if do_p:
            m = jnp.max(m8.pop(u_p), axis=0, keepdims=True)
            l8[u_p] = jnp.zeros((SUBLANES, TILE), F32)
        for c_lo, c_hi in chunks:
            if do_s:
                s = lax.dot_general(k_ref[0, h_s, c_lo:c_hi, :], q, (((1,), (1,)), ((), ())),
                                    preferred_element_type=F32)
                s_scr[u_s % ATTN_BUFFERS, c_lo:c_hi, :] = s
                m8[u_s] = jnp.maximum(m8[u_s], jnp.max(fold(s), axis=0))
            if do_p:
                p = jnp.exp2(s_scr[u_p % ATTN_BUFFERS, c_lo:c_hi, :] - m)
                l8[u_p] = l8[u_p] + jnp.sum(fold(p), axis=0)
                p_scr[u_p % ATTN_BUFFERS, c_lo:c_hi, :] = p.astype(BF16)


def _key_chunks(n_keys, width):
    return tuple((lo, min(lo + width, n_keys)) for lo in range(0, n_keys, width))


def _attention(q, k, vt, *, q_tile0, n_q, n_keys, name):
    bsz, n_heads, _, _ = q.shape
    qts = max(t for t in (1, 2, 4, 8) if n_q % t == 0)
    q_specs = [pl.BlockSpec((1, n_heads, TILE, QK_DIM), lambda b, i, j=j: (b, 0, i * qts + j + q_tile0, 0))
               for j in range(qts)]
    return pl.pallas_call(
        functools.partial(_attn_kernel, chunks=_key_chunks(n_keys, KEY_CHUNK)),
        grid=(bsz, n_q // qts),
        in_specs=q_specs + [
            pl.BlockSpec((1, n_heads, n_keys, QK_DIM), lambda b, i: (b, 0, 0, 0)),
            pl.BlockSpec((1, n_heads, V_DIM, n_keys), lambda b, i: (b, 0, 0, 0)),
        ],
        out_specs=pl.BlockSpec((1, qts * TILE, n_heads * V_DIM), lambda b, i: (b, i, 0)),
        out_shape=jax.ShapeDtypeStruct((bsz, n_q * TILE, n_heads * V_DIM), F32),
        scratch_shapes=[pltpu.VMEM((ATTN_BUFFERS, n_keys, TILE), F32),
                        pltpu.VMEM((ATTN_BUFFERS, n_keys, TILE), BF16)],
        compiler_params=_params(2),
        name=name,
    )(*([q] * qts), k, vt)


STATE_ROWS = MLSTM_DH + 2 * SUBLANES
MLSTM_ROWS_PER_STEP = 4


def _bwd_chunk(j, n_ctx_tiles, n_tiles):
    return jnp.where(j < n_ctx_tiles, n_ctx_tiles - 1 - j, n_tiles - 1 - (j - n_ctx_tiles))


def _mlstm_kernel(kf_ref, qtf_ref, vtf_ref, gf_ref, kb_ref, qtb_ref, vtb_ref, gb_ref, of_ref, ob_ref,
                  st_scr, m_scr, d_scr, p_scr, r_scr):
    nh = MLSTM_HEADS
    rows_per_step = kf_ref.shape[0]

    @pl.when(pl.program_id(1) == 0)
    def _():
        st_scr[...] = jnp.zeros_like(st_scr)
        m_scr[...] = jnp.zeros_like(m_scr)

    rows = [slice(i * SUBLANES, (i + 1) * SUBLANES) for i in range(GATE_ROWS // SUBLANES)]
    s_idx = lax.broadcasted_iota(jnp.int32, (TILE, TILE), 0)
    t_idx = lax.broadcasted_iota(jnp.int32, (TILE, TILE), 1)
    pad_row = lax.broadcasted_iota(jnp.int32, (2 * SUBLANES, TILE), 0) == 0
    ones_rows = jnp.where(pad_row, 1.0, 0.0).astype(BF16)

    units = []
    scans = ((False, gf_ref, 0, kf_ref, qtf_ref, vtf_ref, of_ref), (True, gb_ref, 3, kb_ref, qtb_ref, vtb_ref, ob_ref))
    for bb, (d, (reverse, g_ref, g0, k_ref, qt_ref, vt_ref, o_ref)) in itertools.product(
            range(rows_per_step), enumerate(scans)):
        scan = bb * len(scans) + d
        a, b, a_max = g_ref[bb, rows[g0], :], g_ref[bb, rows[g0 + 1], :], g_ref[bb, rows[g0 + 2], :]
        last = 0 if reverse else TILE - 1
        m_old = m_scr[scan][:, 0:1]
        mx = jnp.maximum(m_old, a_max)
        inter = jnp.exp(m_old - mx)
        e_inv = jnp.exp(-(b + mx))
        mx_last = mx[:, last:last + 1]
        w = jnp.exp(a - mx_last)
        decay = jnp.exp(m_old - mx_last)
        m_scr[scan] = jnp.broadcast_to(b[:, last:last + 1] + mx_last, (SUBLANES, LANES))
        seen = s_idx >= t_idx if reverse else s_idx <= t_idx
        for hd in range(nh):
            sl = slice(hd * MLSTM_DH, (hd + 1) * MLSTM_DH)
            row = slice(hd, hd + 1)
            units.append(dict(u=scan * nh + hd, bb=bb, sl=sl, seen=seen, a=a[row], mx=mx[row], inter=inter[row],
                              e_inv=e_inv[row], w=w[row], decay=decay[row],
                              k_ref=k_ref, qt_ref=qt_ref, vt_ref=vt_ref, o_ref=o_ref))

    for un in units:
        a_rep = jnp.broadcast_to(un["a"] * LOG2E, (LANES, TILE)).T
        log_d = jnp.concatenate([a_rep] * (TILE // LANES), axis=1) - un["mx"] * LOG2E
        d_scr[un["u"]] = jnp.exp2(jnp.where(un["seen"], log_d, -jnp.inf))

    for un in units:
        u, bb, sl = un["u"], un["bb"], un["sl"]
        lhs = jnp.concatenate([un["k_ref"][bb, :, sl], st_scr[u].astype(BF16)], axis=0)
        r = jnp.dot(lhs, un["qt_ref"][bb, sl, :], preferred_element_type=F32)
        p_scr[u] = (r[0:TILE] * d_scr[u]).astype(BF16)
        r_scr[u] = r[TILE:TILE + STATE_ROWS]

    for un in units:
        u, bb, sl = un["u"], un["bb"], un["sl"]
        nd = jnp.dot(jnp.concatenate([un["vt_ref"][bb, sl, :], ones_rows], axis=0), p_scr[u],
                     preferred_element_type=F32)
        r_state = r_scr[u]
        den = nd[MLSTM_DH:MLSTM_DH + 1] + un["inter"] * r_state[MLSTM_DH:MLSTM_DH + 1]
        scale = 1.0 / jnp.maximum(jnp.abs(den), un["e_inv"])
        h_t = (nd[0:MLSTM_DH] + un["inter"] * r_state[0:MLSTM_DH]) * scale
        un["o_ref"][bb, :, sl] = h_t.T

    for un in units:
        u, bb, sl = un["u"], un["bb"], un["sl"]
        vt_h = un["vt_ref"][bb, sl, :]
        vw = jnp.concatenate([(vt_h.astype(F32) * un["w"]).astype(BF16),
                              jnp.where(pad_row, un["w"], 0.0).astype(BF16)], axis=0)
        st_scr[u] = un["decay"] * st_scr[u] + jnp.dot(vw, un["k_ref"][bb, :, sl], preferred_element_type=F32)


def _mlstm(k_m, qt_m, vt_m, gates_t, *, n_ctx_tiles, layer):
    bsz, t_all, width = k_m.shape
    n_tiles = t_all // TILE
    bps = MLSTM_ROWS_PER_STEP if bsz % MLSTM_ROWS_PER_STEP == 0 else 1
    n_units = bps * 2 * MLSTM_HEADS
    bwd = functools.partial(_bwd_chunk, n_ctx_tiles=n_ctx_tiles, n_tiles=n_tiles)

    def specs(chunk):
        return [pl.BlockSpec((bps, TILE, width), lambda b, j: (b, chunk(j), 0)),
                pl.BlockSpec((bps, width, TILE), lambda b, j: (b, 0, chunk(j))),
                pl.BlockSpec((bps, width, TILE), lambda b, j: (b, 0, chunk(j))),
                pl.BlockSpec((bps, GATE_ROWS, TILE), lambda b, j: (b, 0, chunk(j)))]

    out_sds = jax.ShapeDtypeStruct((bsz, t_all, width), F32)
    return pl.pallas_call(
        _mlstm_kernel,
        grid=(bsz // bps, n_tiles),
        in_specs=specs(lambda j: j) + specs(bwd),
        out_specs=(pl.BlockSpec((bps, TILE, width), lambda b, j: (b, j, 0)),
                   pl.BlockSpec((bps, TILE, width), lambda b, j: (b, bwd(j), 0))),
        out_shape=(out_sds, out_sds),
        scratch_shapes=[pltpu.VMEM((n_units, STATE_ROWS, MLSTM_DH), F32),
                        pltpu.VMEM((n_units // MLSTM_HEADS, SUBLANES, LANES), F32),
                        pltpu.VMEM((n_units, TILE, TILE), F32),
                        pltpu.VMEM((n_units, TILE, TILE), BF16),
                        pltpu.VMEM((n_units, STATE_ROWS, TILE), F32)],
        compiler_params=_params(2),
        name=f"mlstm_l{layer}",
    )(k_m, qt_m, vt_m, gates_t, k_m, qt_m, vt_m, gates_t)


def _merge_body(xs, attn, hms, os, gates, g_attn, g_mlstm, g_post, wout_ref):
    a_n, hm_n = [], []
    for a, hm, o in zip(attn, hms, os):
        hm = hm * jax.nn.sigmoid(o)
        normed = []
        for hd in range(MLSTM_HEADS):
            seg = hm[:, hd * MLSTM_DH:(hd + 1) * MLSTM_DH]
            cen = seg - jnp.mean(seg, axis=-1, keepdims=True)
            normed.append(cen * lax.rsqrt(jnp.mean(cen * cen, axis=-1, keepdims=True) + EPS))
        hm_n.append((jnp.concatenate(normed, axis=-1) * g_mlstm).astype(BF16))
        a_n.append(_rms(a, g_attn).astype(BF16))
    y = jnp.dot(jnp.concatenate(a_n, axis=0), wout_ref[0:ATTN_WIDTH, :], preferred_element_type=F32)
    y = y + jnp.dot(jnp.concatenate(hm_n, axis=0), wout_ref[ATTN_WIDTH:ATTN_WIDTH + MLSTM_WIDTH, :],
                    preferred_element_type=F32)
    return [x + gate * _rms(y[k * TILE:(k + 1) * TILE], g_post) for k, (x, gate) in enumerate(zip(xs, gates))]


def _merge_ffn_kernel(*refs, plan, split):
    per_slot = 7 if split else 6
    slots = [refs[k * per_slot:(k + 1) * per_slot] for k in range(plan.tps)]
    (ga_ref, gm_ref, gpost1_ref, wout_ref, gpre2_ref, gpost2_ref, wg_ref, wu_ref, wd_ref,
     out_ref, a_scr) = refs[plan.tps * per_slot:]
    xs = [s[0][0] for s in slots]
    attn = [_slot_tile(plan, k, s[1:-4], split) for k, s in enumerate(slots)]
    hms = [s[-4][0] + s[-3][0] for s in slots]
    os = [s[-2][0] for s in slots]
    mods = [s[-1][0, 0] for s in slots]
    x1 = _merge_body(xs, attn, hms, os, [m[5:6] for m in mods], ga_ref[...], gm_ref[...], gpost1_ref[...], wout_ref)
    outs = _ffn_body(x1, mods, 2, gpre2_ref[...], gpost2_ref[...], wg_ref, wu_ref, wd_ref, a_scr)
    for k, out in enumerate(outs):
        out_ref[k] = out


def _merge_ffn(x, a, h_fwd, h_bwd, o, mod_all, g_attn, g_mlstm, g_pre, g_post, wout, wg, wu, wd,
               *, layer, n_ctx_tiles, t0):
    bsz, t_all, d = x.shape
    n_tiles = t_all // TILE - t0
    ff = wd.shape[-2]
    split = isinstance(a, tuple)
    a_srcs = [_as_tiles(s) for s in (a if split else (a,))]
    stream = [_as_tiles(s) for s in (h_fwd, h_bwd, o)]
    xt = _as_tiles(x)
    plan = _TilePlan(bsz, n_tiles, t0, n_ctx_tiles, MERGE_FFN_TILES_PER_STEP)
    slot_specs, slot_args = [], []
    for k in range(plan.tps):
        slot_specs += [plan.tile_spec(xt, k)]
        if split:
            slot_specs += [plan.tile_spec(a_srcs[0], k, 0, clamp=True),
                           plan.tile_spec(a_srcs[1], k, n_ctx_tiles, clamp=True)]
        else:
            slot_specs += [plan.tile_spec(a_srcs[0], k, t0)]
        slot_specs += [plan.tile_spec(s, k) for s in stream] + [plan.mod_spec(d, k)]
        slot_args += [xt] + a_srcs + stream + [mod_all]
    out = pl.pallas_call(
        functools.partial(_merge_ffn_kernel, plan=plan, split=split),
        grid=plan.grid,
        in_specs=slot_specs + [
            pl.BlockSpec((None, 1, ATTN_WIDTH), lambda i: (layer, 0, 0)),
            pl.BlockSpec((None, 1, MLSTM_WIDTH), lambda i: (layer, 0, 0)),
            pl.BlockSpec((None, None, 1, d), lambda i: (layer, 1, 0, 0)),
            _const1((None, ATTN_WIDTH + MLSTM_WIDTH, d), (layer, 0, 0)),
        ] + _ffn_weight_specs(layer, 1, 2, d, ff),
        out_specs=pl.BlockSpec((plan.tps, TILE, d), lambda i: (i, 0, 0)),
        out_shape=jax.ShapeDtypeStruct((bsz * n_tiles, TILE, d), F32),
        scratch_shapes=[pltpu.VMEM((plan.tps * TILE, ff), BF16)],
        compiler_params=_params(1),
        name=f"merge_ffn_l{layer}",
    )(*slot_args, g_attn, g_mlstm, g_post, wout, g_pre, g_post, wg, wu, wd)
    return out.reshape(bsz, n_tiles * TILE, d)


def _half_swap_perm():
    idx = np.arange(ROPE_DIM)
    axis, half, freq = idx // (2 * AXIS_FREQS), (idx // AXIS_FREQS) % 2, idx % AXIS_FREQS
    return axis * 2 * AXIS_FREQS + (1 - half) * AXIS_FREQS + freq


def _prep_w_in(w_in):
    sizes = (Q_LORA, KV_LORA, ROPE_DIM, 2 * MLSTM_WIDTH, MLSTM_WIDTH, MLSTM_WIDTH, N_GATES)
    offs = np.concatenate([[0], np.cumsum(sizes)])
    cq, ckv, kr, qk, v, o, g = (w_in[..., offs[i]:offs[i + 1]] for i in range(len(sizes)))
    g = jnp.pad(g, ((0, 0), (0, 0), (0, LANES - N_GATES)))
    return jnp.concatenate([cq, ckv, kr, kr[..., _half_swap_perm()], qk, v, o, g], axis=-1).astype(BF16)


def _prep_w_uq(w_uq):
    n_layers, q_lora, _ = w_uq.shape
    w = w_uq.reshape(n_layers, q_lora, ATTN_HEADS, QK_DIM)
    nope = w[..., :NOPE_DIM].reshape(n_layers, q_lora, -1)
    rope = w[..., NOPE_DIM:]
    rope_sw = rope[..., _half_swap_perm()]
    return jnp.concatenate([nope, rope.reshape(n_layers, q_lora, -1), rope_sw.reshape(n_layers, q_lora, -1)],
                           axis=-1).astype(BF16)


def _prep_w_ukv(w_ukv):
    n_layers, kv_lora, _ = w_ukv.shape
    w = w_ukv.reshape(n_layers, kv_lora, ATTN_HEADS, NOPE_DIM + V_DIM)
    return jnp.concatenate([w[..., :NOPE_DIM].reshape(n_layers, kv_lora, -1),
                            w[..., NOPE_DIM:].reshape(n_layers, kv_lora, -1)], axis=-1).astype(BF16)


def _rope_tables(n_ctx, n_tok):
    rows = n_tok // GRID_W
    t_row = jnp.repeat(jnp.arange(rows), GRID_W).astype(F32)
    t_col = jnp.tile(jnp.arange(GRID_W), rows).astype(F32)
    inv = ROPE_BASE ** (-jnp.arange(AXIS_FREQS, dtype=F32) / AXIS_FREQS)
    ang_r = t_row[:, None] * inv
    ang_c = t_col[:, None] * inv
    cos = jnp.concatenate([jnp.cos(ang_r), jnp.cos(ang_r), jnp.cos(ang_c), jnp.cos(ang_c)], axis=-1)
    sin = jnp.concatenate([-jnp.sin(ang_r), jnp.sin(ang_r), -jnp.sin(ang_c), jnp.sin(ang_c)], axis=-1)
    cos = jnp.concatenate([jnp.ones((n_ctx, ROPE_DIM), F32), cos], axis=0)
    sin = jnp.concatenate([jnp.zeros((n_ctx, ROPE_DIM), F32), sin], axis=0)
    return jnp.tile(cos, (1, ATTN_HEADS)), jnp.tile(sin, (1, ATTN_HEADS))


def kernel(x, c, ctx, c_ctx, w_mod, b_mod, norm_pre, norm_post, ffn_w_gate, ffn_w_up, ffn_w_down,
           w_in, q_norm, w_uq, kv_norm, w_ukv, attn_out_norm, conv_w, conv_b, gate_b, mlstm_norm, w_out):
    bsz, n_tok, d = x.shape
    n_ctx = ctx.shape[1]
    n_layers = w_mod.shape[0]
    assert n_tok % TILE == 0 and n_ctx % TILE == 0 and n_tok % GRID_W == 0
    n_ctx_tiles = n_ctx // TILE
    n_lat_tiles = n_tok // TILE

    wg = jnp.swapaxes(ffn_w_gate, 2, 3).astype(BF16)
    wu = jnp.swapaxes(ffn_w_up, 2, 3).astype(BF16)
    wd = ffn_w_down.astype(BF16)
    win = _prep_w_in(w_in)
    wuq = _prep_w_uq(w_uq)
    wukv = _prep_w_ukv(w_ukv)
    wout = w_out.astype(BF16)
    g_pre = norm_pre[:, :, None, :]
    g_post = norm_post[:, :, None, :]
    q_gain = q_norm[:, None, :]
    kv_gain = kv_norm[:, None, :]
    a_gain = attn_out_norm[:, None, :]
    m_gain = mlstm_norm[:, None, :]
    conv_bias = conv_b[:, None, :]
    gate_bias = jnp.pad(gate_b.reshape(n_layers, 1, N_GATES), ((0, 0), (0, 0), (0, LANES - N_GATES)))
    cos4, sin4 = _rope_tables(n_ctx, n_tok)

    c_rows = jnp.zeros((_round_up(bsz + 1, SUBLANES), d), F32).at[:bsz].set(c).at[bsz].set(c_ctx)
    mod = _modulation(c_rows, w_mod, b_mod).reshape(n_layers, -1, N_MOD, d)
    mod_all = jnp.stack([jnp.broadcast_to(mod[:, bsz:bsz + 1], (n_layers, bsz, N_MOD, d)), mod[:, :bsz]], axis=2)

    h = (ctx, x)
    for l in range(n_layers):
        t0 = n_ctx_tiles if l == n_layers - 1 else 0
        ml = mod_all[l]
        h = _ffn(h, ml, g_pre, g_post, wg, wu, wd, layer=l, j=0, sub=0, n_ctx_tiles=n_ctx_tiles)
        q, k, vt, qt_m, k_m, vt_m, o_pre, gates_t = _mixin(
            h, ml, g_pre, win, q_gain, wuq, kv_gain, wukv, cos4, sin4, gate_bias, conv_w, conv_bias,
            layer=l, n_ctx_tiles=n_ctx_tiles)
        a = _attention(q, k, vt, q_tile0=n_ctx_tiles, n_q=n_lat_tiles, n_keys=n_ctx + n_tok,
                       name=f"attention_l{l}")
        if t0 == 0:
            a = (_attention(q, k, vt, q_tile0=0, n_q=n_ctx_tiles, n_keys=n_ctx, name=f"attention_ctx_l{l}"), a)
        h_fwd, h_bwd = _mlstm(k_m, qt_m, vt_m, gates_t, n_ctx_tiles=n_ctx_tiles, layer=l)
        h = _merge_ffn(h, a, h_fwd, h_bwd, o_pre, ml, a_gain, m_gain, g_pre, g_post, wout, wg, wu, wd,
                       layer=l, n_ctx_tiles=n_ctx_tiles, t0=t0)
    return h
```

```python
import functools
import itertools

import jax
import jax.numpy as jnp
import numpy as np
from jax import lax
from jax.experimental import pallas as pl
from jax.experimental.pallas import tpu as pltpu

F32 = jnp.float32
BF16 = jnp.bfloat16

N_MOD = 9
EPS = 1e-6
GRID_W = 64
ATTN_HEADS = 4
Q_LORA = 256
KV_LORA = 128
NOPE_DIM = 128
ROPE_DIM = 64
AXIS_FREQS = ROPE_DIM // 4
V_DIM = 128
QK_DIM = NOPE_DIM + ROPE_DIM
ATTN_WIDTH = ATTN_HEADS * V_DIM
ROPE_BASE = 10000.0
SM_SCALE = QK_DIM ** -0.5
LOG2E = float(np.log2(np.e))
Q_SCALE = SM_SCALE * LOG2E
MLSTM_HEADS = 4
MLSTM_DH = 128
MLSTM_WIDTH = MLSTM_HEADS * MLSTM_DH
CONV_K = 3
N_GATES = 4 * MLSTM_HEADS

LANES = 128
SUBLANES = 8
MXU_DIM = 256
TILE = MXU_DIM
VMEM_LIMIT = 56 * 1024 * 1024

A_W = Q_LORA + KV_LORA + 2 * ROPE_DIM
QK_OFF = A_W
V_OFF = QK_OFF + 2 * MLSTM_WIDTH
O_OFF = V_OFF + MLSTM_WIDTH
G_OFF = O_OFF + MLSTM_WIDTH
IN_W = G_OFF + LANES


def _round_up(n, m):
    return (n + m - 1) // m * m


def _rms(x, g):
    return x * lax.rsqrt(jnp.mean(x * x, axis=-1, keepdims=True) + EPS) * g


def _dot_nt(x, w_t):
    return lax.dot_general(x, w_t, (((1,), (1,)), ((), ())), preferred_element_type=F32)


def _sandwich_in(x, mod, sub, g):
    return _rms(x, g) * (1.0 + mod[3 * sub + 1:3 * sub + 2]) + mod[3 * sub:3 * sub + 1]


def _params(n_grid):
    return pltpu.CompilerParams(dimension_semantics=("arbitrary",) * n_grid,
                                vmem_limit_bytes=VMEM_LIMIT)


def _const_spec(shape, index_map):
    return pl.BlockSpec(shape, index_map, pipeline_mode=pl.Buffered(1))


FFN_TILES_PER_STEP = 4
MERGE_FFN_TILES_PER_STEP = 2


class _TilePlan:
    def __init__(self, bsz, n, t0, n_ctx_tiles, tiles_per_step):
        assert (bsz * n) % tiles_per_step == 0
        self.bsz, self.n, self.t0, self.n_ctx_tiles, self.tps = bsz, n, t0, n_ctx_tiles, tiles_per_step
        self.grid = (bsz * n // tiles_per_step,)

    def coords(self, i, k):
        g = i * self.tps + k
        return g // self.n, g % self.n + self.t0

    def is_ctx(self, k):
        return self.coords(pl.program_id(0), k)[1] < self.n_ctx_tiles

    def tile_spec(self, arr, k, arr_t0=0, clamp=False):
        arr_tiles = arr.shape[0] // self.bsz

        def index(i):
            b, r = self.coords(i, k)
            r = r - arr_t0
            if clamp:
                r = jnp.clip(r, 0, arr_tiles - 1)
            return (b * arr_tiles + r, 0, 0)

        return pl.BlockSpec((1, TILE, arr.shape[2]), index)

    def mod_spec(self, d, k):
        def index(i):
            b, r = self.coords(i, k)
            return (b, (r >= self.n_ctx_tiles).astype(jnp.int32), 0, 0)

        return pl.BlockSpec((1, 1, N_MOD, d), index)


def _as_tiles(x):
    return x.reshape(-1, TILE, x.shape[-1])


def _const1(shape, index):
    return _const_spec(shape, lambda i: index)


def _mod_kernel(c_ref, w_ref, b_ref, o_ref):
    c = c_ref[...]
    h = (c * jax.nn.sigmoid(c)).astype(BF16)
    o_ref[0] = jnp.dot(h, w_ref[0].astype(BF16), preferred_element_type=F32) + b_ref[0]


def _modulation(c_rows, w_mod, b_mod):
    n_layers, d, n_out = w_mod.shape
    rows = c_rows.shape[0]
    n_blk = 2 * N_MOD * LANES
    return pl.pallas_call(
        _mod_kernel,
        grid=(n_layers, n_out // n_blk),
        in_specs=[pl.BlockSpec((rows, d), lambda l, j: (0, 0)),
                  pl.BlockSpec((1, d, n_blk), lambda l, j: (l, 0, j)),
                  pl.BlockSpec((1, 1, n_blk), lambda l, j: (l, 0, j))],
        out_specs=pl.BlockSpec((1, rows, n_blk), lambda l, j: (l, 0, j)),
        out_shape=jax.ShapeDtypeStruct((n_layers, rows, n_out), F32),
        compiler_params=_params(2),
        name="modulation",
    )(c_rows, w_mod, b_mod.reshape(n_layers, 1, n_out))


def _ffn_body(xs, mods, sub, g_pre, g_post, wg_ref, wu_ref, wd_ref, a_ref):
    h = jnp.concatenate([_sandwich_in(x, m, sub, g_pre).astype(BF16) for x, m in zip(xs, mods)], axis=0)
    ff = a_ref.shape[1]
    for lo in range(0, ff, MXU_DIM):
        hi = min(lo + MXU_DIM, ff)
        g = _dot_nt(h, wg_ref[lo:hi, :])
        u = _dot_nt(h, wu_ref[lo:hi, :])
        a_ref[:, lo:hi] = (g * jax.nn.sigmoid(g) * u).astype(BF16)
    ff_main = ff // MXU_DIM * MXU_DIM
    y = jnp.dot(a_ref[:, 0:ff_main], wd_ref[0:ff_main, :], preferred_element_type=F32)
    if ff_main < ff:
        y = y + jnp.dot(a_ref[:, ff_main:ff], wd_ref[ff_main:ff, :], preferred_element_type=F32)
    return [x + 0.5 * m[3 * sub + 2:3 * sub + 3] * _rms(y[k * TILE:(k + 1) * TILE], g_post)
            for k, (x, m) in enumerate(zip(xs, mods))]


def _slot_tile(plan, k, refs, split):
    return jnp.where(plan.is_ctx(k), refs[0][0], refs[1][0]) if split else refs[0][0]


def _ffn_kernel(*refs, plan, sub, split):
    per_slot = 3 if split else 2
    slots = [refs[k * per_slot:(k + 1) * per_slot] for k in range(plan.tps)]
    gpre_ref, gpost_ref, wg_ref, wu_ref, wd_ref, o_ref, a_ref = refs[plan.tps * per_slot:]
    xs = [_slot_tile(plan, k, s[:-1], split) for k, s in enumerate(slots)]
    mods = [s[-1][0, 0] for s in slots]
    outs = _ffn_body(xs, mods, sub, gpre_ref[...], gpost_ref[...], wg_ref, wu_ref, wd_ref, a_ref)
    for k, out in enumerate(outs):
        o_ref[k] = out


def _ffn_weight_specs(layer, j, sub, d, ff):
    return [
        pl.BlockSpec((None, None, 1, d), lambda i: (layer, sub, 0, 0)),
        pl.BlockSpec((None, None, 1, d), lambda i: (layer, sub, 0, 0)),
        _const1((None, None, ff, d), (layer, j, 0, 0)),
        _const1((None, None, ff, d), (layer, j, 0, 0)),
        _const1((None, None, ff, d), (layer, j, 0, 0)),
    ]


def _ffn(src, mod_all, g_pre, g_post, wg, wu, wd, *, layer, j, sub, n_ctx_tiles):
    split = isinstance(src, tuple)
    srcs = [_as_tiles(s) for s in (src if split else (src,))]
    bsz, d = mod_all.shape[0], srcs[0].shape[-1]
    n_tiles = sum(s.shape[0] for s in srcs) // bsz
    ff = wd.shape[-2]
    plan = _TilePlan(bsz, n_tiles, 0, n_ctx_tiles, FFN_TILES_PER_STEP)
    slot_specs, slot_args = [], []
    for k in range(plan.tps):
        if split:
            slot_specs += [plan.tile_spec(srcs[0], k, 0, clamp=True), plan.tile_spec(srcs[1], k, n_ctx_tiles, clamp=True)]
        else:
            slot_specs += [plan.tile_spec(srcs[0], k)]
        slot_specs += [plan.mod_spec(d, k)]
        slot_args += srcs + [mod_all]
    out = pl.pallas_call(
        functools.partial(_ffn_kernel, plan=plan, sub=sub, split=split),
        grid=plan.grid,
        in_specs=slot_specs + _ffn_weight_specs(layer, j, sub, d, ff),
        out_specs=pl.BlockSpec((plan.tps, TILE, d), lambda i: (i, 0, 0)),
        out_shape=jax.ShapeDtypeStruct((bsz * n_tiles, TILE, d), F32),
        scratch_shapes=[pltpu.VMEM((plan.tps * TILE, ff), BF16)],
        compiler_params=_params(1),
        name=f"ffn_l{layer}_{j}",
    )(*slot_args, g_pre, g_post, wg, wu, wd)
    return out.reshape(bsz, n_tiles * TILE, d)


GATE_ROWS = 2 * 3 * SUBLANES
MIXIN_ROWS_PER_STEP = 2


def _log_sigmoid(x):
    return jnp.minimum(x, 0.0) - jnp.log1p(jnp.exp(-jnp.abs(x)))


def _lane_scan(x, op, identity, reverse):
    n = x.shape[1]
    lane = lax.broadcasted_iota(jnp.int32, x.shape, 1)
    shift = 1
    while shift < n:
        if reverse:
            moved = jnp.where(lane < n - shift, pltpu.roll(x, n - shift, axis=1), identity)
        else:
            moved = jnp.where(lane >= shift, pltpu.roll(x, shift, axis=1), identity)
        x = op(x, moved)
        shift *= 2
    return x


def _gate_rows(g8, reverse):
    nh = MLSTM_HEADS
    head_rows = lax.broadcasted_iota(jnp.int32, g8.shape, 0) < nh
    b = _lane_scan(_log_sigmoid(g8), jnp.add, 0.0, reverse)
    b = jnp.where(head_rows, pltpu.roll(b, nh, axis=0), 0.0)
    a = jnp.where(head_rows, g8 - b, 0.0)
    return a, b, _lane_scan(a, jnp.maximum, -jnp.inf, reverse)


def _mixin_kernel(x_ref, xp_ref, xn_ref, mod_ref, gpre_ref, win_ref, qn_ref, wuq_ref, kvn_ref, wukv_ref,
                  cos_ref, sin_ref, gb_ref, cw_ref, cb_ref,
                  q_ref, k_ref, vt_ref, qtm_ref, km_ref, vtm_ref, o_ref, gt_ref, *, n_ctx_tiles, n_tiles):
    ci = pl.program_id(1)
    n_rows = x_ref.shape[0]
    mods = [mod_ref[r, 0] for r in range(n_rows)]
    tiles = [slice(r * TILE, (r + 1) * TILE) for r in range(n_rows)]
    h = jnp.concatenate([_sandwich_in(x_ref[r], mods[r], 1, gpre_ref[...]).astype(BF16) for r in range(n_rows)],
                        axis=0)
    cos4 = cos_ref[...]
    sin4 = sin_ref[...]

    gates = jnp.dot(h, win_ref[:, G_OFF:IN_W], preferred_element_type=F32) + gb_ref[...]
    half = N_GATES // 2
    for r in range(n_rows):
        g_t = gates[tiles[r]].T[0:N_GATES, :]
        gt_ref[r] = jnp.concatenate(_gate_rows(g_t[0:half], False) + _gate_rows(g_t[half:N_GATES], True), axis=0)

    pa = jnp.dot(h, win_ref[:, 0:A_W], preferred_element_type=F32)
    cq = pa[:, 0:Q_LORA]
    ckv = pa[:, Q_LORA:Q_LORA + KV_LORA]
    nope_w = ATTN_HEADS * NOPE_DIM
    rope_w = ATTN_HEADS * ROPE_DIM
    qa = jnp.dot(_rms(cq, qn_ref[...]).astype(BF16), wuq_ref[...], preferred_element_type=F32)
    kva = jnp.dot(_rms(ckv, kvn_ref[...]).astype(BF16), wukv_ref[...], preferred_element_type=F32)
    for r in range(n_rows):
        kr = pa[tiles[r], Q_LORA + KV_LORA:Q_LORA + KV_LORA + ROPE_DIM]
        kr_sw = pa[tiles[r], Q_LORA + KV_LORA + ROPE_DIM:A_W]
        k_rope = (kr * cos4[:, 0:ROPE_DIM] + kr_sw * sin4[:, 0:ROPE_DIM]).astype(BF16)
        qa_r = qa[tiles[r]]
        kva_r = kva[tiles[r]]
        q_rope = qa_r[:, nope_w:nope_w + rope_w] * cos4 + qa_r[:, nope_w + rope_w:nope_w + 2 * rope_w] * sin4
        for hd in range(ATTN_HEADS):
            q_ref[r, hd, :, 0:NOPE_DIM] = (qa_r[:, hd * NOPE_DIM:(hd + 1) * NOPE_DIM] * Q_SCALE).astype(BF16)
            q_ref[r, hd, :, NOPE_DIM:QK_DIM] = (q_rope[:, hd * ROPE_DIM:(hd + 1) * ROPE_DIM] * Q_SCALE).astype(BF16)
            k_ref[r, hd, :, 0:NOPE_DIM] = kva_r[:, hd * NOPE_DIM:(hd + 1) * NOPE_DIM].astype(BF16)
            k_ref[r, hd, :, NOPE_DIM:QK_DIM] = k_rope
            vt_ref[r, hd] = kva_r[:, nope_w + hd * V_DIM:nope_w + (hd + 1) * V_DIM].T.astype(BF16)

    vtm = jnp.dot(h, win_ref[:, V_OFF:O_OFF], preferred_element_type=F32)
    o_pre = jnp.dot(h, win_ref[:, O_OFF:G_OFF], preferred_element_type=F32)
    for r in range(n_rows):
        vtm_ref[r] = vtm[tiles[r]].T.astype(BF16)
        o_ref[r] = o_pre[tiles[r]]

    halo = [_sandwich_in(ref[r], mods[r], 1, gpre_ref[...]).astype(BF16) for ref in (xp_ref, xn_ref)
            for r in range(n_rows)]
    pqk = jnp.dot(jnp.concatenate([h] + halo, axis=0), win_ref[:, QK_OFF:V_OFF], preferred_element_type=F32)
    row = lax.broadcasted_iota(jnp.int32, (TILE, 1), 0)
    has_prev = jnp.logical_and(ci != 0, ci != n_ctx_tiles)
    has_next = jnp.logical_and(ci != n_ctx_tiles - 1, ci != n_tiles - 1)
    for r in range(n_rows):
        x = pqk[tiles[r]]
        prev_at = (n_rows * TILE) + r * SUBLANES + SUBLANES - 1
        next_at = (n_rows * TILE) + (n_rows + r) * SUBLANES
        prev_row = jnp.where(has_prev, pqk[prev_at:prev_at + 1], 0.0)
        next_row = jnp.where(has_next, pqk[next_at:next_at + 1], 0.0)
        x_prev = jnp.where(row == 0, prev_row, pltpu.roll(x, 1, axis=0))
        x_next = jnp.where(row == TILE - 1, next_row, pltpu.roll(x, TILE - 1, axis=0))
        u = x_prev * cw_ref[0:1, :] + x * cw_ref[1:2, :] + x_next * cw_ref[2:3, :] + cb_ref[...]
        u = u * jax.nn.sigmoid(u)
        qtm_ref[r] = u[:, 0:MLSTM_WIDTH].T.astype(BF16)
        km_ref[r] = (u[:, MLSTM_WIDTH:2 * MLSTM_WIDTH] * (MLSTM_DH ** -0.5)).astype(BF16)


def _mixin(x, mod_all, g_pre, win, q_norm, wuq, kv_norm, wukv, cos4, sin4, gate_b, conv_w, conv_b,
           *, layer, n_ctx_tiles):
    bsz, t_all, d = x.shape
    n_tiles = t_all // TILE
    rope_w = ATTN_HEADS * ROPE_DIM
    halo_per_tile = TILE // SUBLANES
    n_halo = t_all // SUBLANES
    rps = MIXIN_ROWS_PER_STEP if bsz % MIXIN_ROWS_PER_STEP == 0 else 1
    tile_map = lambda b, t: (b, t, 0)
    head_map = lambda b, t: (b, 0, t, 0)
    time_last_map = lambda b, t: (b, 0, t)
    out_shape = (
        jax.ShapeDtypeStruct((bsz, ATTN_HEADS, t_all, QK_DIM), BF16),
        jax.ShapeDtypeStruct((bsz, ATTN_HEADS, t_all, QK_DIM), BF16),
        jax.ShapeDtypeStruct((bsz, ATTN_HEADS, V_DIM, t_all), BF16),
        jax.ShapeDtypeStruct((bsz, MLSTM_WIDTH, t_all), BF16),
        jax.ShapeDtypeStruct((bsz, t_all, MLSTM_WIDTH), BF16),
        jax.ShapeDtypeStruct((bsz, MLSTM_WIDTH, t_all), BF16),
        jax.ShapeDtypeStruct((bsz, t_all, MLSTM_WIDTH), F32),
        jax.ShapeDtypeStruct((bsz, GATE_ROWS, t_all), F32),
    )
    return pl.pallas_call(
        functools.partial(_mixin_kernel, n_ctx_tiles=n_ctx_tiles, n_tiles=n_tiles),
        grid=(bsz // rps, n_tiles),
        in_specs=[
            pl.BlockSpec((rps, TILE, d), tile_map),
            pl.BlockSpec((rps, SUBLANES, d), lambda b, t: (b, jnp.maximum(t * halo_per_tile - 1, 0), 0)),
            pl.BlockSpec((rps, SUBLANES, d), lambda b, t: (b, jnp.minimum((t + 1) * halo_per_tile, n_halo - 1), 0)),
            pl.BlockSpec((rps, 1, N_MOD, d), lambda b, t: (b, (t >= n_ctx_tiles).astype(jnp.int32), 0, 0)),
            pl.BlockSpec((None, None, 1, d), lambda b, t: (layer, 1, 0, 0)),
            _const_spec((None, d, IN_W), lambda b, t: (layer, 0, 0)),
            pl.BlockSpec((None, 1, Q_LORA), lambda b, t: (layer, 0, 0)),
            _const_spec((None, Q_LORA, wuq.shape[-1]), lambda b, t: (layer, 0, 0)),
            pl.BlockSpec((None, 1, KV_LORA), lambda b, t: (layer, 0, 0)),
            _const_spec((None, KV_LORA, wukv.shape[-1]), lambda b, t: (layer, 0, 0)),
            pl.BlockSpec((TILE, rope_w), lambda b, t: (t, 0)),
            pl.BlockSpec((TILE, rope_w), lambda b, t: (t, 0)),
            pl.BlockSpec((None, 1, LANES), lambda b, t: (layer, 0, 0)),
            pl.BlockSpec((None, CONV_K, 2 * MLSTM_WIDTH), lambda b, t: (layer, 0, 0)),
            pl.BlockSpec((None, 1, 2 * MLSTM_WIDTH), lambda b, t: (layer, 0, 0)),
        ],
        out_specs=(
            pl.BlockSpec((rps, ATTN_HEADS, TILE, QK_DIM), head_map),
            pl.BlockSpec((rps, ATTN_HEADS, TILE, QK_DIM), head_map),
            pl.BlockSpec((rps, ATTN_HEADS, V_DIM, TILE), lambda b, t: (b, 0, 0, t)),
            pl.BlockSpec((rps, MLSTM_WIDTH, TILE), time_last_map),
            pl.BlockSpec((rps, TILE, MLSTM_WIDTH), tile_map),
            pl.BlockSpec((rps, MLSTM_WIDTH, TILE), time_last_map),
            pl.BlockSpec((rps, TILE, MLSTM_WIDTH), tile_map),
            pl.BlockSpec((rps, GATE_ROWS, TILE), time_last_map),
        ),
        out_shape=out_shape,
        compiler_params=_params(2),
        name=f"mixin_l{layer}",
    )(x, x, x, mod_all, g_pre, win, q_norm, wuq, kv_norm, wukv, cos4, sin4, gate_b, conv_w, conv_b)


KEY_CHUNK = 2 * TILE
ATTN_BUFFERS = 2


def _attn_kernel(*refs, chunks):
    q_refs = refs[:-5]
    k_ref, vt_ref, o_ref, s_scr, p_scr = refs[-5:]
    n_heads = k_ref.shape[1]
    units = [(j, h) for j in range(len(q_refs)) for h in range(n_heads)]
    lo, hi = chunks[0][0], chunks[-1][1]
    fold = lambda v: v.reshape(-1, SUBLANES, TILE)
    m8, l8 = {}, {}
    for stage in range(len(units) + 2):
        u_s, u_p, u_v = stage, stage - 1, stage - 2
        do_s, do_p, do_v = u_s < len(units), 0 <= u_p < len(units), 0 <= u_v < len(units)
        if do_v:
            j, h = units[u_v]
            acc = jnp.dot(vt_ref[0, h, :, lo:hi], p_scr[u_v % ATTN_BUFFERS, lo:hi, :], preferred_element_type=F32)
            l = jnp.sum(l8.pop(u_v), axis=0, keepdims=True)
            o_ref[0, j * TILE:(j + 1) * TILE, h * V_DIM:(h + 1) * V_DIM] = (acc / l).T
        if do_s:
            j, h_s = units[u_s]
            q = q_refs[j][0, h_s]
            m8[u_s] = jnp.full((SUBLANES, TILE), -jnp.inf, F32)
        if do_p:
            m = jnp.max(m8.pop(u_p), axis=0, keepdims=True)
            l8[u_p] = jnp.zeros((SUBLANES, TILE), F32)
        for c_lo, c_hi in chunks:
            if do_s:
                s = lax.dot_general(k_ref[0, h_s, c_lo:c_hi, :], q, (((1,), (1,)), ((), ())),
                                    preferred_element_type=F32)
                s_scr[u_s % ATTN_BUFFERS, c_lo:c_hi, :] = s
                m8[u_s] = jnp.maximum(m8[u_s], jnp.max(fold(s), axis=0))
            if do_p:
                p = jnp.exp2(s_scr[u_p % ATTN_BUFFERS, c_lo:c_hi, :] - m)
                l8[u_p] = l8[u_p] + jnp.sum(fold(p), axis=0)
                p_scr[u_p % ATTN_BUFFERS, c_lo:c_hi, :] = p.astype(BF16)


def _key_chunks(n_keys, width):
    return tuple((lo, min(lo + width, n_keys)) for lo in range(0, n_keys, width))


def _attention(q, k, vt, *, q_tile0, n_q, n_keys, name):
    bsz, n_heads, _, _ = q.shape
    qts = max(t for t in (1, 2, 4) if n_q % t == 0)
    q_specs = [pl.BlockSpec((1, n_heads, TILE, QK_DIM), lambda b, i, j=j: (b, 0, i * qts + j + q_tile0, 0))
               for j in range(qts)]
    return pl.pallas_call(
        functools.partial(_attn_kernel, chunks=_key_chunks(n_keys, KEY_CHUNK)),
        grid=(bsz, n_q // qts),
        in_specs=q_specs + [
            pl.BlockSpec((1, n_heads, n_keys, QK_DIM), lambda b, i: (b, 0, 0, 0)),
            pl.BlockSpec((1, n_heads, V_DIM, n_keys), lambda b, i: (b, 0, 0, 0)),
        ],
        out_specs=pl.BlockSpec((1, qts * TILE, n_heads * V_DIM), lambda b, i: (b, i, 0)),
        out_shape=jax.ShapeDtypeStruct((bsz, n_q * TILE, n_heads * V_DIM), F32),
        scratch_shapes=[pltpu.VMEM((ATTN_BUFFERS, n_keys, TILE), F32),
                        pltpu.VMEM((ATTN_BUFFERS, n_keys, TILE), BF16)],
        compiler_params=_params(2),
        name=name,
    )(*([q] * qts), k, vt)


STATE_ROWS = MLSTM_DH + 2 * SUBLANES
MLSTM_ROWS_PER_STEP = 4


def _bwd_chunk(j, n_ctx_tiles, n_tiles):
    return jnp.where(j < n_ctx_tiles, n_ctx_tiles - 1 - j, n_tiles - 1 - (j - n_ctx_tiles))


def _mlstm_kernel(kf_ref, qtf_ref, vtf_ref, gf_ref, kb_ref, qtb_ref, vtb_ref, gb_ref, of_ref, ob_ref,
                  st_scr, m_scr, d_scr, p_scr, r_scr):
    nh = MLSTM_HEADS
    rows_per_step = kf_ref.shape[0]

    @pl.when(pl.program_id(1) == 0)
    def _():
        st_scr[...] = jnp.zeros_like(st_scr)
        m_scr[...] = jnp.zeros_like(m_scr)

    rows = [slice(i * SUBLANES, (i + 1) * SUBLANES) for i in range(GATE_ROWS // SUBLANES)]
    s_idx = lax.broadcasted_iota(jnp.int32, (TILE, TILE), 0)
    t_idx = lax.broadcasted_iota(jnp.int32, (TILE, TILE), 1)
    pad_row = lax.broadcasted_iota(jnp.int32, (2 * SUBLANES, TILE), 0) == 0
    ones_rows = jnp.where(pad_row, 1.0, 0.0).astype(BF16)

    units = []
    scans = ((False, gf_ref, 0, kf_ref, qtf_ref, vtf_ref, of_ref), (True, gb_ref, 3, kb_ref, qtb_ref, vtb_ref, ob_ref))
    for bb, (d, (reverse, g_ref, g0, k_ref, qt_ref, vt_ref, o_ref)) in itertools.product(
            range(rows_per_step), enumerate(scans)):
        scan = bb * len(scans) + d
        a, b, a_max = g_ref[bb, rows[g0], :], g_ref[bb, rows[g0 + 1], :], g_ref[bb, rows[g0 + 2], :]
        last = 0 if reverse else TILE - 1
        m_old = m_scr[scan][:, 0:1]
        mx = jnp.maximum(m_old, a_max)
        inter = jnp.exp(m_old - mx)
        e_inv = jnp.exp(-(b + mx))
        mx_last = mx[:, last:last + 1]
        w = jnp.exp(a - mx_last)
        decay = jnp.exp(m_old - mx_last)
        m_scr[scan] = jnp.broadcast_to(b[:, last:last + 1] + mx_last, (SUBLANES, LANES))
        seen = s_idx >= t_idx if reverse else s_idx <= t_idx
        for hd in range(nh):
            sl = slice(hd * MLSTM_DH, (hd + 1) * MLSTM_DH)
            row = slice(hd, hd + 1)
            units.append(dict(u=scan * nh + hd, bb=bb, sl=sl, seen=seen, a=a[row], mx=mx[row], inter=inter[row],
                              e_inv=e_inv[row], w=w[row], decay=decay[row],
                              k_ref=k_ref, qt_ref=qt_ref, vt_ref=vt_ref, o_ref=o_ref))

    for un in units:
        a_rep = jnp.broadcast_to(un["a"] * LOG2E, (LANES, TILE)).T
        log_d = jnp.concatenate([a_rep] * (TILE // LANES), axis=1) - un["mx"] * LOG2E
        d_scr[un["u"]] = jnp.exp2(jnp.where(un["seen"], log_d, -jnp.inf))

    for un in units:
        u, bb, sl = un["u"], un["bb"], un["sl"]
        lhs = jnp.concatenate([un["k_ref"][bb, :, sl], st_scr[u].astype(BF16)], axis=0)
        r = jnp.dot(lhs, un["qt_ref"][bb, sl, :], preferred_element_type=F32)
        p_scr[u] = (r[0:TILE] * d_scr[u]).astype(BF16)
        r_scr[u] = r[TILE:TILE + STATE_ROWS]

    for un in units:
        u, bb, sl = un["u"], un["bb"], un["sl"]
        nd = jnp.dot(jnp.concatenate([un["vt_ref"][bb, sl, :], ones_rows], axis=0), p_scr[u],
                     preferred_element_type=F32)
        r_state = r_scr[u]
        den = nd[MLSTM_DH:MLSTM_DH + 1] + un["inter"] * r_state[MLSTM_DH:MLSTM_DH + 1]
        scale = 1.0 / jnp.maximum(jnp.abs(den), un["e_inv"])
        h_t = (nd[0:MLSTM_DH] + un["inter"] * r_state[0:MLSTM_DH]) * scale
        un["o_ref"][bb, :, sl] = h_t.T

    for un in units:
        u, bb, sl = un["u"], un["bb"], un["sl"]
        vt_h = un["vt_ref"][bb, sl, :]
        vw = jnp.concatenate([(vt_h.astype(F32) * un["w"]).astype(BF16),
                              jnp.where(pad_row, un["w"], 0.0).astype(BF16)], axis=0)
        st_scr[u] = un["decay"] * st_scr[u] + jnp.dot(vw, un["k_ref"][bb, :, sl], preferred_element_type=F32)


def _mlstm(k_m, qt_m, vt_m, gates_t, *, n_ctx_tiles, layer):
    bsz, t_all, width = k_m.shape
    n_tiles = t_all // TILE
    bps = MLSTM_ROWS_PER_STEP if bsz % MLSTM_ROWS_PER_STEP == 0 else 1
    n_units = bps * 2 * MLSTM_HEADS
    bwd = functools.partial(_bwd_chunk, n_ctx_tiles=n_ctx_tiles, n_tiles=n_tiles)

    def specs(chunk):
        return [pl.BlockSpec((bps, TILE, width), lambda b, j: (b, chunk(j), 0)),
                pl.BlockSpec((bps, width, TILE), lambda b, j: (b, 0, chunk(j))),
                pl.BlockSpec((bps, width, TILE), lambda b, j: (b, 0, chunk(j))),
                pl.BlockSpec((bps, GATE_ROWS, TILE), lambda b, j: (b, 0, chunk(j)))]

    out_sds = jax.ShapeDtypeStruct((bsz, t_all, width), F32)
    return pl.pallas_call(
        _mlstm_kernel,
        grid=(bsz // bps, n_tiles),
        in_specs=specs(lambda j: j) + specs(bwd),
        out_specs=(pl.BlockSpec((bps, TILE, width), lambda b, j: (b, j, 0)),
                   pl.BlockSpec((bps, TILE, width), lambda b, j: (b, bwd(j), 0))),
        out_shape=(out_sds, out_sds),
        scratch_shapes=[pltpu.VMEM((n_units, STATE_ROWS, MLSTM_DH), F32),
                        pltpu.VMEM((n_units // MLSTM_HEADS, SUBLANES, LANES), F32),
                        pltpu.VMEM((n_units, TILE, TILE), F32),
                        pltpu.VMEM((n_units, TILE, TILE), BF16),
                        pltpu.VMEM((n_units, STATE_ROWS, TILE), F32)],
        compiler_params=_params(2),
        name=f"mlstm_l{layer}",
    )(k_m, qt_m, vt_m, gates_t, k_m, qt_m, vt_m, gates_t)


def _merge_body(xs, attn, hms, os, gates, g_attn, g_mlstm, g_post, wout_ref):
    a_n, hm_n = [], []
    for a, hm, o in zip(attn, hms, os):
        hm = hm * jax.nn.sigmoid(o)
        normed = []
        for hd in range(MLSTM_HEADS):
            seg = hm[:, hd * MLSTM_DH:(hd + 1) * MLSTM_DH]
            cen = seg - jnp.mean(seg, axis=-1, keepdims=True)
            normed.append(cen * lax.rsqrt(jnp.mean(cen * cen, axis=-1, keepdims=True) + EPS))
        hm_n.append((jnp.concatenate(normed, axis=-1) * g_mlstm).astype(BF16))
        a_n.append(_rms(a, g_attn).astype(BF16))
    y = jnp.dot(jnp.concatenate(a_n, axis=0), wout_ref[0:ATTN_WIDTH, :], preferred_element_type=F32)
    y = y + jnp.dot(jnp.concatenate(hm_n, axis=0), wout_ref[ATTN_WIDTH:ATTN_WIDTH + MLSTM_WIDTH, :],
                    preferred_element_type=F32)
    return [x + gate * _rms(y[k * TILE:(k + 1) * TILE], g_post) for k, (x, gate) in enumerate(zip(xs, gates))]


def _merge_ffn_kernel(*refs, plan, split):
    per_slot = 7 if split else 6
    slots = [refs[k * per_slot:(k + 1) * per_slot] for k in range(plan.tps)]
    (ga_ref, gm_ref, gpost1_ref, wout_ref, gpre2_ref, gpost2_ref, wg_ref, wu_ref, wd_ref,
     out_ref, a_scr) = refs[plan.tps * per_slot:]
    xs = [s[0][0] for s in slots]
    attn = [_slot_tile(plan, k, s[1:-4], split) for k, s in enumerate(slots)]
    hms = [s[-4][0] + s[-3][0] for s in slots]
    os = [s[-2][0] for s in slots]
    mods = [s[-1][0, 0] for s in slots]
    x1 = _merge_body(xs, attn, hms, os, [m[5:6] for m in mods], ga_ref[...], gm_ref[...], gpost1_ref[...], wout_ref)
    outs = _ffn_body(x1, mods, 2, gpre2_ref[...], gpost2_ref[...], wg_ref, wu_ref, wd_ref, a_scr)
    for k, out in enumerate(outs):
        out_ref[k] = out


def _merge_ffn(x, a, h_fwd, h_bwd, o, mod_all, g_attn, g_mlstm, g_pre, g_post, wout, wg, wu, wd,
               *, layer, n_ctx_tiles, t0):
    bsz, t_all, d = x.shape
    n_tiles = t_all // TILE - t0
    ff = wd.shape[-2]
    split = isinstance(a, tuple)
    a_srcs = [_as_tiles(s) for s in (a if split else (a,))]
    stream = [_as_tiles(s) for s in (h_fwd, h_bwd, o)]
    xt = _as_tiles(x)
    plan = _TilePlan(bsz, n_tiles, t0, n_ctx_tiles, MERGE_FFN_TILES_PER_STEP)
    slot_specs, slot_args = [], []
    for k in range(plan.tps):
        slot_specs += [plan.tile_spec(xt, k)]
        if split:
            slot_specs += [plan.tile_spec(a_srcs[0], k, 0, clamp=True),
                           plan.tile_spec(a_srcs[1], k, n_ctx_tiles, clamp=True)]
        else:
            slot_specs += [plan.tile_spec(a_srcs[0], k, t0)]
        slot_specs += [plan.tile_spec(s, k) for s in stream] + [plan.mod_spec(d, k)]
        slot_args += [xt] + a_srcs + stream + [mod_all]
    out = pl.pallas_call(
        functools.partial(_merge_ffn_kernel, plan=plan, split=split),
        grid=plan.grid,
        in_specs=slot_specs + [
            pl.BlockSpec((None, 1, ATTN_WIDTH), lambda i: (layer, 0, 0)),
            pl.BlockSpec((None, 1, MLSTM_WIDTH), lambda i: (layer, 0, 0)),
            pl.BlockSpec((None, None, 1, d), lambda i: (layer, 1, 0, 0)),
            _const1((None, ATTN_WIDTH + MLSTM_WIDTH, d), (layer, 0, 0)),
        ] + _ffn_weight_specs(layer, 1, 2, d, ff),
        out_specs=pl.BlockSpec((plan.tps, TILE, d), lambda i: (i, 0, 0)),
        out_shape=jax.ShapeDtypeStruct((bsz * n_tiles, TILE, d), F32),
        scratch_shapes=[pltpu.VMEM((plan.tps * TILE, ff), BF16)],
        compiler_params=_params(1),
        name=f"merge_ffn_l{layer}",
    )(*slot_args, g_attn, g_mlstm, g_post, wout, g_pre, g_post, wg, wu, wd)
    return out.reshape(bsz, n_tiles * TILE, d)


def _half_swap_perm():
    idx = np.arange(ROPE_DIM)
    axis, half, freq = idx // (2 * AXIS_FREQS), (idx // AXIS_FREQS) % 2, idx % AXIS_FREQS
    return axis * 2 * AXIS_FREQS + (1 - half) * AXIS_FREQS + freq


def _prep_w_in(w_in):
    sizes = (Q_LORA, KV_LORA, ROPE_DIM, 2 * MLSTM_WIDTH, MLSTM_WIDTH, MLSTM_WIDTH, N_GATES)
    offs = np.concatenate([[0], np.cumsum(sizes)])
    cq, ckv, kr, qk, v, o, g = (w_in[..., offs[i]:offs[i + 1]] for i in range(len(sizes)))
    g = jnp.pad(g, ((0, 0), (0, 0), (0, LANES - N_GATES)))
    return jnp.concatenate([cq, ckv, kr, kr[..., _half_swap_perm()], qk, v, o, g], axis=-1).astype(BF16)


def _prep_w_uq(w_uq):
    n_layers, q_lora, _ = w_uq.shape
    w = w_uq.reshape(n_layers, q_lora, ATTN_HEADS, QK_DIM)
    nope = w[..., :NOPE_DIM].reshape(n_layers, q_lora, -1)
    rope = w[..., NOPE_DIM:]
    rope_sw = rope[..., _half_swap_perm()]
    return jnp.concatenate([nope, rope.reshape(n_layers, q_lora, -1), rope_sw.reshape(n_layers, q_lora, -1)],
                           axis=-1).astype(BF16)


def _prep_w_ukv(w_ukv):
    n_layers, kv_lora, _ = w_ukv.shape
    w = w_ukv.reshape(n_layers, kv_lora, ATTN_HEADS, NOPE_DIM + V_DIM)
    return jnp.concatenate([w[..., :NOPE_DIM].reshape(n_layers, kv_lora, -1),
                            w[..., NOPE_DIM:].reshape(n_layers, kv_lora, -1)], axis=-1).astype(BF16)


def _rope_tables(n_ctx, n_tok):
    rows = n_tok // GRID_W
    t_row = jnp.repeat(jnp.arange(rows), GRID_W).astype(F32)
    t_col = jnp.tile(jnp.arange(GRID_W), rows).astype(F32)
    inv = ROPE_BASE ** (-jnp.arange(AXIS_FREQS, dtype=F32) / AXIS_FREQS)
    ang_r = t_row[:, None] * inv
    ang_c = t_col[:, None] * inv
    cos = jnp.concatenate([jnp.cos(ang_r), jnp.cos(ang_r), jnp.cos(ang_c), jnp.cos(ang_c)], axis=-1)
    sin = jnp.concatenate([-jnp.sin(ang_r), jnp.sin(ang_r), -jnp.sin(ang_c), jnp.sin(ang_c)], axis=-1)
    cos = jnp.concatenate([jnp.ones((n_ctx, ROPE_DIM), F32), cos], axis=0)
    sin = jnp.concatenate([jnp.zeros((n_ctx, ROPE_DIM), F32), sin], axis=0)
    return jnp.tile(cos, (1, ATTN_HEADS)), jnp.tile(sin, (1, ATTN_HEADS))


def kernel(x, c, ctx, c_ctx, w_mod, b_mod, norm_pre, norm_post, ffn_w_gate, ffn_w_up, ffn_w_down,
           w_in, q_norm, w_uq, kv_norm, w_ukv, attn_out_norm, conv_w, conv_b, gate_b, mlstm_norm, w_out):
    bsz, n_tok, d = x.shape
    n_ctx = ctx.shape[1]
    n_layers = w_mod.shape[0]
    assert n_tok % TILE == 0 and n_ctx % TILE == 0 and n_tok % GRID_W == 0
    n_ctx_tiles = n_ctx // TILE
    n_lat_tiles = n_tok // TILE

    wg = jnp.swapaxes(ffn_w_gate, 2, 3).astype(BF16)
    wu = jnp.swapaxes(ffn_w_up, 2, 3).astype(BF16)
    wd = ffn_w_down.astype(BF16)
    win = _prep_w_in(w_in)
    wuq = _prep_w_uq(w_uq)
    wukv = _prep_w_ukv(w_ukv)
    wout = w_out.astype(BF16)
    g_pre = norm_pre[:, :, None, :]
    g_post = norm_post[:, :, None, :]
    q_gain = q_norm[:, None, :]
    kv_gain = kv_norm[:, None, :]
    a_gain = attn_out_norm[:, None, :]
    m_gain = mlstm_norm[:, None, :]
    conv_bias = conv_b[:, None, :]
    gate_bias = jnp.pad(gate_b.reshape(n_layers, 1, N_GATES), ((0, 0), (0, 0), (0, LANES - N_GATES)))
    cos4, sin4 = _rope_tables(n_ctx, n_tok)

    c_rows = jnp.zeros((_round_up(bsz + 1, SUBLANES), d), F32).at[:bsz].set(c).at[bsz].set(c_ctx)
    mod = _modulation(c_rows, w_mod, b_mod).reshape(n_layers, -1, N_MOD, d)
    mod_all = jnp.stack([jnp.broadcast_to(mod[:, bsz:bsz + 1], (n_layers, bsz, N_MOD, d)), mod[:, :bsz]], axis=2)

    h = (ctx, x)
    for l in range(n_layers):
        t0 = n_ctx_tiles if l == n_layers - 1 else 0
        ml = mod_all[l]
        h = _ffn(h, ml, g_pre, g_post, wg, wu, wd, layer=l, j=0, sub=0, n_ctx_tiles=n_ctx_tiles)
        q, k, vt, qt_m, k_m, vt_m, o_pre, gates_t = _mixin(
            h, ml, g_pre, win, q_gain, wuq, kv_gain, wukv, cos4, sin4, gate_bias, conv_w, conv_bias,
            layer=l, n_ctx_tiles=n_ctx_tiles)
        a = _attention(q, k, vt, q_tile0=n_ctx_tiles, n_q=n_lat_tiles, n_keys=n_ctx + n_tok,
                       name=f"attention_l{l}")
        if t0 == 0:
            a = (_attention(q, k, vt, q_tile0=0, n_q=n_ctx_tiles, n_keys=n_ctx, name=f"attention_ctx_l{l}"), a)
        h_fwd, h_bwd = _mlstm(k_m, qt_m, vt_m, gates_t, n_ctx_tiles=n_ctx_tiles, layer=l)
        h = _merge_ffn(h, a, h_fwd, h_bwd, o_pre, ml, a_gain, m_gain, g_pre, g_post, wout, wg, wu, wd,
                       layer=l, n_ctx_tiles=n_ctx_tiles, t0=t0)
    return h
```

```python
import functools
import itertools

import jax
import jax.numpy as jnp
import numpy as np
from jax import lax
from jax.experimental import pallas as pl
from jax.experimental.pallas import tpu as pltpu

F32 = jnp.float32
BF16 = jnp.bfloat16

N_MOD = 9
EPS = 1e-6
GRID_W = 64
ATTN_HEADS = 4
Q_LORA = 256
KV_LORA = 128
NOPE_DIM = 128
ROPE_DIM = 64
AXIS_FREQS = ROPE_DIM // 4
V_DIM = 128
QK_DIM = NOPE_DIM + ROPE_DIM
ATTN_WIDTH = ATTN_HEADS * V_DIM
ROPE_BASE = 10000.0
SM_SCALE = QK_DIM ** -0.5
LOG2E = float(np.log2(np.e))
Q_SCALE = SM_SCALE * LOG2E
MLSTM_HEADS = 4
MLSTM_DH = 128
MLSTM_WIDTH = MLSTM_HEADS * MLSTM_DH
CONV_K = 3
N_GATES = 4 * MLSTM_HEADS

LANES = 128
SUBLANES = 8
MXU_DIM = 256
TILE = MXU_DIM
VMEM_LIMIT = 56 * 1024 * 1024

A_W = Q_LORA + KV_LORA + 2 * ROPE_DIM
QK_OFF = A_W
V_OFF = QK_OFF + 2 * MLSTM_WIDTH
O_OFF = V_OFF + MLSTM_WIDTH
G_OFF = O_OFF + MLSTM_WIDTH
IN_W = G_OFF + LANES


def _round_up(n, m):
    return (n + m - 1) // m * m


def _rms(x, g):
    return x * lax.rsqrt(jnp.mean(x * x, axis=-1, keepdims=True) + EPS) * g


def _dot_nt(x, w_t):
    return lax.dot_general(x, w_t, (((1,), (1,)), ((), ())), preferred_element_type=F32)


def _sandwich_in(x, mod, sub, g):
    return _rms(x, g) * (1.0 + mod[3 * sub + 1:3 * sub + 2]) + mod[3 * sub:3 * sub + 1]


def _params(n_grid):
    return pltpu.CompilerParams(dimension_semantics=("arbitrary",) * n_grid,
                                vmem_limit_bytes=VMEM_LIMIT)


def _const_spec(shape, index_map):
    return pl.BlockSpec(shape, index_map, pipeline_mode=pl.Buffered(1))


FFN_TILES_PER_STEP = 4
MERGE_FFN_TILES_PER_STEP = 2


class _TilePlan:
    def __init__(self, bsz, n, t0, n_ctx_tiles, tiles_per_step):
        assert (bsz * n) % tiles_per_step == 0
        self.bsz, self.n, self.t0, self.n_ctx_tiles, self.tps = bsz, n, t0, n_ctx_tiles, tiles_per_step
        self.grid = (bsz * n // tiles_per_step,)

    def coords(self, i, k):
        g = i * self.tps + k
        return g // self.n, g % self.n + self.t0

    def is_ctx(self, k):
        return self.coords(pl.program_id(0), k)[1] < self.n_ctx_tiles

    def tile_spec(self, arr, k, arr_t0=0, clamp=False):
        arr_tiles = arr.shape[0] // self.bsz

        def index(i):
            b, r = self.coords(i, k)
            r = r - arr_t0
            if clamp:
                r = jnp.clip(r, 0, arr_tiles - 1)
            return (b * arr_tiles + r, 0, 0)

        return pl.BlockSpec((1, TILE, arr.shape[2]), index)

    def mod_spec(self, d, k):
        def index(i):
            b, r = self.coords(i, k)
            return (b, (r >= self.n_ctx_tiles).astype(jnp.int32), 0, 0)

        return pl.BlockSpec((1, 1, N_MOD, d), index)


def _as_tiles(x):
    return x.reshape(-1, TILE, x.shape[-1])


def _const1(shape, index):
    return _const_spec(shape, lambda i: index)


def _mod_kernel(c_ref, w_ref, b_ref, o_ref):
    c = c_ref[...]
    h = (c * jax.nn.sigmoid(c)).astype(BF16)
    o_ref[0] = jnp.dot(h, w_ref[0].astype(BF16), preferred_element_type=F32) + b_ref[0]


def _modulation(c_rows, w_mod, b_mod):
    n_layers, d, n_out = w_mod.shape
    rows = c_rows.shape[0]
    n_blk = 2 * N_MOD * LANES
    return pl.pallas_call(
        _mod_kernel,
        grid=(n_layers, n_out // n_blk),
        in_specs=[pl.BlockSpec((rows, d), lambda l, j: (0, 0)),
                  pl.BlockSpec((1, d, n_blk), lambda l, j: (l, 0, j)),
                  pl.BlockSpec((1, 1, n_blk), lambda l, j: (l, 0, j))],
        out_specs=pl.BlockSpec((1, rows, n_blk), lambda l, j: (l, 0, j)),
        out_shape=jax.ShapeDtypeStruct((n_layers, rows, n_out), F32),
        compiler_params=_params(2),
        name="modulation",
    )(c_rows, w_mod, b_mod.reshape(n_layers, 1, n_out))


def _ffn_body(xs, mods, sub, g_pre, g_post, wg_ref, wu_ref, wd_ref, a_ref):
    h = jnp.concatenate([_sandwich_in(x, m, sub, g_pre).astype(BF16) for x, m in zip(xs, mods)], axis=0)
    ff = a_ref.shape[1]
    for lo in range(0, ff, MXU_DIM):
        hi = min(lo + MXU_DIM, ff)
        g = _dot_nt(h, wg_ref[lo:hi, :])
        u = _dot_nt(h, wu_ref[lo:hi, :])
        a_ref[:, lo:hi] = (g * jax.nn.sigmoid(g) * u).astype(BF16)
    ff_main = ff // MXU_DIM * MXU_DIM
    y = jnp.dot(a_ref[:, 0:ff_main], wd_ref[0:ff_main, :], preferred_element_type=F32)
    if ff_main < ff:
        y = y + jnp.dot(a_ref[:, ff_main:ff], wd_ref[ff_main:ff, :], preferred_element_type=F32)
    return [x + 0.5 * m[3 * sub + 2:3 * sub + 3] * _rms(y[k * TILE:(k + 1) * TILE], g_post)
            for k, (x, m) in enumerate(zip(xs, mods))]


def _slot_tile(plan, k, refs, split):
    return jnp.where(plan.is_ctx(k), refs[0][0], refs[1][0]) if split else refs[0][0]


def _ffn_kernel(*refs, plan, sub, split):
    per_slot = 3 if split else 2
    slots = [refs[k * per_slot:(k + 1) * per_slot] for k in range(plan.tps)]
    gpre_ref, gpost_ref, wg_ref, wu_ref, wd_ref, o_ref, a_ref = refs[plan.tps * per_slot:]
    xs = [_slot_tile(plan, k, s[:-1], split) for k, s in enumerate(slots)]
    mods = [s[-1][0, 0] for s in slots]
    outs = _ffn_body(xs, mods, sub, gpre_ref[...], gpost_ref[...], wg_ref, wu_ref, wd_ref, a_ref)
    for k, out in enumerate(outs):
        o_ref[k] = out


def _ffn_weight_specs(layer, j, sub, d, ff):
    return [
        pl.BlockSpec((None, None, 1, d), lambda i: (layer, sub, 0, 0)),
        pl.BlockSpec((None, None, 1, d), lambda i: (layer, sub, 0, 0)),
        _const1((None, None, ff, d), (layer, j, 0, 0)),
        _const1((None, None, ff, d), (layer, j, 0, 0)),
        _const1((None, None, ff, d), (layer, j, 0, 0)),
    ]


def _ffn(src, mod_all, g_pre, g_post, wg, wu, wd, *, layer, j, sub, n_ctx_tiles):
    split = isinstance(src, tuple)
    srcs = [_as_tiles(s) for s in (src if split else (src,))]
    bsz, d = mod_all.shape[0], srcs[0].shape[-1]
    n_tiles = sum(s.shape[0] for s in srcs) // bsz
    ff = wd.shape[-2]
    plan = _TilePlan(bsz, n_tiles, 0, n_ctx_tiles, FFN_TILES_PER_STEP)
    slot_specs, slot_args = [], []
    for k in range(plan.tps):
        if split:
            slot_specs += [plan.tile_spec(srcs[0], k, 0, clamp=True), plan.tile_spec(srcs[1], k, n_ctx_tiles, clamp=True)]
        else:
            slot_specs += [plan.tile_spec(srcs[0], k)]
        slot_specs += [plan.mod_spec(d, k)]
        slot_args += srcs + [mod_all]
    out = pl.pallas_call(
        functools.partial(_ffn_kernel, plan=plan, sub=sub, split=split),
        grid=plan.grid,
        in_specs=slot_specs + _ffn_weight_specs(layer, j, sub, d, ff),
        out_specs=pl.BlockSpec((plan.tps, TILE, d), lambda i: (i, 0, 0)),
        out_shape=jax.ShapeDtypeStruct((bsz * n_tiles, TILE, d), F32),
        scratch_shapes=[pltpu.VMEM((plan.tps * TILE, ff), BF16)],
        compiler_params=_params(1),
        name=f"ffn_l{layer}_{j}",
    )(*slot_args, g_pre, g_post, wg, wu, wd)
    return out.reshape(bsz, n_tiles * TILE, d)


GATE_ROWS = 2 * 3 * SUBLANES
MIXIN_ROWS_PER_STEP = 2


def _log_sigmoid(x):
    return jnp.minimum(x, 0.0) - jnp.log1p(jnp.exp(-jnp.abs(x)))


def _lane_scan(x, op, identity, reverse):
    n = x.shape[1]
    lane = lax.broadcasted_iota(jnp.int32, x.shape, 1)
    shift = 1
    while shift < n:
        if reverse:
            moved = jnp.where(lane < n - shift, pltpu.roll(x, n - shift, axis=1), identity)
        else:
            moved = jnp.where(lane >= shift, pltpu.roll(x, shift, axis=1), identity)
        x = op(x, moved)
        shift *= 2
    return x


def _gate_rows(g8, reverse):
    nh = MLSTM_HEADS
    head_rows = lax.broadcasted_iota(jnp.int32, g8.shape, 0) < nh
    b = _lane_scan(_log_sigmoid(g8), jnp.add, 0.0, reverse)
    b = jnp.where(head_rows, pltpu.roll(b, nh, axis=0), 0.0)
    a = jnp.where(head_rows, g8 - b, 0.0)
    return a, b, _lane_scan(a, jnp.maximum, -jnp.inf, reverse)


def _mixin_kernel(x_ref, xp_ref, xn_ref, mod_ref, gpre_ref, win_ref, qn_ref, wuq_ref, kvn_ref, wukv_ref,
                  cos_ref, sin_ref, gb_ref, cw_ref, cb_ref,
                  q_ref, k_ref, vt_ref, qtm_ref, km_ref, vtm_ref, o_ref, gt_ref, *, n_ctx_tiles, n_tiles):
    ci = pl.program_id(1)
    n_rows = x_ref.shape[0]
    mods = [mod_ref[r, 0] for r in range(n_rows)]
    tiles = [slice(r * TILE, (r + 1) * TILE) for r in range(n_rows)]
    h = jnp.concatenate([_sandwich_in(x_ref[r], mods[r], 1, gpre_ref[...]).astype(BF16) for r in range(n_rows)],
                        axis=0)
    cos4 = cos_ref[...]
    sin4 = sin_ref[...]

    gates = jnp.dot(h, win_ref[:, G_OFF:IN_W], preferred_element_type=F32) + gb_ref[...]
    half = N_GATES // 2
    for r in range(n_rows):
        g_t = gates[tiles[r]].T[0:N_GATES, :]
        gt_ref[r] = jnp.concatenate(_gate_rows(g_t[0:half], False) + _gate_rows(g_t[half:N_GATES], True), axis=0)

    pa = jnp.dot(h, win_ref[:, 0:A_W], preferred_element_type=F32)
    cq = pa[:, 0:Q_LORA]
    ckv = pa[:, Q_LORA:Q_LORA + KV_LORA]
    nope_w = ATTN_HEADS * NOPE_DIM
    rope_w = ATTN_HEADS * ROPE_DIM
    qa = jnp.dot(_rms(cq, qn_ref[...]).astype(BF16), wuq_ref[...], preferred_element_type=F32)
    kva = jnp.dot(_rms(ckv, kvn_ref[...]).astype(BF16), wukv_ref[...], preferred_element_type=F32)
    ones_rows = jnp.where(lax.broadcasted_iota(jnp.int32, (VT_ROWS - V_DIM, TILE), 0) == 0, 1.0, 0.0).astype(BF16)
    for r in range(n_rows):
        kr = pa[tiles[r], Q_LORA + KV_LORA:Q_LORA + KV_LORA + ROPE_DIM]
        kr_sw = pa[tiles[r], Q_LORA + KV_LORA + ROPE_DIM:A_W]
        k_rope = (kr * cos4[:, 0:ROPE_DIM] + kr_sw * sin4[:, 0:ROPE_DIM]).astype(BF16)
        qa_r = qa[tiles[r]]
        kva_r = kva[tiles[r]]
        q_rope = qa_r[:, nope_w:nope_w + rope_w] * cos4 + qa_r[:, nope_w + rope_w:nope_w + 2 * rope_w] * sin4
        for hd in range(ATTN_HEADS):
            q_ref[r, hd, :, 0:NOPE_DIM] = (qa_r[:, hd * NOPE_DIM:(hd + 1) * NOPE_DIM] * Q_SCALE).astype(BF16)
            q_ref[r, hd, :, NOPE_DIM:QK_DIM] = (q_rope[:, hd * ROPE_DIM:(hd + 1) * ROPE_DIM] * Q_SCALE).astype(BF16)
            k_ref[r, hd, :, 0:NOPE_DIM] = kva_r[:, hd * NOPE_DIM:(hd + 1) * NOPE_DIM].astype(BF16)
            k_ref[r, hd, :, NOPE_DIM:QK_DIM] = k_rope
            vt_ref[r, hd, 0:V_DIM, :] = kva_r[:, nope_w + hd * V_DIM:nope_w + (hd + 1) * V_DIM].T.astype(BF16)
            vt_ref[r, hd, V_DIM:VT_ROWS, :] = ones_rows

    vtm = jnp.dot(h, win_ref[:, V_OFF:O_OFF], preferred_element_type=F32)
    o_pre = jnp.dot(h, win_ref[:, O_OFF:G_OFF], preferred_element_type=F32)
    for r in range(n_rows):
        vtm_ref[r] = vtm[tiles[r]].T.astype(BF16)
        o_ref[r] = o_pre[tiles[r]]

    halo = [_sandwich_in(ref[r], mods[r], 1, gpre_ref[...]).astype(BF16) for ref in (xp_ref, xn_ref)
            for r in range(n_rows)]
    pqk = jnp.dot(jnp.concatenate([h] + halo, axis=0), win_ref[:, QK_OFF:V_OFF], preferred_element_type=F32)
    row = lax.broadcasted_iota(jnp.int32, (TILE, 1), 0)
    has_prev = jnp.logical_and(ci != 0, ci != n_ctx_tiles)
    has_next = jnp.logical_and(ci != n_ctx_tiles - 1, ci != n_tiles - 1)
    for r in range(n_rows):
        x = pqk[tiles[r]]
        prev_at = (n_rows * TILE) + r * SUBLANES + SUBLANES - 1
        next_at = (n_rows * TILE) + (n_rows + r) * SUBLANES
        prev_row = jnp.where(has_prev, pqk[prev_at:prev_at + 1], 0.0)
        next_row = jnp.where(has_next, pqk[next_at:next_at + 1], 0.0)
        x_prev = jnp.where(row == 0, prev_row, pltpu.roll(x, 1, axis=0))
        x_next = jnp.where(row == TILE - 1, next_row, pltpu.roll(x, TILE - 1, axis=0))
        u = x_prev * cw_ref[0:1, :] + x * cw_ref[1:2, :] + x_next * cw_ref[2:3, :] + cb_ref[...]
        u = u * jax.nn.sigmoid(u)
        qtm_ref[r] = u[:, 0:MLSTM_WIDTH].T.astype(BF16)
        km_ref[r] = (u[:, MLSTM_WIDTH:2 * MLSTM_WIDTH] * (MLSTM_DH ** -0.5)).astype(BF16)


def _mixin(x, mod_all, g_pre, win, q_norm, wuq, kv_norm, wukv, cos4, sin4, gate_b, conv_w, conv_b,
           *, layer, n_ctx_tiles):
    bsz, t_all, d = x.shape
    n_tiles = t_all // TILE
    rope_w = ATTN_HEADS * ROPE_DIM
    halo_per_tile = TILE // SUBLANES
    n_halo = t_all // SUBLANES
    rps = MIXIN_ROWS_PER_STEP if bsz % MIXIN_ROWS_PER_STEP == 0 else 1
    tile_map = lambda b, t: (b, t, 0)
    head_map = lambda b, t: (b, 0, t, 0)
    time_last_map = lambda b, t: (b, 0, t)
    out_shape = (
        jax.ShapeDtypeStruct((bsz, ATTN_HEADS, t_all, QK_DIM), BF16),
        jax.ShapeDtypeStruct((bsz, ATTN_HEADS, t_all, QK_DIM), BF16),
        jax.ShapeDtypeStruct((bsz, ATTN_HEADS, VT_ROWS, t_all), BF16),
        jax.ShapeDtypeStruct((bsz, MLSTM_WIDTH, t_all), BF16),
        jax.ShapeDtypeStruct((bsz, t_all, MLSTM_WIDTH), BF16),
        jax.ShapeDtypeStruct((bsz, MLSTM_WIDTH, t_all), BF16),
        jax.ShapeDtypeStruct((bsz, t_all, MLSTM_WIDTH), F32),
        jax.ShapeDtypeStruct((bsz, GATE_ROWS, t_all), F32),
    )
    return pl.pallas_call(
        functools.partial(_mixin_kernel, n_ctx_tiles=n_ctx_tiles, n_tiles=n_tiles),
        grid=(bsz // rps, n_tiles),
        in_specs=[
            pl.BlockSpec((rps, TILE, d), tile_map),
            pl.BlockSpec((rps, SUBLANES, d), lambda b, t: (b, jnp.maximum(t * halo_per_tile - 1, 0), 0)),
            pl.BlockSpec((rps, SUBLANES, d), lambda b, t: (b, jnp.minimum((t + 1) * halo_per_tile, n_halo - 1), 0)),
            pl.BlockSpec((rps, 1, N_MOD, d), lambda b, t: (b, (t >= n_ctx_tiles).astype(jnp.int32), 0, 0)),
            pl.BlockSpec((None, None, 1, d), lambda b, t: (layer, 1, 0, 0)),
            _const_spec((None, d, IN_W), lambda b, t: (layer, 0, 0)),
            pl.BlockSpec((None, 1, Q_LORA), lambda b, t: (layer, 0, 0)),
            _const_spec((None, Q_LORA, wuq.shape[-1]), lambda b, t: (layer, 0, 0)),
            pl.BlockSpec((None, 1, KV_LORA), lambda b, t: (layer, 0, 0)),
            _const_spec((None, KV_LORA, wukv.shape[-1]), lambda b, t: (layer, 0, 0)),
            pl.BlockSpec((TILE, rope_w), lambda b, t: (t, 0)),
            pl.BlockSpec((TILE, rope_w), lambda b, t: (t, 0)),
            pl.BlockSpec((None, 1, LANES), lambda b, t: (layer, 0, 0)),
            pl.BlockSpec((None, CONV_K, 2 * MLSTM_WIDTH), lambda b, t: (layer, 0, 0)),
            pl.BlockSpec((None, 1, 2 * MLSTM_WIDTH), lambda b, t: (layer, 0, 0)),
        ],
        out_specs=(
            pl.BlockSpec((rps, ATTN_HEADS, TILE, QK_DIM), head_map),
            pl.BlockSpec((rps, ATTN_HEADS, TILE, QK_DIM), head_map),
            pl.BlockSpec((rps, ATTN_HEADS, VT_ROWS, TILE), lambda b, t: (b, 0, 0, t)),
            pl.BlockSpec((rps, MLSTM_WIDTH, TILE), time_last_map),
            pl.BlockSpec((rps, TILE, MLSTM_WIDTH), tile_map),
            pl.BlockSpec((rps, MLSTM_WIDTH, TILE), time_last_map),
            pl.BlockSpec((rps, TILE, MLSTM_WIDTH), tile_map),
            pl.BlockSpec((rps, GATE_ROWS, TILE), time_last_map),
        ),
        out_shape=out_shape,
        compiler_params=_params(2),
        name=f"mixin_l{layer}",
    )(x, x, x, mod_all, g_pre, win, q_norm, wuq, kv_norm, wukv, cos4, sin4, gate_b, conv_w, conv_b)


KEY_CHUNK = 2 * TILE
ATTN_BUFFERS = 2
VT_ROWS = V_DIM + 2 * SUBLANES


def _attn_kernel(*refs, chunks):
    q_refs = refs[:-5]
    k_ref, vt_ref, o_ref, s_scr, p_scr = refs[-5:]
    n_heads = k_ref.shape[1]
    units = [(j, h) for j in range(len(q_refs)) for h in range(n_heads)]
    lo, hi = chunks[0][0], chunks[-1][1]
    fold = lambda v: v.reshape(-1, SUBLANES, TILE)
    m8 = {}
    for stage in range(len(units) + 2):
        u_s, u_p, u_v = stage, stage - 1, stage - 2
        do_s, do_p, do_v = u_s < len(units), 0 <= u_p < len(units), 0 <= u_v < len(units)
        if do_v:
            j, h = units[u_v]
            acc = jnp.dot(vt_ref[0, h, :, lo:hi], p_scr[u_v % ATTN_BUFFERS, lo:hi, :], preferred_element_type=F32)
            o_ref[0, j * TILE:(j + 1) * TILE, h * V_DIM:(h + 1) * V_DIM] = (acc[0:V_DIM] / acc[V_DIM:V_DIM + 1]).T
        if do_s:
            j, h_s = units[u_s]
            q = q_refs[j][0, h_s]
            m8[u_s] = jnp.full((SUBLANES, TILE), -jnp.inf, F32)
        if do_p:
            m = jnp.max(m8.pop(u_p), axis=0, keepdims=True)
        for c_lo, c_hi in chunks:
            if do_s:
                s = lax.dot_general(k_ref[0, h_s, c_lo:c_hi, :], q, (((1,), (1,)), ((), ())),
                                    preferred_element_type=F32)
                s_scr[u_s % ATTN_BUFFERS, c_lo:c_hi, :] = s
                m8[u_s] = jnp.maximum(m8[u_s], jnp.max(fold(s), axis=0))
            if do_p:
                p_scr[u_p % ATTN_BUFFERS, c_lo:c_hi, :] = jnp.exp2(
                    s_scr[u_p % ATTN_BUFFERS, c_lo:c_hi, :] - m).astype(BF16)


def _key_chunks(n_keys, width):
    return tuple((lo, min(lo + width, n_keys)) for lo in range(0, n_keys, width))


def _attention(q, k, vt, *, q_tile0, n_q, n_keys, name):
    bsz, n_heads, _, _ = q.shape
    qts = max(t for t in (1, 2, 4) if n_q % t == 0)
    q_specs = [pl.BlockSpec((1, n_heads, TILE, QK_DIM), lambda b, i, j=j: (b, 0, i * qts + j + q_tile0, 0))
               for j in range(qts)]
    return pl.pallas_call(
        functools.partial(_attn_kernel, chunks=_key_chunks(n_keys, KEY_CHUNK)),
        grid=(bsz, n_q // qts),
        in_specs=q_specs + [
            pl.BlockSpec((1, n_heads, n_keys, QK_DIM), lambda b, i: (b, 0, 0, 0)),
            pl.BlockSpec((1, n_heads, VT_ROWS, n_keys), lambda b, i: (b, 0, 0, 0)),
        ],
        out_specs=pl.BlockSpec((1, qts * TILE, n_heads * V_DIM), lambda b, i: (b, i, 0)),
        out_shape=jax.ShapeDtypeStruct((bsz, n_q * TILE, n_heads * V_DIM), F32),
        scratch_shapes=[pltpu.VMEM((ATTN_BUFFERS, n_keys, TILE), F32),
                        pltpu.VMEM((ATTN_BUFFERS, n_keys, TILE), BF16)],
        compiler_params=_params(2),
        name=name,
    )(*([q] * qts), k, vt)


STATE_ROWS = MLSTM_DH + 2 * SUBLANES
MLSTM_ROWS_PER_STEP = 4


def _bwd_chunk(j, n_ctx_tiles, n_tiles):
    return jnp.where(j < n_ctx_tiles, n_ctx_tiles - 1 - j, n_tiles - 1 - (j - n_ctx_tiles))


def _mlstm_kernel(kf_ref, qtf_ref, vtf_ref, gf_ref, kb_ref, qtb_ref, vtb_ref, gb_ref, of_ref, ob_ref,
                  st_scr, m_scr, d_scr, p_scr, r_scr):
    nh = MLSTM_HEADS
    rows_per_step = kf_ref.shape[0]

    @pl.when(pl.program_id(1) == 0)
    def _():
        st_scr[...] = jnp.zeros_like(st_scr)
        m_scr[...] = jnp.zeros_like(m_scr)

    rows = [slice(i * SUBLANES, (i + 1) * SUBLANES) for i in range(GATE_ROWS // SUBLANES)]
    s_idx = lax.broadcasted_iota(jnp.int32, (TILE, TILE), 0)
    t_idx = lax.broadcasted_iota(jnp.int32, (TILE, TILE), 1)
    pad_row = lax.broadcasted_iota(jnp.int32, (2 * SUBLANES, TILE), 0) == 0
    ones_rows = jnp.where(pad_row, 1.0, 0.0).astype(BF16)

    units = []
    scans = ((False, gf_ref, 0, kf_ref, qtf_ref, vtf_ref, of_ref), (True, gb_ref, 3, kb_ref, qtb_ref, vtb_ref, ob_ref))
    for bb, (d, (reverse, g_ref, g0, k_ref, qt_ref, vt_ref, o_ref)) in itertools.product(
            range(rows_per_step), enumerate(scans)):
        scan = bb * len(scans) + d
        a, b, a_max = g_ref[bb, rows[g0], :], g_ref[bb, rows[g0 + 1], :], g_ref[bb, rows[g0 + 2], :]
        last = 0 if reverse else TILE - 1
        m_old = m_scr[scan][:, 0:1]
        mx = jnp.maximum(m_old, a_max)
        inter = jnp.exp(m_old - mx)
        e_inv = jnp.exp(-(b + mx))
        mx_last = mx[:, last:last + 1]
        w = jnp.exp(a - mx_last)
        decay = jnp.exp(m_old - mx_last)
        m_scr[scan] = jnp.broadcast_to(b[:, last:last + 1] + mx_last, (SUBLANES, LANES))
        seen = s_idx >= t_idx if reverse else s_idx <= t_idx
        for hd in range(nh):
            sl = slice(hd * MLSTM_DH, (hd + 1) * MLSTM_DH)
            row = slice(hd, hd + 1)
            units.append(dict(u=scan * nh + hd, bb=bb, sl=sl, seen=seen, a=a[row], mx=mx[row], inter=inter[row],
                              e_inv=e_inv[row], w=w[row], decay=decay[row],
                              k_ref=k_ref, qt_ref=qt_ref, vt_ref=vt_ref, o_ref=o_ref))

    for un in units:
        a_rep = jnp.broadcast_to(un["a"] * LOG2E, (LANES, TILE)).T
        log_d = jnp.concatenate([a_rep] * (TILE // LANES), axis=1) - un["mx"] * LOG2E
        d_scr[un["u"]] = jnp.exp2(jnp.where(un["seen"], log_d, -jnp.inf))

    for un in units:
        u, bb, sl = un["u"], un["bb"], un["sl"]
        lhs = jnp.concatenate([un["k_ref"][bb, :, sl], st_scr[u].astype(BF16)], axis=0)
        r = jnp.dot(lhs, un["qt_ref"][bb, sl, :], preferred_element_type=F32)
        p_scr[u] = (r[0:TILE] * d_scr[u]).astype(BF16)
        r_scr[u] = r[TILE:TILE + STATE_ROWS]

    for un in units:
        u, bb, sl = un["u"], un["bb"], un["sl"]
        nd = jnp.dot(jnp.concatenate([un["vt_ref"][bb, sl, :], ones_rows], axis=0), p_scr[u],
                     preferred_element_type=F32)
        r_state = r_scr[u]
        den = nd[MLSTM_DH:MLSTM_DH + 1] + un["inter"] * r_state[MLSTM_DH:MLSTM_DH + 1]
        scale = 1.0 / jnp.maximum(jnp.abs(den), un["e_inv"])
        h_t = (nd[0:MLSTM_DH] + un["inter"] * r_state[0:MLSTM_DH]) * scale
        un["o_ref"][bb, :, sl] = h_t.T

    for un in units:
        u, bb, sl = un["u"], un["bb"], un["sl"]
        vt_h = un["vt_ref"][bb, sl, :]
        vw = jnp.concatenate([(vt_h.astype(F32) * un["w"]).astype(BF16),
                              jnp.where(pad_row, un["w"], 0.0).astype(BF16)], axis=0)
        st_scr[u] = un["decay"] * st_scr[u] + jnp.dot(vw, un["k_ref"][bb, :, sl], preferred_element_type=F32)


def _mlstm(k_m, qt_m, vt_m, gates_t, *, n_ctx_tiles, layer):
    bsz, t_all, width = k_m.shape
    n_tiles = t_all // TILE
    bps = MLSTM_ROWS_PER_STEP if bsz % MLSTM_ROWS_PER_STEP == 0 else 1
    n_units = bps * 2 * MLSTM_HEADS
    bwd = functools.partial(_bwd_chunk, n_ctx_tiles=n_ctx_tiles, n_tiles=n_tiles)

    def specs(chunk):
        return [pl.BlockSpec((bps, TILE, width), lambda b, j: (b, chunk(j), 0)),
                pl.BlockSpec((bps, width, TILE), lambda b, j: (b, 0, chunk(j))),
                pl.BlockSpec((bps, width, TILE), lambda b, j: (b, 0, chunk(j))),
                pl.BlockSpec((bps, GATE_ROWS, TILE), lambda b, j: (b, 0, chunk(j)))]

    out_sds = jax.ShapeDtypeStruct((bsz, t_all, width), F32)
    return pl.pallas_call(
        _mlstm_kernel,
        grid=(bsz // bps, n_tiles),
        in_specs=specs(lambda j: j) + specs(bwd),
        out_specs=(pl.BlockSpec((bps, TILE, width), lambda b, j: (b, j, 0)),
                   pl.BlockSpec((bps, TILE, width), lambda b, j: (b, bwd(j), 0))),
        out_shape=(out_sds, out_sds),
        scratch_shapes=[pltpu.VMEM((n_units, STATE_ROWS, MLSTM_DH), F32),
                        pltpu.VMEM((n_units // MLSTM_HEADS, SUBLANES, LANES), F32),
                        pltpu.VMEM((n_units, TILE, TILE), F32),
                        pltpu.VMEM((n_units, TILE, TILE), BF16),
                        pltpu.VMEM((n_units, STATE_ROWS, TILE), F32)],
        compiler_params=_params(2),
        name=f"mlstm_l{layer}",
    )(k_m, qt_m, vt_m, gates_t, k_m, qt_m, vt_m, gates_t)


def _merge_body(xs, attn, hms, os, gates, g_attn, g_mlstm, g_post, wout_ref):
    a_n, hm_n = [], []
    for a, hm, o in zip(attn, hms, os):
        hm = hm * jax.nn.sigmoid(o)
        normed = []
        for hd in range(MLSTM_HEADS):
            seg = hm[:, hd * MLSTM_DH:(hd + 1) * MLSTM_DH]
            cen = seg - jnp.mean(seg, axis=-1, keepdims=True)
            normed.append(cen * lax.rsqrt(jnp.mean(cen * cen, axis=-1, keepdims=True) + EPS))
        hm_n.append((jnp.concatenate(normed, axis=-1) * g_mlstm).astype(BF16))
        a_n.append(_rms(a, g_attn).astype(BF16))
    y = jnp.dot(jnp.concatenate(a_n, axis=0), wout_ref[0:ATTN_WIDTH, :], preferred_element_type=F32)
    y = y + jnp.dot(jnp.concatenate(hm_n, axis=0), wout_ref[ATTN_WIDTH:ATTN_WIDTH + MLSTM_WIDTH, :],
                    preferred_element_type=F32)
    return [x + gate * _rms(y[k * TILE:(k + 1) * TILE], g_post) for k, (x, gate) in enumerate(zip(xs, gates))]


def _merge_ffn_kernel(*refs, plan, split):
    per_slot = 7 if split else 6
    slots = [refs[k * per_slot:(k + 1) * per_slot] for k in range(plan.tps)]
    (ga_ref, gm_ref, gpost1_ref, wout_ref, gpre2_ref, gpost2_ref, wg_ref, wu_ref, wd_ref,
     out_ref, a_scr) = refs[plan.tps * per_slot:]
    xs = [s[0][0] for s in slots]
    attn = [_slot_tile(plan, k, s[1:-4], split) for k, s in enumerate(slots)]
    hms = [s[-4][0] + s[-3][0] for s in slots]
    os = [s[-2][0] for s in slots]
    mods = [s[-1][0, 0] for s in slots]
    x1 = _merge_body(xs, attn, hms, os, [m[5:6] for m in mods], ga_ref[...], gm_ref[...], gpost1_ref[...], wout_ref)
    outs = _ffn_body(x1, mods, 2, gpre2_ref[...], gpost2_ref[...], wg_ref, wu_ref, wd_ref, a_scr)
    for k, out in enumerate(outs):
        out_ref[k] = out


def _merge_ffn(x, a, h_fwd, h_bwd, o, mod_all, g_attn, g_mlstm, g_pre, g_post, wout, wg, wu, wd,
               *, layer, n_ctx_tiles, t0):
    bsz, t_all, d = x.shape
    n_tiles = t_all // TILE - t0
    ff = wd.shape[-2]
    split = isinstance(a, tuple)
    a_srcs = [_as_tiles(s) for s in (a if split else (a,))]
    stream = [_as_tiles(s) for s in (h_fwd, h_bwd, o)]
    xt = _as_tiles(x)
    plan = _TilePlan(bsz, n_tiles, t0, n_ctx_tiles, MERGE_FFN_TILES_PER_STEP)
    slot_specs, slot_args = [], []
    for k in range(plan.tps):
        slot_specs += [plan.tile_spec(xt, k)]
        if split:
            slot_specs += [plan.tile_spec(a_srcs[0], k, 0, clamp=True),
                           plan.tile_spec(a_srcs[1], k, n_ctx_tiles, clamp=True)]
        else:
            slot_specs += [plan.tile_spec(a_srcs[0], k, t0)]
        slot_specs += [plan.tile_spec(s, k) for s in stream] + [plan.mod_spec(d, k)]
        slot_args += [xt] + a_srcs + stream + [mod_all]
    out = pl.pallas_call(
        functools.partial(_merge_ffn_kernel, plan=plan, split=split),
        grid=plan.grid,
        in_specs=slot_specs + [
            pl.BlockSpec((None, 1, ATTN_WIDTH), lambda i: (layer, 0, 0)),
            pl.BlockSpec((None, 1, MLSTM_WIDTH), lambda i: (layer, 0, 0)),
            pl.BlockSpec((None, None, 1, d), lambda i: (layer, 1, 0, 0)),
            _const1((None, ATTN_WIDTH + MLSTM_WIDTH, d), (layer, 0, 0)),
        ] + _ffn_weight_specs(layer, 1, 2, d, ff),
        out_specs=pl.BlockSpec((plan.tps, TILE, d), lambda i: (i, 0, 0)),
        out_shape=jax.ShapeDtypeStruct((bsz * n_tiles, TILE, d), F32),
        scratch_shapes=[pltpu.VMEM((plan.tps * TILE, ff), BF16)],
        compiler_params=_params(1),
        name=f"merge_ffn_l{layer}",
    )(*slot_args, g_attn, g_mlstm, g_post, wout, g_pre, g_post, wg, wu, wd)
    return out.reshape(bsz, n_tiles * TILE, d)


def _half_swap_perm():
    idx = np.arange(ROPE_DIM)
    axis, half, freq = idx // (2 * AXIS_FREQS), (idx // AXIS_FREQS) % 2, idx % AXIS_FREQS
    return axis * 2 * AXIS_FREQS + (1 - half) * AXIS_FREQS + freq


def _prep_w_in(w_in):
    sizes = (Q_LORA, KV_LORA, ROPE_DIM, 2 * MLSTM_WIDTH, MLSTM_WIDTH, MLSTM_WIDTH, N_GATES)
    offs = np.concatenate([[0], np.cumsum(sizes)])
    cq, ckv, kr, qk, v, o, g = (w_in[..., offs[i]:offs[i + 1]] for i in range(len(sizes)))
    g = jnp.pad(g, ((0, 0), (0, 0), (0, LANES - N_GATES)))
    return jnp.concatenate([cq, ckv, kr, kr[..., _half_swap_perm()], qk, v, o, g], axis=-1).astype(BF16)


def _prep_w_uq(w_uq):
    n_layers, q_lora, _ = w_uq.shape
    w = w_uq.reshape(n_layers, q_lora, ATTN_HEADS, QK_DIM)
    nope = w[..., :NOPE_DIM].reshape(n_layers, q_lora, -1)
    rope = w[..., NOPE_DIM:]
    rope_sw = rope[..., _half_swap_perm()]
    return jnp.concatenate([nope, rope.reshape(n_layers, q_lora, -1), rope_sw.reshape(n_layers, q_lora, -1)],
                           axis=-1).astype(BF16)


def _prep_w_ukv(w_ukv):
    n_layers, kv_lora, _ = w_ukv.shape
    w = w_ukv.reshape(n_layers, kv_lora, ATTN_HEADS, NOPE_DIM + V_DIM)
    return jnp.concatenate([w[..., :NOPE_DIM].reshape(n_layers, kv_lora, -1),
                            w[..., NOPE_DIM:].reshape(n_layers, kv_lora, -1)], axis=-1).astype(BF16)


def _rope_tables(n_ctx, n_tok):
    rows = n_tok // GRID_W
    t_row = jnp.repeat(jnp.arange(rows), GRID_W).astype(F32)
    t_col = jnp.tile(jnp.arange(GRID_W), rows).astype(F32)
    inv = ROPE_BASE ** (-jnp.arange(AXIS_FREQS, dtype=F32) / AXIS_FREQS)
    ang_r = t_row[:, None] * inv
    ang_c = t_col[:, None] * inv
    cos = jnp.concatenate([jnp.cos(ang_r), jnp.cos(ang_r), jnp.cos(ang_c), jnp.cos(ang_c)], axis=-1)
    sin = jnp.concatenate([-jnp.sin(ang_r), jnp.sin(ang_r), -jnp.sin(ang_c), jnp.sin(ang_c)], axis=-1)
    cos = jnp.concatenate([jnp.ones((n_ctx, ROPE_DIM), F32), cos], axis=0)
    sin = jnp.concatenate([jnp.zeros((n_ctx, ROPE_DIM), F32), sin], axis=0)
    return jnp.tile(cos, (1, ATTN_HEADS)), jnp.tile(sin, (1, ATTN_HEADS))


def kernel(x, c, ctx, c_ctx, w_mod, b_mod, norm_pre, norm_post, ffn_w_gate, ffn_w_up, ffn_w_down,
           w_in, q_norm, w_uq, kv_norm, w_ukv, attn_out_norm, conv_w, conv_b, gate_b, mlstm_norm, w_out):
    bsz, n_tok, d = x.shape
    n_ctx = ctx.shape[1]
    n_layers = w_mod.shape[0]
    assert n_tok % TILE == 0 and n_ctx % TILE == 0 and n_tok % GRID_W == 0
    n_ctx_tiles = n_ctx // TILE
    n_lat_tiles = n_tok // TILE

    wg = jnp.swapaxes(ffn_w_gate, 2, 3).astype(BF16)
    wu = jnp.swapaxes(ffn_w_up, 2, 3).astype(BF16)
    wd = ffn_w_down.astype(BF16)
    win = _prep_w_in(w_in)
    wuq = _prep_w_uq(w_uq)
    wukv = _prep_w_ukv(w_ukv)
    wout = w_out.astype(BF16)
    g_pre = norm_pre[:, :, None, :]
    g_post = norm_post[:, :, None, :]
    q_gain = q_norm[:, None, :]
    kv_gain = kv_norm[:, None, :]
    a_gain = attn_out_norm[:, None, :]
    m_gain = mlstm_norm[:, None, :]
    conv_bias = conv_b[:, None, :]
    gate_bias = jnp.pad(gate_b.reshape(n_layers, 1, N_GATES), ((0, 0), (0, 0), (0, LANES - N_GATES)))
    cos4, sin4 = _rope_tables(n_ctx, n_tok)

    c_rows = jnp.zeros((_round_up(bsz + 1, SUBLANES), d), F32).at[:bsz].set(c).at[bsz].set(c_ctx)
    mod = _modulation(c_rows, w_mod, b_mod).reshape(n_layers, -1, N_MOD, d)
    mod_all = jnp.stack([jnp.broadcast_to(mod[:, bsz:bsz + 1], (n_layers, bsz, N_MOD, d)), mod[:, :bsz]], axis=2)

    h = (ctx, x)
    for l in range(n_layers):
        t0 = n_ctx_tiles if l == n_layers - 1 else 0
        ml = mod_all[l]
        h = _ffn(h, ml, g_pre, g_post, wg, wu, wd, layer=l, j=0, sub=0, n_ctx_tiles=n_ctx_tiles)
        q, k, vt, qt_m, k_m, vt_m, o_pre, gates_t = _mixin(
            h, ml, g_pre, win, q_gain, wuq, kv_gain, wukv, cos4, sin4, gate_bias, conv_w, conv_bias,
            layer=l, n_ctx_tiles=n_ctx_tiles)
        a = _attention(q, k, vt, q_tile0=n_ctx_tiles, n_q=n_lat_tiles, n_keys=n_ctx + n_tok,
                       name=f"attention_l{l}")
        if t0 == 0:
            a = (_attention(q, k, vt, q_tile0=0, n_q=n_ctx_tiles, n_keys=n_ctx, name=f"attention_ctx_l{l}"), a)
        h_fwd, h_bwd = _mlstm(k_m, qt_m, vt_m, gates_t, n_ctx_tiles=n_ctx_tiles, layer=l)
        h = _merge_ffn(h, a, h_fwd, h_bwd, o_pre, ml, a_gain, m_gain, g_pre, g_post, wout, wg, wu, wd,
                       layer=l, n_ctx_tiles=n_ctx_tiles, t0=t0)
    return h
```
